```python
import math
import jax
import jax.numpy as jnp
from jax import lax
import numpy as np

D_MODEL = 1024
BATCH = 2
SEQ = 8192
DEPTH = 1
DEC_BATCH = 128
DEC_SEQ = 8
PAST_LEN = 16384
PAGE_SIZE = 128

HEAD_DIM = 64
ATT_HEADS = (D_MODEL // 2) // HEAD_DIM
ATT_KV_HEADS = ATT_HEADS // 4
ATT_GROUP = ATT_HEADS // ATT_KV_HEADS
WINDOW = 128
ATT_BLOCK = WINDOW
ROPE_THETA = 10000.0
GLA_HEADS = 4
GLA_DK = (D_MODEL // 4) // GLA_HEADS
GLA_DV = (D_MODEL // 2) // GLA_HEADS
GLA_RANK = 16
GLA_GATE_TEMP = 16.0
GLA_CHUNK = 16
D_FF = 4 * D_MODEL
ATT_WIDTH = ATT_HEADS * HEAD_DIM
KV_WIDTH = ATT_KV_HEADS * HEAD_DIM
GLA_KW = GLA_HEADS * GLA_DK
GLA_VW = GLA_HEADS * GLA_DV
MIX_WIDTH = ATT_WIDTH + GLA_VW
IN_WIDTH = ATT_WIDTH + 2 * KV_WIDTH + 2 * GLA_KW + 2 * GLA_VW + GLA_RANK
EPS = 1e-6

kernel_name = "hymba_swa_sink_gla_sandwich_step"


def rms_norm(x, g):
    xf = x.astype(jnp.float32)
    y = xf * lax.rsqrt(jnp.mean(xf * xf, axis=-1, keepdims=True) + EPS)
    return (y * g.astype(jnp.float32)).astype(x.dtype)


def rope(x, pos):
    half = HEAD_DIM // 2
    inv = ROPE_THETA ** (-jnp.arange(half, dtype=jnp.float32) / half)
    ang = pos.astype(jnp.float32)[:, None] * inv[None, :]
    cos = jnp.cos(ang)[:, None, :]
    sin = jnp.sin(ang)[:, None, :]
    xf = x.astype(jnp.float32)
    x1, x2 = xf[..., :half], xf[..., half:]
    return jnp.concatenate([x1 * cos - x2 * sin, x2 * cos + x1 * sin], axis=-1).astype(x.dtype)


def sink_softmax(scores, mask, sink):
    scores = jnp.where(mask, scores, -jnp.inf)
    m = jnp.maximum(jnp.max(scores, axis=-1, keepdims=True), sink)
    p = jnp.exp(scores - m)
    return p / (jnp.sum(p, axis=-1, keepdims=True) + jnp.exp(sink - m))


def swa_banded(q, k, v, sinks):
    B, T = q.shape[:2]
    L = ATT_BLOCK
    nb = T // L
    qb = q.reshape(B, nb, L, ATT_KV_HEADS, ATT_GROUP, HEAD_DIM)
    kb = k.reshape(B, nb, L, ATT_KV_HEADS, HEAD_DIM)
    vb = v.reshape(B, nb, L, ATT_KV_HEADS, HEAD_DIM)
    kk = jnp.concatenate([jnp.concatenate([jnp.zeros_like(kb[:, :1]), kb[:, :-1]], 1), kb], 2)
    vv = jnp.concatenate([jnp.concatenate([jnp.zeros_like(vb[:, :1]), vb[:, :-1]], 1), vb], 2)
    scores = jnp.einsum('bnqhgd,bnkhd->bnhgqk', qb, kk,
                        preferred_element_type=jnp.float32) * (HEAD_DIM ** -0.5)
    qi = jnp.arange(L)[:, None] + L
    ki = jnp.arange(2 * L)[None, :]
    rel = qi - ki
    band = (rel >= 0) & (rel < WINDOW)
    blk_ok = (jnp.arange(nb)[:, None] > 0) | (jnp.arange(2 * L)[None, :] >= L)
    mask = (band[None] & blk_ok[:, None, :])[None, :, None, None]
    sink = sinks.astype(jnp.float32).reshape(ATT_KV_HEADS, ATT_GROUP, 1, 1)
    p = sink_softmax(scores, mask, sink)
    out = jnp.einsum('bnhgqk,bnkhd->bnqhgd', p.astype(v.dtype), vv)
    return out.reshape(B, T, ATT_WIDTH)


def swa_window(q, k_all, v_all, sinks):
    B, T = q.shape[:2]
    qg = q.reshape(B, T, ATT_KV_HEADS, ATT_GROUP, HEAD_DIM)
    scores = jnp.einsum('bqhgd,bkhd->bhgqk', qg, k_all,
                        preferred_element_type=jnp.float32) * (HEAD_DIM ** -0.5)
    rel = (jnp.arange(T)[:, None] + WINDOW) - jnp.arange(WINDOW + T)[None, :]
    mask = (rel >= 0) & (rel < WINDOW)
    sink = sinks.astype(jnp.float32).reshape(ATT_KV_HEADS, ATT_GROUP, 1, 1)
    p = sink_softmax(scores, mask, sink)
    out = jnp.einsum('bhgqk,bkhd->bqhgd', p.astype(v_all.dtype), v_all)
    return out.reshape(B, T, ATT_WIDTH)


def gla_chunked(q, k, v, log_a, s0):
    B, T = q.shape[:2]
    C = math.gcd(T, GLA_CHUNK)
    n = T // C
    f32 = jnp.float32
    qc = q.astype(f32).reshape(B, n, C, GLA_HEADS, GLA_DK) * (GLA_DK ** -0.5)
    kc = k.astype(f32).reshape(B, n, C, GLA_HEADS, GLA_DK)
    vc = v.astype(f32).reshape(B, n, C, GLA_HEADS, GLA_DV)
    b = jnp.cumsum(log_a.reshape(B, n, C, GLA_HEADS, GLA_DK), axis=2)
    causal = jnp.tril(jnp.ones((C, C), dtype=bool))[:, :, None, None]
    diff = b[:, :, :, None] - b[:, :, None, :]
    k_rel = jnp.exp(jnp.where(causal, diff, -jnp.inf)) * kc[:, :, None]
    attn = jnp.einsum('bnthk,bntshk->bnhts', qc, k_rel)
    o_intra = jnp.einsum('bnhts,bnshv->bnthv', attn, vc)
    b_last = b[:, :, -1:]
    q_dec = qc * jnp.exp(b)
    k_dec = kc * jnp.exp(b_last - b)
    a_blk = jnp.exp(b_last[:, :, 0])

    def step(S, xs):
        qd, kd, vv, ab = xs
        o = jnp.einsum('bthk,bhkv->bthv', qd, S)
        S = ab[..., None] * S + jnp.einsum('bthk,bthv->bhkv', kd, vv)
        return S, o

    xs = (jnp.moveaxis(q_dec, 1, 0), jnp.moveaxis(k_dec, 1, 0),
          jnp.moveaxis(vc, 1, 0), jnp.moveaxis(a_blk, 1, 0))
    S, o_inter = lax.scan(step, s0, xs)
    o = o_intra + jnp.moveaxis(o_inter, 0, 1)
    return o.reshape(B, T, GLA_HEADS, GLA_DV), S


def hybrid_layer(x, pos, k_win, v_win, s0, w_in, w_gk2, b_gk, g_gla, sinks, w_out,
                 g_mix_pre, g_mix_post, g_ffn_pre, g_ffn_post, w_up, w_down):
    B, T, _ = x.shape
    h = rms_norm(x, g_mix_pre)
    proj = jnp.einsum('btd,de->bte', h, w_in)
    widths = (ATT_WIDTH, KV_WIDTH, KV_WIDTH, GLA_KW, GLA_KW, GLA_VW, GLA_VW, GLA_RANK)
    points, acc = [], 0
    for wdt in widths[:-1]:
        acc += wdt
        points.append(acc)
    q, k, v, gq, gk, gv, gr, glr = jnp.split(proj, points, axis=-1)

    q = rope(q.reshape(B, T, ATT_HEADS, HEAD_DIM), pos)
    k = rope(k.reshape(B, T, ATT_KV_HEADS, HEAD_DIM), pos)
    v = v.reshape(B, T, ATT_KV_HEADS, HEAD_DIM)
    if k_win is None:
        attn = swa_banded(q, k, v, sinks)
        k_all, v_all = k, v
    else:
        k_all = jnp.concatenate([k_win.astype(k.dtype), k], axis=1)
        v_all = jnp.concatenate([v_win.astype(v.dtype), v], axis=1)
        attn = swa_window(q, k_all, v_all, sinks)
    new_k = k_all[:, -WINDOW:]
    new_v = v_all[:, -WINDOW:]

    log_a = jax.nn.log_sigmoid(
        (jnp.einsum('btr,rk->btk', glr, w_gk2) + b_gk).astype(jnp.float32)) / GLA_GATE_TEMP
    if s0 is None:
        s_init = jnp.zeros((B, GLA_HEADS, GLA_DK, GLA_DV), jnp.float32)
    else:
        s_init = s0.astype(jnp.float32)
    o, s_new = gla_chunked(gq.reshape(B, T, GLA_HEADS, GLA_DK), gk.reshape(B, T, GLA_HEADS, GLA_DK),
                           gv.reshape(B, T, GLA_HEADS, GLA_DV),
                           log_a.reshape(B, T, GLA_HEADS, GLA_DK), s_init)
    o = rms_norm(o, g_gla) * jax.nn.silu(gr.astype(jnp.float32)).reshape(B, T, GLA_HEADS, GLA_DV)
    gla_out = o.reshape(B, T, GLA_VW).astype(x.dtype)

    mix = jnp.einsum('bte,ed->btd', jnp.concatenate([attn.astype(x.dtype), gla_out], axis=-1), w_out)
    x = x + rms_norm(mix, g_mix_post)

    h = rms_norm(x, g_ffn_pre)
    u = jax.nn.relu(jnp.einsum('btd,df->btf', h, w_up))
    f = jnp.einsum('btf,fd->btd', u * u, w_down)
    x = x + rms_norm(f, g_ffn_post)
    return x, new_k, new_v, s_new.astype(x.dtype)


def setup_inputs(seed: int = 0) -> dict:
    key = jax.random.key(seed)
    ks = jax.random.split(key, 20)
    nrm = jax.random.normal
    f32 = jnp.float32
    return {
        "x_prompt": nrm(ks[0], (BATCH, SEQ, D_MODEL), f32),
        "x_sample": nrm(ks[1], (DEC_BATCH, DEC_SEQ, D_MODEL), f32),
        "cache_k": nrm(ks[2], (DEPTH, DEC_BATCH, WINDOW, ATT_KV_HEADS, HEAD_DIM), f32),
        "cache_v": nrm(ks[3], (DEPTH, DEC_BATCH, WINDOW, ATT_KV_HEADS, HEAD_DIM), f32),
        "state_gla": nrm(ks[4], (DEPTH, DEC_BATCH, GLA_HEADS, GLA_DK, GLA_DV), f32),
        "w_in": nrm(ks[5], (DEPTH, D_MODEL, IN_WIDTH), f32) * D_MODEL ** -0.5,
        "w_gk2": nrm(ks[6], (DEPTH, GLA_RANK, GLA_KW), f32) * GLA_RANK ** -0.5,
        "b_gk": 0.02 * nrm(ks[7], (DEPTH, GLA_KW), f32),
        "g_gla": 1.0 + 0.05 * nrm(ks[8], (DEPTH, GLA_DV), f32),
        "sinks": 0.5 * nrm(ks[9], (DEPTH, ATT_HEADS), f32),
        "w_out": nrm(ks[10], (DEPTH, MIX_WIDTH, D_MODEL), f32) * MIX_WIDTH ** -0.5,
        "g_mix_pre": 1.0 + 0.05 * nrm(ks[11], (DEPTH, D_MODEL), f32),
        "g_mix_post": 1.0 + 0.05 * nrm(ks[12], (DEPTH, D_MODEL), f32),
        "g_ffn_pre": 1.0 + 0.05 * nrm(ks[13], (DEPTH, D_MODEL), f32),
        "g_ffn_post": 1.0 + 0.05 * nrm(ks[14], (DEPTH, D_MODEL), f32),
        "w_up": nrm(ks[15], (DEPTH, D_MODEL, D_FF), f32) * D_MODEL ** -0.5,
        "w_down": nrm(ks[16], (DEPTH, D_FF, D_MODEL), f32) * D_FF ** -0.5,
    }


def reference(x_prompt, x_sample, cache_k, cache_v, state_gla, w_in, w_gk2, b_gk, g_gla, sinks,
              w_out, g_mix_pre, g_mix_post, g_ffn_pre, g_ffn_post, w_up, w_down):
    pos_p = jnp.arange(x_prompt.shape[1], dtype=jnp.int32)
    pos_s = PAST_LEN + jnp.arange(x_sample.shape[1], dtype=jnp.int32)
    yp, ys = x_prompt, x_sample
    kp, vp, sp, ksm, vsm, ssm = [], [], [], [], [], []
    for l in range(DEPTH):
        lw = (w_in[l], w_gk2[l], b_gk[l], g_gla[l], sinks[l], w_out[l],
              g_mix_pre[l], g_mix_post[l], g_ffn_pre[l], g_ffn_post[l], w_up[l], w_down[l])
        yp, k1, v1, s1 = hybrid_layer(yp, pos_p, None, None, None, *lw)
        ys, k2, v2, s2 = hybrid_layer(ys, pos_s, cache_k[l], cache_v[l], state_gla[l], *lw)
        kp.append(k1); vp.append(v1); sp.append(s1)
        ksm.append(k2); vsm.append(v2); ssm.append(s2)
    return (yp, ys, jnp.stack(kp), jnp.stack(vp), jnp.stack(sp),
            jnp.stack(ksm), jnp.stack(vsm), jnp.stack(ssm))
```

```python
import functools
import math

import numpy as np
import jax
import jax.numpy as jnp
from jax import lax
from jax.experimental import pallas as pl
from jax.experimental.pallas import tpu as pltpu

F32 = jnp.float32
BF16 = jnp.bfloat16

HEAD_DIM = 64
ATT_HEADS = 8
ATT_KV_HEADS = 2
WINDOW = 128
PAST_LEN = 16384
ROPE_THETA = 10000.0
GLA_HEADS = 4
GLA_DK = 64
GLA_DV = 128
GLA_RANK = 16
GLA_GATE_TEMP = 16.0
EPS = 1e-6
ATT_W = ATT_HEADS * HEAD_DIM
KV_W = ATT_KV_HEADS * HEAD_DIM
GK_W = GLA_HEADS * GLA_DK
GV_W = GLA_HEADS * GLA_DV
COL_Q = 0
COL_K = COL_Q + ATT_W
COL_V = COL_K + KV_W
COL_GQ = COL_V + KV_W
COL_GK = COL_GQ + GK_W
COL_GV = COL_GK + GK_W
COL_GR = COL_GV + GV_W
MAIN_W = COL_GR + GV_W
QK_SCALE = HEAD_DIM ** -0.5
GLA_SCALE = GLA_DK ** -0.5

CHUNK = 128
LANES = 128
PROMPT_TILE = 256
SAMPLE_SEQS = 16
FF_CHUNK = 1024
PROMPT_LEVELS = (1, 2, 4, 8, 16, 32, 64)
VMEM_LIMIT_BYTES = 56 * 1024 * 1024


def _dot(a, b):
    return jnp.dot(a, b, preferred_element_type=F32)


def _dot_nt(a, b):
    return lax.dot_general(a, b, (((1,), (1,)), ((), ())), preferred_element_type=F32)


def _dot_tn(a, b):
    return lax.dot_general(a, b, (((0,), (0,)), ((), ())), preferred_element_type=F32)


def _rms(x, g):
    ms = jnp.mean(x * x, axis=-1, keepdims=True)
    return x * lax.rsqrt(ms + EPS) * g


def _rope(x, cos, sin_a, sin_b):
    return x * cos + pltpu.roll(x, 96, 1) * sin_a + pltpu.roll(x, 32, 1) * sin_b


def _split3(x):
    hi = x.astype(BF16)
    r1 = x - hi.astype(F32)
    mid = r1.astype(BF16)
    lo = (r1 - mid.astype(F32)).astype(BF16)
    return hi, mid, lo


def _pre_stage(x, cos, sin_a, sin_b, g_pre, w_main, w_glr, w_gk2, b_gk):
    h = _rms(x, g_pre).astype(BF16)
    proj = _dot(h, w_main[...])
    glr = _dot(h, w_glr[...])
    z = _dot(glr.astype(BF16), w_gk2[...]) + b_gk
    log_a = (jnp.minimum(z, 0.0) - jnp.log1p(jnp.exp(-jnp.abs(z)))) * (1.0 / GLA_GATE_TEMP)
    q = jnp.concatenate(
        [_rope(proj[:, COL_Q + LANES * j:COL_Q + LANES * (j + 1)], cos, sin_a, sin_b) * QK_SCALE
         for j in range(ATT_W // LANES)], axis=1)
    k = _rope(proj[:, COL_K:COL_V], cos, sin_a, sin_b)
    v = proj[:, COL_V:COL_GQ]
    gq = proj[:, COL_GQ:COL_GK] * GLA_SCALE
    gk = proj[:, COL_GK:COL_GV]
    gv = proj[:, COL_GV:COL_GR]
    gr = proj[:, COL_GR:MAIN_W]
    return q, k, v, gq, gk, gv, gr, log_a


def _post_stage(x, mix_b, g_mix_post, g_ffn_pre, g_ffn_post, w_out, w_up, w_down):
    mix = _dot(mix_b, w_out[...])
    x1 = x + _rms(mix, g_mix_post)
    h2 = _rms(x1, g_ffn_pre).astype(BF16)
    f = None
    for j in range(w_up.shape[1] // FF_CHUNK):
        u = jnp.maximum(_dot(h2, w_up[:, FF_CHUNK * j:FF_CHUNK * (j + 1)]), 0.0)
        part = _dot((u * u).astype(BF16), w_down[FF_CHUNK * j:FF_CHUNK * (j + 1), :])
        f = part if f is None else f + part
    return x1 + _rms(f, g_ffn_post)


def _lo_hi_forms(a):
    lo = lax.broadcasted_iota(jnp.int32, a.shape, 1) < HEAD_DIM
    r = pltpu.roll(a, HEAD_DIM, 1)
    zero = jnp.zeros_like(a)
    return ((jnp.where(lo, a, zero), jnp.where(lo, zero, r)),
            (jnp.where(lo, r, zero), jnp.where(lo, zero, a)))


def _gla_exponents(log_a, gmat):
    hi, mid, lo = _split3(log_a)
    return _dot(gmat, hi) + _dot(gmat, mid) + _dot(gmat, lo)


def _gla_intra(gq, gk, expo, levels, masks_ref):
    row = lax.broadcasted_iota(jnp.int32, gq.shape, 0)
    lane = lax.broadcasted_iota(jnp.int32, (gq.shape[0], LANES), 1)
    terms = [(gq.astype(BF16), gk.astype(BF16), 0)]
    for li, m in enumerate(levels):
        ex = jnp.exp(expo[CHUNK * (2 + li):CHUNK * (3 + li)])
        z = (jnp.where((row & m) != 0, gq, gk) * ex).astype(BF16)
        terms.append((z, z, 1 + li))
    scores = []
    for h in range(GLA_HEADS):
        p, e = divmod(h, 2)
        in_head = (lane >= GLA_DK * e) & (lane < GLA_DK * (e + 1))
        acc = None
        for zq, zk, mi in terms:
            zq_p = zq[:, LANES * p:LANES * (p + 1)]
            zk_p = zk[:, LANES * p:LANES * (p + 1)]
            zk_h = jnp.where(in_head, zk_p, jnp.zeros_like(zk_p))
            t = _dot_nt(zq_p, zk_h) * masks_ref[mi]
            acc = t if acc is None else acc + t
        scores.append(acc)
    return scores


def _gla_out_gate(o_h, gr_h, g_gla):
    on = _rms(o_h, g_gla)
    return on * (gr_h * (1.0 / (1.0 + jnp.exp(-gr_h))))


def _prompt_kernel(sinks_ref, x_ref, cos_ref, sa_ref, sb_ref, bias_ref, gmat_ref, masks_ref,
                   g_pre_ref, g_post_ref, g_fpre_ref, g_fpost_ref, g_gla_ref, b_gk_ref,
                   w_main_ref, w_glr_ref, w_gk2_ref, w_out_ref, w_up_ref, w_down_ref,
                   y_ref, kwin_ref, vwin_ref, sout_ref,
                   q_scr, kbd_scr, vbd_scr, gq_scr, gk_scr, gv_scr, gr_scr, la_scr, s_scr, mix_scr):
    t_step = pl.program_id(1)
    n_steps = pl.num_programs(1)
    tile = x_ref.shape[1]
    n_chunks = tile // CHUNK

    @pl.when(t_step == 0)
    def _():
        s_scr[...] = jnp.zeros_like(s_scr)
        kbd_scr[:, 0:CHUNK, :] = jnp.zeros((4, CHUNK, LANES), BF16)
        vbd_scr[:, 0:CHUNK, :] = jnp.zeros((4, CHUNK, LANES), BF16)

    x = x_ref[0]
    q, k, v, gq, gk, gv, gr, log_a = _pre_stage(
        x, cos_ref[...], sa_ref[...], sb_ref[...], g_pre_ref[...],
        w_main_ref, w_glr_ref, w_gk2_ref, b_gk_ref[...])
    q_scr[...] = q.astype(BF16)
    k_forms = _lo_hi_forms(k)
    v_forms = _lo_hi_forms(v)
    for g in range(ATT_KV_HEADS):
        for e in range(2):
            kbd_scr[2 * g + e, CHUNK:CHUNK + tile, :] = k_forms[g][e].astype(BF16)
            vbd_scr[2 * g + e, CHUNK:CHUNK + tile, :] = v_forms[g][e].astype(BF16)
    gq_scr[...] = gq
    gk_scr[...] = gk
    gv_scr[...] = gv
    gr_scr[...] = gr
    la_scr[...] = log_a

    @pl.when(t_step == n_steps - 1)
    def _():
        kwin_ref[0] = k[tile - WINDOW:tile, :]
        vwin_ref[0] = v[tile - WINDOW:tile, :]

    lane = lax.broadcasted_iota(jnp.int32, (CHUNK, LANES), 1)
    lo_lane = lane < HEAD_DIM
    g_gla = g_gla_ref[...]

    def chunk_body(c, carry):
        r0 = pl.multiple_of(c * CHUNK, CHUNK)
        rows = pl.ds(r0, CHUNK)
        rows2 = pl.ds(r0, 2 * CHUNK)
        first = jnp.where((t_step == 0) & (c == 0), 1, 0)
        bias = bias_ref[first]

        for g in range(ATT_KV_HEADS):
            lhs = jnp.concatenate([q_scr[rows, 2 * LANES * g:2 * LANES * g + LANES],
                                   q_scr[rows, 2 * LANES * g + LANES:2 * LANES * (g + 1)]], axis=0)
            kcat = jnp.concatenate([kbd_scr[2 * g, rows2, :], kbd_scr[2 * g + 1, rows2, :]], axis=0)
            s = _dot_nt(lhs, kcat) + bias
            p_blocks = []
            inv = []
            for blk in range(2):
                p_row = []
                inv_row = []
                for e in range(2):
                    sub = s[CHUNK * blk:CHUNK * (blk + 1), 2 * CHUNK * e:2 * CHUNK * (e + 1)]
                    sink = sinks_ref[4 * g + 2 * blk + e]
                    m = jnp.maximum(jnp.max(sub, axis=-1, keepdims=True), sink)
                    p = jnp.exp(sub - m)
                    den = jnp.sum(p, axis=-1, keepdims=True) + jnp.exp(sink - m)
                    p_row.append(p.astype(BF16))
                    inv_row.append(1.0 / den)
                p_blocks.append(jnp.concatenate(p_row, axis=1))
                inv.append(inv_row)
            p_all = jnp.concatenate(p_blocks, axis=0)
            vcat = jnp.concatenate([vbd_scr[2 * g, rows2, :], vbd_scr[2 * g + 1, rows2, :]], axis=0)
            o = _dot(p_all, vcat)
            for blk in range(2):
                scale = jnp.where(lo_lane, inv[blk][0], inv[blk][1])
                c0 = 2 * LANES * g + LANES * blk
                mix_scr[rows, c0:c0 + LANES] = (o[CHUNK * blk:CHUNK * (blk + 1)] * scale).astype(BF16)

        gq_c = gq_scr[rows, :]
        gk_c = gk_scr[rows, :]
        expo = _gla_exponents(la_scr[rows, :], gmat_ref[...])
        b = expo[0:CHUNK]
        q_dec = (gq_c * jnp.exp(b)).astype(BF16)
        k_dec_t = jnp.transpose(gk_c * jnp.exp(expo[CHUNK:2 * CHUNK])).astype(BF16)
        decay_t = jnp.exp(jnp.transpose(jnp.broadcast_to(b[CHUNK - 1:CHUNK, :], b.shape)))
        scores = _gla_intra(gq_c, gk_c, expo, PROMPT_LEVELS, masks_ref)
        s0 = s_scr[...]
        s0_b = s0.astype(BF16)
        for h in range(GLA_HEADS):
            p_, e_ = divmod(h, 2)
            v_h = gv_scr[rows, GLA_DV * h:GLA_DV * (h + 1)].astype(BF16)
            in_head = (lane >= GLA_DK * e_) & (lane < GLA_DK * (e_ + 1))
            qd_p = q_dec[:, LANES * p_:LANES * (p_ + 1)]
            qd_h = jnp.where(in_head, qd_p, jnp.zeros_like(qd_p))
            o_h = (_dot(scores[h].astype(BF16), v_h)
                   + _dot(qd_h, s0_b[LANES * p_:LANES * (p_ + 1), :]))
            out_h = _gla_out_gate(o_h, gr_scr[rows, GLA_DV * h:GLA_DV * (h + 1)], g_gla)
            mix_scr[rows, ATT_W + GLA_DV * h:ATT_W + GLA_DV * (h + 1)] = out_h.astype(BF16)
            hs = slice(GLA_DK * h, GLA_DK * (h + 1))
            s_scr[hs, :] = decay_t[hs, :] * s0[hs, :] + _dot(k_dec_t[hs, :], v_h)
        return carry

    lax.fori_loop(0, n_chunks, chunk_body, 0)

    kbd_scr[:, 0:CHUNK, :] = kbd_scr[:, tile:tile + CHUNK, :]
    vbd_scr[:, 0:CHUNK, :] = vbd_scr[:, tile:tile + CHUNK, :]

    y_ref[0] = _post_stage(x, mix_scr[...], g_post_ref[...], g_fpre_ref[...], g_fpost_ref[...],
                           w_out_ref, w_up_ref, w_down_ref)

    @pl.when(t_step == n_steps - 1)
    def _():
        sout_ref[0] = s_scr[...]


def _sample_kernel(sinks_ref, x_ref, cos_ref, sa_ref, sb_ref, gmat_ref, masks_ref,
                   ck_ref, cv_ref, st_ref,
                   g_pre_ref, g_post_ref, g_fpre_ref, g_fpost_ref, g_gla_ref, b_gk_ref,
                   w_main_ref, w_glr_ref, w_gk2_ref, w_out_ref, w_up_ref, w_down_ref,
                   y_ref, kwin_ref, vwin_ref, sout_ref,
                   q_scr, k_scr, v_scr, gv_scr, qd_scr, kd_scr, la_scr, og_scr, mix_scr,
                   *, dec_seq, levels):
    n_seqs = ck_ref.shape[0]
    rows_total = x_ref.shape[0]
    x = x_ref[...]
    q, k, v, gq, gk, gv, gr, log_a = _pre_stage(
        x, cos_ref[...], sa_ref[...], sb_ref[...], g_pre_ref[...],
        w_main_ref, w_glr_ref, w_gk2_ref, b_gk_ref[...])
    q_scr[...] = q
    k_scr[...] = k
    v_scr[...] = v
    gv_scr[...] = gv
    la_scr[...] = log_a

    kwin_ref[:, 0:WINDOW - dec_seq, :] = ck_ref[:, dec_seq:WINDOW, :]
    vwin_ref[:, 0:WINDOW - dec_seq, :] = cv_ref[:, dec_seq:WINDOW, :]
    kwin_ref[:, WINDOW - dec_seq:WINDOW, :] = k.reshape(n_seqs, dec_seq, KV_W)
    vwin_ref[:, WINDOW - dec_seq:WINDOW, :] = v.reshape(n_seqs, dec_seq, KV_W)

    expo = _gla_exponents(log_a, gmat_ref[...])
    qd_scr[...] = gq * jnp.exp(expo[0:CHUNK])
    kd_scr[...] = gk * jnp.exp(expo[CHUNK:2 * CHUNK])
    scores = _gla_intra(gq, gk, expo, levels, masks_ref)
    for h in range(GLA_HEADS):
        og_scr[:, GLA_DV * h:GLA_DV * (h + 1)] = _dot(
            scores[h].astype(BF16), gv[:, GLA_DV * h:GLA_DV * (h + 1)].astype(BF16))

    n_keys = WINDOW + dec_seq
    key_j = lax.broadcasted_iota(jnp.int32, (2 * dec_seq, n_keys), 1)
    qrow = lax.broadcasted_iota(jnp.int32, (2 * dec_seq, n_keys), 0)
    q_t = jnp.where(qrow >= dec_seq, qrow - dec_seq, qrow)
    visible = (key_j > q_t) & (key_j <= q_t + WINDOW)
    upper_rows = lax.broadcasted_iota(jnp.int32, (2 * dec_seq, 1), 0) < dec_seq
    lane_k = lax.broadcasted_iota(jnp.int32, (dec_seq, GK_W), 1)
    ones_cols = jnp.ones((dec_seq, LANES), F32)

    def seq_body(s, carry):
        r = pl.multiple_of(s * dec_seq, dec_seq)
        rows = pl.ds(r, dec_seq)
        k_all = jnp.concatenate([ck_ref[s], k_scr[rows, :]], axis=0)
        v_all = jnp.concatenate([cv_ref[s], v_scr[rows, :]], axis=0)
        k_forms = _lo_hi_forms(k_all)
        v_forms = _lo_hi_forms(v_all)
        for g in range(ATT_KV_HEADS):
            lhs = jnp.concatenate([q_scr[rows, 2 * LANES * g:2 * LANES * g + LANES],
                                   q_scr[rows, 2 * LANES * g + LANES:2 * LANES * (g + 1)]], axis=0)
            o = None
            for e in range(2):
                sc = jnp.where(visible, _dot_nt(lhs, k_forms[g][e]), -jnp.inf)
                sink = jnp.where(upper_rows, sinks_ref[4 * g + e], sinks_ref[4 * g + 2 + e])
                m = jnp.maximum(jnp.max(sc, axis=-1, keepdims=True), sink)
                p = jnp.exp(sc - m)
                den = jnp.sum(p, axis=-1, keepdims=True) + jnp.exp(sink - m)
                part = _dot(p, v_forms[g][e]) * (1.0 / den)
                o = part if o is None else o + part
            c0 = 2 * LANES * g
            mix_scr[rows, c0:c0 + LANES] = o[0:dec_seq]
            mix_scr[rows, c0 + LANES:c0 + 2 * LANES] = o[dec_seq:2 * dec_seq]

        s0 = st_ref[s]
        qd = qd_scr[rows, :]
        lhs = jnp.concatenate(
            [jnp.where((lane_k >= GLA_DK * h) & (lane_k < GLA_DK * (h + 1)), qd, 0.0)
             for h in range(GLA_HEADS)], axis=0)
        o_inter = _dot(lhs, s0)
        for h in range(GLA_HEADS):
            cs = slice(GLA_DV * h, GLA_DV * (h + 1))
            og_scr[rows, cs] = og_scr[rows, cs] + o_inter[dec_seq * h:dec_seq * (h + 1)]
        la = la_scr[rows, :]
        hi = la.astype(BF16).astype(F32)
        r1 = la - hi
        mid = r1.astype(BF16).astype(F32)
        lo = r1 - mid
        b_last_t = _dot_tn(hi, ones_cols) + _dot_tn(mid, ones_cols) + _dot_tn(lo, ones_cols)
        upd = _dot_tn(kd_scr[rows, :], gv_scr[rows, :])
        upd_d = jnp.concatenate(
            [upd[GLA_DK * h:GLA_DK * (h + 1), GLA_DV * h:GLA_DV * (h + 1)] for h in range(GLA_HEADS)], axis=0)
        sout_ref[s] = jnp.exp(b_last_t) * s0 + upd_d
        return carry

    lax.fori_loop(0, n_seqs, seq_body, 0)

    g_gla = g_gla_ref[...]
    for h in range(GLA_HEADS):
        cs = slice(GLA_DV * h, GLA_DV * (h + 1))
        mix_scr[:, ATT_W + GLA_DV * h:ATT_W + GLA_DV * (h + 1)] = _gla_out_gate(og_scr[:, cs], gr[:, cs], g_gla)

    y_ref[...] = _post_stage(x, mix_scr[...].astype(BF16), g_post_ref[...], g_fpre_ref[...],
                             g_fpost_ref[...], w_out_ref, w_up_ref, w_down_ref)


def _gla_constants(seg, levels):
    t = np.arange(CHUNK)[:, None]
    u = np.arange(CHUNK)[None, :]
    same = (t // seg) == (u // seg)
    blocks = [same & (u <= t), same & (u > t)]
    masks = [np.eye(CHUNK, dtype=bool)]
    for m in levels:
        ref = (t // (2 * m)) * (2 * m) + m - 1
        second = (t & m) != 0
        blocks.append(np.where(second, (u > ref) & (u <= t), (u > t) & (u <= ref)))
        masks.append(((t // (2 * m)) == (u // (2 * m))) & second & ((u & m) == 0))
    gmat = jnp.asarray(np.concatenate(blocks, axis=0).astype(np.float32), dtype=BF16)
    return gmat, jnp.asarray(np.stack(masks).astype(np.float32))


def _attention_bias():
    t = (np.arange(2 * CHUNK) % CHUNK)[:, None]
    j = (np.arange(4 * CHUNK) % (2 * CHUNK))[None, :]
    band = (j >= t + 1) & (j <= t + WINDOW)
    first = band & (j >= CHUNK)
    neg = np.float32(-np.inf)
    return jnp.asarray(np.stack([np.where(band, np.float32(0), neg), np.where(first, np.float32(0), neg)]))


def _rope_tables(pos):
    half = HEAD_DIM // 2
    inv = ROPE_THETA ** (-jnp.arange(half, dtype=F32) / half)
    ang = pos.astype(F32)[:, None] * inv[None, :]
    cos = jnp.cos(ang)
    sin = jnp.sin(ang)
    zero = jnp.zeros_like(sin)
    reps = LANES // HEAD_DIM
    return (jnp.tile(cos, (1, 2 * reps)),
            jnp.tile(jnp.concatenate([-sin, zero], axis=1), (1, reps)),
            jnp.tile(jnp.concatenate([zero, sin], axis=1), (1, reps)))


def _const_spec(shape):
    zeros = (0,) * len(shape)
    return pl.BlockSpec(shape, lambda *_: zeros, pipeline_mode=pl.Buffered(1))


def kernel(x_prompt, x_sample, cache_k, cache_v, state_gla, w_in, w_gk2, b_gk, g_gla, sinks, w_out, g_mix_pre, g_mix_post, g_ffn_pre, g_ffn_post, w_up, w_down):
    depth = w_in.shape[0]
    assert depth == 1, "single trunk layer"
    batch, seq, d_model = x_prompt.shape
    dec_batch, dec_seq, _ = x_sample.shape
    d_ff = w_up.shape[2]
    assert seq % PROMPT_TILE == 0 and dec_batch % SAMPLE_SEQS == 0
    assert SAMPLE_SEQS * dec_seq == CHUNK and CHUNK % dec_seq == 0 and d_ff % FF_CHUNK == 0
    assert w_in.shape[2] == MAIN_W + GLA_RANK

    w_main = w_in[0, :, :MAIN_W].astype(BF16)
    w_glr = w_in[0, :, MAIN_W:].astype(BF16)
    w_gk2_b = w_gk2[0].astype(BF16)
    w_out_b = w_out[0].astype(BF16)
    w_up_b = w_up[0].astype(BF16)
    w_down_b = w_down[0].astype(BF16)
    g_pre = g_mix_pre[0][None, :]
    g_post = g_mix_post[0][None, :]
    g_fpre = g_ffn_pre[0][None, :]
    g_fpost = g_ffn_post[0][None, :]
    g_gla2 = g_gla[0][None, :]
    b_gk2 = b_gk[0][None, :]
    sinks1 = sinks[0]

    weights = (g_pre, g_post, g_fpre, g_fpost, g_gla2, b_gk2, w_main, w_glr, w_gk2_b, w_out_b, w_up_b, w_down_b)
    weight_specs = [_const_spec(w.shape) for w in weights]
    smem_spec = pl.BlockSpec(memory_space=pltpu.SMEM)

    cos_p, sa_p, sb_p = _rope_tables(jnp.arange(seq, dtype=jnp.int32))
    gmat_p, masks_p = _gla_constants(CHUNK, PROMPT_LEVELS)
    bias = _attention_bias()
    n_t = seq // PROMPT_TILE
    tile = PROMPT_TILE
    tab_spec = pl.BlockSpec((tile, LANES), lambda b, t: (t, 0))
    yp, kp, vp, sp = pl.pallas_call(
        _prompt_kernel,
        grid=(batch, n_t),
        in_specs=[smem_spec,
                  pl.BlockSpec((1, tile, d_model), lambda b, t: (b, t, 0)),
                  tab_spec, tab_spec, tab_spec,
                  _const_spec(bias.shape), _const_spec(gmat_p.shape), _const_spec(masks_p.shape)] + weight_specs,
        out_specs=[pl.BlockSpec((1, tile, d_model), lambda b, t: (b, t, 0)),
                   pl.BlockSpec((1, WINDOW, KV_W), lambda b, t: (b, 0, 0)),
                   pl.BlockSpec((1, WINDOW, KV_W), lambda b, t: (b, 0, 0)),
                   pl.BlockSpec((1, GK_W, GLA_DV), lambda b, t: (b, 0, 0))],
        out_shape=[jax.ShapeDtypeStruct((batch, seq, d_model), F32),
                   jax.ShapeDtypeStruct((batch, WINDOW, KV_W), F32),
                   jax.ShapeDtypeStruct((batch, WINDOW, KV_W), F32),
                   jax.ShapeDtypeStruct((batch, GK_W, GLA_DV), F32)],
        scratch_shapes=[pltpu.VMEM((tile, ATT_W), BF16),
                        pltpu.VMEM((4, CHUNK + tile, LANES), BF16),
                        pltpu.VMEM((4, CHUNK + tile, LANES), BF16),
                        pltpu.VMEM((tile, GK_W), F32),
                        pltpu.VMEM((tile, GK_W), F32),
                        pltpu.VMEM((tile, GV_W), F32),
                        pltpu.VMEM((tile, GV_W), F32),
                        pltpu.VMEM((tile, GK_W), F32),
                        pltpu.VMEM((GK_W, GLA_DV), F32),
                        pltpu.VMEM((tile, d_model), BF16)],
        compiler_params=pltpu.CompilerParams(
            dimension_semantics=("arbitrary", "arbitrary"), vmem_limit_bytes=VMEM_LIMIT_BYTES),
        name="prompt_layer",
    )(sinks1, x_prompt, cos_p, sa_p, sb_p, bias, gmat_p, masks_p, *weights)

    levels_s = tuple(m for m in PROMPT_LEVELS if m < dec_seq)
    pos_s = PAST_LEN + jnp.arange(dec_seq, dtype=jnp.int32)
    cos_s, sa_s, sb_s = (jnp.tile(a, (SAMPLE_SEQS, 1)) for a in _rope_tables(pos_s))
    gmat_s, masks_s = _gla_constants(dec_seq, levels_s)
    rows = SAMPLE_SEQS * dec_seq
    n_g = dec_batch // SAMPLE_SEQS
    xs = x_sample.reshape(dec_batch * dec_seq, d_model)
    ck = cache_k[0].reshape(dec_batch, WINDOW, KV_W)
    cv = cache_v[0].reshape(dec_batch, WINDOW, KV_W)
    st = state_gla[0].reshape(dec_batch, GK_W, GLA_DV)
    seq_spec = pl.BlockSpec((SAMPLE_SEQS, WINDOW, KV_W), lambda i: (i, 0, 0))
    st_spec = pl.BlockSpec((SAMPLE_SEQS, GK_W, GLA_DV), lambda i: (i, 0, 0))
    row_spec = pl.BlockSpec((rows, d_model), lambda i: (i, 0))
    ys, ks, vs, ss = pl.pallas_call(
        functools.partial(_sample_kernel, dec_seq=dec_seq, levels=levels_s),
        grid=(n_g,),
        in_specs=[smem_spec, row_spec,
                  _const_spec(cos_s.shape), _const_spec(sa_s.shape), _const_spec(sb_s.shape),
                  _const_spec(gmat_s.shape), _const_spec(masks_s.shape),
                  seq_spec, seq_spec, st_spec] + weight_specs,
        out_specs=[row_spec, seq_spec, seq_spec, st_spec],
        out_shape=[jax.ShapeDtypeStruct((dec_batch * dec_seq, d_model), F32),
                   jax.ShapeDtypeStruct((dec_batch, WINDOW, KV_W), F32),
                   jax.ShapeDtypeStruct((dec_batch, WINDOW, KV_W), F32),
                   jax.ShapeDtypeStruct((dec_batch, GK_W, GLA_DV), F32)],
        scratch_shapes=[pltpu.VMEM((rows, ATT_W), F32),
                        pltpu.VMEM((rows, KV_W), F32),
                        pltpu.VMEM((rows, KV_W), F32),
                        pltpu.VMEM((rows, GV_W), F32),
                        pltpu.VMEM((rows, GK_W), F32),
                        pltpu.VMEM((rows, GK_W), F32),
                        pltpu.VMEM((rows, GK_W), F32),
                        pltpu.VMEM((rows, GV_W), F32),
                        pltpu.VMEM((rows, d_model), F32)],
        compiler_params=pltpu.CompilerParams(
            dimension_semantics=("arbitrary",), vmem_limit_bytes=VMEM_LIMIT_BYTES),
        name="sample_layer",
    )(sinks1, xs, cos_s, sa_s, sb_s, gmat_s, masks_s, ck, cv, st, *weights)

    head_shape = (ATT_KV_HEADS, HEAD_DIM)
    state_shape = (GLA_HEADS, GLA_DK, GLA_DV)
    return (yp,
            ys.reshape(dec_batch, dec_seq, d_model),
            kp.reshape((1, batch, WINDOW) + head_shape),
            vp.reshape((1, batch, WINDOW) + head_shape),
            sp.reshape((1, batch) + state_shape),
            ks.reshape((1, dec_batch, WINDOW) + head_shape),
            vs.reshape((1, dec_batch, WINDOW) + head_shape),
            ss.reshape((1, dec_batch) + state_shape))
```

```python
import functools
import math

import numpy as np
import jax
import jax.numpy as jnp
from jax import lax
from jax.experimental import pallas as pl
from jax.experimental.pallas import tpu as pltpu

F32 = jnp.float32
BF16 = jnp.bfloat16

HEAD_DIM = 64
ATT_HEADS = 8
ATT_KV_HEADS = 2
WINDOW = 128
PAST_LEN = 16384
ROPE_THETA = 10000.0
GLA_HEADS = 4
GLA_DK = 64
GLA_DV = 128
GLA_RANK = 16
GLA_GATE_TEMP = 16.0
EPS = 1e-6
ATT_W = ATT_HEADS * HEAD_DIM
KV_W = ATT_KV_HEADS * HEAD_DIM
GK_W = GLA_HEADS * GLA_DK
GV_W = GLA_HEADS * GLA_DV
COL_Q = 0
COL_K = COL_Q + ATT_W
COL_V = COL_K + KV_W
COL_GQ = COL_V + KV_W
COL_GK = COL_GQ + GK_W
COL_GV = COL_GK + GK_W
COL_GR = COL_GV + GV_W
MAIN_W = COL_GR + GV_W
QK_SCALE = HEAD_DIM ** -0.5
GLA_SCALE = GLA_DK ** -0.5

CHUNK = 128
LANES = 128
PROMPT_TILE = 256
SAMPLE_SEQS = 16
FF_CHUNK = 1024
PROMPT_LEVELS = (1, 2, 4, 8, 16, 32, 64)
BOUNDED_LOG_DECAY = 30.0
VMEM_LIMIT_BYTES = 56 * 1024 * 1024


def _dot(a, b):
    return jnp.dot(a, b, preferred_element_type=F32)


def _dot_nt(a, b):
    return lax.dot_general(a, b, (((1,), (1,)), ((), ())), preferred_element_type=F32)


def _dot_tn(a, b):
    return lax.dot_general(a, b, (((0,), (0,)), ((), ())), preferred_element_type=F32)


def _rms(x, g):
    ms = jnp.mean(x * x, axis=-1, keepdims=True)
    return x * lax.rsqrt(ms + EPS) * g


def _rope(x, cos, sin_a, sin_b):
    return x * cos + pltpu.roll(x, 96, 1) * sin_a + pltpu.roll(x, 32, 1) * sin_b


def _split3(x):
    hi = x.astype(BF16)
    r1 = x - hi.astype(F32)
    mid = r1.astype(BF16)
    lo = (r1 - mid.astype(F32)).astype(BF16)
    return hi, mid, lo


def _pre_stage(x, cos, sin_a, sin_b, g_pre, w_main, w_glr, w_gk2, b_gk):
    h = _rms(x, g_pre).astype(BF16)
    proj = _dot(h, w_main[...])
    glr = _dot(h, w_glr[...])
    z = _dot(glr.astype(BF16), w_gk2[...]) + b_gk
    log_a = (jnp.minimum(z, 0.0) - jnp.log1p(jnp.exp(-jnp.abs(z)))) * (1.0 / GLA_GATE_TEMP)
    q = jnp.concatenate(
        [_rope(proj[:, COL_Q + LANES * j:COL_Q + LANES * (j + 1)], cos, sin_a, sin_b) * QK_SCALE
         for j in range(ATT_W // LANES)], axis=1)
    k = _rope(proj[:, COL_K:COL_V], cos, sin_a, sin_b)
    v = proj[:, COL_V:COL_GQ]
    gq = proj[:, COL_GQ:COL_GK] * GLA_SCALE
    gk = proj[:, COL_GK:COL_GV]
    gv = proj[:, COL_GV:COL_GR]
    gr = proj[:, COL_GR:MAIN_W]
    return q, k, v, gq, gk, gv, gr, log_a


def _post_stage(x, mix_b, g_mix_post, g_ffn_pre, g_ffn_post, w_out, w_up, w_down):
    mix = _dot(mix_b, w_out[...])
    x1 = x + _rms(mix, g_mix_post)
    h2 = _rms(x1, g_ffn_pre).astype(BF16)
    f = None
    for j in range(w_up.shape[1] // FF_CHUNK):
        u = jnp.maximum(_dot(h2, w_up[:, FF_CHUNK * j:FF_CHUNK * (j + 1)]), 0.0)
        part = _dot((u * u).astype(BF16), w_down[FF_CHUNK * j:FF_CHUNK * (j + 1), :])
        f = part if f is None else f + part
    return x1 + _rms(f, g_ffn_post)


def _lo_hi_forms(a):
    lo = lax.broadcasted_iota(jnp.int32, a.shape, 1) < HEAD_DIM
    r = pltpu.roll(a, HEAD_DIM, 1)
    zero = jnp.zeros_like(a)
    return ((jnp.where(lo, a, zero), jnp.where(lo, zero, r)),
            (jnp.where(lo, r, zero), jnp.where(lo, zero, a)))


def _gla_exponents(log_a, gmat):
    hi, mid, lo = _split3(log_a)
    return _dot(gmat, hi) + _dot(gmat, mid) + _dot(gmat, lo)


def _head_lanes(h, rows):
    lane = lax.broadcasted_iota(jnp.int32, (rows, LANES), 1)
    e = h % 2
    return (lane >= GLA_DK * e) & (lane < GLA_DK * (e + 1))


def _gla_intra(gq, gk, log_a, gmat_lv, levels, masks_ref):
    expo = _gla_exponents(log_a, gmat_lv)
    row = lax.broadcasted_iota(jnp.int32, gq.shape, 0)
    terms = [(gq.astype(BF16), gk.astype(BF16), 0)]
    for li, m in enumerate(levels):
        ex = jnp.exp(expo[CHUNK * li:CHUNK * (li + 1)])
        z = (jnp.where((row & m) != 0, gq, gk) * ex).astype(BF16)
        terms.append((z, z, 1 + li))
    scores = []
    for h in range(GLA_HEADS):
        p = h // 2
        in_head = _head_lanes(h, gq.shape[0])
        acc = None
        for zq, zk, mi in terms:
            zq_p = zq[:, LANES * p:LANES * (p + 1)]
            zk_p = zk[:, LANES * p:LANES * (p + 1)]
            zk_h = jnp.where(in_head, zk_p, jnp.zeros_like(zk_p))
            t = _dot_nt(zq_p, zk_h) * masks_ref[mi]
            acc = t if acc is None else acc + t
        scores.append(acc)
    return jnp.concatenate(scores, axis=1)


def _gla_intra_bounded(q_dec, gk, b, causal):
    k_grow_t = jnp.transpose(gk * jnp.exp(-b)).astype(BF16)
    scores = []
    for h in range(GLA_HEADS):
        p = h // 2
        qd_p = q_dec[:, LANES * p:LANES * (p + 1)]
        qd_h = jnp.where(_head_lanes(h, q_dec.shape[0]), qd_p, jnp.zeros_like(qd_p))
        scores.append(_dot(qd_h, k_grow_t[LANES * p:LANES * (p + 1), :]) * causal)
    return jnp.concatenate(scores, axis=1)


def _gla_out_gate(o_h, gr_h, g_gla):
    on = _rms(o_h, g_gla)
    return on * (gr_h * (1.0 / (1.0 + jnp.exp(-gr_h))))


def _prompt_kernel(sinks_ref, x_ref, cos_ref, sa_ref, sb_ref, bias_ref, gmat_ref, masks_ref,
                   g_pre_ref, g_post_ref, g_fpre_ref, g_fpost_ref, g_gla_ref, b_gk_ref,
                   w_main_ref, w_glr_ref, w_gk2_ref, w_out_ref, w_up_ref, w_down_ref,
                   y_ref, kwin_ref, vwin_ref, sout_ref,
                   q_scr, kbd_scr, vbd_scr, gq_scr, gk_scr, gv_scr, gr_scr, la_scr, s_scr, mix_scr):
    t_step = pl.program_id(1)
    n_steps = pl.num_programs(1)
    tile = x_ref.shape[1]
    n_chunks = tile // CHUNK

    @pl.when(t_step == 0)
    def _():
        s_scr[...] = jnp.zeros_like(s_scr)
        kbd_scr[:, 0:CHUNK, :] = jnp.zeros((4, CHUNK, LANES), BF16)
        vbd_scr[:, 0:CHUNK, :] = jnp.zeros((4, CHUNK, LANES), BF16)

    x = x_ref[0]
    q, k, v, gq, gk, gv, gr, log_a = _pre_stage(
        x, cos_ref[...], sa_ref[...], sb_ref[...], g_pre_ref[...],
        w_main_ref, w_glr_ref, w_gk2_ref, b_gk_ref[...])
    q_scr[...] = q.astype(BF16)
    k_forms = _lo_hi_forms(k)
    v_forms = _lo_hi_forms(v)
    for g in range(ATT_KV_HEADS):
        for e in range(2):
            kbd_scr[2 * g + e, CHUNK:CHUNK + tile, :] = k_forms[g][e].astype(BF16)
            vbd_scr[2 * g + e, CHUNK:CHUNK + tile, :] = v_forms[g][e].astype(BF16)
    gq_scr[...] = gq
    gk_scr[...] = gk
    gv_scr[...] = gv
    gr_scr[...] = gr
    la_scr[...] = log_a

    @pl.when(t_step == n_steps - 1)
    def _():
        kwin_ref[0] = k[tile - WINDOW:tile, :]
        vwin_ref[0] = v[tile - WINDOW:tile, :]

    lane = lax.broadcasted_iota(jnp.int32, (CHUNK, LANES), 1)
    lo_lane = lane < HEAD_DIM
    g_gla = g_gla_ref[...]

    def chunk_body(c, carry):
        r0 = pl.multiple_of(c * CHUNK, CHUNK)
        rows = pl.ds(r0, CHUNK)
        rows2 = pl.ds(r0, 2 * CHUNK)
        first = jnp.where((t_step == 0) & (c == 0), 1, 0)
        bias = bias_ref[first]

        for g in range(ATT_KV_HEADS):
            lhs = jnp.concatenate([q_scr[rows, 2 * LANES * g:2 * LANES * g + LANES],
                                   q_scr[rows, 2 * LANES * g + LANES:2 * LANES * (g + 1)]], axis=0)
            kcat = jnp.concatenate([kbd_scr[2 * g, rows2, :], kbd_scr[2 * g + 1, rows2, :]], axis=0)
            s = _dot_nt(lhs, kcat) + bias
            p_blocks = []
            inv = []
            for blk in range(2):
                p_row = []
                inv_row = []
                for e in range(2):
                    sub = s[CHUNK * blk:CHUNK * (blk + 1), 2 * CHUNK * e:2 * CHUNK * (e + 1)]
                    sink = sinks_ref[4 * g + 2 * blk + e]
                    m = jnp.maximum(jnp.max(sub, axis=-1, keepdims=True), sink)
                    p = jnp.exp(sub - m)
                    den = jnp.sum(p, axis=-1, keepdims=True) + jnp.exp(sink - m)
                    p_row.append(p.astype(BF16))
                    inv_row.append(1.0 / den)
                p_blocks.append(jnp.concatenate(p_row, axis=1))
                inv.append(inv_row)
            p_all = jnp.concatenate(p_blocks, axis=0)
            vcat = jnp.concatenate([vbd_scr[2 * g, rows2, :], vbd_scr[2 * g + 1, rows2, :]], axis=0)
            o = _dot(p_all, vcat)
            for blk in range(2):
                scale = jnp.where(lo_lane, inv[blk][0], inv[blk][1])
                c0 = 2 * LANES * g + LANES * blk
                mix_scr[rows, c0:c0 + LANES] = (o[CHUNK * blk:CHUNK * (blk + 1)] * scale).astype(BF16)

        gq_c = gq_scr[rows, :]
        gk_c = gk_scr[rows, :]
        la_c = la_scr[rows, :]
        expo = _gla_exponents(la_c, gmat_ref[0:2 * CHUNK, :])
        b = expo[0:CHUNK]
        b_last = b[CHUNK - 1:CHUNK, :]
        q_dec_f = gq_c * jnp.exp(b)
        q_dec = q_dec_f.astype(BF16)
        k_dec_t = jnp.transpose(gk_c * jnp.exp(expo[CHUNK:2 * CHUNK])).astype(BF16)
        decay_t = jnp.exp(jnp.transpose(jnp.broadcast_to(b_last, b.shape)))
        scores = lax.cond(
            jnp.min(b_last) >= -BOUNDED_LOG_DECAY,
            lambda: _gla_intra_bounded(q_dec_f, gk_c, b, masks_ref[len(PROMPT_LEVELS) + 1]),
            lambda: _gla_intra(gq_c, gk_c, la_c, gmat_ref[2 * CHUNK:, :], PROMPT_LEVELS, masks_ref))
        s0 = s_scr[...]
        s0_b = s0.astype(BF16)
        for h in range(GLA_HEADS):
            p_ = h // 2
            v_h = gv_scr[rows, GLA_DV * h:GLA_DV * (h + 1)].astype(BF16)
            qd_p = q_dec[:, LANES * p_:LANES * (p_ + 1)]
            qd_h = jnp.where(_head_lanes(h, CHUNK), qd_p, jnp.zeros_like(qd_p))
            o_h = (_dot(scores[:, CHUNK * h:CHUNK * (h + 1)].astype(BF16), v_h)
                   + _dot(qd_h, s0_b[LANES * p_:LANES * (p_ + 1), :]))
            out_h = _gla_out_gate(o_h, gr_scr[rows, GLA_DV * h:GLA_DV * (h + 1)], g_gla)
            mix_scr[rows, ATT_W + GLA_DV * h:ATT_W + GLA_DV * (h + 1)] = out_h.astype(BF16)
            hs = slice(GLA_DK * h, GLA_DK * (h + 1))
            s_scr[hs, :] = decay_t[hs, :] * s0[hs, :] + _dot(k_dec_t[hs, :], v_h)
        return carry

    lax.fori_loop(0, n_chunks, chunk_body, 0)

    kbd_scr[:, 0:CHUNK, :] = kbd_scr[:, tile:tile + CHUNK, :]
    vbd_scr[:, 0:CHUNK, :] = vbd_scr[:, tile:tile + CHUNK, :]

    y_ref[0] = _post_stage(x, mix_scr[...], g_post_ref[...], g_fpre_ref[...], g_fpost_ref[...],
                           w_out_ref, w_up_ref, w_down_ref)

    @pl.when(t_step == n_steps - 1)
    def _():
        sout_ref[0] = s_scr[...]


def _sample_kernel(sinks_ref, x_ref, cos_ref, sa_ref, sb_ref, gmat_ref, masks_ref,
                   ck_ref, cv_ref, st_ref,
                   g_pre_ref, g_post_ref, g_fpre_ref, g_fpost_ref, g_gla_ref, b_gk_ref,
                   w_main_ref, w_glr_ref, w_gk2_ref, w_out_ref, w_up_ref, w_down_ref,
                   y_ref, kwin_ref, vwin_ref, sout_ref,
                   q_scr, k_scr, v_scr, gv_scr, qd_scr, kd_scr, la_scr, og_scr, mix_scr,
                   *, dec_seq, levels):
    n_seqs = ck_ref.shape[0]
    rows_total = x_ref.shape[0]
    x = x_ref[...]
    q, k, v, gq, gk, gv, gr, log_a = _pre_stage(
        x, cos_ref[...], sa_ref[...], sb_ref[...], g_pre_ref[...],
        w_main_ref, w_glr_ref, w_gk2_ref, b_gk_ref[...])
    q_scr[...] = q
    k_scr[...] = k
    v_scr[...] = v
    gv_scr[...] = gv
    la_scr[...] = log_a

    kwin_ref[:, 0:WINDOW - dec_seq, :] = ck_ref[:, dec_seq:WINDOW, :]
    vwin_ref[:, 0:WINDOW - dec_seq, :] = cv_ref[:, dec_seq:WINDOW, :]
    kwin_ref[:, WINDOW - dec_seq:WINDOW, :] = k.reshape(n_seqs, dec_seq, KV_W)
    vwin_ref[:, WINDOW - dec_seq:WINDOW, :] = v.reshape(n_seqs, dec_seq, KV_W)

    expo = _gla_exponents(log_a, gmat_ref[0:2 * CHUNK, :])
    qd_scr[...] = gq * jnp.exp(expo[0:CHUNK])
    kd_scr[...] = gk * jnp.exp(expo[CHUNK:2 * CHUNK])
    scores = _gla_intra(gq, gk, log_a, gmat_ref[2 * CHUNK:, :], levels, masks_ref)
    for h in range(GLA_HEADS):
        og_scr[:, GLA_DV * h:GLA_DV * (h + 1)] = _dot(
            scores[:, CHUNK * h:CHUNK * (h + 1)].astype(BF16), gv[:, GLA_DV * h:GLA_DV * (h + 1)].astype(BF16))

    n_keys = WINDOW + dec_seq
    key_j = lax.broadcasted_iota(jnp.int32, (2 * dec_seq, n_keys), 1)
    qrow = lax.broadcasted_iota(jnp.int32, (2 * dec_seq, n_keys), 0)
    q_t = jnp.where(qrow >= dec_seq, qrow - dec_seq, qrow)
    visible = (key_j > q_t) & (key_j <= q_t + WINDOW)
    upper_rows = lax.broadcasted_iota(jnp.int32, (2 * dec_seq, 1), 0) < dec_seq
    lane_k = lax.broadcasted_iota(jnp.int32, (dec_seq, GK_W), 1)
    ones_cols = jnp.ones((dec_seq, LANES), F32)

    def seq_body(s, carry):
        r = pl.multiple_of(s * dec_seq, dec_seq)
        rows = pl.ds(r, dec_seq)
        k_all = jnp.concatenate([ck_ref[s], k_scr[rows, :]], axis=0)
        v_all = jnp.concatenate([cv_ref[s], v_scr[rows, :]], axis=0)
        k_forms = _lo_hi_forms(k_all)
        v_forms = _lo_hi_forms(v_all)
        for g in range(ATT_KV_HEADS):
            lhs = jnp.concatenate([q_scr[rows, 2 * LANES * g:2 * LANES * g + LANES],
                                   q_scr[rows, 2 * LANES * g + LANES:2 * LANES * (g + 1)]], axis=0)
            o = None
            for e in range(2):
                sc = jnp.where(visible, _dot_nt(lhs, k_forms[g][e]), -jnp.inf)
                sink = jnp.where(upper_rows, sinks_ref[4 * g + e], sinks_ref[4 * g + 2 + e])
                m = jnp.maximum(jnp.max(sc, axis=-1, keepdims=True), sink)
                p = jnp.exp(sc - m)
                den = jnp.sum(p, axis=-1, keepdims=True) + jnp.exp(sink - m)
                part = _dot(p, v_forms[g][e]) * (1.0 / den)
                o = part if o is None else o + part
            c0 = 2 * LANES * g
            mix_scr[rows, c0:c0 + LANES] = o[0:dec_seq]
            mix_scr[rows, c0 + LANES:c0 + 2 * LANES] = o[dec_seq:2 * dec_seq]

        s0 = st_ref[s]
        qd = qd_scr[rows, :]
        lhs = jnp.concatenate(
            [jnp.where((lane_k >= GLA_DK * h) & (lane_k < GLA_DK * (h + 1)), qd, 0.0)
             for h in range(GLA_HEADS)], axis=0)
        o_inter = _dot(lhs, s0)
        for h in range(GLA_HEADS):
            cs = slice(GLA_DV * h, GLA_DV * (h + 1))
            og_scr[rows, cs] = og_scr[rows, cs] + o_inter[dec_seq * h:dec_seq * (h + 1)]
        la = la_scr[rows, :]
        hi = la.astype(BF16).astype(F32)
        r1 = la - hi
        mid = r1.astype(BF16).astype(F32)
        lo = r1 - mid
        b_last_t = _dot_tn(hi, ones_cols) + _dot_tn(mid, ones_cols) + _dot_tn(lo, ones_cols)
        upd = _dot_tn(kd_scr[rows, :], gv_scr[rows, :])
        upd_d = jnp.concatenate(
            [upd[GLA_DK * h:GLA_DK * (h + 1), GLA_DV * h:GLA_DV * (h + 1)] for h in range(GLA_HEADS)], axis=0)
        sout_ref[s] = jnp.exp(b_last_t) * s0 + upd_d
        return carry

    lax.fori_loop(0, n_seqs, seq_body, 0)

    g_gla = g_gla_ref[...]
    for h in range(GLA_HEADS):
        cs = slice(GLA_DV * h, GLA_DV * (h + 1))
        mix_scr[:, ATT_W + GLA_DV * h:ATT_W + GLA_DV * (h + 1)] = _gla_out_gate(og_scr[:, cs], gr[:, cs], g_gla)

    y_ref[...] = _post_stage(x, mix_scr[...].astype(BF16), g_post_ref[...], g_fpre_ref[...],
                             g_fpost_ref[...], w_out_ref, w_up_ref, w_down_ref)


def _gla_constants(seg, levels):
    t = np.arange(CHUNK)[:, None]
    u = np.arange(CHUNK)[None, :]
    same = (t // seg) == (u // seg)
    blocks = [same & (u <= t), same & (u > t)]
    masks = [np.eye(CHUNK, dtype=bool)]
    for m in levels:
        ref = (t // (2 * m)) * (2 * m) + m - 1
        second = (t & m) != 0
        blocks.append(np.where(second, (u > ref) & (u <= t), (u > t) & (u <= ref)))
        masks.append(((t // (2 * m)) == (u // (2 * m))) & second & ((u & m) == 0))
    masks.append(blocks[0])
    gmat = jnp.asarray(np.concatenate(blocks, axis=0).astype(np.float32), dtype=BF16)
    return gmat, jnp.asarray(np.stack(masks).astype(np.float32))


def _attention_bias():
    t = (np.arange(2 * CHUNK) % CHUNK)[:, None]
    j = (np.arange(4 * CHUNK) % (2 * CHUNK))[None, :]
    band = (j >= t + 1) & (j <= t + WINDOW)
    first = band & (j >= CHUNK)
    neg = np.float32(-np.inf)
    return jnp.asarray(np.stack([np.where(band, np.float32(0), neg), np.where(first, np.float32(0), neg)]))


def _rope_tables(pos):
    half = HEAD_DIM // 2
    inv = ROPE_THETA ** (-jnp.arange(half, dtype=F32) / half)
    ang = pos.astype(F32)[:, None] * inv[None, :]
    cos = jnp.cos(ang)
    sin = jnp.sin(ang)
    zero = jnp.zeros_like(sin)
    reps = LANES // HEAD_DIM
    return (jnp.tile(cos, (1, 2 * reps)),
            jnp.tile(jnp.concatenate([-sin, zero], axis=1), (1, reps)),
            jnp.tile(jnp.concatenate([zero, sin], axis=1), (1, reps)))


def _const_spec(shape):
    zeros = (0,) * len(shape)
    return pl.BlockSpec(shape, lambda *_: zeros, pipeline_mode=pl.Buffered(1))


def kernel(x_prompt, x_sample, cache_k, cache_v, state_gla, w_in, w_gk2, b_gk, g_gla, sinks, w_out, g_mix_pre, g_mix_post, g_ffn_pre, g_ffn_post, w_up, w_down):
    depth = w_in.shape[0]
    assert depth == 1, "single trunk layer"
    batch, seq, d_model = x_prompt.shape
    dec_batch, dec_seq, _ = x_sample.shape
    d_ff = w_up.shape[2]
    assert seq % PROMPT_TILE == 0 and dec_batch % SAMPLE_SEQS == 0
    assert SAMPLE_SEQS * dec_seq == CHUNK and CHUNK % dec_seq == 0 and d_ff % FF_CHUNK == 0
    assert w_in.shape[2] == MAIN_W + GLA_RANK

    w_main = w_in[0, :, :MAIN_W].astype(BF16)
    w_glr = w_in[0, :, MAIN_W:].astype(BF16)
    w_gk2_b = w_gk2[0].astype(BF16)
    w_out_b = w_out[0].astype(BF16)
    w_up_b = w_up[0].astype(BF16)
    w_down_b = w_down[0].astype(BF16)
    g_pre = g_mix_pre[0][None, :]
    g_post = g_mix_post[0][None, :]
    g_fpre = g_ffn_pre[0][None, :]
    g_fpost = g_ffn_post[0][None, :]
    g_gla2 = g_gla[0][None, :]
    b_gk2 = b_gk[0][None, :]
    sinks1 = sinks[0]

    weights = (g_pre, g_post, g_fpre, g_fpost, g_gla2, b_gk2, w_main, w_glr, w_gk2_b, w_out_b, w_up_b, w_down_b)
    weight_specs = [_const_spec(w.shape) for w in weights]
    smem_spec = pl.BlockSpec(memory_space=pltpu.SMEM)

    cos_p, sa_p, sb_p = _rope_tables(jnp.arange(seq, dtype=jnp.int32))
    gmat_p, masks_p = _gla_constants(CHUNK, PROMPT_LEVELS)
    bias = _attention_bias()
    n_t = seq // PROMPT_TILE
    tile = PROMPT_TILE
    tab_spec = pl.BlockSpec((tile, LANES), lambda b, t: (t, 0))
    yp, kp, vp, sp = pl.pallas_call(
        _prompt_kernel,
        grid=(batch, n_t),
        in_specs=[smem_spec,
                  pl.BlockSpec((1, tile, d_model), lambda b, t: (b, t, 0)),
                  tab_spec, tab_spec, tab_spec,
                  _const_spec(bias.shape), _const_spec(gmat_p.shape), _const_spec(masks_p.shape)] + weight_specs,
        out_specs=[pl.BlockSpec((1, tile, d_model), lambda b, t: (b, t, 0)),
                   pl.BlockSpec((1, WINDOW, KV_W), lambda b, t: (b, 0, 0)),
                   pl.BlockSpec((1, WINDOW, KV_W), lambda b, t: (b, 0, 0)),
                   pl.BlockSpec((1, GK_W, GLA_DV), lambda b, t: (b, 0, 0))],
        out_shape=[jax.ShapeDtypeStruct((batch, seq, d_model), F32),
                   jax.ShapeDtypeStruct((batch, WINDOW, KV_W), F32),
                   jax.ShapeDtypeStruct((batch, WINDOW, KV_W), F32),
                   jax.ShapeDtypeStruct((batch, GK_W, GLA_DV), F32)],
        scratch_shapes=[pltpu.VMEM((tile, ATT_W), BF16),
                        pltpu.VMEM((4, CHUNK + tile, LANES), BF16),
                        pltpu.VMEM((4, CHUNK + tile, LANES), BF16),
                        pltpu.VMEM((tile, GK_W), F32),
                        pltpu.VMEM((tile, GK_W), F32),
                        pltpu.VMEM((tile, GV_W), F32),
                        pltpu.VMEM((tile, GV_W), F32),
                        pltpu.VMEM((tile, GK_W), F32),
                        pltpu.VMEM((GK_W, GLA_DV), F32),
                        pltpu.VMEM((tile, d_model), BF16)],
        compiler_params=pltpu.CompilerParams(
            dimension_semantics=("arbitrary", "arbitrary"), vmem_limit_bytes=VMEM_LIMIT_BYTES),
        name="prompt_layer",
    )(sinks1, x_prompt, cos_p, sa_p, sb_p, bias, gmat_p, masks_p, *weights)

    levels_s = tuple(m for m in PROMPT_LEVELS if m < dec_seq)
    pos_s = PAST_LEN + jnp.arange(dec_seq, dtype=jnp.int32)
    cos_s, sa_s, sb_s = (jnp.tile(a, (SAMPLE_SEQS, 1)) for a in _rope_tables(pos_s))
    gmat_s, masks_s = _gla_constants(dec_seq, levels_s)
    rows = SAMPLE_SEQS * dec_seq
    n_g = dec_batch // SAMPLE_SEQS
    xs = x_sample.reshape(dec_batch * dec_seq, d_model)
    ck = cache_k[0].reshape(dec_batch, WINDOW, KV_W)
    cv = cache_v[0].reshape(dec_batch, WINDOW, KV_W)
    st = state_gla[0].reshape(dec_batch, GK_W, GLA_DV)
    seq_spec = pl.BlockSpec((SAMPLE_SEQS, WINDOW, KV_W), lambda i: (i, 0, 0))
    st_spec = pl.BlockSpec((SAMPLE_SEQS, GK_W, GLA_DV), lambda i: (i, 0, 0))
    row_spec = pl.BlockSpec((rows, d_model), lambda i: (i, 0))
    ys, ks, vs, ss = pl.pallas_call(
        functools.partial(_sample_kernel, dec_seq=dec_seq, levels=levels_s),
        grid=(n_g,),
        in_specs=[smem_spec, row_spec,
                  _const_spec(cos_s.shape), _const_spec(sa_s.shape), _const_spec(sb_s.shape),
                  _const_spec(gmat_s.shape), _const_spec(masks_s.shape),
                  seq_spec, seq_spec, st_spec] + weight_specs,
        out_specs=[row_spec, seq_spec, seq_spec, st_spec],
        out_shape=[jax.ShapeDtypeStruct((dec_batch * dec_seq, d_model), F32),
                   jax.ShapeDtypeStruct((dec_batch, WINDOW, KV_W), F32),
                   jax.ShapeDtypeStruct((dec_batch, WINDOW, KV_W), F32),
                   jax.ShapeDtypeStruct((dec_batch, GK_W, GLA_DV), F32)],
        scratch_shapes=[pltpu.VMEM((rows, ATT_W), F32),
                        pltpu.VMEM((rows, KV_W), F32),
                        pltpu.VMEM((rows, KV_W), F32),
                        pltpu.VMEM((rows, GV_W), F32),
                        pltpu.VMEM((rows, GK_W), F32),
                        pltpu.VMEM((rows, GK_W), F32),
                        pltpu.VMEM((rows, GK_W), F32),
                        pltpu.VMEM((rows, GV_W), F32),
                        pltpu.VMEM((rows, d_model), F32)],
        compiler_params=pltpu.CompilerParams(
            dimension_semantics=("arbitrary",), vmem_limit_bytes=VMEM_LIMIT_BYTES),
        name="sample_layer",
    )(sinks1, xs, cos_s, sa_s, sb_s, gmat_s, masks_s, ck, cv, st, *weights)

    head_shape = (ATT_KV_HEADS, HEAD_DIM)
    state_shape = (GLA_HEADS, GLA_DK, GLA_DV)
    return (yp,
            ys.reshape(dec_batch, dec_seq, d_model),
            kp.reshape((1, batch, WINDOW) + head_shape),
            vp.reshape((1, batch, WINDOW) + head_shape),
            sp.reshape((1, batch) + state_shape),
            ks.reshape((1, dec_batch, WINDOW) + head_shape),
            vs.reshape((1, dec_batch, WINDOW) + head_shape),
            ss.reshape((1, dec_batch) + state_shape))
```

```python
import functools
import math

import numpy as np
import jax
import jax.numpy as jnp
from jax import lax
from jax.experimental import pallas as pl
from jax.experimental.pallas import tpu as pltpu

F32 = jnp.float32
BF16 = jnp.bfloat16

HEAD_DIM = 64
ATT_HEADS = 8
ATT_KV_HEADS = 2
WINDOW = 128
PAST_LEN = 16384
ROPE_THETA = 10000.0
GLA_HEADS = 4
GLA_DK = 64
GLA_DV = 128
GLA_RANK = 16
GLA_GATE_TEMP = 16.0
EPS = 1e-6
ATT_W = ATT_HEADS * HEAD_DIM
KV_W = ATT_KV_HEADS * HEAD_DIM
GK_W = GLA_HEADS * GLA_DK
GV_W = GLA_HEADS * GLA_DV
COL_Q = 0
COL_K = COL_Q + ATT_W
COL_V = COL_K + KV_W
COL_GQ = COL_V + KV_W
COL_GK = COL_GQ + GK_W
COL_GV = COL_GK + GK_W
COL_GR = COL_GV + GV_W
MAIN_W = COL_GR + GV_W
QK_SCALE = HEAD_DIM ** -0.5
GLA_SCALE = GLA_DK ** -0.5

CHUNK = 128
LANES = 128
PROMPT_TILE = 256
SAMPLE_SEQS = 16
FF_CHUNK = 1024
PROMPT_LEVELS = (1, 2, 4, 8, 16, 32, 64)
BOUNDED_LOG_DECAY = 30.0
VMEM_LIMIT_BYTES = 56 * 1024 * 1024


def _dot(a, b):
    return jnp.dot(a, b, preferred_element_type=F32)


def _dot_nt(a, b):
    return lax.dot_general(a, b, (((1,), (1,)), ((), ())), preferred_element_type=F32)


def _dot_tn(a, b):
    return lax.dot_general(a, b, (((0,), (0,)), ((), ())), preferred_element_type=F32)


def _rms(x, g):
    ms = jnp.mean(x * x, axis=-1, keepdims=True)
    return x * lax.rsqrt(ms + EPS) * g


def _rope(x, cos, sin_a, sin_b):
    return x * cos + pltpu.roll(x, 96, 1) * sin_a + pltpu.roll(x, 32, 1) * sin_b


def _split3(x):
    hi = x.astype(BF16)
    r1 = x - hi.astype(F32)
    mid = r1.astype(BF16)
    lo = (r1 - mid.astype(F32)).astype(BF16)
    return hi, mid, lo


def _pre_stage(x, cos, sin_a, sin_b, g_pre, w_main, w_glr, w_gk2, b_gk):
    h = _rms(x, g_pre).astype(BF16)
    proj = _dot(h, w_main[...])
    glr = _dot(h, w_glr[...])
    z = _dot(glr.astype(BF16), w_gk2[...]) + b_gk
    log_a = (jnp.minimum(z, 0.0) - jnp.log1p(jnp.exp(-jnp.abs(z)))) * (1.0 / GLA_GATE_TEMP)
    q = jnp.concatenate(
        [_rope(proj[:, COL_Q + LANES * j:COL_Q + LANES * (j + 1)], cos, sin_a, sin_b) * QK_SCALE
         for j in range(ATT_W // LANES)], axis=1)
    k = _rope(proj[:, COL_K:COL_V], cos, sin_a, sin_b)
    v = proj[:, COL_V:COL_GQ]
    gq = proj[:, COL_GQ:COL_GK] * GLA_SCALE
    gk = proj[:, COL_GK:COL_GV]
    gv = proj[:, COL_GV:COL_GR]
    gr = proj[:, COL_GR:MAIN_W]
    return q, k, v, gq, gk, gv, gr, log_a


def _post_stage(x, mix_b, g_mix_post, g_ffn_pre, g_ffn_post, w_out, w_up, w_down):
    mix = _dot(mix_b, w_out[...])
    x1 = x + _rms(mix, g_mix_post)
    h2 = _rms(x1, g_ffn_pre).astype(BF16)
    f = None
    for j in range(w_up.shape[1] // FF_CHUNK):
        u = jnp.maximum(_dot(h2, w_up[:, FF_CHUNK * j:FF_CHUNK * (j + 1)]), 0.0)
        part = _dot((u * u).astype(BF16), w_down[FF_CHUNK * j:FF_CHUNK * (j + 1), :])
        f = part if f is None else f + part
    return x1 + _rms(f, g_ffn_post)


def _lo_hi_forms(a):
    lo = lax.broadcasted_iota(jnp.int32, a.shape, 1) < HEAD_DIM
    r = pltpu.roll(a, HEAD_DIM, 1)
    zero = jnp.zeros_like(a)
    return ((jnp.where(lo, a, zero), jnp.where(lo, zero, r)),
            (jnp.where(lo, r, zero), jnp.where(lo, zero, a)))


def _gla_exponents(log_a, gmat):
    hi, mid, lo = _split3(log_a)
    return _dot(gmat, hi) + _dot(gmat, mid) + _dot(gmat, lo)


def _head_lanes(h, rows):
    lane = lax.broadcasted_iota(jnp.int32, (rows, LANES), 1)
    e = h % 2
    return (lane >= GLA_DK * e) & (lane < GLA_DK * (e + 1))


def _gla_intra(gq, gk, log_a, gmat_lv, levels, masks_ref):
    expo = _gla_exponents(log_a, gmat_lv)
    row = lax.broadcasted_iota(jnp.int32, gq.shape, 0)
    terms = [(gq.astype(BF16), gk.astype(BF16), 0)]
    for li, m in enumerate(levels):
        ex = jnp.exp(expo[CHUNK * li:CHUNK * (li + 1)])
        z = (jnp.where((row & m) != 0, gq, gk) * ex).astype(BF16)
        terms.append((z, z, 1 + li))
    scores = []
    for h in range(GLA_HEADS):
        p = h // 2
        in_head = _head_lanes(h, gq.shape[0])
        acc = None
        for zq, zk, mi in terms:
            zq_p = zq[:, LANES * p:LANES * (p + 1)]
            zk_p = zk[:, LANES * p:LANES * (p + 1)]
            zk_h = jnp.where(in_head, zk_p, jnp.zeros_like(zk_p))
            t = _dot_nt(zq_p, zk_h) * masks_ref[mi]
            acc = t if acc is None else acc + t
        scores.append(acc)
    return jnp.concatenate(scores, axis=1)


def _gla_intra_bounded(q_dec, gk, b, causal):
    k_grow_t = jnp.transpose(gk * jnp.exp(-b)).astype(BF16)
    scores = []
    for h in range(GLA_HEADS):
        p = h // 2
        qd_p = q_dec[:, LANES * p:LANES * (p + 1)]
        qd_h = jnp.where(_head_lanes(h, q_dec.shape[0]), qd_p, jnp.zeros_like(qd_p))
        scores.append(_dot(qd_h, k_grow_t[LANES * p:LANES * (p + 1), :]) * causal)
    return jnp.concatenate(scores, axis=1)


def _gla_out_gate(o_h, gr_h, g_gla):
    on = _rms(o_h, g_gla)
    return on * (gr_h * (1.0 / (1.0 + jnp.exp(-gr_h))))


def _prompt_kernel(sinks_ref, x_ref, cos_ref, sa_ref, sb_ref, bias_ref, gmat_ref, masks_ref,
                   g_pre_ref, g_post_ref, g_fpre_ref, g_fpost_ref, g_gla_ref, b_gk_ref,
                   w_main_ref, w_glr_ref, w_gk2_ref, w_out_ref, w_up_ref, w_down_ref,
                   y_ref, kwin_ref, vwin_ref, sout_ref,
                   q_scr, kbd_scr, vbd_scr, gq_scr, gk_scr, gv_scr, gr_scr, la_scr, s_scr, mix_scr):
    t_step = pl.program_id(1)
    n_steps = pl.num_programs(1)
    tile = x_ref.shape[1]
    n_chunks = tile // CHUNK

    @pl.when(t_step == 0)
    def _():
        s_scr[...] = jnp.zeros_like(s_scr)
        kbd_scr[:, 0:CHUNK, :] = jnp.zeros((4, CHUNK, LANES), BF16)
        vbd_scr[:, 0:CHUNK, :] = jnp.zeros((4, CHUNK, LANES), BF16)

    x = x_ref[0]
    q, k, v, gq, gk, gv, gr, log_a = _pre_stage(
        x, cos_ref[...], sa_ref[...], sb_ref[...], g_pre_ref[...],
        w_main_ref, w_glr_ref, w_gk2_ref, b_gk_ref[...])
    q_scr[...] = q.astype(BF16)
    k_forms = _lo_hi_forms(k)
    v_forms = _lo_hi_forms(v)
    for g in range(ATT_KV_HEADS):
        for e in range(2):
            kbd_scr[2 * g + e, CHUNK:CHUNK + tile, :] = k_forms[g][e].astype(BF16)
            vbd_scr[2 * g + e, CHUNK:CHUNK + tile, :] = v_forms[g][e].astype(BF16)
    gq_scr[...] = gq
    gk_scr[...] = gk
    gv_scr[...] = gv
    gr_scr[...] = gr
    la_scr[...] = log_a

    @pl.when(t_step == n_steps - 1)
    def _():
        kwin_ref[0] = k[tile - WINDOW:tile, :]
        vwin_ref[0] = v[tile - WINDOW:tile, :]

    lane = lax.broadcasted_iota(jnp.int32, (CHUNK, LANES), 1)
    lo_lane = lane < HEAD_DIM
    g_gla = g_gla_ref[...]

    def chunk_body(c, carry):
        r0 = pl.multiple_of(c * CHUNK, CHUNK)
        rows = pl.ds(r0, CHUNK)
        rows2 = pl.ds(r0, 2 * CHUNK)
        first = jnp.where((t_step == 0) & (c == 0), 1, 0)
        bias = bias_ref[first]

        for g in range(ATT_KV_HEADS):
            lhs = jnp.concatenate([q_scr[rows, 2 * LANES * g:2 * LANES * g + LANES],
                                   q_scr[rows, 2 * LANES * g + LANES:2 * LANES * (g + 1)]], axis=0)
            kcat = jnp.concatenate([kbd_scr[2 * g, rows2, :], kbd_scr[2 * g + 1, rows2, :]], axis=0)
            s = _dot_nt(lhs, kcat) + bias
            p_blocks = []
            inv = []
            for blk in range(2):
                p_row = []
                inv_row = []
                for e in range(2):
                    sub = s[CHUNK * blk:CHUNK * (blk + 1), 2 * CHUNK * e:2 * CHUNK * (e + 1)]
                    sink = sinks_ref[4 * g + 2 * blk + e]
                    m = jnp.maximum(jnp.max(sub, axis=-1, keepdims=True), sink)
                    p = jnp.exp(sub - m)
                    den = jnp.sum(p, axis=-1, keepdims=True) + jnp.exp(sink - m)
                    p_row.append(p.astype(BF16))
                    inv_row.append(1.0 / den)
                p_blocks.append(jnp.concatenate(p_row, axis=1))
                inv.append(inv_row)
            p_all = jnp.concatenate(p_blocks, axis=0)
            vcat = jnp.concatenate([vbd_scr[2 * g, rows2, :], vbd_scr[2 * g + 1, rows2, :]], axis=0)
            o = _dot(p_all, vcat)
            for blk in range(2):
                scale = jnp.where(lo_lane, inv[blk][0], inv[blk][1])
                c0 = 2 * LANES * g + LANES * blk
                mix_scr[rows, c0:c0 + LANES] = (o[CHUNK * blk:CHUNK * (blk + 1)] * scale).astype(BF16)

        gq_c = gq_scr[rows, :]
        gk_c = gk_scr[rows, :]
        la_c = la_scr[rows, :]
        expo = _gla_exponents(la_c, gmat_ref[0:2 * CHUNK, :])
        b = expo[0:CHUNK]
        b_last = b[CHUNK - 1:CHUNK, :]
        q_dec_f = gq_c * jnp.exp(b)
        q_dec = q_dec_f.astype(BF16)
        k_dec_t = jnp.transpose(gk_c * jnp.exp(expo[CHUNK:2 * CHUNK])).astype(BF16)
        decay_t = jnp.exp(jnp.transpose(jnp.broadcast_to(b_last, b.shape)))
        scores = lax.cond(
            jnp.min(b_last) >= -BOUNDED_LOG_DECAY,
            lambda: _gla_intra_bounded(q_dec_f, gk_c, b, masks_ref[len(PROMPT_LEVELS) + 1]),
            lambda: _gla_intra(gq_c, gk_c, la_c, gmat_ref[2 * CHUNK:, :], PROMPT_LEVELS, masks_ref))
        s0 = s_scr[...]
        s0_b = s0.astype(BF16)
        for h in range(GLA_HEADS):
            p_ = h // 2
            v_h = gv_scr[rows, GLA_DV * h:GLA_DV * (h + 1)].astype(BF16)
            qd_p = q_dec[:, LANES * p_:LANES * (p_ + 1)]
            qd_h = jnp.where(_head_lanes(h, CHUNK), qd_p, jnp.zeros_like(qd_p))
            o_h = (_dot(scores[:, CHUNK * h:CHUNK * (h + 1)].astype(BF16), v_h)
                   + _dot(qd_h, s0_b[LANES * p_:LANES * (p_ + 1), :]))
            out_h = _gla_out_gate(o_h, gr_scr[rows, GLA_DV * h:GLA_DV * (h + 1)], g_gla)
            mix_scr[rows, ATT_W + GLA_DV * h:ATT_W + GLA_DV * (h + 1)] = out_h.astype(BF16)
            hs = slice(GLA_DK * h, GLA_DK * (h + 1))
            s_scr[hs, :] = decay_t[hs, :] * s0[hs, :] + _dot(k_dec_t[hs, :], v_h)
        return carry

    lax.fori_loop(0, n_chunks, chunk_body, 0)

    kbd_scr[:, 0:CHUNK, :] = kbd_scr[:, tile:tile + CHUNK, :]
    vbd_scr[:, 0:CHUNK, :] = vbd_scr[:, tile:tile + CHUNK, :]

    y_ref[0] = _post_stage(x, mix_scr[...], g_post_ref[...], g_fpre_ref[...], g_fpost_ref[...],
                           w_out_ref, w_up_ref, w_down_ref)

    @pl.when(t_step == n_steps - 1)
    def _():
        sout_ref[0] = s_scr[...]


def _sample_kernel(sinks_ref, x_ref, cos_ref, sa_ref, sb_ref, gmat_ref, masks_ref, bias_win_ref, bias_new_ref,
                   ck_ref, cv_ref, st_ref,
                   g_pre_ref, g_post_ref, g_fpre_ref, g_fpost_ref, g_gla_ref, b_gk_ref,
                   w_main_ref, w_glr_ref, w_gk2_ref, w_out_ref, w_up_ref, w_down_ref,
                   y_ref, kwin_ref, vwin_ref, sout_ref,
                   *, dec_seq, levels):
    n_seqs = ck_ref.shape[0]
    x = x_ref[...]
    q, k, v, gq, gk, gv, gr, log_a = _pre_stage(
        x, cos_ref[...], sa_ref[...], sb_ref[...], g_pre_ref[...],
        w_main_ref, w_glr_ref, w_gk2_ref, b_gk_ref[...])

    kwin_ref[:, 0:WINDOW - dec_seq, :] = ck_ref[:, dec_seq:WINDOW, :]
    vwin_ref[:, 0:WINDOW - dec_seq, :] = cv_ref[:, dec_seq:WINDOW, :]
    kwin_ref[:, WINDOW - dec_seq:WINDOW, :] = k.reshape(n_seqs, dec_seq, KV_W)
    vwin_ref[:, WINDOW - dec_seq:WINDOW, :] = v.reshape(n_seqs, dec_seq, KV_W)

    def seq_rows(a, s):
        return a[dec_seq * s:dec_seq * (s + 1)]

    def gather_seq(per_head, s):
        return jnp.concatenate([seq_rows(a, s) for a in per_head], axis=0)

    def scatter_heads(per_seq, h):
        return jnp.concatenate([seq_rows(a, h) for a in per_seq], axis=0)

    lo_lane = lax.broadcasted_iota(jnp.int32, (CHUNK, LANES), 1) < HEAD_DIM
    q_heads = []
    for j in range(ATT_W // LANES):
        blk = q[:, LANES * j:LANES * (j + 1)]
        rolled = pltpu.roll(blk, HEAD_DIM, 1)
        g = (2 * j) // (ATT_HEADS // ATT_KV_HEADS)
        keep = lo_lane if g == 0 else jnp.logical_not(lo_lane)
        for e in range(2):
            q_heads.append(jnp.where(keep, blk if e == g else rolled, 0.0))
    s_win = [_dot_nt(gather_seq(q_heads, s), ck_ref[s]) for s in range(n_seqs)]
    k_b = k.astype(BF16)
    v_b = v.astype(BF16)
    bias_win = bias_win_ref[...]
    bias_new = bias_new_ref[...]
    p_win = []
    o_new = []
    inv_den = []
    for h in range(ATT_HEADS):
        sw = scatter_heads(s_win, h) + bias_win
        sn = _dot_nt(q_heads[h].astype(BF16), k_b) + bias_new
        sink = sinks_ref[h]
        m = jnp.maximum(jnp.maximum(jnp.max(sw, axis=-1, keepdims=True),
                                    jnp.max(sn, axis=-1, keepdims=True)), sink)
        pw = jnp.exp(sw - m)
        pn = jnp.exp(sn - m)
        den = (jnp.sum(pw, axis=-1, keepdims=True) + jnp.sum(pn, axis=-1, keepdims=True)
               + jnp.exp(sink - m))
        p_win.append(pw)
        o_new.append(_dot(pn.astype(BF16), v_b))
        inv_den.append(1.0 / den)
    o_win = [_dot(gather_seq(p_win, s), cv_ref[s]) for s in range(n_seqs)]
    o_heads = [(scatter_heads(o_win, h) + o_new[h]) * inv_den[h] for h in range(ATT_HEADS)]
    mix_cols = []
    for j in range(ATT_W // LANES):
        g = (2 * j) // (ATT_HEADS // ATT_KV_HEADS)
        first = o_heads[2 * j] if g == 0 else pltpu.roll(o_heads[2 * j], HEAD_DIM, 1)
        second = o_heads[2 * j + 1] if g == 1 else pltpu.roll(o_heads[2 * j + 1], HEAD_DIM, 1)
        mix_cols.append(jnp.where(lo_lane, first, second).astype(BF16))

    expo = _gla_exponents(log_a, gmat_ref[0:2 * CHUNK, :])
    b = expo[0:CHUNK]
    q_dec = gq * jnp.exp(b)
    k_dec_t = jnp.transpose(gk * jnp.exp(expo[CHUNK:2 * CHUNK]))
    b3 = b.reshape(n_seqs, dec_seq, GK_W)
    b_last = jnp.broadcast_to(b3[:, dec_seq - 1:dec_seq, :], b3.shape).reshape(CHUNK, GK_W)
    decay_t = jnp.exp(jnp.transpose(b_last))
    scores = _gla_intra(gq, gk, log_a, gmat_ref[2 * CHUNK:, :], levels, masks_ref)
    lane_k = lax.broadcasted_iota(jnp.int32, (dec_seq, GK_W), 1)
    lane_t = lax.broadcasted_iota(jnp.int32, (GLA_DK, LANES), 1)
    o_inter = []
    for s in range(n_seqs):
        qd = seq_rows(q_dec, s)
        lhs = jnp.concatenate(
            [jnp.where((lane_k >= GLA_DK * h) & (lane_k < GLA_DK * (h + 1)), qd, 0.0)
             for h in range(GLA_HEADS)], axis=0)
        o_inter.append(_dot(lhs, st_ref[s]))
    g_gla = g_gla_ref[...]
    for h in range(GLA_HEADS):
        cs = slice(GLA_DV * h, GLA_DV * (h + 1))
        hs = slice(GLA_DK * h, GLA_DK * (h + 1))
        v_h = gv[:, cs].astype(BF16)
        o_h = _dot(scores[:, CHUNK * h:CHUNK * (h + 1)].astype(BF16), v_h) + scatter_heads(o_inter, h)
        mix_cols.append(_gla_out_gate(o_h, gr[:, cs], g_gla).astype(BF16))
        kd_h = k_dec_t[hs, :]
        lhs = jnp.concatenate(
            [jnp.where((lane_t >= dec_seq * s) & (lane_t < dec_seq * (s + 1)), kd_h, 0.0).astype(BF16)
             for s in range(n_seqs)], axis=0)
        upd = _dot(lhs, v_h)
        for s in range(n_seqs):
            a_col = jnp.broadcast_to(decay_t[hs, dec_seq * s:dec_seq * s + 1], (GLA_DK, GLA_DV))
            sout_ref[s, hs, :] = a_col * st_ref[s, hs, :] + upd[GLA_DK * s:GLA_DK * (s + 1)]

    y_ref[...] = _post_stage(x, jnp.concatenate(mix_cols, axis=1), g_post_ref[...], g_fpre_ref[...],
                             g_fpost_ref[...], w_out_ref, w_up_ref, w_down_ref)


def _gla_constants(seg, levels):
    t = np.arange(CHUNK)[:, None]
    u = np.arange(CHUNK)[None, :]
    same = (t // seg) == (u // seg)
    blocks = [same & (u <= t), same & (u > t)]
    masks = [np.eye(CHUNK, dtype=bool)]
    for m in levels:
        ref = (t // (2 * m)) * (2 * m) + m - 1
        second = (t & m) != 0
        blocks.append(np.where(second, (u > ref) & (u <= t), (u > t) & (u <= ref)))
        masks.append(((t // (2 * m)) == (u // (2 * m))) & second & ((u & m) == 0))
    masks.append(blocks[0])
    gmat = jnp.asarray(np.concatenate(blocks, axis=0).astype(np.float32), dtype=BF16)
    return gmat, jnp.asarray(np.stack(masks).astype(np.float32))


def _attention_bias():
    t = (np.arange(2 * CHUNK) % CHUNK)[:, None]
    j = (np.arange(4 * CHUNK) % (2 * CHUNK))[None, :]
    band = (j >= t + 1) & (j <= t + WINDOW)
    first = band & (j >= CHUNK)
    neg = np.float32(-np.inf)
    return jnp.asarray(np.stack([np.where(band, np.float32(0), neg), np.where(first, np.float32(0), neg)]))


def _sample_attention_bias(dec_seq):
    r = np.arange(CHUNK)[:, None]
    c = np.arange(CHUNK)[None, :]
    t = r % dec_seq
    win = c > t
    new = ((r // dec_seq) == (c // dec_seq)) & ((c % dec_seq) <= t)
    neg = np.float32(-np.inf)
    zero = np.float32(0)
    return jnp.asarray(np.where(win, zero, neg)), jnp.asarray(np.where(new, zero, neg))


def _rope_tables(pos):
    half = HEAD_DIM // 2
    inv = ROPE_THETA ** (-jnp.arange(half, dtype=F32) / half)
    ang = pos.astype(F32)[:, None] * inv[None, :]
    cos = jnp.cos(ang)
    sin = jnp.sin(ang)
    zero = jnp.zeros_like(sin)
    reps = LANES // HEAD_DIM
    return (jnp.tile(cos, (1, 2 * reps)),
            jnp.tile(jnp.concatenate([-sin, zero], axis=1), (1, reps)),
            jnp.tile(jnp.concatenate([zero, sin], axis=1), (1, reps)))


def _const_spec(shape):
    zeros = (0,) * len(shape)
    return pl.BlockSpec(shape, lambda *_: zeros, pipeline_mode=pl.Buffered(1))


def kernel(x_prompt, x_sample, cache_k, cache_v, state_gla, w_in, w_gk2, b_gk, g_gla, sinks, w_out, g_mix_pre, g_mix_post, g_ffn_pre, g_ffn_post, w_up, w_down):
    depth = w_in.shape[0]
    assert depth == 1, "single trunk layer"
    batch, seq, d_model = x_prompt.shape
    dec_batch, dec_seq, _ = x_sample.shape
    d_ff = w_up.shape[2]
    assert seq % PROMPT_TILE == 0 and dec_batch % SAMPLE_SEQS == 0
    assert SAMPLE_SEQS * dec_seq == CHUNK and CHUNK % dec_seq == 0 and d_ff % FF_CHUNK == 0
    assert w_in.shape[2] == MAIN_W + GLA_RANK

    w_main = w_in[0, :, :MAIN_W].astype(BF16)
    w_glr = w_in[0, :, MAIN_W:].astype(BF16)
    w_gk2_b = w_gk2[0].astype(BF16)
    w_out_b = w_out[0].astype(BF16)
    w_up_b = w_up[0].astype(BF16)
    w_down_b = w_down[0].astype(BF16)
    g_pre = g_mix_pre[0][None, :]
    g_post = g_mix_post[0][None, :]
    g_fpre = g_ffn_pre[0][None, :]
    g_fpost = g_ffn_post[0][None, :]
    g_gla2 = g_gla[0][None, :]
    b_gk2 = b_gk[0][None, :]
    sinks1 = sinks[0]

    weights = (g_pre, g_post, g_fpre, g_fpost, g_gla2, b_gk2, w_main, w_glr, w_gk2_b, w_out_b, w_up_b, w_down_b)
    weight_specs = [_const_spec(w.shape) for w in weights]
    smem_spec = pl.BlockSpec(memory_space=pltpu.SMEM)

    cos_p, sa_p, sb_p = _rope_tables(jnp.arange(seq, dtype=jnp.int32))
    gmat_p, masks_p = _gla_constants(CHUNK, PROMPT_LEVELS)
    bias = _attention_bias()
    n_t = seq // PROMPT_TILE
    tile = PROMPT_TILE
    tab_spec = pl.BlockSpec((tile, LANES), lambda b, t: (t, 0))
    yp, kp, vp, sp = pl.pallas_call(
        _prompt_kernel,
        grid=(batch, n_t),
        in_specs=[smem_spec,
                  pl.BlockSpec((1, tile, d_model), lambda b, t: (b, t, 0)),
                  tab_spec, tab_spec, tab_spec,
                  _const_spec(bias.shape), _const_spec(gmat_p.shape), _const_spec(masks_p.shape)] + weight_specs,
        out_specs=[pl.BlockSpec((1, tile, d_model), lambda b, t: (b, t, 0)),
                   pl.BlockSpec((1, WINDOW, KV_W), lambda b, t: (b, 0, 0)),
                   pl.BlockSpec((1, WINDOW, KV_W), lambda b, t: (b, 0, 0)),
                   pl.BlockSpec((1, GK_W, GLA_DV), lambda b, t: (b, 0, 0))],
        out_shape=[jax.ShapeDtypeStruct((batch, seq, d_model), F32),
                   jax.ShapeDtypeStruct((batch, WINDOW, KV_W), F32),
                   jax.ShapeDtypeStruct((batch, WINDOW, KV_W), F32),
                   jax.ShapeDtypeStruct((batch, GK_W, GLA_DV), F32)],
        scratch_shapes=[pltpu.VMEM((tile, ATT_W), BF16),
                        pltpu.VMEM((4, CHUNK + tile, LANES), BF16),
                        pltpu.VMEM((4, CHUNK + tile, LANES), BF16),
                        pltpu.VMEM((tile, GK_W), F32),
                        pltpu.VMEM((tile, GK_W), F32),
                        pltpu.VMEM((tile, GV_W), F32),
                        pltpu.VMEM((tile, GV_W), F32),
                        pltpu.VMEM((tile, GK_W), F32),
                        pltpu.VMEM((GK_W, GLA_DV), F32),
                        pltpu.VMEM((tile, d_model), BF16)],
        compiler_params=pltpu.CompilerParams(
            dimension_semantics=("arbitrary", "arbitrary"), vmem_limit_bytes=VMEM_LIMIT_BYTES),
        name="prompt_layer",
    )(sinks1, x_prompt, cos_p, sa_p, sb_p, bias, gmat_p, masks_p, *weights)

    levels_s = tuple(m for m in PROMPT_LEVELS if m < dec_seq)
    pos_s = PAST_LEN + jnp.arange(dec_seq, dtype=jnp.int32)
    cos_s, sa_s, sb_s = (jnp.tile(a, (SAMPLE_SEQS, 1)) for a in _rope_tables(pos_s))
    gmat_s, masks_s = _gla_constants(dec_seq, levels_s)
    bias_win, bias_new = _sample_attention_bias(dec_seq)
    rows = SAMPLE_SEQS * dec_seq
    n_g = dec_batch // SAMPLE_SEQS
    xs = x_sample.reshape(dec_batch * dec_seq, d_model)
    ck = cache_k[0].reshape(dec_batch, WINDOW, KV_W)
    cv = cache_v[0].reshape(dec_batch, WINDOW, KV_W)
    st = state_gla[0].reshape(dec_batch, GK_W, GLA_DV)
    seq_spec = pl.BlockSpec((SAMPLE_SEQS, WINDOW, KV_W), lambda i: (i, 0, 0))
    st_spec = pl.BlockSpec((SAMPLE_SEQS, GK_W, GLA_DV), lambda i: (i, 0, 0))
    row_spec = pl.BlockSpec((rows, d_model), lambda i: (i, 0))
    ys, ks, vs, ss = pl.pallas_call(
        functools.partial(_sample_kernel, dec_seq=dec_seq, levels=levels_s),
        grid=(n_g,),
        in_specs=[smem_spec, row_spec,
                  _const_spec(cos_s.shape), _const_spec(sa_s.shape), _const_spec(sb_s.shape),
                  _const_spec(gmat_s.shape), _const_spec(masks_s.shape),
                  _const_spec(bias_win.shape), _const_spec(bias_new.shape),
                  seq_spec, seq_spec, st_spec] + weight_specs,
        out_specs=[row_spec, seq_spec, seq_spec, st_spec],
        out_shape=[jax.ShapeDtypeStruct((dec_batch * dec_seq, d_model), F32),
                   jax.ShapeDtypeStruct((dec_batch, WINDOW, KV_W), F32),
                   jax.ShapeDtypeStruct((dec_batch, WINDOW, KV_W), F32),
                   jax.ShapeDtypeStruct((dec_batch, GK_W, GLA_DV), F32)],
        compiler_params=pltpu.CompilerParams(
            dimension_semantics=("arbitrary",), vmem_limit_bytes=VMEM_LIMIT_BYTES),
        name="sample_layer",
    )(sinks1, xs, cos_s, sa_s, sb_s, gmat_s, masks_s, bias_win, bias_new, ck, cv, st, *weights)

    head_shape = (ATT_KV_HEADS, HEAD_DIM)
    state_shape = (GLA_HEADS, GLA_DK, GLA_DV)
    return (yp,
            ys.reshape(dec_batch, dec_seq, d_model),
            kp.reshape((1, batch, WINDOW) + head_shape),
            vp.reshape((1, batch, WINDOW) + head_shape),
            sp.reshape((1, batch) + state_shape),
            ks.reshape((1, dec_batch, WINDOW) + head_shape),
            vs.reshape((1, dec_batch, WINDOW) + head_shape),
            ss.reshape((1, dec_batch) + state_shape))
```

```python
import functools
import math

import numpy as np
import jax
import jax.numpy as jnp
from jax import lax
from jax.experimental import pallas as pl
from jax.experimental.pallas import tpu as pltpu

F32 = jnp.float32
BF16 = jnp.bfloat16

HEAD_DIM = 64
ATT_HEADS = 8
ATT_KV_HEADS = 2
WINDOW = 128
PAST_LEN = 16384
ROPE_THETA = 10000.0
GLA_HEADS = 4
GLA_DK = 64
GLA_DV = 128
GLA_RANK = 16
GLA_GATE_TEMP = 16.0
EPS = 1e-6
ATT_W = ATT_HEADS * HEAD_DIM
KV_W = ATT_KV_HEADS * HEAD_DIM
GK_W = GLA_HEADS * GLA_DK
GV_W = GLA_HEADS * GLA_DV
COL_Q = 0
COL_K = COL_Q + ATT_W
COL_V = COL_K + KV_W
COL_GQ = COL_V + KV_W
COL_GK = COL_GQ + GK_W
COL_GV = COL_GK + GK_W
COL_GR = COL_GV + GV_W
MAIN_W = COL_GR + GV_W
QK_SCALE = HEAD_DIM ** -0.5
GLA_SCALE = GLA_DK ** -0.5

CHUNK = 128
LANES = 128
PROMPT_TILE = 256
SAMPLE_SEQS = 16
FF_CHUNK = 1024
PROMPT_LEVELS = (1, 2, 4, 8, 16, 32, 64)
BOUNDED_LOG_DECAY = 30.0
VMEM_LIMIT_BYTES = 56 * 1024 * 1024


def _dot(a, b):
    return jnp.dot(a, b, preferred_element_type=F32)


def _dot_nt(a, b):
    return lax.dot_general(a, b, (((1,), (1,)), ((), ())), preferred_element_type=F32)


def _dot_tn(a, b):
    return lax.dot_general(a, b, (((0,), (0,)), ((), ())), preferred_element_type=F32)


def _rms(x, g):
    ms = jnp.mean(x * x, axis=-1, keepdims=True)
    return x * lax.rsqrt(ms + EPS) * g


def _rope(x, cos, sin_a, sin_b):
    return x * cos + pltpu.roll(x, 96, 1) * sin_a + pltpu.roll(x, 32, 1) * sin_b


def _split3(x):
    hi = x.astype(BF16)
    r1 = x - hi.astype(F32)
    mid = r1.astype(BF16)
    lo = (r1 - mid.astype(F32)).astype(BF16)
    return hi, mid, lo


def _pre_stage(x, cos, sin_a, sin_b, g_pre, w_main, w_glr, w_gk2, b_gk):
    h = _rms(x, g_pre).astype(BF16)
    proj = _dot(h, w_main[...])
    glr = _dot(h, w_glr[...])
    z = _dot(glr.astype(BF16), w_gk2[...]) + b_gk
    log_a = (jnp.minimum(z, 0.0) - jnp.log1p(jnp.exp(-jnp.abs(z)))) * (1.0 / GLA_GATE_TEMP)
    q = jnp.concatenate(
        [_rope(proj[:, COL_Q + LANES * j:COL_Q + LANES * (j + 1)], cos, sin_a, sin_b) * QK_SCALE
         for j in range(ATT_W // LANES)], axis=1)
    k = _rope(proj[:, COL_K:COL_V], cos, sin_a, sin_b)
    v = proj[:, COL_V:COL_GQ]
    gq = proj[:, COL_GQ:COL_GK] * GLA_SCALE
    gk = proj[:, COL_GK:COL_GV]
    gv = proj[:, COL_GV:COL_GR]
    gr = proj[:, COL_GR:MAIN_W]
    return q, k, v, gq, gk, gv, gr, log_a


def _post_stage(x, mix_b, g_mix_post, g_ffn_pre, g_ffn_post, w_out, w_up, w_down):
    mix = _dot(mix_b, w_out[...])
    x1 = x + _rms(mix, g_mix_post)
    h2 = _rms(x1, g_ffn_pre).astype(BF16)
    f = None
    for j in range(w_up.shape[1] // FF_CHUNK):
        u = jnp.maximum(_dot(h2, w_up[:, FF_CHUNK * j:FF_CHUNK * (j + 1)]), 0.0)
        part = _dot((u * u).astype(BF16), w_down[FF_CHUNK * j:FF_CHUNK * (j + 1), :])
        f = part if f is None else f + part
    return x1 + _rms(f, g_ffn_post)


def _lo_hi_forms(a):
    lo = lax.broadcasted_iota(jnp.int32, a.shape, 1) < HEAD_DIM
    r = pltpu.roll(a, HEAD_DIM, 1)
    zero = jnp.zeros_like(a)
    return ((jnp.where(lo, a, zero), jnp.where(lo, zero, r)),
            (jnp.where(lo, r, zero), jnp.where(lo, zero, a)))


def _gla_exponents(log_a, gmat):
    hi, mid, lo = _split3(log_a)
    return _dot(gmat, hi) + _dot(gmat, mid) + _dot(gmat, lo)


def _head_lanes(h, rows):
    lane = lax.broadcasted_iota(jnp.int32, (rows, LANES), 1)
    e = h % 2
    return (lane >= GLA_DK * e) & (lane < GLA_DK * (e + 1))


def _gla_intra(gq, gk, log_a, gmat_lv, levels, masks_ref):
    expo = _gla_exponents(log_a, gmat_lv)
    row = lax.broadcasted_iota(jnp.int32, gq.shape, 0)
    terms = [(gq.astype(BF16), gk.astype(BF16), 0)]
    for li, m in enumerate(levels):
        ex = jnp.exp(expo[CHUNK * li:CHUNK * (li + 1)])
        z = (jnp.where((row & m) != 0, gq, gk) * ex).astype(BF16)
        terms.append((z, z, 1 + li))
    scores = []
    for h in range(GLA_HEADS):
        p = h // 2
        in_head = _head_lanes(h, gq.shape[0])
        acc = None
        for zq, zk, mi in terms:
            zq_p = zq[:, LANES * p:LANES * (p + 1)]
            zk_p = zk[:, LANES * p:LANES * (p + 1)]
            zk_h = jnp.where(in_head, zk_p, jnp.zeros_like(zk_p))
            t = _dot_nt(zq_p, zk_h) * masks_ref[mi]
            acc = t if acc is None else acc + t
        scores.append(acc)
    return jnp.concatenate(scores, axis=1)


def _gla_intra_bounded(q_dec, gk, b, causal):
    k_grow_t = jnp.transpose(gk * jnp.exp(-b)).astype(BF16)
    scores = []
    for h in range(GLA_HEADS):
        p = h // 2
        qd_p = q_dec[:, LANES * p:LANES * (p + 1)]
        qd_h = jnp.where(_head_lanes(h, q_dec.shape[0]), qd_p, jnp.zeros_like(qd_p))
        scores.append(_dot(qd_h, k_grow_t[LANES * p:LANES * (p + 1), :]) * causal)
    return jnp.concatenate(scores, axis=1)


def _gla_out_gate(o_h, gr_h, g_gla):
    on = _rms(o_h, g_gla)
    return on * (gr_h * (1.0 / (1.0 + jnp.exp(-gr_h))))


def _prompt_kernel(sinks_ref, x_ref, cos_ref, sa_ref, sb_ref, bias_ref, gmat_ref, masks_ref,
                   g_pre_ref, g_post_ref, g_fpre_ref, g_fpost_ref, g_gla_ref, b_gk_ref,
                   w_main_ref, w_glr_ref, w_gk2_ref, w_out_ref, w_up_ref, w_down_ref,
                   y_ref, kwin_ref, vwin_ref, sout_ref,
                   q_scr, kbd_scr, vbd_scr, gq_scr, gk_scr, gv_scr, gr_scr, la_scr, s_scr,
                   mix_scr, xp_scr, mixp_scr, *, tiles_per_seq, n_tiles):
    j = pl.program_id(0)
    t_step = lax.rem(jnp.minimum(j, n_tiles - 1), tiles_per_seq)
    tile = x_ref.shape[1]
    n_chunks = tile // CHUNK

    @pl.when(j == 0)
    def _():
        s_scr[...] = jnp.zeros_like(s_scr)
        kbd_scr[:, 0:CHUNK, :] = jnp.zeros((4, CHUNK, LANES), BF16)
        vbd_scr[:, 0:CHUNK, :] = jnp.zeros((4, CHUNK, LANES), BF16)
        xp_scr[...] = jnp.zeros_like(xp_scr)
        mixp_scr[...] = jnp.zeros_like(mixp_scr)

    x = x_ref[0]
    q, k, v, gq, gk, gv, gr, log_a = _pre_stage(
        x, cos_ref[...], sa_ref[...], sb_ref[...], g_pre_ref[...],
        w_main_ref, w_glr_ref, w_gk2_ref, b_gk_ref[...])
    q_scr[...] = q.astype(BF16)
    k_forms = _lo_hi_forms(k)
    v_forms = _lo_hi_forms(v)
    for g in range(ATT_KV_HEADS):
        for e in range(2):
            kbd_scr[2 * g + e, CHUNK:CHUNK + tile, :] = k_forms[g][e].astype(BF16)
            vbd_scr[2 * g + e, CHUNK:CHUNK + tile, :] = v_forms[g][e].astype(BF16)
    gq_scr[...] = gq
    gk_scr[...] = gk
    gv_scr[...] = gv
    gr_scr[...] = gr
    la_scr[...] = log_a

    kwin_ref[0] = k[tile - WINDOW:tile, :]
    vwin_ref[0] = v[tile - WINDOW:tile, :]

    chunk_decay = jnp.concatenate(
        [jnp.sum(log_a[CHUNK * c:CHUNK * (c + 1)], axis=0, keepdims=True) for c in range(n_chunks)], axis=0)
    bounded = jnp.min(chunk_decay) >= -BOUNDED_LOG_DECAY
    branch = jnp.where(j == n_tiles, 2, jnp.where(bounded, 0, 1))

    lo_lane = lax.broadcasted_iota(jnp.int32, (CHUNK, LANES), 1) < HEAD_DIM
    g_gla = g_gla_ref[...]

    def attention_chunk(c):
        rows = slice(CHUNK * c, CHUNK * (c + 1))
        rows2 = slice(CHUNK * c, CHUNK * (c + 2))
        bias = bias_ref[jnp.where(t_step == 0, 1, 0)] if c == 0 else bias_ref[0]
        for g in range(ATT_KV_HEADS):
            lhs = jnp.concatenate([q_scr[rows, 2 * LANES * g:2 * LANES * g + LANES],
                                   q_scr[rows, 2 * LANES * g + LANES:2 * LANES * (g + 1)]], axis=0)
            kcat = jnp.concatenate([kbd_scr[2 * g, rows2, :], kbd_scr[2 * g + 1, rows2, :]], axis=0)
            s = _dot_nt(lhs, kcat) + bias
            p_blocks = []
            inv = []
            for blk in range(2):
                p_row = []
                inv_row = []
                for e in range(2):
                    sub = s[CHUNK * blk:CHUNK * (blk + 1), 2 * CHUNK * e:2 * CHUNK * (e + 1)]
                    sink = sinks_ref[4 * g + 2 * blk + e]
                    m = jnp.maximum(jnp.max(sub, axis=-1, keepdims=True), sink)
                    p = jnp.exp(sub - m)
                    den = jnp.sum(p, axis=-1, keepdims=True) + jnp.exp(sink - m)
                    p_row.append(p.astype(BF16))
                    inv_row.append(1.0 / den)
                p_blocks.append(jnp.concatenate(p_row, axis=1))
                inv.append(inv_row)
            p_all = jnp.concatenate(p_blocks, axis=0)
            vcat = jnp.concatenate([vbd_scr[2 * g, rows2, :], vbd_scr[2 * g + 1, rows2, :]], axis=0)
            o = _dot(p_all, vcat)
            for blk in range(2):
                scale = jnp.where(lo_lane, inv[blk][0], inv[blk][1])
                c0 = 2 * LANES * g + LANES * blk
                mix_scr[rows,c0:c0 + LANES] = (o[CHUNK * blk:CHUNK * (blk + 1)] * scale).astype(BF16)

    def gla_chunk(c, s0, bounded_decay):
        rows = slice(CHUNK * c, CHUNK * (c + 1))
        gq_c = gq_scr[rows, :]
        gk_c = gk_scr[rows, :]
        la_c = la_scr[rows, :]
        expo = _gla_exponents(la_c, gmat_ref[0:2 * CHUNK, :])
        b = expo[0:CHUNK]
        q_dec_f = gq_c * jnp.exp(b)
        q_dec = q_dec_f.astype(BF16)
        k_dec_t = jnp.transpose(gk_c * jnp.exp(expo[CHUNK:2 * CHUNK])).astype(BF16)
        decay_t = jnp.exp(jnp.transpose(jnp.broadcast_to(b[CHUNK - 1:CHUNK, :], b.shape)))
        if bounded_decay:
            scores = _gla_intra_bounded(q_dec_f, gk_c, b, masks_ref[len(PROMPT_LEVELS) + 1])
        else:
            scores = _gla_intra(gq_c, gk_c, la_c, gmat_ref[2 * CHUNK:, :], PROMPT_LEVELS, masks_ref)
        s_new = []
        for h in range(GLA_HEADS):
            p_ = h // 2
            v_h = gv_scr[rows, GLA_DV * h:GLA_DV * (h + 1)].astype(BF16)
            qd_p = q_dec[:, LANES * p_:LANES * (p_ + 1)]
            qd_h = jnp.where(_head_lanes(h, CHUNK), qd_p, jnp.zeros_like(qd_p))
            s0_pair = jnp.concatenate([s0[2 * p_], s0[2 * p_ + 1]], axis=0).astype(BF16)
            o_h = _dot(scores[:, CHUNK * h:CHUNK * (h + 1)].astype(BF16), v_h) + _dot(qd_h, s0_pair)
            out_h = _gla_out_gate(o_h, gr_scr[rows, GLA_DV * h:GLA_DV * (h + 1)], g_gla)
            mix_scr[rows,ATT_W + GLA_DV * h:ATT_W + GLA_DV * (h + 1)] = out_h.astype(BF16)
            hs = slice(GLA_DK * h, GLA_DK * (h + 1))
            s_new.append(decay_t[hs, :] * s0[h] + _dot(k_dec_t[hs, :], v_h))
        return s_new

    def mixers(bounded_decay):
        keep = jnp.where(t_step == 0, 0.0, 1.0)
        state = [s_scr[GLA_DK * h:GLA_DK * (h + 1), :] * keep for h in range(GLA_HEADS)]
        for c in range(n_chunks):
            attention_chunk(c)
            state = gla_chunk(c, state, bounded_decay)
        s_all = jnp.concatenate(state, axis=0)
        s_scr[...] = s_all
        sout_ref[0] = s_all
        kbd_scr[:, 0:CHUNK, :] = kbd_scr[:, tile:tile + CHUNK, :]
        vbd_scr[:, 0:CHUNK, :] = vbd_scr[:, tile:tile + CHUNK, :]

    def finish_previous_tile():
        y_ref[0] = _post_stage(xp_scr[...], mixp_scr[...], g_post_ref[...], g_fpre_ref[...], g_fpost_ref[...],
                               w_out_ref, w_up_ref, w_down_ref)

    def hand_over():
        xp_scr[...] = x_ref[0]
        mixp_scr[...] = mix_scr[...]

    def step_bounded():
        finish_previous_tile()
        mixers(True)
        hand_over()

    def step_general():
        finish_previous_tile()
        mixers(False)
        hand_over()

    lax.switch(branch, [step_bounded, step_general, finish_previous_tile])


def _sample_kernel(sinks_ref, x_ref, cos_ref, sa_ref, sb_ref, gmat_ref, masks_ref, bias_win_ref, bias_new_ref,
                   ck_ref, cv_ref, st_ref,
                   g_pre_ref, g_post_ref, g_fpre_ref, g_fpost_ref, g_gla_ref, b_gk_ref,
                   w_main_ref, w_glr_ref, w_gk2_ref, w_out_ref, w_up_ref, w_down_ref,
                   y_ref, kwin_ref, vwin_ref, sout_ref,
                   *, dec_seq, levels):
    n_seqs = ck_ref.shape[0]
    x = x_ref[...]
    q, k, v, gq, gk, gv, gr, log_a = _pre_stage(
        x, cos_ref[...], sa_ref[...], sb_ref[...], g_pre_ref[...],
        w_main_ref, w_glr_ref, w_gk2_ref, b_gk_ref[...])

    kwin_ref[:, 0:WINDOW - dec_seq, :] = ck_ref[:, dec_seq:WINDOW, :]
    vwin_ref[:, 0:WINDOW - dec_seq, :] = cv_ref[:, dec_seq:WINDOW, :]
    kwin_ref[:, WINDOW - dec_seq:WINDOW, :] = k.reshape(n_seqs, dec_seq, KV_W)
    vwin_ref[:, WINDOW - dec_seq:WINDOW, :] = v.reshape(n_seqs, dec_seq, KV_W)

    def seq_rows(a, s):
        return a[dec_seq * s:dec_seq * (s + 1)]

    def gather_seq(per_head, s):
        return jnp.concatenate([seq_rows(a, s) for a in per_head], axis=0)

    def scatter_heads(per_seq, h):
        return jnp.concatenate([seq_rows(a, h) for a in per_seq], axis=0)

    lo_lane = lax.broadcasted_iota(jnp.int32, (CHUNK, LANES), 1) < HEAD_DIM
    q_heads = []
    for j in range(ATT_W // LANES):
        blk = q[:, LANES * j:LANES * (j + 1)]
        rolled = pltpu.roll(blk, HEAD_DIM, 1)
        g = (2 * j) // (ATT_HEADS // ATT_KV_HEADS)
        keep = lo_lane if g == 0 else jnp.logical_not(lo_lane)
        for e in range(2):
            q_heads.append(jnp.where(keep, blk if e == g else rolled, 0.0))
    s_win = [_dot_nt(gather_seq(q_heads, s), ck_ref[s]) for s in range(n_seqs)]
    k_b = k.astype(BF16)
    v_b = v.astype(BF16)
    bias_win = bias_win_ref[...]
    bias_new = bias_new_ref[...]
    p_win = []
    o_new = []
    inv_den = []
    for h in range(ATT_HEADS):
        sw = scatter_heads(s_win, h) + bias_win
        sn = _dot_nt(q_heads[h].astype(BF16), k_b) + bias_new
        sink = sinks_ref[h]
        m = jnp.maximum(jnp.maximum(jnp.max(sw, axis=-1, keepdims=True),
                                    jnp.max(sn, axis=-1, keepdims=True)), sink)
        pw = jnp.exp(sw - m)
        pn = jnp.exp(sn - m)
        den = (jnp.sum(pw, axis=-1, keepdims=True) + jnp.sum(pn, axis=-1, keepdims=True)
               + jnp.exp(sink - m))
        p_win.append(pw)
        o_new.append(_dot(pn.astype(BF16), v_b))
        inv_den.append(1.0 / den)
    o_win = [_dot(gather_seq(p_win, s), cv_ref[s]) for s in range(n_seqs)]
    o_heads = [(scatter_heads(o_win, h) + o_new[h]) * inv_den[h] for h in range(ATT_HEADS)]
    mix_cols = []
    for j in range(ATT_W // LANES):
        g = (2 * j) // (ATT_HEADS // ATT_KV_HEADS)
        first = o_heads[2 * j] if g == 0 else pltpu.roll(o_heads[2 * j], HEAD_DIM, 1)
        second = o_heads[2 * j + 1] if g == 1 else pltpu.roll(o_heads[2 * j + 1], HEAD_DIM, 1)
        mix_cols.append(jnp.where(lo_lane, first, second).astype(BF16))

    expo = _gla_exponents(log_a, gmat_ref[0:2 * CHUNK, :])
    b = expo[0:CHUNK]
    q_dec = gq * jnp.exp(b)
    k_dec_t = jnp.transpose(gk * jnp.exp(expo[CHUNK:2 * CHUNK]))
    b3 = b.reshape(n_seqs, dec_seq, GK_W)
    b_last = jnp.broadcast_to(b3[:, dec_seq - 1:dec_seq, :], b3.shape).reshape(CHUNK, GK_W)
    decay_t = jnp.exp(jnp.transpose(b_last))
    scores = _gla_intra(gq, gk, log_a, gmat_ref[2 * CHUNK:, :], levels, masks_ref)
    lane_k = lax.broadcasted_iota(jnp.int32, (dec_seq, GK_W), 1)
    lane_t = lax.broadcasted_iota(jnp.int32, (GLA_DK, LANES), 1)
    o_inter = []
    for s in range(n_seqs):
        qd = seq_rows(q_dec, s)
        lhs = jnp.concatenate(
            [jnp.where((lane_k >= GLA_DK * h) & (lane_k < GLA_DK * (h + 1)), qd, 0.0)
             for h in range(GLA_HEADS)], axis=0)
        o_inter.append(_dot(lhs, st_ref[s]))
    g_gla = g_gla_ref[...]
    for h in range(GLA_HEADS):
        cs = slice(GLA_DV * h, GLA_DV * (h + 1))
        hs = slice(GLA_DK * h, GLA_DK * (h + 1))
        v_h = gv[:, cs].astype(BF16)
        o_h = _dot(scores[:, CHUNK * h:CHUNK * (h + 1)].astype(BF16), v_h) + scatter_heads(o_inter, h)
        mix_cols.append(_gla_out_gate(o_h, gr[:, cs], g_gla).astype(BF16))
        kd_h = k_dec_t[hs, :]
        lhs = jnp.concatenate(
            [jnp.where((lane_t >= dec_seq * s) & (lane_t < dec_seq * (s + 1)), kd_h, 0.0).astype(BF16)
             for s in range(n_seqs)], axis=0)
        upd = _dot(lhs, v_h)
        for s in range(n_seqs):
            a_col = jnp.broadcast_to(decay_t[hs, dec_seq * s:dec_seq * s + 1], (GLA_DK, GLA_DV))
            sout_ref[s, hs, :] = a_col * st_ref[s, hs, :] + upd[GLA_DK * s:GLA_DK * (s + 1)]

    y_ref[...] = _post_stage(x, jnp.concatenate(mix_cols, axis=1), g_post_ref[...], g_fpre_ref[...],
                             g_fpost_ref[...], w_out_ref, w_up_ref, w_down_ref)


def _gla_constants(seg, levels):
    t = np.arange(CHUNK)[:, None]
    u = np.arange(CHUNK)[None, :]
    same = (t // seg) == (u // seg)
    blocks = [same & (u <= t), same & (u > t)]
    masks = [np.eye(CHUNK, dtype=bool)]
    for m in levels:
        ref = (t // (2 * m)) * (2 * m) + m - 1
        second = (t & m) != 0
        blocks.append(np.where(second, (u > ref) & (u <= t), (u > t) & (u <= ref)))
        masks.append(((t // (2 * m)) == (u // (2 * m))) & second & ((u & m) == 0))
    masks.append(blocks[0])
    gmat = jnp.asarray(np.concatenate(blocks, axis=0).astype(np.float32), dtype=BF16)
    return gmat, jnp.asarray(np.stack(masks).astype(np.float32))


def _attention_bias():
    t = (np.arange(2 * CHUNK) % CHUNK)[:, None]
    j = (np.arange(4 * CHUNK) % (2 * CHUNK))[None, :]
    band = (j >= t + 1) & (j <= t + WINDOW)
    first = band & (j >= CHUNK)
    neg = np.float32(-np.inf)
    return jnp.asarray(np.stack([np.where(band, np.float32(0), neg), np.where(first, np.float32(0), neg)]))


def _sample_attention_bias(dec_seq):
    r = np.arange(CHUNK)[:, None]
    c = np.arange(CHUNK)[None, :]
    t = r % dec_seq
    win = c > t
    new = ((r // dec_seq) == (c // dec_seq)) & ((c % dec_seq) <= t)
    neg = np.float32(-np.inf)
    zero = np.float32(0)
    return jnp.asarray(np.where(win, zero, neg)), jnp.asarray(np.where(new, zero, neg))


def _rope_tables(pos):
    half = HEAD_DIM // 2
    inv = ROPE_THETA ** (-jnp.arange(half, dtype=F32) / half)
    ang = pos.astype(F32)[:, None] * inv[None, :]
    cos = jnp.cos(ang)
    sin = jnp.sin(ang)
    zero = jnp.zeros_like(sin)
    reps = LANES // HEAD_DIM
    return (jnp.tile(cos, (1, 2 * reps)),
            jnp.tile(jnp.concatenate([-sin, zero], axis=1), (1, reps)),
            jnp.tile(jnp.concatenate([zero, sin], axis=1), (1, reps)))


def _const_spec(shape):
    zeros = (0,) * len(shape)
    return pl.BlockSpec(shape, lambda *_: zeros, pipeline_mode=pl.Buffered(1))


def kernel(x_prompt, x_sample, cache_k, cache_v, state_gla, w_in, w_gk2, b_gk, g_gla, sinks, w_out, g_mix_pre, g_mix_post, g_ffn_pre, g_ffn_post, w_up, w_down):
    depth = w_in.shape[0]
    assert depth == 1, "single trunk layer"
    batch, seq, d_model = x_prompt.shape
    dec_batch, dec_seq, _ = x_sample.shape
    d_ff = w_up.shape[2]
    assert seq % PROMPT_TILE == 0 and dec_batch % SAMPLE_SEQS == 0
    assert SAMPLE_SEQS * dec_seq == CHUNK and CHUNK % dec_seq == 0 and d_ff % FF_CHUNK == 0
    assert w_in.shape[2] == MAIN_W + GLA_RANK

    w_main = w_in[0, :, :MAIN_W].astype(BF16)
    w_glr = w_in[0, :, MAIN_W:].astype(BF16)
    w_gk2_b = w_gk2[0].astype(BF16)
    w_out_b = w_out[0].astype(BF16)
    w_up_b = w_up[0].astype(BF16)
    w_down_b = w_down[0].astype(BF16)
    g_pre = g_mix_pre[0][None, :]
    g_post = g_mix_post[0][None, :]
    g_fpre = g_ffn_pre[0][None, :]
    g_fpost = g_ffn_post[0][None, :]
    g_gla2 = g_gla[0][None, :]
    b_gk2 = b_gk[0][None, :]
    sinks1 = sinks[0]

    weights = (g_pre, g_post, g_fpre, g_fpost, g_gla2, b_gk2, w_main, w_glr, w_gk2_b, w_out_b, w_up_b, w_down_b)
    weight_specs = [_const_spec(w.shape) for w in weights]
    smem_spec = pl.BlockSpec(memory_space=pltpu.SMEM)

    cos_p, sa_p, sb_p = _rope_tables(jnp.arange(seq, dtype=jnp.int32))
    gmat_p, masks_p = _gla_constants(CHUNK, PROMPT_LEVELS)
    bias = _attention_bias()
    n_t = seq // PROMPT_TILE
    tile = PROMPT_TILE
    n_tiles = batch * n_t

    def cur_tile(j):
        return jnp.minimum(j, n_tiles - 1)

    def done_tile(j):
        return jnp.maximum(j - 1, 0)

    tab_spec = pl.BlockSpec((tile, LANES), lambda j: (cur_tile(j) % n_t, 0))
    seq_out = lambda j: (cur_tile(j) // n_t, 0, 0)
    yp, kp, vp, sp = pl.pallas_call(
        functools.partial(_prompt_kernel, tiles_per_seq=n_t, n_tiles=n_tiles),
        grid=(n_tiles + 1,),
        in_specs=[smem_spec,
                  pl.BlockSpec((1, tile, d_model), lambda j: (cur_tile(j) // n_t, cur_tile(j) % n_t, 0)),
                  tab_spec, tab_spec, tab_spec,
                  _const_spec(bias.shape), _const_spec(gmat_p.shape), _const_spec(masks_p.shape)] + weight_specs,
        out_specs=[pl.BlockSpec((1, tile, d_model), lambda j: (done_tile(j) // n_t, done_tile(j) % n_t, 0)),
                   pl.BlockSpec((1, WINDOW, KV_W), seq_out),
                   pl.BlockSpec((1, WINDOW, KV_W), seq_out),
                   pl.BlockSpec((1, GK_W, GLA_DV), seq_out)],
        out_shape=[jax.ShapeDtypeStruct((batch, seq, d_model), F32),
                   jax.ShapeDtypeStruct((batch, WINDOW, KV_W), F32),
                   jax.ShapeDtypeStruct((batch, WINDOW, KV_W), F32),
                   jax.ShapeDtypeStruct((batch, GK_W, GLA_DV), F32)],
        scratch_shapes=[pltpu.VMEM((tile, ATT_W), BF16),
                        pltpu.VMEM((4, CHUNK + tile, LANES), BF16),
                        pltpu.VMEM((4, CHUNK + tile, LANES), BF16),
                        pltpu.VMEM((tile, GK_W), F32),
                        pltpu.VMEM((tile, GK_W), F32),
                        pltpu.VMEM((tile, GV_W), F32),
                        pltpu.VMEM((tile, GV_W), F32),
                        pltpu.VMEM((tile, GK_W), F32),
                        pltpu.VMEM((GK_W, GLA_DV), F32),
                        pltpu.VMEM((tile, d_model), BF16),
                        pltpu.VMEM((tile, d_model), F32),
                        pltpu.VMEM((tile, d_model), BF16)],
        compiler_params=pltpu.CompilerParams(
            dimension_semantics=("arbitrary",), vmem_limit_bytes=VMEM_LIMIT_BYTES),
        name="prompt_layer",
    )(sinks1, x_prompt, cos_p, sa_p, sb_p, bias, gmat_p, masks_p, *weights)

    levels_s = tuple(m for m in PROMPT_LEVELS if m < dec_seq)
    pos_s = PAST_LEN + jnp.arange(dec_seq, dtype=jnp.int32)
    cos_s, sa_s, sb_s = (jnp.tile(a, (SAMPLE_SEQS, 1)) for a in _rope_tables(pos_s))
    gmat_s, masks_s = _gla_constants(dec_seq, levels_s)
    bias_win, bias_new = _sample_attention_bias(dec_seq)
    rows = SAMPLE_SEQS * dec_seq
    n_g = dec_batch // SAMPLE_SEQS
    xs = x_sample.reshape(dec_batch * dec_seq, d_model)
    ck = cache_k[0].reshape(dec_batch, WINDOW, KV_W)
    cv = cache_v[0].reshape(dec_batch, WINDOW, KV_W)
    st = state_gla[0].reshape(dec_batch, GK_W, GLA_DV)
    seq_spec = pl.BlockSpec((SAMPLE_SEQS, WINDOW, KV_W), lambda i: (i, 0, 0))
    st_spec = pl.BlockSpec((SAMPLE_SEQS, GK_W, GLA_DV), lambda i: (i, 0, 0))
    row_spec = pl.BlockSpec((rows, d_model), lambda i: (i, 0))
    ys, ks, vs, ss = pl.pallas_call(
        functools.partial(_sample_kernel, dec_seq=dec_seq, levels=levels_s),
        grid=(n_g,),
        in_specs=[smem_spec, row_spec,
                  _const_spec(cos_s.shape), _const_spec(sa_s.shape), _const_spec(sb_s.shape),
                  _const_spec(gmat_s.shape), _const_spec(masks_s.shape),
                  _const_spec(bias_win.shape), _const_spec(bias_new.shape),
                  seq_spec, seq_spec, st_spec] + weight_specs,
        out_specs=[row_spec, seq_spec, seq_spec, st_spec],
        out_shape=[jax.ShapeDtypeStruct((dec_batch * dec_seq, d_model), F32),
                   jax.ShapeDtypeStruct((dec_batch, WINDOW, KV_W), F32),
                   jax.ShapeDtypeStruct((dec_batch, WINDOW, KV_W), F32),
                   jax.ShapeDtypeStruct((dec_batch, GK_W, GLA_DV), F32)],
        compiler_params=pltpu.CompilerParams(
            dimension_semantics=("arbitrary",), vmem_limit_bytes=VMEM_LIMIT_BYTES),
        name="sample_layer",
    )(sinks1, xs, cos_s, sa_s, sb_s, gmat_s, masks_s, bias_win, bias_new, ck, cv, st, *weights)

    head_shape = (ATT_KV_HEADS, HEAD_DIM)
    state_shape = (GLA_HEADS, GLA_DK, GLA_DV)
    return (yp,
            ys.reshape(dec_batch, dec_seq, d_model),
            kp.reshape((1, batch, WINDOW) + head_shape),
            vp.reshape((1, batch, WINDOW) + head_shape),
            sp.reshape((1, batch) + state_shape),
            ks.reshape((1, dec_batch, WINDOW) + head_shape),
            vs.reshape((1, dec_batch, WINDOW) + head_shape),
            ss.reshape((1, dec_batch) + state_shape))
```

```python
import functools
import math

import numpy as np
import jax
import jax.numpy as jnp
from jax import lax
from jax.experimental import pallas as pl
from jax.experimental.pallas import tpu as pltpu

F32 = jnp.float32
BF16 = jnp.bfloat16

HEAD_DIM = 64
ATT_HEADS = 8
ATT_KV_HEADS = 2
WINDOW = 128
PAST_LEN = 16384
ROPE_THETA = 10000.0
GLA_HEADS = 4
GLA_DK = 64
GLA_DV = 128
GLA_RANK = 16
GLA_GATE_TEMP = 16.0
EPS = 1e-6
ATT_W = ATT_HEADS * HEAD_DIM
KV_W = ATT_KV_HEADS * HEAD_DIM
GK_W = GLA_HEADS * GLA_DK
GV_W = GLA_HEADS * GLA_DV
COL_Q = 0
COL_K = COL_Q + ATT_W
COL_V = COL_K + KV_W
COL_GQ = COL_V + KV_W
COL_GK = COL_GQ + GK_W
COL_GV = COL_GK + GK_W
COL_GR = COL_GV + GV_W
MAIN_W = COL_GR + GV_W
QK_SCALE = HEAD_DIM ** -0.5
GLA_SCALE = GLA_DK ** -0.5

CHUNK = 128
LANES = 128
PROMPT_TILE = 256
SAMPLE_SEQS = 16
FF_CHUNK = 1024
PROMPT_LEVELS = (1, 2, 4, 8, 16, 32, 64)
BOUNDED_LOG_DECAY = 30.0
VMEM_LIMIT_BYTES = 56 * 1024 * 1024


def _dot(a, b):
    return jnp.dot(a, b, preferred_element_type=F32)


def _dot_nt(a, b):
    return lax.dot_general(a, b, (((1,), (1,)), ((), ())), preferred_element_type=F32)


def _dot_tn(a, b):
    return lax.dot_general(a, b, (((0,), (0,)), ((), ())), preferred_element_type=F32)


def _rms(x, g):
    ms = jnp.mean(x * x, axis=-1, keepdims=True)
    return x * lax.rsqrt(ms + EPS) * g


def _rope(x, cos, sin_a, sin_b):
    return x * cos + pltpu.roll(x, 96, 1) * sin_a + pltpu.roll(x, 32, 1) * sin_b


def _split3(x):
    hi = x.astype(BF16)
    r1 = x - hi.astype(F32)
    mid = r1.astype(BF16)
    lo = (r1 - mid.astype(F32)).astype(BF16)
    return hi, mid, lo


def _pre_stage(x, cos, sin_a, sin_b, g_pre, w_main, w_glr, w_gk2, b_gk):
    h = _rms(x, g_pre).astype(BF16)
    proj = _dot(h, w_main[...])
    glr = _dot(h, w_glr[...])
    z = _dot(glr.astype(BF16), w_gk2[...]) + b_gk
    log_a = (jnp.minimum(z, 0.0) - jnp.log1p(jnp.exp(-jnp.abs(z)))) * (1.0 / GLA_GATE_TEMP)
    q = jnp.concatenate(
        [_rope(proj[:, COL_Q + LANES * j:COL_Q + LANES * (j + 1)], cos, sin_a, sin_b) * QK_SCALE
         for j in range(ATT_W // LANES)], axis=1)
    k = _rope(proj[:, COL_K:COL_V], cos, sin_a, sin_b)
    v = proj[:, COL_V:COL_GQ]
    gq = proj[:, COL_GQ:COL_GK] * GLA_SCALE
    gk = proj[:, COL_GK:COL_GV]
    gv = proj[:, COL_GV:COL_GR]
    gr = proj[:, COL_GR:MAIN_W]
    return q, k, v, gq, gk, gv, gr, log_a


def _post_stage(x, mix_b, g_mix_post, g_ffn_pre, g_ffn_post, w_out, w_up, w_down):
    mix = _dot(mix_b, w_out[...])
    x1 = x + _rms(mix, g_mix_post)
    h2 = _rms(x1, g_ffn_pre).astype(BF16)
    f = None
    for j in range(w_up.shape[1] // FF_CHUNK):
        u = jnp.maximum(_dot(h2, w_up[:, FF_CHUNK * j:FF_CHUNK * (j + 1)]), 0.0)
        part = _dot((u * u).astype(BF16), w_down[FF_CHUNK * j:FF_CHUNK * (j + 1), :])
        f = part if f is None else f + part
    return x1 + _rms(f, g_ffn_post)


def _lo_hi_forms(a):
    lo = lax.broadcasted_iota(jnp.int32, a.shape, 1) < HEAD_DIM
    r = pltpu.roll(a, HEAD_DIM, 1)
    zero = jnp.zeros_like(a)
    return ((jnp.where(lo, a, zero), jnp.where(lo, zero, r)),
            (jnp.where(lo, r, zero), jnp.where(lo, zero, a)))


def _gla_exponents(log_a, gmat):
    hi, mid, lo = _split3(log_a)
    return _dot(gmat, hi) + _dot(gmat, mid) + _dot(gmat, lo)


def _head_lanes(h, rows):
    lane = lax.broadcasted_iota(jnp.int32, (rows, LANES), 1)
    e = h % 2
    return (lane >= GLA_DK * e) & (lane < GLA_DK * (e + 1))


def _gla_intra(gq, gk, log_a, gmat_lv, levels, masks_ref):
    expo = _gla_exponents(log_a, gmat_lv)
    row = lax.broadcasted_iota(jnp.int32, gq.shape, 0)
    terms = [(gq.astype(BF16), gk.astype(BF16), 0)]
    for li, m in enumerate(levels):
        ex = jnp.exp(expo[CHUNK * li:CHUNK * (li + 1)])
        z = (jnp.where((row & m) != 0, gq, gk) * ex).astype(BF16)
        terms.append((z, z, 1 + li))
    scores = []
    for h in range(GLA_HEADS):
        p = h // 2
        in_head = _head_lanes(h, gq.shape[0])
        acc = None
        for zq, zk, mi in terms:
            zq_p = zq[:, LANES * p:LANES * (p + 1)]
            zk_p = zk[:, LANES * p:LANES * (p + 1)]
            zk_h = jnp.where(in_head, zk_p, jnp.zeros_like(zk_p))
            t = _dot_nt(zq_p, zk_h) * masks_ref[mi]
            acc = t if acc is None else acc + t
        scores.append(acc)
    return jnp.concatenate(scores, axis=1)


def _gla_intra_bounded(q_dec, gk, b, causal):
    k_grow_t = jnp.transpose(gk * jnp.exp(-b)).astype(BF16)
    scores = []
    for h in range(GLA_HEADS):
        p = h // 2
        qd_p = q_dec[:, LANES * p:LANES * (p + 1)]
        qd_h = jnp.where(_head_lanes(h, q_dec.shape[0]), qd_p, jnp.zeros_like(qd_p))
        scores.append(_dot(qd_h, k_grow_t[LANES * p:LANES * (p + 1), :]) * causal)
    return jnp.concatenate(scores, axis=1)


def _gla_out_gate(o_h, gr_h, g_gla):
    on = _rms(o_h, g_gla)
    return on * (gr_h * (1.0 / (1.0 + jnp.exp(-gr_h))))


def _prompt_kernel(sinks_ref, x_ref, cos_ref, sa_ref, sb_ref, bias_ref, gmat_ref, masks_ref,
                   g_pre_ref, g_post_ref, g_fpre_ref, g_fpost_ref, g_gla_ref, b_gk_ref,
                   w_main_ref, w_glr_ref, w_gk2_ref, w_out_ref, w_up_ref, w_down_ref,
                   y_ref, kwin_ref, vwin_ref, sout_ref,
                   q_scr, kbd_scr, vbd_scr, gq_scr, gk_scr, gv_scr, gr_scr, la_scr, s_scr,
                   mix_scr, xp_scr, mixp_scr, *, tiles_per_seq, n_tiles):
    j = pl.program_id(0)
    t_step = lax.rem(jnp.minimum(j, n_tiles - 1), tiles_per_seq)
    tile = x_ref.shape[1]
    n_chunks = tile // CHUNK

    @pl.when(j == 0)
    def _():
        s_scr[...] = jnp.zeros_like(s_scr)
        kbd_scr[:, 0:CHUNK, :] = jnp.zeros((4, CHUNK, LANES), BF16)
        vbd_scr[:, 0:CHUNK, :] = jnp.zeros((4, CHUNK, LANES), BF16)
        xp_scr[...] = jnp.zeros_like(xp_scr)
        mixp_scr[...] = jnp.zeros_like(mixp_scr)

    x = x_ref[0]
    q, k, v, gq, gk, gv, gr, log_a = _pre_stage(
        x, cos_ref[...], sa_ref[...], sb_ref[...], g_pre_ref[...],
        w_main_ref, w_glr_ref, w_gk2_ref, b_gk_ref[...])
    q_scr[...] = q.astype(BF16)
    k_forms = _lo_hi_forms(k)
    v_forms = _lo_hi_forms(v)
    for g in range(ATT_KV_HEADS):
        for e in range(2):
            kbd_scr[2 * g + e, CHUNK:CHUNK + tile, :] = k_forms[g][e].astype(BF16)
            vbd_scr[2 * g + e, CHUNK:CHUNK + tile, :] = v_forms[g][e].astype(BF16)
    gq_scr[...] = gq
    gk_scr[...] = gk
    gv_scr[...] = gv
    gr_scr[...] = gr
    la_scr[...] = log_a

    kwin_ref[0] = k[tile - WINDOW:tile, :]
    vwin_ref[0] = v[tile - WINDOW:tile, :]

    chunk_decay = jnp.concatenate(
        [jnp.sum(log_a[CHUNK * c:CHUNK * (c + 1)], axis=0, keepdims=True) for c in range(n_chunks)], axis=0)
    bounded = jnp.min(chunk_decay) >= -BOUNDED_LOG_DECAY
    branch = jnp.where(j == n_tiles, 2, jnp.where(bounded, 0, 1))

    lo_lane = lax.broadcasted_iota(jnp.int32, (CHUNK, LANES), 1) < HEAD_DIM
    g_gla = g_gla_ref[...]

    def attention_chunk(c):
        rows = slice(CHUNK * c, CHUNK * (c + 1))
        rows2 = slice(CHUNK * c, CHUNK * (c + 2))
        bias = bias_ref[jnp.where(t_step == 0, 1, 0)] if c == 0 else bias_ref[0]
        scores = []
        for g in range(ATT_KV_HEADS):
            lhs = jnp.concatenate([q_scr[rows, 2 * LANES * g:2 * LANES * g + LANES],
                                   q_scr[rows, 2 * LANES * g + LANES:2 * LANES * (g + 1)]], axis=0)
            kcat = jnp.concatenate([kbd_scr[2 * g, rows2, :], kbd_scr[2 * g + 1, rows2, :]], axis=0)
            scores.append(_dot_nt(lhs, kcat) + bias)
        yield
        for g in range(ATT_KV_HEADS):
            s = scores[g]
            p_blocks = []
            inv = []
            for blk in range(2):
                p_row = []
                inv_row = []
                for e in range(2):
                    sub = s[CHUNK * blk:CHUNK * (blk + 1), 2 * CHUNK * e:2 * CHUNK * (e + 1)]
                    sink = sinks_ref[4 * g + 2 * blk + e]
                    m = jnp.maximum(jnp.max(sub, axis=-1, keepdims=True), sink)
                    p = jnp.exp(sub - m)
                    den = jnp.sum(p, axis=-1, keepdims=True) + jnp.exp(sink - m)
                    p_row.append(p.astype(BF16))
                    inv_row.append(1.0 / den)
                p_blocks.append(jnp.concatenate(p_row, axis=1))
                inv.append(inv_row)
            p_all = jnp.concatenate(p_blocks, axis=0)
            vcat = jnp.concatenate([vbd_scr[2 * g, rows2, :], vbd_scr[2 * g + 1, rows2, :]], axis=0)
            o = _dot(p_all, vcat)
            for blk in range(2):
                scale = jnp.where(lo_lane, inv[blk][0], inv[blk][1])
                c0 = 2 * LANES * g + LANES * blk
                mix_scr[rows, c0:c0 + LANES] = (o[CHUNK * blk:CHUNK * (blk + 1)] * scale).astype(BF16)
        yield

    def gla_chunk(c, state, bounded_decay):
        rows = slice(CHUNK * c, CHUNK * (c + 1))
        gq_c = gq_scr[rows, :]
        gk_c = gk_scr[rows, :]
        la_c = la_scr[rows, :]
        expo = _gla_exponents(la_c, gmat_ref[0:2 * CHUNK, :])
        b = expo[0:CHUNK]
        q_dec_f = gq_c * jnp.exp(b)
        q_dec = q_dec_f.astype(BF16)
        k_dec_t = jnp.transpose(gk_c * jnp.exp(expo[CHUNK:2 * CHUNK])).astype(BF16)
        decay_t = jnp.exp(jnp.transpose(jnp.broadcast_to(b[CHUNK - 1:CHUNK, :], b.shape)))
        yield
        if bounded_decay:
            scores = _gla_intra_bounded(q_dec_f, gk_c, b, masks_ref[len(PROMPT_LEVELS) + 1])
        else:
            scores = _gla_intra(gq_c, gk_c, la_c, gmat_ref[2 * CHUNK:, :], PROMPT_LEVELS, masks_ref)
        yield
        for h in range(GLA_HEADS):
            p_ = h // 2
            v_h = gv_scr[rows, GLA_DV * h:GLA_DV * (h + 1)].astype(BF16)
            qd_p = q_dec[:, LANES * p_:LANES * (p_ + 1)]
            qd_h = jnp.where(_head_lanes(h, CHUNK), qd_p, jnp.zeros_like(qd_p))
            s0_pair = jnp.concatenate([state[2 * p_], state[2 * p_ + 1]], axis=0).astype(BF16)
            o_h = _dot(scores[:, CHUNK * h:CHUNK * (h + 1)].astype(BF16), v_h) + _dot(qd_h, s0_pair)
            out_h = _gla_out_gate(o_h, gr_scr[rows, GLA_DV * h:GLA_DV * (h + 1)], g_gla)
            mix_scr[rows, ATT_W + GLA_DV * h:ATT_W + GLA_DV * (h + 1)] = out_h.astype(BF16)
        new_state = [decay_t[GLA_DK * h:GLA_DK * (h + 1), :] * state[h]
                     + _dot(k_dec_t[GLA_DK * h:GLA_DK * (h + 1), :],
                            gv_scr[rows, GLA_DV * h:GLA_DV * (h + 1)].astype(BF16))
                     for h in range(GLA_HEADS)]
        state[:] = new_state
        yield

    def output_stage():
        mix = _dot(mixp_scr[...], w_out_ref[...])
        yield
        x1 = xp_scr[...] + _rms(mix, g_post_ref[...])
        h2 = _rms(x1, g_fpre_ref[...]).astype(BF16)
        f = None
        for i in range(w_up_ref.shape[1] // FF_CHUNK):
            u = jnp.maximum(_dot(h2, w_up_ref[:, FF_CHUNK * i:FF_CHUNK * (i + 1)]), 0.0)
            yield
            part = _dot((u * u).astype(BF16), w_down_ref[FF_CHUNK * i:FF_CHUNK * (i + 1), :])
            f = part if f is None else f + part
            yield
        y_ref[0] = x1 + _rms(f, g_fpost_ref[...])

    def run_step(bounded_decay):
        keep = jnp.where(t_step == 0, 0.0, 1.0)
        state = [s_scr[GLA_DK * h:GLA_DK * (h + 1), :] * keep for h in range(GLA_HEADS)]
        out = output_stage()
        att = [attention_chunk(c) for c in range(n_chunks)]
        gla = [gla_chunk(c, state, bounded_decay) for c in range(n_chunks)]
        next(out)
        for c in range(n_chunks):
            next(att[c])
            next(gla[c])
        next(out)
        for c in range(n_chunks):
            next(att[c])
            next(gla[c])
        for c in range(n_chunks):
            next(out)
            next(gla[c])
        for _ in out:
            pass
        s_all = jnp.concatenate(state, axis=0)
        s_scr[...] = s_all
        sout_ref[0] = s_all
        kbd_scr[:, 0:CHUNK, :] = kbd_scr[:, tile:tile + CHUNK, :]
        vbd_scr[:, 0:CHUNK, :] = vbd_scr[:, tile:tile + CHUNK, :]
        xp_scr[...] = x_ref[0]
        mixp_scr[...] = mix_scr[...]

    def finish_last_tile():
        for _ in output_stage():
            pass

    lax.switch(branch, [lambda: run_step(True), lambda: run_step(False), finish_last_tile])


def _sample_kernel(sinks_ref, x_ref, cos_ref, sa_ref, sb_ref, gmat_ref, masks_ref, bias_win_ref, bias_new_ref,
                   ck_ref, cv_ref, st_ref,
                   g_pre_ref, g_post_ref, g_fpre_ref, g_fpost_ref, g_gla_ref, b_gk_ref,
                   w_main_ref, w_glr_ref, w_gk2_ref, w_out_ref, w_up_ref, w_down_ref,
                   y_ref, kwin_ref, vwin_ref, sout_ref,
                   *, dec_seq, levels):
    n_seqs = ck_ref.shape[0]
    x = x_ref[...]
    q, k, v, gq, gk, gv, gr, log_a = _pre_stage(
        x, cos_ref[...], sa_ref[...], sb_ref[...], g_pre_ref[...],
        w_main_ref, w_glr_ref, w_gk2_ref, b_gk_ref[...])

    kwin_ref[:, 0:WINDOW - dec_seq, :] = ck_ref[:, dec_seq:WINDOW, :]
    vwin_ref[:, 0:WINDOW - dec_seq, :] = cv_ref[:, dec_seq:WINDOW, :]
    kwin_ref[:, WINDOW - dec_seq:WINDOW, :] = k.reshape(n_seqs, dec_seq, KV_W)
    vwin_ref[:, WINDOW - dec_seq:WINDOW, :] = v.reshape(n_seqs, dec_seq, KV_W)

    def seq_rows(a, s):
        return a[dec_seq * s:dec_seq * (s + 1)]

    def gather_seq(per_head, s):
        return jnp.concatenate([seq_rows(a, s) for a in per_head], axis=0)

    def scatter_heads(per_seq, h):
        return jnp.concatenate([seq_rows(a, h) for a in per_seq], axis=0)

    lo_lane = lax.broadcasted_iota(jnp.int32, (CHUNK, LANES), 1) < HEAD_DIM
    q_heads = []
    for j in range(ATT_W // LANES):
        blk = q[:, LANES * j:LANES * (j + 1)]
        rolled = pltpu.roll(blk, HEAD_DIM, 1)
        g = (2 * j) // (ATT_HEADS // ATT_KV_HEADS)
        keep = lo_lane if g == 0 else jnp.logical_not(lo_lane)
        for e in range(2):
            q_heads.append(jnp.where(keep, blk if e == g else rolled, 0.0))
    s_win = [_dot_nt(gather_seq(q_heads, s), ck_ref[s]) for s in range(n_seqs)]
    k_b = k.astype(BF16)
    v_b = v.astype(BF16)
    bias_win = bias_win_ref[...]
    bias_new = bias_new_ref[...]
    p_win = []
    o_new = []
    inv_den = []
    for h in range(ATT_HEADS):
        sw = scatter_heads(s_win, h) + bias_win
        sn = _dot_nt(q_heads[h].astype(BF16), k_b) + bias_new
        sink = sinks_ref[h]
        m = jnp.maximum(jnp.maximum(jnp.max(sw, axis=-1, keepdims=True),
                                    jnp.max(sn, axis=-1, keepdims=True)), sink)
        pw = jnp.exp(sw - m)
        pn = jnp.exp(sn - m)
        den = (jnp.sum(pw, axis=-1, keepdims=True) + jnp.sum(pn, axis=-1, keepdims=True)
               + jnp.exp(sink - m))
        p_win.append(pw)
        o_new.append(_dot(pn.astype(BF16), v_b))
        inv_den.append(1.0 / den)
    o_win = [_dot(gather_seq(p_win, s), cv_ref[s]) for s in range(n_seqs)]
    o_heads = [(scatter_heads(o_win, h) + o_new[h]) * inv_den[h] for h in range(ATT_HEADS)]
    mix_cols = []
    for j in range(ATT_W // LANES):
        g = (2 * j) // (ATT_HEADS // ATT_KV_HEADS)
        first = o_heads[2 * j] if g == 0 else pltpu.roll(o_heads[2 * j], HEAD_DIM, 1)
        second = o_heads[2 * j + 1] if g == 1 else pltpu.roll(o_heads[2 * j + 1], HEAD_DIM, 1)
        mix_cols.append(jnp.where(lo_lane, first, second).astype(BF16))

    expo = _gla_exponents(log_a, gmat_ref[0:2 * CHUNK, :])
    b = expo[0:CHUNK]
    q_dec = gq * jnp.exp(b)
    k_dec_t = jnp.transpose(gk * jnp.exp(expo[CHUNK:2 * CHUNK]))
    b3 = b.reshape(n_seqs, dec_seq, GK_W)
    b_last = jnp.broadcast_to(b3[:, dec_seq - 1:dec_seq, :], b3.shape).reshape(CHUNK, GK_W)
    decay_t = jnp.exp(jnp.transpose(b_last))
    scores = _gla_intra(gq, gk, log_a, gmat_ref[2 * CHUNK:, :], levels, masks_ref)
    lane_k = lax.broadcasted_iota(jnp.int32, (dec_seq, GK_W), 1)
    lane_t = lax.broadcasted_iota(jnp.int32, (GLA_DK, LANES), 1)
    o_inter = []
    for s in range(n_seqs):
        qd = seq_rows(q_dec, s)
        lhs = jnp.concatenate(
            [jnp.where((lane_k >= GLA_DK * h) & (lane_k < GLA_DK * (h + 1)), qd, 0.0)
             for h in range(GLA_HEADS)], axis=0)
        o_inter.append(_dot(lhs, st_ref[s]))
    g_gla = g_gla_ref[...]
    for h in range(GLA_HEADS):
        cs = slice(GLA_DV * h, GLA_DV * (h + 1))
        hs = slice(GLA_DK * h, GLA_DK * (h + 1))
        v_h = gv[:, cs].astype(BF16)
        o_h = _dot(scores[:, CHUNK * h:CHUNK * (h + 1)].astype(BF16), v_h) + scatter_heads(o_inter, h)
        mix_cols.append(_gla_out_gate(o_h, gr[:, cs], g_gla).astype(BF16))
        kd_h = k_dec_t[hs, :]
        lhs = jnp.concatenate(
            [jnp.where((lane_t >= dec_seq * s) & (lane_t < dec_seq * (s + 1)), kd_h, 0.0).astype(BF16)
             for s in range(n_seqs)], axis=0)
        upd = _dot(lhs, v_h)
        for s in range(n_seqs):
            a_col = jnp.broadcast_to(decay_t[hs, dec_seq * s:dec_seq * s + 1], (GLA_DK, GLA_DV))
            sout_ref[s, hs, :] = a_col * st_ref[s, hs, :] + upd[GLA_DK * s:GLA_DK * (s + 1)]

    y_ref[...] = _post_stage(x, jnp.concatenate(mix_cols, axis=1), g_post_ref[...], g_fpre_ref[...],
                             g_fpost_ref[...], w_out_ref, w_up_ref, w_down_ref)


def _gla_constants(seg, levels):
    t = np.arange(CHUNK)[:, None]
    u = np.arange(CHUNK)[None, :]
    same = (t // seg) == (u // seg)
    blocks = [same & (u <= t), same & (u > t)]
    masks = [np.eye(CHUNK, dtype=bool)]
    for m in levels:
        ref = (t // (2 * m)) * (2 * m) + m - 1
        second = (t & m) != 0
        blocks.append(np.where(second, (u > ref) & (u <= t), (u > t) & (u <= ref)))
        masks.append(((t // (2 * m)) == (u // (2 * m))) & second & ((u & m) == 0))
    masks.append(blocks[0])
    gmat = jnp.asarray(np.concatenate(blocks, axis=0).astype(np.float32), dtype=BF16)
    return gmat, jnp.asarray(np.stack(masks).astype(np.float32))


def _attention_bias():
    t = (np.arange(2 * CHUNK) % CHUNK)[:, None]
    j = (np.arange(4 * CHUNK) % (2 * CHUNK))[None, :]
    band = (j >= t + 1) & (j <= t + WINDOW)
    first = band & (j >= CHUNK)
    neg = np.float32(-np.inf)
    return jnp.asarray(np.stack([np.where(band, np.float32(0), neg), np.where(first, np.float32(0), neg)]))


def _sample_attention_bias(dec_seq):
    r = np.arange(CHUNK)[:, None]
    c = np.arange(CHUNK)[None, :]
    t = r % dec_seq
    win = c > t
    new = ((r // dec_seq) == (c // dec_seq)) & ((c % dec_seq) <= t)
    neg = np.float32(-np.inf)
    zero = np.float32(0)
    return jnp.asarray(np.where(win, zero, neg)), jnp.asarray(np.where(new, zero, neg))


def _rope_tables(pos):
    half = HEAD_DIM // 2
    inv = ROPE_THETA ** (-jnp.arange(half, dtype=F32) / half)
    ang = pos.astype(F32)[:, None] * inv[None, :]
    cos = jnp.cos(ang)
    sin = jnp.sin(ang)
    zero = jnp.zeros_like(sin)
    reps = LANES // HEAD_DIM
    return (jnp.tile(cos, (1, 2 * reps)),
            jnp.tile(jnp.concatenate([-sin, zero], axis=1), (1, reps)),
            jnp.tile(jnp.concatenate([zero, sin], axis=1), (1, reps)))


def _const_spec(shape):
    zeros = (0,) * len(shape)
    return pl.BlockSpec(shape, lambda *_: zeros, pipeline_mode=pl.Buffered(1))


def kernel(x_prompt, x_sample, cache_k, cache_v, state_gla, w_in, w_gk2, b_gk, g_gla, sinks, w_out, g_mix_pre, g_mix_post, g_ffn_pre, g_ffn_post, w_up, w_down):
    depth = w_in.shape[0]
    assert depth == 1, "single trunk layer"
    batch, seq, d_model = x_prompt.shape
    dec_batch, dec_seq, _ = x_sample.shape
    d_ff = w_up.shape[2]
    assert seq % PROMPT_TILE == 0 and dec_batch % SAMPLE_SEQS == 0
    assert SAMPLE_SEQS * dec_seq == CHUNK and CHUNK % dec_seq == 0 and d_ff % FF_CHUNK == 0
    assert w_in.shape[2] == MAIN_W + GLA_RANK

    w_main = w_in[0, :, :MAIN_W].astype(BF16)
    w_glr = w_in[0, :, MAIN_W:].astype(BF16)
    w_gk2_b = w_gk2[0].astype(BF16)
    w_out_b = w_out[0].astype(BF16)
    w_up_b = w_up[0].astype(BF16)
    w_down_b = w_down[0].astype(BF16)
    g_pre = g_mix_pre[0][None, :]
    g_post = g_mix_post[0][None, :]
    g_fpre = g_ffn_pre[0][None, :]
    g_fpost = g_ffn_post[0][None, :]
    g_gla2 = g_gla[0][None, :]
    b_gk2 = b_gk[0][None, :]
    sinks1 = sinks[0]

    weights = (g_pre, g_post, g_fpre, g_fpost, g_gla2, b_gk2, w_main, w_glr, w_gk2_b, w_out_b, w_up_b, w_down_b)
    weight_specs = [_const_spec(w.shape) for w in weights]
    smem_spec = pl.BlockSpec(memory_space=pltpu.SMEM)

    cos_p, sa_p, sb_p = _rope_tables(jnp.arange(seq, dtype=jnp.int32))
    gmat_p, masks_p = _gla_constants(CHUNK, PROMPT_LEVELS)
    bias = _attention_bias()
    n_t = seq // PROMPT_TILE
    tile = PROMPT_TILE
    n_tiles = batch * n_t

    def cur_tile(j):
        return jnp.minimum(j, n_tiles - 1)

    def done_tile(j):
        return jnp.maximum(j - 1, 0)

    tab_spec = pl.BlockSpec((tile, LANES), lambda j: (cur_tile(j) % n_t, 0))
    seq_out = lambda j: (cur_tile(j) // n_t, 0, 0)
    yp, kp, vp, sp = pl.pallas_call(
        functools.partial(_prompt_kernel, tiles_per_seq=n_t, n_tiles=n_tiles),
        grid=(n_tiles + 1,),
        in_specs=[smem_spec,
                  pl.BlockSpec((1, tile, d_model), lambda j: (cur_tile(j) // n_t, cur_tile(j) % n_t, 0)),
                  tab_spec, tab_spec, tab_spec,
                  _const_spec(bias.shape), _const_spec(gmat_p.shape), _const_spec(masks_p.shape)] + weight_specs,
        out_specs=[pl.BlockSpec((1, tile, d_model), lambda j: (done_tile(j) // n_t, done_tile(j) % n_t, 0)),
                   pl.BlockSpec((1, WINDOW, KV_W), seq_out),
                   pl.BlockSpec((1, WINDOW, KV_W), seq_out),
                   pl.BlockSpec((1, GK_W, GLA_DV), seq_out)],
        out_shape=[jax.ShapeDtypeStruct((batch, seq, d_model), F32),
                   jax.ShapeDtypeStruct((batch, WINDOW, KV_W), F32),
                   jax.ShapeDtypeStruct((batch, WINDOW, KV_W), F32),
                   jax.ShapeDtypeStruct((batch, GK_W, GLA_DV), F32)],
        scratch_shapes=[pltpu.VMEM((tile, ATT_W), BF16),
                        pltpu.VMEM((4, CHUNK + tile, LANES), BF16),
                        pltpu.VMEM((4, CHUNK + tile, LANES), BF16),
                        pltpu.VMEM((tile, GK_W), F32),
                        pltpu.VMEM((tile, GK_W), F32),
                        pltpu.VMEM((tile, GV_W), F32),
                        pltpu.VMEM((tile, GV_W), F32),
                        pltpu.VMEM((tile, GK_W), F32),
                        pltpu.VMEM((GK_W, GLA_DV), F32),
                        pltpu.VMEM((tile, d_model), BF16),
                        pltpu.VMEM((tile, d_model), F32),
                        pltpu.VMEM((tile, d_model), BF16)],
        compiler_params=pltpu.CompilerParams(
            dimension_semantics=("arbitrary",), vmem_limit_bytes=VMEM_LIMIT_BYTES),
        name="prompt_layer",
    )(sinks1, x_prompt, cos_p, sa_p, sb_p, bias, gmat_p, masks_p, *weights)

    levels_s = tuple(m for m in PROMPT_LEVELS if m < dec_seq)
    pos_s = PAST_LEN + jnp.arange(dec_seq, dtype=jnp.int32)
    cos_s, sa_s, sb_s = (jnp.tile(a, (SAMPLE_SEQS, 1)) for a in _rope_tables(pos_s))
    gmat_s, masks_s = _gla_constants(dec_seq, levels_s)
    bias_win, bias_new = _sample_attention_bias(dec_seq)
    rows = SAMPLE_SEQS * dec_seq
    n_g = dec_batch // SAMPLE_SEQS
    xs = x_sample.reshape(dec_batch * dec_seq, d_model)
    ck = cache_k[0].reshape(dec_batch, WINDOW, KV_W)
    cv = cache_v[0].reshape(dec_batch, WINDOW, KV_W)
    st = state_gla[0].reshape(dec_batch, GK_W, GLA_DV)
    seq_spec = pl.BlockSpec((SAMPLE_SEQS, WINDOW, KV_W), lambda i: (i, 0, 0))
    st_spec = pl.BlockSpec((SAMPLE_SEQS, GK_W, GLA_DV), lambda i: (i, 0, 0))
    row_spec = pl.BlockSpec((rows, d_model), lambda i: (i, 0))
    ys, ks, vs, ss = pl.pallas_call(
        functools.partial(_sample_kernel, dec_seq=dec_seq, levels=levels_s),
        grid=(n_g,),
        in_specs=[smem_spec, row_spec,
                  _const_spec(cos_s.shape), _const_spec(sa_s.shape), _const_spec(sb_s.shape),
                  _const_spec(gmat_s.shape), _const_spec(masks_s.shape),
                  _const_spec(bias_win.shape), _const_spec(bias_new.shape),
                  seq_spec, seq_spec, st_spec] + weight_specs,
        out_specs=[row_spec, seq_spec, seq_spec, st_spec],
        out_shape=[jax.ShapeDtypeStruct((dec_batch * dec_seq, d_model), F32),
                   jax.ShapeDtypeStruct((dec_batch, WINDOW, KV_W), F32),
                   jax.ShapeDtypeStruct((dec_batch, WINDOW, KV_W), F32),
                   jax.ShapeDtypeStruct((dec_batch, GK_W, GLA_DV), F32)],
        compiler_params=pltpu.CompilerParams(
            dimension_semantics=("arbitrary",), vmem_limit_bytes=VMEM_LIMIT_BYTES),
        name="sample_layer",
    )(sinks1, xs, cos_s, sa_s, sb_s, gmat_s, masks_s, bias_win, bias_new, ck, cv, st, *weights)

    head_shape = (ATT_KV_HEADS, HEAD_DIM)
    state_shape = (GLA_HEADS, GLA_DK, GLA_DV)
    return (yp,
            ys.reshape(dec_batch, dec_seq, d_model),
            kp.reshape((1, batch, WINDOW) + head_shape),
            vp.reshape((1, batch, WINDOW) + head_shape),
            sp.reshape((1, batch) + state_shape),
            ks.reshape((1, dec_batch, WINDOW) + head_shape),
            vs.reshape((1, dec_batch, WINDOW) + head_shape),
            ss.reshape((1, dec_batch) + state_shape))
```

```python
import functools
import math

import numpy as np
import jax
import jax.numpy as jnp
from jax import lax
from jax.experimental import pallas as pl
from jax.experimental.pallas import tpu as pltpu

F32 = jnp.float32
BF16 = jnp.bfloat16

HEAD_DIM = 64
ATT_HEADS = 8
ATT_KV_HEADS = 2
WINDOW = 128
PAST_LEN = 16384
ROPE_THETA = 10000.0
GLA_HEADS = 4
GLA_DK = 64
GLA_DV = 128
GLA_RANK = 16
GLA_GATE_TEMP = 16.0
EPS = 1e-6
ATT_W = ATT_HEADS * HEAD_DIM
KV_W = ATT_KV_HEADS * HEAD_DIM
GK_W = GLA_HEADS * GLA_DK
GV_W = GLA_HEADS * GLA_DV
COL_Q = 0
COL_K = COL_Q + ATT_W
COL_V = COL_K + KV_W
COL_GQ = COL_V + KV_W
COL_GK = COL_GQ + GK_W
COL_GV = COL_GK + GK_W
COL_GR = COL_GV + GV_W
MAIN_W = COL_GR + GV_W
QK_SCALE = HEAD_DIM ** -0.5
GLA_SCALE = GLA_DK ** -0.5

CHUNK = 128
LANES = 128
PROMPT_TILE = 256
SAMPLE_SEQS = 16
FF_CHUNK = 1024
PROMPT_LEVELS = (1, 2, 4, 8, 16, 32, 64)
BOUNDED_LOG_DECAY = 30.0
VMEM_LIMIT_BYTES = 56 * 1024 * 1024


def _dot(a, b):
    return jnp.dot(a, b, preferred_element_type=F32)


def _dot_nt(a, b):
    return lax.dot_general(a, b, (((1,), (1,)), ((), ())), preferred_element_type=F32)


def _dot_tn(a, b):
    return lax.dot_general(a, b, (((0,), (0,)), ((), ())), preferred_element_type=F32)


def _rms(x, g):
    ms = jnp.mean(x * x, axis=-1, keepdims=True)
    return x * lax.rsqrt(ms + EPS) * g


def _rope(x, cos, sin_a, sin_b):
    return x * cos + pltpu.roll(x, 96, 1) * sin_a + pltpu.roll(x, 32, 1) * sin_b


def _split3(x):
    hi = x.astype(BF16)
    r1 = x - hi.astype(F32)
    mid = r1.astype(BF16)
    lo = (r1 - mid.astype(F32)).astype(BF16)
    return hi, mid, lo


def _pre_stage(x, cos, sin_a, sin_b, g_pre, w_main, w_glr, w_gk2, b_gk):
    h = _rms(x, g_pre).astype(BF16)
    proj = _dot(h, w_main[...])
    glr = _dot(h, w_glr[...])
    z = _dot(glr.astype(BF16), w_gk2[...]) + b_gk
    log_a = (jnp.minimum(z, 0.0) - jnp.log1p(jnp.exp(-jnp.abs(z)))) * (1.0 / GLA_GATE_TEMP)
    q = jnp.concatenate(
        [_rope(proj[:, COL_Q + LANES * j:COL_Q + LANES * (j + 1)], cos, sin_a, sin_b) * QK_SCALE
         for j in range(ATT_W // LANES)], axis=1)
    k = _rope(proj[:, COL_K:COL_V], cos, sin_a, sin_b)
    v = proj[:, COL_V:COL_GQ]
    gq = proj[:, COL_GQ:COL_GK] * GLA_SCALE
    gk = proj[:, COL_GK:COL_GV]
    gv = proj[:, COL_GV:COL_GR]
    gr = proj[:, COL_GR:MAIN_W]
    return q, k, v, gq, gk, gv, gr, log_a


def _post_stage(x, mix_b, g_mix_post, g_ffn_pre, g_ffn_post, w_out, w_up, w_down):
    mix = _dot(mix_b, w_out[...])
    x1 = x + _rms(mix, g_mix_post)
    h2 = _rms(x1, g_ffn_pre).astype(BF16)
    f = None
    for j in range(w_up.shape[1] // FF_CHUNK):
        u = jnp.maximum(_dot(h2, w_up[:, FF_CHUNK * j:FF_CHUNK * (j + 1)]), 0.0)
        part = _dot((u * u).astype(BF16), w_down[FF_CHUNK * j:FF_CHUNK * (j + 1), :])
        f = part if f is None else f + part
    return x1 + _rms(f, g_ffn_post)


def _lo_hi_forms(a):
    lo = lax.broadcasted_iota(jnp.int32, a.shape, 1) < HEAD_DIM
    r = pltpu.roll(a, HEAD_DIM, 1)
    zero = jnp.zeros_like(a)
    return ((jnp.where(lo, a, zero), jnp.where(lo, zero, r)),
            (jnp.where(lo, r, zero), jnp.where(lo, zero, a)))


def _gla_exponents(log_a, gmat):
    hi, mid, lo = _split3(log_a)
    return _dot(gmat, hi) + _dot(gmat, mid) + _dot(gmat, lo)


def _head_lanes(h, rows):
    lane = lax.broadcasted_iota(jnp.int32, (rows, LANES), 1)
    e = h % 2
    return (lane >= GLA_DK * e) & (lane < GLA_DK * (e + 1))


def _gla_intra(gq, gk, log_a, gmat_lv, levels, masks_ref):
    expo = _gla_exponents(log_a, gmat_lv)
    row = lax.broadcasted_iota(jnp.int32, gq.shape, 0)
    terms = [(gq.astype(BF16), gk.astype(BF16), 0)]
    for li, m in enumerate(levels):
        ex = jnp.exp(expo[CHUNK * li:CHUNK * (li + 1)])
        z = (jnp.where((row & m) != 0, gq, gk) * ex).astype(BF16)
        terms.append((z, z, 1 + li))
    scores = []
    for h in range(GLA_HEADS):
        p = h // 2
        in_head = _head_lanes(h, gq.shape[0])
        acc = None
        for zq, zk, mi in terms:
            zq_p = zq[:, LANES * p:LANES * (p + 1)]
            zk_p = zk[:, LANES * p:LANES * (p + 1)]
            zk_h = jnp.where(in_head, zk_p, jnp.zeros_like(zk_p))
            t = _dot_nt(zq_p, zk_h) * masks_ref[mi]
            acc = t if acc is None else acc + t
        scores.append(acc)
    return jnp.concatenate(scores, axis=1)


def _gla_intra_bounded(q_dec, gk, b, causal):
    k_grow_t = jnp.transpose(gk * jnp.exp(-b)).astype(BF16)
    scores = []
    for h in range(GLA_HEADS):
        p = h // 2
        qd_p = q_dec[:, LANES * p:LANES * (p + 1)]
        qd_h = jnp.where(_head_lanes(h, q_dec.shape[0]), qd_p, jnp.zeros_like(qd_p))
        scores.append(_dot(qd_h, k_grow_t[LANES * p:LANES * (p + 1), :]) * causal)
    return jnp.concatenate(scores, axis=1)


def _gla_out_gate(o_h, gr_h, g_gla):
    on = _rms(o_h, g_gla)
    return on * (gr_h * (1.0 / (1.0 + jnp.exp(-gr_h))))


def _prompt_kernel(sinks_ref, x_ref, cos_ref, sa_ref, sb_ref, bias_ref, gmat_ref, masks_ref,
                   g_pre_ref, g_post_ref, g_fpre_ref, g_fpost_ref, g_gla_ref, b_gk_ref,
                   w_main_ref, w_glr_ref, w_gk2_ref, w_out_ref, w_up_ref, w_down_ref,
                   y_ref, kwin_ref, vwin_ref, sout_ref,
                   q_scr, kbd_scr, vbd_scr, gq_scr, gk_scr, gv_scr, gr_scr, la_scr, s_scr,
                   mix_scr, mixp_scr, xp1_scr, xp2_scr, bounded_scr, *, tiles_per_seq, n_tiles):
    j = pl.program_id(0)
    t_step = lax.rem(jnp.clip(j - 1, 0, n_tiles - 1), tiles_per_seq)
    tile = x_ref.shape[1]
    n_chunks = tile // CHUNK

    @pl.when(j == 0)
    def _():
        for ref in (s_scr, q_scr, kbd_scr, vbd_scr, gq_scr, gk_scr, gv_scr, gr_scr, la_scr,
                    mixp_scr, xp1_scr, xp2_scr):
            ref[...] = jnp.zeros_like(ref)
        bounded_scr[0] = 1

    branch = jnp.where(j == n_tiles + 1, 2, jnp.where(bounded_scr[0] == 1, 0, 1))

    lo_lane = lax.broadcasted_iota(jnp.int32, (CHUNK, LANES), 1) < HEAD_DIM
    g_gla = g_gla_ref[...]

    def input_stage(results):
        x = x_ref[0]
        h = _rms(x, g_pre_ref[...]).astype(BF16)
        proj = _dot(h, w_main_ref[...])
        glr = _dot(h, w_glr_ref[...])
        yield
        cos, sin_a, sin_b = cos_ref[...], sa_ref[...], sb_ref[...]
        z = _dot(glr.astype(BF16), w_gk2_ref[...]) + b_gk_ref[...]
        log_a = (jnp.minimum(z, 0.0) - jnp.log1p(jnp.exp(-jnp.abs(z)))) * (1.0 / GLA_GATE_TEMP)
        q = jnp.concatenate(
            [_rope(proj[:, COL_Q + LANES * i:COL_Q + LANES * (i + 1)], cos, sin_a, sin_b) * QK_SCALE
             for i in range(ATT_W // LANES)], axis=1)
        k = _rope(proj[:, COL_K:COL_V], cos, sin_a, sin_b)
        v = proj[:, COL_V:COL_GQ]
        chunk_decay = jnp.concatenate(
            [jnp.sum(log_a[CHUNK * c:CHUNK * (c + 1)], axis=0, keepdims=True) for c in range(n_chunks)], axis=0)
        results.update(
            x=x, q=q.astype(BF16), k=k, v=v, k_forms=_lo_hi_forms(k), v_forms=_lo_hi_forms(v),
            gq=proj[:, COL_GQ:COL_GK] * GLA_SCALE, gk=proj[:, COL_GK:COL_GV],
            gv=proj[:, COL_GV:COL_GR], gr=proj[:, COL_GR:MAIN_W], log_a=log_a,
            bounded=jnp.min(chunk_decay) >= -BOUNDED_LOG_DECAY)
        yield

    def hand_over(results):
        kbd_scr[:, 0:CHUNK, :] = kbd_scr[:, tile:tile + CHUNK, :]
        vbd_scr[:, 0:CHUNK, :] = vbd_scr[:, tile:tile + CHUNK, :]
        for g in range(ATT_KV_HEADS):
            for e in range(2):
                kbd_scr[2 * g + e, CHUNK:CHUNK + tile, :] = results["k_forms"][g][e].astype(BF16)
                vbd_scr[2 * g + e, CHUNK:CHUNK + tile, :] = results["v_forms"][g][e].astype(BF16)
        q_scr[...] = results["q"]
        gq_scr[...] = results["gq"]
        gk_scr[...] = results["gk"]
        gv_scr[...] = results["gv"]
        gr_scr[...] = results["gr"]
        la_scr[...] = results["log_a"]
        bounded_scr[0] = jnp.where(results["bounded"], 1, 0)
        kwin_ref[0] = results["k"][tile - WINDOW:tile, :]
        vwin_ref[0] = results["v"][tile - WINDOW:tile, :]
        xp2_scr[...] = xp1_scr[...]
        xp1_scr[...] = results["x"]
        mixp_scr[...] = mix_scr[...]

    def attention_chunk(c):
        rows = slice(CHUNK * c, CHUNK * (c + 1))
        rows2 = slice(CHUNK * c, CHUNK * (c + 2))
        bias = bias_ref[jnp.where(t_step == 0, 1, 0)] if c == 0 else bias_ref[0]
        scores = []
        for g in range(ATT_KV_HEADS):
            lhs = jnp.concatenate([q_scr[rows, 2 * LANES * g:2 * LANES * g + LANES],
                                   q_scr[rows, 2 * LANES * g + LANES:2 * LANES * (g + 1)]], axis=0)
            kcat = jnp.concatenate([kbd_scr[2 * g, rows2, :], kbd_scr[2 * g + 1, rows2, :]], axis=0)
            scores.append(_dot_nt(lhs, kcat) + bias)
        yield
        for g in range(ATT_KV_HEADS):
            s = scores[g]
            p_blocks = []
            inv = []
            for blk in range(2):
                p_row = []
                inv_row = []
                for e in range(2):
                    sub = s[CHUNK * blk:CHUNK * (blk + 1), 2 * CHUNK * e:2 * CHUNK * (e + 1)]
                    sink = sinks_ref[4 * g + 2 * blk + e]
                    m = jnp.maximum(jnp.max(sub, axis=-1, keepdims=True), sink)
                    p = jnp.exp(sub - m)
                    den = jnp.sum(p, axis=-1, keepdims=True) + jnp.exp(sink - m)
                    p_row.append(p.astype(BF16))
                    inv_row.append(1.0 / den)
                p_blocks.append(jnp.concatenate(p_row, axis=1))
                inv.append(inv_row)
            p_all = jnp.concatenate(p_blocks, axis=0)
            vcat = jnp.concatenate([vbd_scr[2 * g, rows2, :], vbd_scr[2 * g + 1, rows2, :]], axis=0)
            o = _dot(p_all, vcat)
            for blk in range(2):
                scale = jnp.where(lo_lane, inv[blk][0], inv[blk][1])
                c0 = 2 * LANES * g + LANES * blk
                mix_scr[rows, c0:c0 + LANES] = (o[CHUNK * blk:CHUNK * (blk + 1)] * scale).astype(BF16)
        yield

    def gla_chunk(c, state, bounded_decay):
        rows = slice(CHUNK * c, CHUNK * (c + 1))
        gq_c = gq_scr[rows, :]
        gk_c = gk_scr[rows, :]
        la_c = la_scr[rows, :]
        expo = _gla_exponents(la_c, gmat_ref[0:2 * CHUNK, :])
        b = expo[0:CHUNK]
        q_dec_f = gq_c * jnp.exp(b)
        q_dec = q_dec_f.astype(BF16)
        k_dec_t = jnp.transpose(gk_c * jnp.exp(expo[CHUNK:2 * CHUNK])).astype(BF16)
        decay_t = jnp.exp(jnp.transpose(jnp.broadcast_to(b[CHUNK - 1:CHUNK, :], b.shape)))
        yield
        if bounded_decay:
            scores = _gla_intra_bounded(q_dec_f, gk_c, b, masks_ref[len(PROMPT_LEVELS) + 1])
        else:
            scores = _gla_intra(gq_c, gk_c, la_c, gmat_ref[2 * CHUNK:, :], PROMPT_LEVELS, masks_ref)
        yield
        for h in range(GLA_HEADS):
            p_ = h // 2
            v_h = gv_scr[rows, GLA_DV * h:GLA_DV * (h + 1)].astype(BF16)
            qd_p = q_dec[:, LANES * p_:LANES * (p_ + 1)]
            qd_h = jnp.where(_head_lanes(h, CHUNK), qd_p, jnp.zeros_like(qd_p))
            s0_pair = jnp.concatenate([state[2 * p_], state[2 * p_ + 1]], axis=0).astype(BF16)
            o_h = _dot(scores[:, CHUNK * h:CHUNK * (h + 1)].astype(BF16), v_h) + _dot(qd_h, s0_pair)
            out_h = _gla_out_gate(o_h, gr_scr[rows, GLA_DV * h:GLA_DV * (h + 1)], g_gla)
            mix_scr[rows, ATT_W + GLA_DV * h:ATT_W + GLA_DV * (h + 1)] = out_h.astype(BF16)
        new_state = [decay_t[GLA_DK * h:GLA_DK * (h + 1), :] * state[h]
                     + _dot(k_dec_t[GLA_DK * h:GLA_DK * (h + 1), :],
                            gv_scr[rows, GLA_DV * h:GLA_DV * (h + 1)].astype(BF16))
                     for h in range(GLA_HEADS)]
        state[:] = new_state
        yield

    def output_stage():
        mix = _dot(mixp_scr[...], w_out_ref[...])
        yield
        x1 = xp2_scr[...] + _rms(mix, g_post_ref[...])
        h2 = _rms(x1, g_fpre_ref[...]).astype(BF16)
        f = None
        for i in range(w_up_ref.shape[1] // FF_CHUNK):
            u = jnp.maximum(_dot(h2, w_up_ref[:, FF_CHUNK * i:FF_CHUNK * (i + 1)]), 0.0)
            yield
            part = _dot((u * u).astype(BF16), w_down_ref[FF_CHUNK * i:FF_CHUNK * (i + 1), :])
            f = part if f is None else f + part
            yield
        y_ref[0] = x1 + _rms(f, g_fpost_ref[...])

    def run_step(bounded_decay):
        keep = jnp.where(t_step == 0, 0.0, 1.0)
        state = [s_scr[GLA_DK * h:GLA_DK * (h + 1), :] * keep for h in range(GLA_HEADS)]
        results = {}
        out = output_stage()
        inp = input_stage(results)
        att = [attention_chunk(c) for c in range(n_chunks)]
        gla = [gla_chunk(c, state, bounded_decay) for c in range(n_chunks)]
        next(out)
        for c in range(n_chunks):
            next(att[c])
            next(gla[c])
        next(inp)
        for c in range(n_chunks):
            next(att[c])
            next(gla[c])
        next(out)
        for c in range(n_chunks):
            next(out)
            next(gla[c])
        next(inp)
        for _ in out:
            pass
        s_all = jnp.concatenate(state, axis=0)
        s_scr[...] = s_all
        sout_ref[0] = s_all
        hand_over(results)

    def finish_last_tile():
        for _ in output_stage():
            pass

    lax.switch(branch, [lambda: run_step(True), lambda: run_step(False), finish_last_tile])


def _sample_kernel(sinks_ref, x_ref, cos_ref, sa_ref, sb_ref, gmat_ref, masks_ref, bias_win_ref, bias_new_ref,
                   ck_ref, cv_ref, st_ref,
                   g_pre_ref, g_post_ref, g_fpre_ref, g_fpost_ref, g_gla_ref, b_gk_ref,
                   w_main_ref, w_glr_ref, w_gk2_ref, w_out_ref, w_up_ref, w_down_ref,
                   y_ref, kwin_ref, vwin_ref, sout_ref,
                   *, dec_seq, levels):
    n_seqs = ck_ref.shape[0]
    x = x_ref[...]
    q, k, v, gq, gk, gv, gr, log_a = _pre_stage(
        x, cos_ref[...], sa_ref[...], sb_ref[...], g_pre_ref[...],
        w_main_ref, w_glr_ref, w_gk2_ref, b_gk_ref[...])

    kwin_ref[:, 0:WINDOW - dec_seq, :] = ck_ref[:, dec_seq:WINDOW, :]
    vwin_ref[:, 0:WINDOW - dec_seq, :] = cv_ref[:, dec_seq:WINDOW, :]
    kwin_ref[:, WINDOW - dec_seq:WINDOW, :] = k.reshape(n_seqs, dec_seq, KV_W)
    vwin_ref[:, WINDOW - dec_seq:WINDOW, :] = v.reshape(n_seqs, dec_seq, KV_W)

    def seq_rows(a, s):
        return a[dec_seq * s:dec_seq * (s + 1)]

    def gather_seq(per_head, s):
        return jnp.concatenate([seq_rows(a, s) for a in per_head], axis=0)

    def scatter_heads(per_seq, h):
        return jnp.concatenate([seq_rows(a, h) for a in per_seq], axis=0)

    lo_lane = lax.broadcasted_iota(jnp.int32, (CHUNK, LANES), 1) < HEAD_DIM
    q_heads = []
    for j in range(ATT_W // LANES):
        blk = q[:, LANES * j:LANES * (j + 1)]
        rolled = pltpu.roll(blk, HEAD_DIM, 1)
        g = (2 * j) // (ATT_HEADS // ATT_KV_HEADS)
        keep = lo_lane if g == 0 else jnp.logical_not(lo_lane)
        for e in range(2):
            q_heads.append(jnp.where(keep, blk if e == g else rolled, 0.0))
    s_win = [_dot_nt(gather_seq(q_heads, s), ck_ref[s]) for s in range(n_seqs)]
    k_b = k.astype(BF16)
    v_b = v.astype(BF16)
    bias_win = bias_win_ref[...]
    bias_new = bias_new_ref[...]
    p_win = []
    o_new = []
    inv_den = []
    for h in range(ATT_HEADS):
        sw = scatter_heads(s_win, h) + bias_win
        sn = _dot_nt(q_heads[h].astype(BF16), k_b) + bias_new
        sink = sinks_ref[h]
        m = jnp.maximum(jnp.maximum(jnp.max(sw, axis=-1, keepdims=True),
                                    jnp.max(sn, axis=-1, keepdims=True)), sink)
        pw = jnp.exp(sw - m)
        pn = jnp.exp(sn - m)
        den = (jnp.sum(pw, axis=-1, keepdims=True) + jnp.sum(pn, axis=-1, keepdims=True)
               + jnp.exp(sink - m))
        p_win.append(pw)
        o_new.append(_dot(pn.astype(BF16), v_b))
        inv_den.append(1.0 / den)
    o_win = [_dot(gather_seq(p_win, s), cv_ref[s]) for s in range(n_seqs)]
    o_heads = [(scatter_heads(o_win, h) + o_new[h]) * inv_den[h] for h in range(ATT_HEADS)]
    mix_cols = []
    for j in range(ATT_W // LANES):
        g = (2 * j) // (ATT_HEADS // ATT_KV_HEADS)
        first = o_heads[2 * j] if g == 0 else pltpu.roll(o_heads[2 * j], HEAD_DIM, 1)
        second = o_heads[2 * j + 1] if g == 1 else pltpu.roll(o_heads[2 * j + 1], HEAD_DIM, 1)
        mix_cols.append(jnp.where(lo_lane, first, second).astype(BF16))

    expo = _gla_exponents(log_a, gmat_ref[0:2 * CHUNK, :])
    b = expo[0:CHUNK]
    q_dec = gq * jnp.exp(b)
    k_dec_t = jnp.transpose(gk * jnp.exp(expo[CHUNK:2 * CHUNK]))
    b3 = b.reshape(n_seqs, dec_seq, GK_W)
    b_last = jnp.broadcast_to(b3[:, dec_seq - 1:dec_seq, :], b3.shape).reshape(CHUNK, GK_W)
    decay_t = jnp.exp(jnp.transpose(b_last))
    scores = _gla_intra(gq, gk, log_a, gmat_ref[2 * CHUNK:, :], levels, masks_ref)
    lane_k = lax.broadcasted_iota(jnp.int32, (dec_seq, GK_W), 1)
    lane_t = lax.broadcasted_iota(jnp.int32, (GLA_DK, LANES), 1)
    o_inter = []
    for s in range(n_seqs):
        qd = seq_rows(q_dec, s)
        lhs = jnp.concatenate(
            [jnp.where((lane_k >= GLA_DK * h) & (lane_k < GLA_DK * (h + 1)), qd, 0.0)
             for h in range(GLA_HEADS)], axis=0)
        o_inter.append(_dot(lhs, st_ref[s]))
    g_gla = g_gla_ref[...]
    for h in range(GLA_HEADS):
        cs = slice(GLA_DV * h, GLA_DV * (h + 1))
        hs = slice(GLA_DK * h, GLA_DK * (h + 1))
        v_h = gv[:, cs].astype(BF16)
        o_h = _dot(scores[:, CHUNK * h:CHUNK * (h + 1)].astype(BF16), v_h) + scatter_heads(o_inter, h)
        mix_cols.append(_gla_out_gate(o_h, gr[:, cs], g_gla).astype(BF16))
        kd_h = k_dec_t[hs, :]
        lhs = jnp.concatenate(
            [jnp.where((lane_t >= dec_seq * s) & (lane_t < dec_seq * (s + 1)), kd_h, 0.0).astype(BF16)
             for s in range(n_seqs)], axis=0)
        upd = _dot(lhs, v_h)
        for s in range(n_seqs):
            a_col = jnp.broadcast_to(decay_t[hs, dec_seq * s:dec_seq * s + 1], (GLA_DK, GLA_DV))
            sout_ref[s, hs, :] = a_col * st_ref[s, hs, :] + upd[GLA_DK * s:GLA_DK * (s + 1)]

    y_ref[...] = _post_stage(x, jnp.concatenate(mix_cols, axis=1), g_post_ref[...], g_fpre_ref[...],
                             g_fpost_ref[...], w_out_ref, w_up_ref, w_down_ref)


def _gla_constants(seg, levels):
    t = np.arange(CHUNK)[:, None]
    u = np.arange(CHUNK)[None, :]
    same = (t // seg) == (u // seg)
    blocks = [same & (u <= t), same & (u > t)]
    masks = [np.eye(CHUNK, dtype=bool)]
    for m in levels:
        ref = (t // (2 * m)) * (2 * m) + m - 1
        second = (t & m) != 0
        blocks.append(np.where(second, (u > ref) & (u <= t), (u > t) & (u <= ref)))
        masks.append(((t // (2 * m)) == (u // (2 * m))) & second & ((u & m) == 0))
    masks.append(blocks[0])
    gmat = jnp.asarray(np.concatenate(blocks, axis=0).astype(np.float32), dtype=BF16)
    return gmat, jnp.asarray(np.stack(masks).astype(np.float32))


def _attention_bias():
    t = (np.arange(2 * CHUNK) % CHUNK)[:, None]
    j = (np.arange(4 * CHUNK) % (2 * CHUNK))[None, :]
    band = (j >= t + 1) & (j <= t + WINDOW)
    first = band & (j >= CHUNK)
    neg = np.float32(-np.inf)
    return jnp.asarray(np.stack([np.where(band, np.float32(0), neg), np.where(first, np.float32(0), neg)]))


def _sample_attention_bias(dec_seq):
    r = np.arange(CHUNK)[:, None]
    c = np.arange(CHUNK)[None, :]
    t = r % dec_seq
    win = c > t
    new = ((r // dec_seq) == (c // dec_seq)) & ((c % dec_seq) <= t)
    neg = np.float32(-np.inf)
    zero = np.float32(0)
    return jnp.asarray(np.where(win, zero, neg)), jnp.asarray(np.where(new, zero, neg))


def _rope_tables(pos):
    half = HEAD_DIM // 2
    inv = ROPE_THETA ** (-jnp.arange(half, dtype=F32) / half)
    ang = pos.astype(F32)[:, None] * inv[None, :]
    cos = jnp.cos(ang)
    sin = jnp.sin(ang)
    zero = jnp.zeros_like(sin)
    reps = LANES // HEAD_DIM
    return (jnp.tile(cos, (1, 2 * reps)),
            jnp.tile(jnp.concatenate([-sin, zero], axis=1), (1, reps)),
            jnp.tile(jnp.concatenate([zero, sin], axis=1), (1, reps)))


def _const_spec(shape):
    zeros = (0,) * len(shape)
    return pl.BlockSpec(shape, lambda *_: zeros, pipeline_mode=pl.Buffered(1))


def kernel(x_prompt, x_sample, cache_k, cache_v, state_gla, w_in, w_gk2, b_gk, g_gla, sinks, w_out, g_mix_pre, g_mix_post, g_ffn_pre, g_ffn_post, w_up, w_down):
    depth = w_in.shape[0]
    assert depth == 1, "single trunk layer"
    batch, seq, d_model = x_prompt.shape
    dec_batch, dec_seq, _ = x_sample.shape
    d_ff = w_up.shape[2]
    assert seq % PROMPT_TILE == 0 and dec_batch % SAMPLE_SEQS == 0
    assert SAMPLE_SEQS * dec_seq == CHUNK and CHUNK % dec_seq == 0 and d_ff % FF_CHUNK == 0
    assert w_in.shape[2] == MAIN_W + GLA_RANK

    w_main = w_in[0, :, :MAIN_W].astype(BF16)
    w_glr = w_in[0, :, MAIN_W:].astype(BF16)
    w_gk2_b = w_gk2[0].astype(BF16)
    w_out_b = w_out[0].astype(BF16)
    w_up_b = w_up[0].astype(BF16)
    w_down_b = w_down[0].astype(BF16)
    g_pre = g_mix_pre[0][None, :]
    g_post = g_mix_post[0][None, :]
    g_fpre = g_ffn_pre[0][None, :]
    g_fpost = g_ffn_post[0][None, :]
    g_gla2 = g_gla[0][None, :]
    b_gk2 = b_gk[0][None, :]
    sinks1 = sinks[0]

    weights = (g_pre, g_post, g_fpre, g_fpost, g_gla2, b_gk2, w_main, w_glr, w_gk2_b, w_out_b, w_up_b, w_down_b)
    weight_specs = [_const_spec(w.shape) for w in weights]
    smem_spec = pl.BlockSpec(memory_space=pltpu.SMEM)

    cos_p, sa_p, sb_p = _rope_tables(jnp.arange(seq, dtype=jnp.int32))
    gmat_p, masks_p = _gla_constants(CHUNK, PROMPT_LEVELS)
    bias = _attention_bias()
    n_t = seq // PROMPT_TILE
    tile = PROMPT_TILE
    n_tiles = batch * n_t

    def cur_tile(j):
        return jnp.minimum(j, n_tiles - 1)

    def mix_tile(j):
        return jnp.clip(j - 1, 0, n_tiles - 1)

    def done_tile(j):
        return jnp.clip(j - 2, 0, n_tiles - 1)

    tab_spec = pl.BlockSpec((tile, LANES), lambda j: (cur_tile(j) % n_t, 0))
    seq_out = lambda j: (cur_tile(j) // n_t, 0, 0)
    yp, kp, vp, sp = pl.pallas_call(
        functools.partial(_prompt_kernel, tiles_per_seq=n_t, n_tiles=n_tiles),
        grid=(n_tiles + 2,),
        in_specs=[smem_spec,
                  pl.BlockSpec((1, tile, d_model), lambda j: (cur_tile(j) // n_t, cur_tile(j) % n_t, 0)),
                  tab_spec, tab_spec, tab_spec,
                  _const_spec(bias.shape), _const_spec(gmat_p.shape), _const_spec(masks_p.shape)] + weight_specs,
        out_specs=[pl.BlockSpec((1, tile, d_model), lambda j: (done_tile(j) // n_t, done_tile(j) % n_t, 0)),
                   pl.BlockSpec((1, WINDOW, KV_W), seq_out),
                   pl.BlockSpec((1, WINDOW, KV_W), seq_out),
                   pl.BlockSpec((1, GK_W, GLA_DV), lambda j: (mix_tile(j) // n_t, 0, 0))],
        out_shape=[jax.ShapeDtypeStruct((batch, seq, d_model), F32),
                   jax.ShapeDtypeStruct((batch, WINDOW, KV_W), F32),
                   jax.ShapeDtypeStruct((batch, WINDOW, KV_W), F32),
                   jax.ShapeDtypeStruct((batch, GK_W, GLA_DV), F32)],
        scratch_shapes=[pltpu.VMEM((tile, ATT_W), BF16),
                        pltpu.VMEM((4, CHUNK + tile, LANES), BF16),
                        pltpu.VMEM((4, CHUNK + tile, LANES), BF16),
                        pltpu.VMEM((tile, GK_W), F32),
                        pltpu.VMEM((tile, GK_W), F32),
                        pltpu.VMEM((tile, GV_W), F32),
                        pltpu.VMEM((tile, GV_W), F32),
                        pltpu.VMEM((tile, GK_W), F32),
                        pltpu.VMEM((GK_W, GLA_DV), F32),
                        pltpu.VMEM((tile, d_model), BF16),
                        pltpu.VMEM((tile, d_model), BF16),
                        pltpu.VMEM((tile, d_model), F32),
                        pltpu.VMEM((tile, d_model), F32),
                        pltpu.SMEM((1,), jnp.int32)],
        compiler_params=pltpu.CompilerParams(
            dimension_semantics=("arbitrary",), vmem_limit_bytes=VMEM_LIMIT_BYTES),
        name="prompt_layer",
    )(sinks1, x_prompt, cos_p, sa_p, sb_p, bias, gmat_p, masks_p, *weights)

    levels_s = tuple(m for m in PROMPT_LEVELS if m < dec_seq)
    pos_s = PAST_LEN + jnp.arange(dec_seq, dtype=jnp.int32)
    cos_s, sa_s, sb_s = (jnp.tile(a, (SAMPLE_SEQS, 1)) for a in _rope_tables(pos_s))
    gmat_s, masks_s = _gla_constants(dec_seq, levels_s)
    bias_win, bias_new = _sample_attention_bias(dec_seq)
    rows = SAMPLE_SEQS * dec_seq
    n_g = dec_batch // SAMPLE_SEQS
    xs = x_sample.reshape(dec_batch * dec_seq, d_model)
    ck = cache_k[0].reshape(dec_batch, WINDOW, KV_W)
    cv = cache_v[0].reshape(dec_batch, WINDOW, KV_W)
    st = state_gla[0].reshape(dec_batch, GK_W, GLA_DV)
    seq_spec = pl.BlockSpec((SAMPLE_SEQS, WINDOW, KV_W), lambda i: (i, 0, 0))
    st_spec = pl.BlockSpec((SAMPLE_SEQS, GK_W, GLA_DV), lambda i: (i, 0, 0))
    row_spec = pl.BlockSpec((rows, d_model), lambda i: (i, 0))
    ys, ks, vs, ss = pl.pallas_call(
        functools.partial(_sample_kernel, dec_seq=dec_seq, levels=levels_s),
        grid=(n_g,),
        in_specs=[smem_spec, row_spec,
                  _const_spec(cos_s.shape), _const_spec(sa_s.shape), _const_spec(sb_s.shape),
                  _const_spec(gmat_s.shape), _const_spec(masks_s.shape),
                  _const_spec(bias_win.shape), _const_spec(bias_new.shape),
                  seq_spec, seq_spec, st_spec] + weight_specs,
        out_specs=[row_spec, seq_spec, seq_spec, st_spec],
        out_shape=[jax.ShapeDtypeStruct((dec_batch * dec_seq, d_model), F32),
                   jax.ShapeDtypeStruct((dec_batch, WINDOW, KV_W), F32),
                   jax.ShapeDtypeStruct((dec_batch, WINDOW, KV_W), F32),
                   jax.ShapeDtypeStruct((dec_batch, GK_W, GLA_DV), F32)],
        compiler_params=pltpu.CompilerParams(
            dimension_semantics=("arbitrary",), vmem_limit_bytes=VMEM_LIMIT_BYTES),
        name="sample_layer",
    )(sinks1, xs, cos_s, sa_s, sb_s, gmat_s, masks_s, bias_win, bias_new, ck, cv, st, *weights)

    head_shape = (ATT_KV_HEADS, HEAD_DIM)
    state_shape = (GLA_HEADS, GLA_DK, GLA_DV)
    return (yp,
            ys.reshape(dec_batch, dec_seq, d_model),
            kp.reshape((1, batch, WINDOW) + head_shape),
            vp.reshape((1, batch, WINDOW) + head_shape),
            sp.reshape((1, batch) + state_shape),
            ks.reshape((1, dec_batch, WINDOW) + head_shape),
            vs.reshape((1, dec_batch, WINDOW) + head_shape),
            ss.reshape((1, dec_batch) + state_shape))
```

```python
import functools
import math

import numpy as np
import jax
import jax.numpy as jnp
from jax import lax
from jax.experimental import pallas as pl
from jax.experimental.pallas import tpu as pltpu

F32 = jnp.float32
BF16 = jnp.bfloat16

HEAD_DIM = 64
ATT_HEADS = 8
ATT_KV_HEADS = 2
WINDOW = 128
PAST_LEN = 16384
ROPE_THETA = 10000.0
GLA_HEADS = 4
GLA_DK = 64
GLA_DV = 128
GLA_RANK = 16
GLA_GATE_TEMP = 16.0
EPS = 1e-6
ATT_W = ATT_HEADS * HEAD_DIM
KV_W = ATT_KV_HEADS * HEAD_DIM
GK_W = GLA_HEADS * GLA_DK
GV_W = GLA_HEADS * GLA_DV
COL_Q = 0
COL_K = COL_Q + ATT_W
COL_V = COL_K + KV_W
COL_GQ = COL_V + KV_W
COL_GK = COL_GQ + GK_W
COL_GV = COL_GK + GK_W
COL_GR = COL_GV + GV_W
MAIN_W = COL_GR + GV_W
QK_SCALE = HEAD_DIM ** -0.5
GLA_SCALE = GLA_DK ** -0.5

CHUNK = 128
LANES = 128
PROMPT_TILE = 256
SAMPLE_SEQS = 16
FF_CHUNK = 1024
PROMPT_LEVELS = (1, 2, 4, 8, 16, 32, 64)
BOUNDED_LOG_DECAY = 30.0
VMEM_LIMIT_BYTES = 56 * 1024 * 1024


def _dot(a, b):
    return jnp.dot(a, b, preferred_element_type=F32)


def _dot_nt(a, b):
    return lax.dot_general(a, b, (((1,), (1,)), ((), ())), preferred_element_type=F32)


def _dot_tn(a, b):
    return lax.dot_general(a, b, (((0,), (0,)), ((), ())), preferred_element_type=F32)


def _rms(x, g):
    ms = jnp.mean(x * x, axis=-1, keepdims=True)
    return x * lax.rsqrt(ms + EPS) * g


def _rope(x, cos, sin_a, sin_b):
    return x * cos + pltpu.roll(x, 96, 1) * sin_a + pltpu.roll(x, 32, 1) * sin_b


def _split3(x):
    hi = x.astype(BF16)
    r1 = x - hi.astype(F32)
    mid = r1.astype(BF16)
    lo = (r1 - mid.astype(F32)).astype(BF16)
    return hi, mid, lo


def _pre_stage(x, cos, sin_a, sin_b, g_pre, w_in, w_gk2, b_gk):
    h = _rms(x, g_pre).astype(BF16)
    proj = _dot(h, w_in[:, 0:MAIN_W])
    glr = _dot(h, w_in[:, MAIN_W:])
    z = _dot(glr.astype(BF16), w_gk2[...]) + b_gk
    log_a = (jnp.minimum(z, 0.0) - jnp.log1p(jnp.exp(-jnp.abs(z)))) * (1.0 / GLA_GATE_TEMP)
    q = jnp.concatenate(
        [_rope(proj[:, COL_Q + LANES * j:COL_Q + LANES * (j + 1)], cos, sin_a, sin_b) * QK_SCALE
         for j in range(ATT_W // LANES)], axis=1)
    k = _rope(proj[:, COL_K:COL_V], cos, sin_a, sin_b)
    v = proj[:, COL_V:COL_GQ]
    gq = proj[:, COL_GQ:COL_GK] * GLA_SCALE
    gk = proj[:, COL_GK:COL_GV]
    gv = proj[:, COL_GV:COL_GR]
    gr = proj[:, COL_GR:MAIN_W]
    return q, k, v, gq, gk, gv, gr, log_a


def _post_stage(x, mix_b, g_mix_post, g_ffn_pre, g_ffn_post, w_out, w_up, w_down):
    mix = _dot(mix_b, w_out[...])
    x1 = x + _rms(mix, g_mix_post)
    h2 = _rms(x1, g_ffn_pre).astype(BF16)
    f = None
    for j in range(w_up.shape[1] // FF_CHUNK):
        u = jnp.maximum(_dot(h2, w_up[:, FF_CHUNK * j:FF_CHUNK * (j + 1)]), 0.0)
        part = _dot((u * u).astype(BF16), w_down[FF_CHUNK * j:FF_CHUNK * (j + 1), :])
        f = part if f is None else f + part
    return x1 + _rms(f, g_ffn_post)


def _lo_hi_forms(a):
    lo = lax.broadcasted_iota(jnp.int32, a.shape, 1) < HEAD_DIM
    r = pltpu.roll(a, HEAD_DIM, 1)
    zero = jnp.zeros_like(a)
    return ((jnp.where(lo, a, zero), jnp.where(lo, zero, r)),
            (jnp.where(lo, r, zero), jnp.where(lo, zero, a)))


def _gla_exponents(log_a, gmat):
    hi, mid, lo = _split3(log_a)
    return _dot(gmat, hi) + _dot(gmat, mid) + _dot(gmat, lo)


def _head_lanes(h, rows):
    lane = lax.broadcasted_iota(jnp.int32, (rows, LANES), 1)
    e = h % 2
    return (lane >= GLA_DK * e) & (lane < GLA_DK * (e + 1))


def _gla_intra(gq, gk, log_a, gmat_lv, levels, masks_ref):
    expo = _gla_exponents(log_a, gmat_lv)
    row = lax.broadcasted_iota(jnp.int32, gq.shape, 0)
    terms = [(gq.astype(BF16), gk.astype(BF16), 0)]
    for li, m in enumerate(levels):
        ex = jnp.exp(expo[CHUNK * li:CHUNK * (li + 1)])
        z = (jnp.where((row & m) != 0, gq, gk) * ex).astype(BF16)
        terms.append((z, z, 1 + li))
    scores = []
    for h in range(GLA_HEADS):
        p = h // 2
        in_head = _head_lanes(h, gq.shape[0])
        acc = None
        for zq, zk, mi in terms:
            zq_p = zq[:, LANES * p:LANES * (p + 1)]
            zk_p = zk[:, LANES * p:LANES * (p + 1)]
            zk_h = jnp.where(in_head, zk_p, jnp.zeros_like(zk_p))
            t = _dot_nt(zq_p, zk_h) * masks_ref[mi]
            acc = t if acc is None else acc + t
        scores.append(acc)
    return jnp.concatenate(scores, axis=1)


def _gla_intra_bounded(q_dec, gk, b, causal):
    k_grow_t = jnp.transpose(gk * jnp.exp(-b)).astype(BF16)
    scores = []
    for h in range(GLA_HEADS):
        p = h // 2
        qd_p = q_dec[:, LANES * p:LANES * (p + 1)]
        qd_h = jnp.where(_head_lanes(h, q_dec.shape[0]), qd_p, jnp.zeros_like(qd_p))
        scores.append(_dot(qd_h, k_grow_t[LANES * p:LANES * (p + 1), :]) * causal)
    return jnp.concatenate(scores, axis=1)


def _gla_out_gate(o_h, gr_h, g_gla):
    on = _rms(o_h, g_gla)
    return on * (gr_h * (1.0 / (1.0 + jnp.exp(-gr_h))))


def _prompt_kernel(sinks_ref, x_ref, cos_ref, sa_ref, sb_ref, bias_ref, gmat_ref, masks_ref,
                   g_pre_ref, g_post_ref, g_fpre_ref, g_fpost_ref, g_gla_ref, b_gk_ref,
                   w_in_ref, w_gk2_ref, w_out_ref, w_up_ref, w_down_ref,
                   y_ref, kwin_ref, vwin_ref, sout_ref,
                   q_scr, kbd_scr, vbd_scr, gq_scr, gk_scr, gv_scr, gr_scr, la_scr, s_scr,
                   mix_scr, mixp_scr, xp1_scr, xp2_scr, bounded_scr, *, tiles_per_seq, n_tiles):
    j = pl.program_id(0)
    t_step = lax.rem(jnp.clip(j - 1, 0, n_tiles - 1), tiles_per_seq)
    tile = x_ref.shape[1]
    n_chunks = tile // CHUNK

    @pl.when(j == 0)
    def _():
        for ref in (s_scr, q_scr, kbd_scr, vbd_scr, gq_scr, gk_scr, gv_scr, gr_scr, la_scr,
                    mixp_scr, xp1_scr, xp2_scr):
            ref[...] = jnp.zeros_like(ref)
        bounded_scr[0] = 1

    branch = jnp.where(j == n_tiles + 1, 2, jnp.where(bounded_scr[0] == 1, 0, 1))

    lo_lane = lax.broadcasted_iota(jnp.int32, (CHUNK, LANES), 1) < HEAD_DIM
    g_gla = g_gla_ref[...]

    def input_stage(results):
        x = x_ref[0]
        h = _rms(x, g_pre_ref[...]).astype(BF16)
        proj = _dot(h, w_in_ref[:, 0:MAIN_W])
        glr = _dot(h, w_in_ref[:, MAIN_W:])
        yield
        cos, sin_a, sin_b = cos_ref[...], sa_ref[...], sb_ref[...]
        z = _dot(glr.astype(BF16), w_gk2_ref[...]) + b_gk_ref[...]
        log_a = (jnp.minimum(z, 0.0) - jnp.log1p(jnp.exp(-jnp.abs(z)))) * (1.0 / GLA_GATE_TEMP)
        q = jnp.concatenate(
            [_rope(proj[:, COL_Q + LANES * i:COL_Q + LANES * (i + 1)], cos, sin_a, sin_b) * QK_SCALE
             for i in range(ATT_W // LANES)], axis=1)
        k = _rope(proj[:, COL_K:COL_V], cos, sin_a, sin_b)
        v = proj[:, COL_V:COL_GQ]
        chunk_decay = jnp.concatenate(
            [jnp.sum(log_a[CHUNK * c:CHUNK * (c + 1)], axis=0, keepdims=True) for c in range(n_chunks)], axis=0)
        results.update(
            x=x, q=q.astype(BF16), k=k, v=v, k_forms=_lo_hi_forms(k), v_forms=_lo_hi_forms(v),
            gq=proj[:, COL_GQ:COL_GK] * GLA_SCALE, gk=proj[:, COL_GK:COL_GV],
            gv=proj[:, COL_GV:COL_GR], gr=proj[:, COL_GR:MAIN_W], log_a=log_a,
            bounded=jnp.min(chunk_decay) >= -BOUNDED_LOG_DECAY)
        yield

    def hand_over(results):
        kbd_scr[:, 0:CHUNK, :] = kbd_scr[:, tile:tile + CHUNK, :]
        vbd_scr[:, 0:CHUNK, :] = vbd_scr[:, tile:tile + CHUNK, :]
        for g in range(ATT_KV_HEADS):
            for e in range(2):
                kbd_scr[2 * g + e, CHUNK:CHUNK + tile, :] = results["k_forms"][g][e].astype(BF16)
                vbd_scr[2 * g + e, CHUNK:CHUNK + tile, :] = results["v_forms"][g][e].astype(BF16)
        q_scr[...] = results["q"]
        gq_scr[...] = results["gq"]
        gk_scr[...] = results["gk"]
        gv_scr[...] = results["gv"]
        gr_scr[...] = results["gr"]
        la_scr[...] = results["log_a"]
        bounded_scr[0] = jnp.where(results["bounded"], 1, 0)
        kwin_ref[0] = results["k"][tile - WINDOW:tile, :]
        vwin_ref[0] = results["v"][tile - WINDOW:tile, :]
        xp2_scr[...] = xp1_scr[...]
        xp1_scr[...] = results["x"]
        mixp_scr[...] = mix_scr[...]

    def attention_chunk(c):
        rows = slice(CHUNK * c, CHUNK * (c + 1))
        rows2 = slice(CHUNK * c, CHUNK * (c + 2))
        bias = bias_ref[jnp.where(t_step == 0, 1, 0)] if c == 0 else bias_ref[0]
        scores = []
        for g in range(ATT_KV_HEADS):
            lhs = jnp.concatenate([q_scr[rows, 2 * LANES * g:2 * LANES * g + LANES],
                                   q_scr[rows, 2 * LANES * g + LANES:2 * LANES * (g + 1)]], axis=0)
            kcat = jnp.concatenate([kbd_scr[2 * g, rows2, :], kbd_scr[2 * g + 1, rows2, :]], axis=0)
            scores.append(_dot_nt(lhs, kcat) + bias)
        yield
        for g in range(ATT_KV_HEADS):
            s = scores[g]
            p_blocks = []
            inv = []
            for blk in range(2):
                p_row = []
                inv_row = []
                for e in range(2):
                    sub = s[CHUNK * blk:CHUNK * (blk + 1), 2 * CHUNK * e:2 * CHUNK * (e + 1)]
                    sink = sinks_ref[4 * g + 2 * blk + e]
                    m = jnp.maximum(jnp.max(sub, axis=-1, keepdims=True), sink)
                    p = jnp.exp(sub - m)
                    den = jnp.sum(p, axis=-1, keepdims=True) + jnp.exp(sink - m)
                    p_row.append(p.astype(BF16))
                    inv_row.append(1.0 / den)
                p_blocks.append(jnp.concatenate(p_row, axis=1))
                inv.append(inv_row)
            p_all = jnp.concatenate(p_blocks, axis=0)
            vcat = jnp.concatenate([vbd_scr[2 * g, rows2, :], vbd_scr[2 * g + 1, rows2, :]], axis=0)
            o = _dot(p_all, vcat)
            for blk in range(2):
                scale = jnp.where(lo_lane, inv[blk][0], inv[blk][1])
                c0 = 2 * LANES * g + LANES * blk
                mix_scr[rows, c0:c0 + LANES] = (o[CHUNK * blk:CHUNK * (blk + 1)] * scale).astype(BF16)
        yield

    def gla_chunk(c, state, bounded_decay):
        rows = slice(CHUNK * c, CHUNK * (c + 1))
        gq_c = gq_scr[rows, :]
        gk_c = gk_scr[rows, :]
        la_c = la_scr[rows, :]
        expo = _gla_exponents(la_c, gmat_ref[0:2 * CHUNK, :])
        b = expo[0:CHUNK]
        q_dec_f = gq_c * jnp.exp(b)
        q_dec = q_dec_f.astype(BF16)
        k_dec_t = jnp.transpose(gk_c * jnp.exp(expo[CHUNK:2 * CHUNK])).astype(BF16)
        decay_t = jnp.exp(jnp.transpose(jnp.broadcast_to(b[CHUNK - 1:CHUNK, :], b.shape)))
        yield
        if bounded_decay:
            scores = _gla_intra_bounded(q_dec_f, gk_c, b, masks_ref[len(PROMPT_LEVELS) + 1])
        else:
            scores = _gla_intra(gq_c, gk_c, la_c, gmat_ref[2 * CHUNK:, :], PROMPT_LEVELS, masks_ref)
        yield
        for h in range(GLA_HEADS):
            p_ = h // 2
            v_h = gv_scr[rows, GLA_DV * h:GLA_DV * (h + 1)].astype(BF16)
            qd_p = q_dec[:, LANES * p_:LANES * (p_ + 1)]
            qd_h = jnp.where(_head_lanes(h, CHUNK), qd_p, jnp.zeros_like(qd_p))
            s0_pair = jnp.concatenate([state[2 * p_], state[2 * p_ + 1]], axis=0).astype(BF16)
            o_h = _dot(scores[:, CHUNK * h:CHUNK * (h + 1)].astype(BF16), v_h) + _dot(qd_h, s0_pair)
            out_h = _gla_out_gate(o_h, gr_scr[rows, GLA_DV * h:GLA_DV * (h + 1)], g_gla)
            mix_scr[rows, ATT_W + GLA_DV * h:ATT_W + GLA_DV * (h + 1)] = out_h.astype(BF16)
        new_state = [decay_t[GLA_DK * h:GLA_DK * (h + 1), :] * state[h]
                     + _dot(k_dec_t[GLA_DK * h:GLA_DK * (h + 1), :],
                            gv_scr[rows, GLA_DV * h:GLA_DV * (h + 1)].astype(BF16))
                     for h in range(GLA_HEADS)]
        state[:] = new_state
        yield

    def output_stage():
        mix = _dot(mixp_scr[...], w_out_ref[...])
        yield
        x1 = xp2_scr[...] + _rms(mix, g_post_ref[...])
        h2 = _rms(x1, g_fpre_ref[...]).astype(BF16)
        f = None
        for i in range(w_up_ref.shape[1] // FF_CHUNK):
            u = jnp.maximum(_dot(h2, w_up_ref[:, FF_CHUNK * i:FF_CHUNK * (i + 1)]), 0.0)
            yield
            part = _dot((u * u).astype(BF16), w_down_ref[FF_CHUNK * i:FF_CHUNK * (i + 1), :])
            f = part if f is None else f + part
            yield
        y_ref[0] = x1 + _rms(f, g_fpost_ref[...])

    def run_step(bounded_decay):
        keep = jnp.where(t_step == 0, 0.0, 1.0)
        state = [s_scr[GLA_DK * h:GLA_DK * (h + 1), :] * keep for h in range(GLA_HEADS)]
        results = {}
        out = output_stage()
        inp = input_stage(results)
        att = [attention_chunk(c) for c in range(n_chunks)]
        gla = [gla_chunk(c, state, bounded_decay) for c in range(n_chunks)]
        next(out)
        for c in range(n_chunks):
            next(att[c])
            next(gla[c])
        next(inp)
        for c in range(n_chunks):
            next(att[c])
            next(gla[c])
        next(out)
        for c in range(n_chunks):
            next(out)
            next(gla[c])
        next(inp)
        for _ in out:
            pass
        s_all = jnp.concatenate(state, axis=0)
        s_scr[...] = s_all
        sout_ref[0] = s_all
        hand_over(results)

    def finish_last_tile():
        for _ in output_stage():
            pass

    lax.switch(branch, [lambda: run_step(True), lambda: run_step(False), finish_last_tile])


def _sample_kernel(sinks_ref, x_ref, cos_ref, sa_ref, sb_ref, gmat_ref, masks_ref, bias_win_ref, bias_new_ref,
                   ck_ref, cv_ref, st_ref,
                   g_pre_ref, g_post_ref, g_fpre_ref, g_fpost_ref, g_gla_ref, b_gk_ref,
                   w_in_ref, w_gk2_ref, w_out_ref, w_up_ref, w_down_ref,
                   y_ref, kwin_ref, vwin_ref, sout_ref,
                   *, dec_seq, levels):
    n_seqs = ck_ref.shape[0]
    x = x_ref[...]
    q, k, v, gq, gk, gv, gr, log_a = _pre_stage(
        x, cos_ref[...], sa_ref[...], sb_ref[...], g_pre_ref[...],
        w_in_ref, w_gk2_ref, b_gk_ref[...])

    kwin_ref[:, 0:WINDOW - dec_seq, :] = ck_ref[:, dec_seq:WINDOW, :]
    vwin_ref[:, 0:WINDOW - dec_seq, :] = cv_ref[:, dec_seq:WINDOW, :]
    kwin_ref[:, WINDOW - dec_seq:WINDOW, :] = k.reshape(n_seqs, dec_seq, KV_W)
    vwin_ref[:, WINDOW - dec_seq:WINDOW, :] = v.reshape(n_seqs, dec_seq, KV_W)

    def seq_rows(a, s):
        return a[dec_seq * s:dec_seq * (s + 1)]

    def gather_seq(per_head, s):
        return jnp.concatenate([seq_rows(a, s) for a in per_head], axis=0)

    def scatter_heads(per_seq, h):
        return jnp.concatenate([seq_rows(a, h) for a in per_seq], axis=0)

    lo_lane = lax.broadcasted_iota(jnp.int32, (CHUNK, LANES), 1) < HEAD_DIM
    q_heads = []
    for j in range(ATT_W // LANES):
        blk = q[:, LANES * j:LANES * (j + 1)]
        rolled = pltpu.roll(blk, HEAD_DIM, 1)
        g = (2 * j) // (ATT_HEADS // ATT_KV_HEADS)
        keep = lo_lane if g == 0 else jnp.logical_not(lo_lane)
        for e in range(2):
            q_heads.append(jnp.where(keep, blk if e == g else rolled, 0.0))
    s_win = [_dot_nt(gather_seq(q_heads, s), ck_ref[s]) for s in range(n_seqs)]
    k_b = k.astype(BF16)
    v_b = v.astype(BF16)
    bias_win = bias_win_ref[...]
    bias_new = bias_new_ref[...]
    p_win = []
    o_new = []
    inv_den = []
    for h in range(ATT_HEADS):
        sw = scatter_heads(s_win, h) + bias_win
        sn = _dot_nt(q_heads[h].astype(BF16), k_b) + bias_new
        sink = sinks_ref[h]
        m = jnp.maximum(jnp.maximum(jnp.max(sw, axis=-1, keepdims=True),
                                    jnp.max(sn, axis=-1, keepdims=True)), sink)
        pw = jnp.exp(sw - m)
        pn = jnp.exp(sn - m)
        den = (jnp.sum(pw, axis=-1, keepdims=True) + jnp.sum(pn, axis=-1, keepdims=True)
               + jnp.exp(sink - m))
        p_win.append(pw)
        o_new.append(_dot(pn.astype(BF16), v_b))
        inv_den.append(1.0 / den)
    o_win = [_dot(gather_seq(p_win, s), cv_ref[s]) for s in range(n_seqs)]
    o_heads = [(scatter_heads(o_win, h) + o_new[h]) * inv_den[h] for h in range(ATT_HEADS)]
    mix_cols = []
    for j in range(ATT_W // LANES):
        g = (2 * j) // (ATT_HEADS // ATT_KV_HEADS)
        first = o_heads[2 * j] if g == 0 else pltpu.roll(o_heads[2 * j], HEAD_DIM, 1)
        second = o_heads[2 * j + 1] if g == 1 else pltpu.roll(o_heads[2 * j + 1], HEAD_DIM, 1)
        mix_cols.append(jnp.where(lo_lane, first, second).astype(BF16))

    expo = _gla_exponents(log_a, gmat_ref[0:2 * CHUNK, :])
    b = expo[0:CHUNK]
    q_dec = gq * jnp.exp(b)
    k_dec_t = jnp.transpose(gk * jnp.exp(expo[CHUNK:2 * CHUNK]))
    b3 = b.reshape(n_seqs, dec_seq, GK_W)
    b_last = jnp.broadcast_to(b3[:, dec_seq - 1:dec_seq, :], b3.shape).reshape(CHUNK, GK_W)
    decay_t = jnp.exp(jnp.transpose(b_last))
    scores = _gla_intra(gq, gk, log_a, gmat_ref[2 * CHUNK:, :], levels, masks_ref)
    lane_k = lax.broadcasted_iota(jnp.int32, (dec_seq, GK_W), 1)
    lane_t = lax.broadcasted_iota(jnp.int32, (GLA_DK, LANES), 1)
    o_inter = []
    for s in range(n_seqs):
        qd = seq_rows(q_dec, s)
        lhs = jnp.concatenate(
            [jnp.where((lane_k >= GLA_DK * h) & (lane_k < GLA_DK * (h + 1)), qd, 0.0)
             for h in range(GLA_HEADS)], axis=0)
        o_inter.append(_dot(lhs, st_ref[s]))
    g_gla = g_gla_ref[...]
    for h in range(GLA_HEADS):
        cs = slice(GLA_DV * h, GLA_DV * (h + 1))
        hs = slice(GLA_DK * h, GLA_DK * (h + 1))
        v_h = gv[:, cs].astype(BF16)
        o_h = _dot(scores[:, CHUNK * h:CHUNK * (h + 1)].astype(BF16), v_h) + scatter_heads(o_inter, h)
        mix_cols.append(_gla_out_gate(o_h, gr[:, cs], g_gla).astype(BF16))
        kd_h = k_dec_t[hs, :]
        lhs = jnp.concatenate(
            [jnp.where((lane_t >= dec_seq * s) & (lane_t < dec_seq * (s + 1)), kd_h, 0.0).astype(BF16)
             for s in range(n_seqs)], axis=0)
        upd = _dot(lhs, v_h)
        for s in range(n_seqs):
            a_col = jnp.broadcast_to(decay_t[hs, dec_seq * s:dec_seq * s + 1], (GLA_DK, GLA_DV))
            sout_ref[s, hs, :] = a_col * st_ref[s, hs, :] + upd[GLA_DK * s:GLA_DK * (s + 1)]

    y_ref[...] = _post_stage(x, jnp.concatenate(mix_cols, axis=1), g_post_ref[...], g_fpre_ref[...],
                             g_fpost_ref[...], w_out_ref, w_up_ref, w_down_ref)


def _gla_constants(seg, levels):
    t = np.arange(CHUNK)[:, None]
    u = np.arange(CHUNK)[None, :]
    same = (t // seg) == (u // seg)
    blocks = [same & (u <= t), same & (u > t)]
    masks = [np.eye(CHUNK, dtype=bool)]
    for m in levels:
        ref = (t // (2 * m)) * (2 * m) + m - 1
        second = (t & m) != 0
        blocks.append(np.where(second, (u > ref) & (u <= t), (u > t) & (u <= ref)))
        masks.append(((t // (2 * m)) == (u // (2 * m))) & second & ((u & m) == 0))
    masks.append(blocks[0])
    gmat = jnp.asarray(np.concatenate(blocks, axis=0).astype(np.float32), dtype=BF16)
    return gmat, jnp.asarray(np.stack(masks).astype(np.float32))


def _attention_bias():
    t = (np.arange(2 * CHUNK) % CHUNK)[:, None]
    j = (np.arange(4 * CHUNK) % (2 * CHUNK))[None, :]
    band = (j >= t + 1) & (j <= t + WINDOW)
    first = band & (j >= CHUNK)
    neg = np.float32(-np.inf)
    return jnp.asarray(np.stack([np.where(band, np.float32(0), neg), np.where(first, np.float32(0), neg)]))


def _sample_attention_bias(dec_seq):
    r = np.arange(CHUNK)[:, None]
    c = np.arange(CHUNK)[None, :]
    t = r % dec_seq
    win = c > t
    new = ((r // dec_seq) == (c // dec_seq)) & ((c % dec_seq) <= t)
    neg = np.float32(-np.inf)
    zero = np.float32(0)
    return jnp.asarray(np.where(win, zero, neg)), jnp.asarray(np.where(new, zero, neg))


def _rope_tables(pos):
    half = HEAD_DIM // 2
    inv = ROPE_THETA ** (-jnp.arange(half, dtype=F32) / half)
    lane = np.arange(LANES)
    ang = pos.astype(F32)[:, None] * inv[lane % half][None, :]
    sin = jnp.sin(ang)
    first_half = jnp.asarray((lane % HEAD_DIM) < half)[None, :]
    return (jnp.cos(ang),
            jnp.where(first_half, -sin, 0.0),
            jnp.where(first_half, 0.0, sin))


def _const_spec(shape):
    zeros = (0,) * len(shape)
    return pl.BlockSpec(shape, lambda *_: zeros, pipeline_mode=pl.Buffered(1))


def kernel(x_prompt, x_sample, cache_k, cache_v, state_gla, w_in, w_gk2, b_gk, g_gla, sinks, w_out, g_mix_pre, g_mix_post, g_ffn_pre, g_ffn_post, w_up, w_down):
    depth = w_in.shape[0]
    assert depth == 1, "single trunk layer"
    batch, seq, d_model = x_prompt.shape
    dec_batch, dec_seq, _ = x_sample.shape
    d_ff = w_up.shape[2]
    assert seq % PROMPT_TILE == 0 and dec_batch % SAMPLE_SEQS == 0
    assert SAMPLE_SEQS * dec_seq == CHUNK and CHUNK % dec_seq == 0 and d_ff % FF_CHUNK == 0
    assert w_in.shape[2] == MAIN_W + GLA_RANK

    w_in_b = w_in[0].astype(BF16)
    w_gk2_b = w_gk2[0].astype(BF16)
    w_out_b = w_out[0].astype(BF16)
    w_up_b = w_up[0].astype(BF16)
    w_down_b = w_down[0].astype(BF16)
    g_pre = g_mix_pre[0][None, :]
    g_post = g_mix_post[0][None, :]
    g_fpre = g_ffn_pre[0][None, :]
    g_fpost = g_ffn_post[0][None, :]
    g_gla2 = g_gla[0][None, :]
    b_gk2 = b_gk[0][None, :]
    sinks1 = sinks[0]

    weights = (g_pre, g_post, g_fpre, g_fpost, g_gla2, b_gk2, w_in_b, w_gk2_b, w_out_b, w_up_b, w_down_b)
    weight_specs = [_const_spec(w.shape) for w in weights]
    smem_spec = pl.BlockSpec(memory_space=pltpu.SMEM)

    cos_p, sa_p, sb_p = _rope_tables(jnp.arange(seq, dtype=jnp.int32))
    gmat_p, masks_p = _gla_constants(CHUNK, PROMPT_LEVELS)
    bias = _attention_bias()
    n_t = seq // PROMPT_TILE
    tile = PROMPT_TILE
    n_tiles = batch * n_t

    def cur_tile(j):
        return jnp.minimum(j, n_tiles - 1)

    def mix_tile(j):
        return jnp.clip(j - 1, 0, n_tiles - 1)

    def done_tile(j):
        return jnp.clip(j - 2, 0, n_tiles - 1)

    tab_spec = pl.BlockSpec((tile, LANES), lambda j: (cur_tile(j) % n_t, 0))
    seq_out = lambda j: (cur_tile(j) // n_t, 0, 0)
    yp, kp, vp, sp = pl.pallas_call(
        functools.partial(_prompt_kernel, tiles_per_seq=n_t, n_tiles=n_tiles),
        grid=(n_tiles + 2,),
        in_specs=[smem_spec,
                  pl.BlockSpec((1, tile, d_model), lambda j: (cur_tile(j) // n_t, cur_tile(j) % n_t, 0)),
                  tab_spec, tab_spec, tab_spec,
                  _const_spec(bias.shape), _const_spec(gmat_p.shape), _const_spec(masks_p.shape)] + weight_specs,
        out_specs=[pl.BlockSpec((1, tile, d_model), lambda j: (done_tile(j) // n_t, done_tile(j) % n_t, 0)),
                   pl.BlockSpec((1, WINDOW, KV_W), seq_out),
                   pl.BlockSpec((1, WINDOW, KV_W), seq_out),
                   pl.BlockSpec((1, GK_W, GLA_DV), lambda j: (mix_tile(j) // n_t, 0, 0))],
        out_shape=[jax.ShapeDtypeStruct((batch, seq, d_model), F32),
                   jax.ShapeDtypeStruct((batch, WINDOW, KV_W), F32),
                   jax.ShapeDtypeStruct((batch, WINDOW, KV_W), F32),
                   jax.ShapeDtypeStruct((batch, GK_W, GLA_DV), F32)],
        scratch_shapes=[pltpu.VMEM((tile, ATT_W), BF16),
                        pltpu.VMEM((4, CHUNK + tile, LANES), BF16),
                        pltpu.VMEM((4, CHUNK + tile, LANES), BF16),
                        pltpu.VMEM((tile, GK_W), F32),
                        pltpu.VMEM((tile, GK_W), F32),
                        pltpu.VMEM((tile, GV_W), F32),
                        pltpu.VMEM((tile, GV_W), F32),
                        pltpu.VMEM((tile, GK_W), F32),
                        pltpu.VMEM((GK_W, GLA_DV), F32),
                        pltpu.VMEM((tile, d_model), BF16),
                        pltpu.VMEM((tile, d_model), BF16),
                        pltpu.VMEM((tile, d_model), F32),
                        pltpu.VMEM((tile, d_model), F32),
                        pltpu.SMEM((1,), jnp.int32)],
        compiler_params=pltpu.CompilerParams(
            dimension_semantics=("arbitrary",), vmem_limit_bytes=VMEM_LIMIT_BYTES),
        name="prompt_layer",
    )(sinks1, x_prompt, cos_p, sa_p, sb_p, bias, gmat_p, masks_p, *weights)

    levels_s = tuple(m for m in PROMPT_LEVELS if m < dec_seq)
    pos_s = PAST_LEN + jnp.arange(dec_seq, dtype=jnp.int32)
    cos_s, sa_s, sb_s = (jnp.tile(a, (SAMPLE_SEQS, 1)) for a in _rope_tables(pos_s))
    gmat_s, masks_s = _gla_constants(dec_seq, levels_s)
    bias_win, bias_new = _sample_attention_bias(dec_seq)
    rows = SAMPLE_SEQS * dec_seq
    n_g = dec_batch // SAMPLE_SEQS
    xs = x_sample.reshape(dec_batch * dec_seq, d_model)
    ck = cache_k[0].reshape(dec_batch, WINDOW, KV_W)
    cv = cache_v[0].reshape(dec_batch, WINDOW, KV_W)
    st = state_gla[0].reshape(dec_batch, GK_W, GLA_DV)
    seq_spec = pl.BlockSpec((SAMPLE_SEQS, WINDOW, KV_W), lambda i: (i, 0, 0))
    st_spec = pl.BlockSpec((SAMPLE_SEQS, GK_W, GLA_DV), lambda i: (i, 0, 0))
    row_spec = pl.BlockSpec((rows, d_model), lambda i: (i, 0))
    ys, ks, vs, ss = pl.pallas_call(
        functools.partial(_sample_kernel, dec_seq=dec_seq, levels=levels_s),
        grid=(n_g,),
        in_specs=[smem_spec, row_spec,
                  _const_spec(cos_s.shape), _const_spec(sa_s.shape), _const_spec(sb_s.shape),
                  _const_spec(gmat_s.shape), _const_spec(masks_s.shape),
                  _const_spec(bias_win.shape), _const_spec(bias_new.shape),
                  seq_spec, seq_spec, st_spec] + weight_specs,
        out_specs=[row_spec, seq_spec, seq_spec, st_spec],
        out_shape=[jax.ShapeDtypeStruct((dec_batch * dec_seq, d_model), F32),
                   jax.ShapeDtypeStruct((dec_batch, WINDOW, KV_W), F32),
                   jax.ShapeDtypeStruct((dec_batch, WINDOW, KV_W), F32),
                   jax.ShapeDtypeStruct((dec_batch, GK_W, GLA_DV), F32)],
        compiler_params=pltpu.CompilerParams(
            dimension_semantics=("arbitrary",), vmem_limit_bytes=VMEM_LIMIT_BYTES),
        name="sample_layer",
    )(sinks1, xs, cos_s, sa_s, sb_s, gmat_s, masks_s, bias_win, bias_new, ck, cv, st, *weights)

    head_shape = (ATT_KV_HEADS, HEAD_DIM)
    state_shape = (GLA_HEADS, GLA_DK, GLA_DV)
    return (yp,
            ys.reshape(dec_batch, dec_seq, d_model),
            kp.reshape((1, batch, WINDOW) + head_shape),
            vp.reshape((1, batch, WINDOW) + head_shape),
            sp.reshape((1, batch) + state_shape),
            ks.reshape((1, dec_batch, WINDOW) + head_shape),
            vs.reshape((1, dec_batch, WINDOW) + head_shape),
            ss.reshape((1, dec_batch) + state_shape))
```

```python
import functools
import math

import numpy as np
import jax
import jax.numpy as jnp
from jax import lax
from jax.experimental import pallas as pl
from jax.experimental.pallas import tpu as pltpu

F32 = jnp.float32
BF16 = jnp.bfloat16

HEAD_DIM = 64
ATT_HEADS = 8
ATT_KV_HEADS = 2
WINDOW = 128
PAST_LEN = 16384
ROPE_THETA = 10000.0
GLA_HEADS = 4
GLA_DK = 64
GLA_DV = 128
GLA_RANK = 16
GLA_GATE_TEMP = 16.0
EPS = 1e-6
ATT_W = ATT_HEADS * HEAD_DIM
KV_W = ATT_KV_HEADS * HEAD_DIM
GK_W = GLA_HEADS * GLA_DK
GV_W = GLA_HEADS * GLA_DV
COL_Q = 0
COL_K = COL_Q + ATT_W
COL_V = COL_K + KV_W
COL_GQ = COL_V + KV_W
COL_GK = COL_GQ + GK_W
COL_GV = COL_GK + GK_W
COL_GR = COL_GV + GV_W
MAIN_W = COL_GR + GV_W
QK_SCALE = HEAD_DIM ** -0.5
GLA_SCALE = GLA_DK ** -0.5

CHUNK = 128
LANES = 128
PROMPT_TILE = 256
SAMPLE_SEQS = 16
FF_CHUNK = 1024
PROMPT_LEVELS = (1, 2, 4, 8, 16, 32, 64)
BOUNDED_LOG_DECAY = 30.0
VMEM_LIMIT_BYTES = 56 * 1024 * 1024


def _dot(a, b):
    return jnp.dot(a, b, preferred_element_type=F32)


def _dot_nt(a, b):
    return lax.dot_general(a, b, (((1,), (1,)), ((), ())), preferred_element_type=F32)


def _dot_tn(a, b):
    return lax.dot_general(a, b, (((0,), (0,)), ((), ())), preferred_element_type=F32)


def _rms(x, g):
    ms = jnp.mean(x * x, axis=-1, keepdims=True)
    return x * lax.rsqrt(ms + EPS) * g


def _rope(x, cos, sin_a, sin_b):
    return x * cos + pltpu.roll(x, 96, 1) * sin_a + pltpu.roll(x, 32, 1) * sin_b


def _rope_lanes(cos, sin):
    reps = LANES // cos.shape[1]
    cos_l = jnp.concatenate([cos] * reps, axis=1)
    sin_l = jnp.concatenate([sin] * reps, axis=1)
    lane = lax.broadcasted_iota(jnp.int32, cos_l.shape, 1)
    first_half = (lane & (HEAD_DIM // 2)) == 0
    return cos_l, jnp.where(first_half, -sin_l, 0.0), jnp.where(first_half, 0.0, sin_l)


def _split3(x):
    hi = x.astype(BF16)
    r1 = x - hi.astype(F32)
    mid = r1.astype(BF16)
    lo = (r1 - mid.astype(F32)).astype(BF16)
    return hi, mid, lo


def _pre_stage(x, cos, sin_a, sin_b, g_pre, w_in, w_gk2, b_gk):
    h = _rms(x, g_pre).astype(BF16)
    proj = _dot(h, w_in[:, 0:MAIN_W])
    glr = _dot(h, w_in[:, MAIN_W:])
    z = _dot(glr.astype(BF16), w_gk2[...]) + b_gk
    log_a = (jnp.minimum(z, 0.0) - jnp.log1p(jnp.exp(-jnp.abs(z)))) * (1.0 / GLA_GATE_TEMP)
    q = jnp.concatenate(
        [_rope(proj[:, COL_Q + LANES * j:COL_Q + LANES * (j + 1)], cos, sin_a, sin_b) * QK_SCALE
         for j in range(ATT_W // LANES)], axis=1)
    k = _rope(proj[:, COL_K:COL_V], cos, sin_a, sin_b)
    v = proj[:, COL_V:COL_GQ]
    gq = proj[:, COL_GQ:COL_GK] * GLA_SCALE
    gk = proj[:, COL_GK:COL_GV]
    gv = proj[:, COL_GV:COL_GR]
    gr = proj[:, COL_GR:MAIN_W]
    return q, k, v, gq, gk, gv, gr, log_a


def _post_stage(x, mix_b, g_mix_post, g_ffn_pre, g_ffn_post, w_out, w_up, w_down):
    mix = _dot(mix_b, w_out[...])
    x1 = x + _rms(mix, g_mix_post)
    h2 = _rms(x1, g_ffn_pre).astype(BF16)
    f = None
    for j in range(w_up.shape[1] // FF_CHUNK):
        u = jnp.maximum(_dot(h2, w_up[:, FF_CHUNK * j:FF_CHUNK * (j + 1)]), 0.0)
        part = _dot((u * u).astype(BF16), w_down[FF_CHUNK * j:FF_CHUNK * (j + 1), :])
        f = part if f is None else f + part
    return x1 + _rms(f, g_ffn_post)


def _lo_hi_forms(a):
    lo = lax.broadcasted_iota(jnp.int32, a.shape, 1) < HEAD_DIM
    r = pltpu.roll(a, HEAD_DIM, 1)
    zero = jnp.zeros_like(a)
    return ((jnp.where(lo, a, zero), jnp.where(lo, zero, r)),
            (jnp.where(lo, r, zero), jnp.where(lo, zero, a)))


def _gla_exponents(log_a, gmat):
    hi, mid, lo = _split3(log_a)
    return _dot(gmat, hi) + _dot(gmat, mid) + _dot(gmat, lo)


def _head_lanes(h, rows):
    lane = lax.broadcasted_iota(jnp.int32, (rows, LANES), 1)
    e = h % 2
    return (lane >= GLA_DK * e) & (lane < GLA_DK * (e + 1))


def _gla_intra(gq, gk, log_a, gmat_lv, levels, masks_ref):
    expo = _gla_exponents(log_a, gmat_lv)
    row = lax.broadcasted_iota(jnp.int32, gq.shape, 0)
    terms = [(gq.astype(BF16), gk.astype(BF16), 0)]
    for li, m in enumerate(levels):
        ex = jnp.exp(expo[CHUNK * li:CHUNK * (li + 1)])
        z = (jnp.where((row & m) != 0, gq, gk) * ex).astype(BF16)
        terms.append((z, z, 1 + li))
    scores = []
    for h in range(GLA_HEADS):
        p = h // 2
        in_head = _head_lanes(h, gq.shape[0])
        acc = None
        for zq, zk, mi in terms:
            zq_p = zq[:, LANES * p:LANES * (p + 1)]
            zk_p = zk[:, LANES * p:LANES * (p + 1)]
            zk_h = jnp.where(in_head, zk_p, jnp.zeros_like(zk_p))
            t = _dot_nt(zq_p, zk_h) * masks_ref[mi]
            acc = t if acc is None else acc + t
        scores.append(acc)
    return jnp.concatenate(scores, axis=1)


def _gla_intra_bounded(q_dec, gk, b, causal):
    k_grow_t = jnp.transpose(gk * jnp.exp(-b)).astype(BF16)
    scores = []
    for h in range(GLA_HEADS):
        p = h // 2
        qd_p = q_dec[:, LANES * p:LANES * (p + 1)]
        qd_h = jnp.where(_head_lanes(h, q_dec.shape[0]), qd_p, jnp.zeros_like(qd_p))
        scores.append(_dot(qd_h, k_grow_t[LANES * p:LANES * (p + 1), :]) * causal)
    return jnp.concatenate(scores, axis=1)


def _gla_out_gate(o_h, gr_h, g_gla):
    on = _rms(o_h, g_gla)
    return on * (gr_h * (1.0 / (1.0 + jnp.exp(-gr_h))))


def _prompt_kernel(sinks_ref, x_ref, cos_ref, sin_ref, bias_ref, gmat_ref, masks_ref,
                   g_pre_ref, g_post_ref, g_fpre_ref, g_fpost_ref, g_gla_ref, b_gk_ref,
                   w_in_ref, w_gk2_ref, w_out_ref, w_up_ref, w_down_ref,
                   y_ref, kwin_ref, vwin_ref, sout_ref,
                   q_scr, kbd_scr, vbd_scr, gq_scr, gk_scr, gv_scr, gr_scr, la_scr, s_scr,
                   mix_scr, mixp_scr, xp1_scr, xp2_scr, bounded_scr, *, tiles_per_seq, n_tiles):
    j = pl.program_id(0)
    t_step = lax.rem(jnp.clip(j - 1, 0, n_tiles - 1), tiles_per_seq)
    tile = x_ref.shape[1]
    n_chunks = tile // CHUNK

    @pl.when(j == 0)
    def _():
        for ref in (s_scr, q_scr, kbd_scr, vbd_scr, gq_scr, gk_scr, gv_scr, gr_scr, la_scr,
                    mixp_scr, xp1_scr, xp2_scr):
            ref[...] = jnp.zeros_like(ref)
        bounded_scr[0] = 1

    branch = jnp.where(j == n_tiles + 1, 2, jnp.where(bounded_scr[0] == 1, 0, 1))

    lo_lane = lax.broadcasted_iota(jnp.int32, (CHUNK, LANES), 1) < HEAD_DIM
    g_gla = g_gla_ref[...]

    def input_stage(results):
        x = x_ref[0]
        h = _rms(x, g_pre_ref[...]).astype(BF16)
        proj = _dot(h, w_in_ref[:, 0:MAIN_W])
        glr = _dot(h, w_in_ref[:, MAIN_W:])
        yield
        cos, sin_a, sin_b = _rope_lanes(cos_ref[...], sin_ref[...])
        z = _dot(glr.astype(BF16), w_gk2_ref[...]) + b_gk_ref[...]
        log_a = (jnp.minimum(z, 0.0) - jnp.log1p(jnp.exp(-jnp.abs(z)))) * (1.0 / GLA_GATE_TEMP)
        q = jnp.concatenate(
            [_rope(proj[:, COL_Q + LANES * i:COL_Q + LANES * (i + 1)], cos, sin_a, sin_b) * QK_SCALE
             for i in range(ATT_W // LANES)], axis=1)
        k = _rope(proj[:, COL_K:COL_V], cos, sin_a, sin_b)
        v = proj[:, COL_V:COL_GQ]
        chunk_decay = jnp.concatenate(
            [jnp.sum(log_a[CHUNK * c:CHUNK * (c + 1)], axis=0, keepdims=True) for c in range(n_chunks)], axis=0)
        results.update(
            x=x, q=q.astype(BF16), k=k, v=v, k_forms=_lo_hi_forms(k), v_forms=_lo_hi_forms(v),
            gq=proj[:, COL_GQ:COL_GK] * GLA_SCALE, gk=proj[:, COL_GK:COL_GV],
            gv=proj[:, COL_GV:COL_GR], gr=proj[:, COL_GR:MAIN_W], log_a=log_a,
            bounded=jnp.min(chunk_decay) >= -BOUNDED_LOG_DECAY)
        yield

    def hand_over(results):
        kbd_scr[:, 0:CHUNK, :] = kbd_scr[:, tile:tile + CHUNK, :]
        vbd_scr[:, 0:CHUNK, :] = vbd_scr[:, tile:tile + CHUNK, :]
        for g in range(ATT_KV_HEADS):
            for e in range(2):
                kbd_scr[2 * g + e, CHUNK:CHUNK + tile, :] = results["k_forms"][g][e].astype(BF16)
                vbd_scr[2 * g + e, CHUNK:CHUNK + tile, :] = results["v_forms"][g][e].astype(BF16)
        q_scr[...] = results["q"]
        gq_scr[...] = results["gq"]
        gk_scr[...] = results["gk"]
        gv_scr[...] = results["gv"]
        gr_scr[...] = results["gr"]
        la_scr[...] = results["log_a"]
        bounded_scr[0] = jnp.where(results["bounded"], 1, 0)
        kwin_ref[0] = results["k"][tile - WINDOW:tile, :]
        vwin_ref[0] = results["v"][tile - WINDOW:tile, :]
        xp2_scr[...] = xp1_scr[...]
        xp1_scr[...] = results["x"]
        mixp_scr[...] = mix_scr[...]

    def attention_chunk(c):
        rows = slice(CHUNK * c, CHUNK * (c + 1))
        rows2 = slice(CHUNK * c, CHUNK * (c + 2))
        bias = bias_ref[jnp.where(t_step == 0, 1, 0)] if c == 0 else bias_ref[0]
        scores = []
        for g in range(ATT_KV_HEADS):
            lhs = jnp.concatenate([q_scr[rows, 2 * LANES * g:2 * LANES * g + LANES],
                                   q_scr[rows, 2 * LANES * g + LANES:2 * LANES * (g + 1)]], axis=0)
            kcat = jnp.concatenate([kbd_scr[2 * g, rows2, :], kbd_scr[2 * g + 1, rows2, :]], axis=0)
            scores.append(_dot_nt(lhs, kcat) + bias)
        yield
        for g in range(ATT_KV_HEADS):
            s = scores[g]
            p_blocks = []
            inv = []
            for blk in range(2):
                p_row = []
                inv_row = []
                for e in range(2):
                    sub = s[CHUNK * blk:CHUNK * (blk + 1), 2 * CHUNK * e:2 * CHUNK * (e + 1)]
                    sink = sinks_ref[4 * g + 2 * blk + e]
                    m = jnp.maximum(jnp.max(sub, axis=-1, keepdims=True), sink)
                    p = jnp.exp(sub - m)
                    den = jnp.sum(p, axis=-1, keepdims=True) + jnp.exp(sink - m)
                    p_row.append(p.astype(BF16))
                    inv_row.append(1.0 / den)
                p_blocks.append(jnp.concatenate(p_row, axis=1))
                inv.append(inv_row)
            p_all = jnp.concatenate(p_blocks, axis=0)
            vcat = jnp.concatenate([vbd_scr[2 * g, rows2, :], vbd_scr[2 * g + 1, rows2, :]], axis=0)
            o = _dot(p_all, vcat)
            for blk in range(2):
                scale = jnp.where(lo_lane, inv[blk][0], inv[blk][1])
                c0 = 2 * LANES * g + LANES * blk
                mix_scr[rows, c0:c0 + LANES] = (o[CHUNK * blk:CHUNK * (blk + 1)] * scale).astype(BF16)
        yield

    def gla_chunk(c, state, bounded_decay):
        rows = slice(CHUNK * c, CHUNK * (c + 1))
        gq_c = gq_scr[rows, :]
        gk_c = gk_scr[rows, :]
        la_c = la_scr[rows, :]
        expo = _gla_exponents(la_c, gmat_ref[0:2 * CHUNK, :])
        b = expo[0:CHUNK]
        q_dec_f = gq_c * jnp.exp(b)
        q_dec = q_dec_f.astype(BF16)
        k_dec_t = jnp.transpose(gk_c * jnp.exp(expo[CHUNK:2 * CHUNK])).astype(BF16)
        decay_t = jnp.exp(jnp.transpose(jnp.broadcast_to(b[CHUNK - 1:CHUNK, :], b.shape)))
        yield
        if bounded_decay:
            scores = _gla_intra_bounded(q_dec_f, gk_c, b, masks_ref[len(PROMPT_LEVELS) + 1])
        else:
            scores = _gla_intra(gq_c, gk_c, la_c, gmat_ref[2 * CHUNK:, :], PROMPT_LEVELS, masks_ref)
        yield
        for h in range(GLA_HEADS):
            p_ = h // 2
            v_h = gv_scr[rows, GLA_DV * h:GLA_DV * (h + 1)].astype(BF16)
            qd_p = q_dec[:, LANES * p_:LANES * (p_ + 1)]
            qd_h = jnp.where(_head_lanes(h, CHUNK), qd_p, jnp.zeros_like(qd_p))
            s0_pair = jnp.concatenate([state[2 * p_], state[2 * p_ + 1]], axis=0).astype(BF16)
            o_h = _dot(scores[:, CHUNK * h:CHUNK * (h + 1)].astype(BF16), v_h) + _dot(qd_h, s0_pair)
            out_h = _gla_out_gate(o_h, gr_scr[rows, GLA_DV * h:GLA_DV * (h + 1)], g_gla)
            mix_scr[rows, ATT_W + GLA_DV * h:ATT_W + GLA_DV * (h + 1)] = out_h.astype(BF16)
        new_state = [decay_t[GLA_DK * h:GLA_DK * (h + 1), :] * state[h]
                     + _dot(k_dec_t[GLA_DK * h:GLA_DK * (h + 1), :],
                            gv_scr[rows, GLA_DV * h:GLA_DV * (h + 1)].astype(BF16))
                     for h in range(GLA_HEADS)]
        state[:] = new_state
        yield

    def output_stage():
        mix = _dot(mixp_scr[...], w_out_ref[...])
        yield
        x1 = xp2_scr[...] + _rms(mix, g_post_ref[...])
        h2 = _rms(x1, g_fpre_ref[...]).astype(BF16)
        f = None
        for i in range(w_up_ref.shape[1] // FF_CHUNK):
            u = jnp.maximum(_dot(h2, w_up_ref[:, FF_CHUNK * i:FF_CHUNK * (i + 1)]), 0.0)
            yield
            part = _dot((u * u).astype(BF16), w_down_ref[FF_CHUNK * i:FF_CHUNK * (i + 1), :])
            f = part if f is None else f + part
            yield
        y_ref[0] = x1 + _rms(f, g_fpost_ref[...])

    def run_step(bounded_decay):
        keep = jnp.where(t_step == 0, 0.0, 1.0)
        state = [s_scr[GLA_DK * h:GLA_DK * (h + 1), :] * keep for h in range(GLA_HEADS)]
        results = {}
        out = output_stage()
        inp = input_stage(results)
        att = [attention_chunk(c) for c in range(n_chunks)]
        gla = [gla_chunk(c, state, bounded_decay) for c in range(n_chunks)]
        next(out)
        for c in range(n_chunks):
            next(att[c])
            next(gla[c])
        next(inp)
        for c in range(n_chunks):
            next(att[c])
            next(gla[c])
        next(out)
        for c in range(n_chunks):
            next(out)
            next(gla[c])
        next(inp)
        for _ in out:
            pass
        s_all = jnp.concatenate(state, axis=0)
        s_scr[...] = s_all
        sout_ref[0] = s_all
        hand_over(results)

    def finish_last_tile():
        for _ in output_stage():
            pass

    lax.switch(branch, [lambda: run_step(True), lambda: run_step(False), finish_last_tile])


def _sample_kernel(sinks_ref, x_ref, cos_ref, sin_ref, gmat_ref, masks_ref, bias_win_ref, bias_new_ref,
                   ck_ref, cv_ref, st_ref,
                   g_pre_ref, g_post_ref, g_fpre_ref, g_fpost_ref, g_gla_ref, b_gk_ref,
                   w_in_ref, w_gk2_ref, w_out_ref, w_up_ref, w_down_ref,
                   y_ref, kwin_ref, vwin_ref, sout_ref,
                   *, dec_seq, levels):
    n_seqs = ck_ref.shape[0]
    x = x_ref[...]
    q, k, v, gq, gk, gv, gr, log_a = _pre_stage(
        x, *_rope_lanes(cos_ref[...], sin_ref[...]), g_pre_ref[...],
        w_in_ref, w_gk2_ref, b_gk_ref[...])

    kwin_ref[:, 0:WINDOW - dec_seq, :] = ck_ref[:, dec_seq:WINDOW, :]
    vwin_ref[:, 0:WINDOW - dec_seq, :] = cv_ref[:, dec_seq:WINDOW, :]
    kwin_ref[:, WINDOW - dec_seq:WINDOW, :] = k.reshape(n_seqs, dec_seq, KV_W)
    vwin_ref[:, WINDOW - dec_seq:WINDOW, :] = v.reshape(n_seqs, dec_seq, KV_W)

    def seq_rows(a, s):
        return a[dec_seq * s:dec_seq * (s + 1)]

    def gather_seq(per_head, s):
        return jnp.concatenate([seq_rows(a, s) for a in per_head], axis=0)

    def scatter_heads(per_seq, h):
        return jnp.concatenate([seq_rows(a, h) for a in per_seq], axis=0)

    lo_lane = lax.broadcasted_iota(jnp.int32, (CHUNK, LANES), 1) < HEAD_DIM
    q_heads = []
    for j in range(ATT_W // LANES):
        blk = q[:, LANES * j:LANES * (j + 1)]
        rolled = pltpu.roll(blk, HEAD_DIM, 1)
        g = (2 * j) // (ATT_HEADS // ATT_KV_HEADS)
        keep = lo_lane if g == 0 else jnp.logical_not(lo_lane)
        for e in range(2):
            q_heads.append(jnp.where(keep, blk if e == g else rolled, 0.0))
    s_win = [_dot_nt(gather_seq(q_heads, s), ck_ref[s]) for s in range(n_seqs)]
    k_b = k.astype(BF16)
    v_b = v.astype(BF16)
    bias_win = bias_win_ref[...]
    bias_new = bias_new_ref[...]
    p_win = []
    o_new = []
    inv_den = []
    for h in range(ATT_HEADS):
        sw = scatter_heads(s_win, h) + bias_win
        sn = _dot_nt(q_heads[h].astype(BF16), k_b) + bias_new
        sink = sinks_ref[h]
        m = jnp.maximum(jnp.maximum(jnp.max(sw, axis=-1, keepdims=True),
                                    jnp.max(sn, axis=-1, keepdims=True)), sink)
        pw = jnp.exp(sw - m)
        pn = jnp.exp(sn - m)
        den = (jnp.sum(pw, axis=-1, keepdims=True) + jnp.sum(pn, axis=-1, keepdims=True)
               + jnp.exp(sink - m))
        p_win.append(pw)
        o_new.append(_dot(pn.astype(BF16), v_b))
        inv_den.append(1.0 / den)
    o_win = [_dot(gather_seq(p_win, s), cv_ref[s]) for s in range(n_seqs)]
    o_heads = [(scatter_heads(o_win, h) + o_new[h]) * inv_den[h] for h in range(ATT_HEADS)]
    mix_cols = []
    for j in range(ATT_W // LANES):
        g = (2 * j) // (ATT_HEADS // ATT_KV_HEADS)
        first = o_heads[2 * j] if g == 0 else pltpu.roll(o_heads[2 * j], HEAD_DIM, 1)
        second = o_heads[2 * j + 1] if g == 1 else pltpu.roll(o_heads[2 * j + 1], HEAD_DIM, 1)
        mix_cols.append(jnp.where(lo_lane, first, second).astype(BF16))

    expo = _gla_exponents(log_a, gmat_ref[0:2 * CHUNK, :])
    b = expo[0:CHUNK]
    q_dec = gq * jnp.exp(b)
    k_dec_t = jnp.transpose(gk * jnp.exp(expo[CHUNK:2 * CHUNK]))
    b3 = b.reshape(n_seqs, dec_seq, GK_W)
    b_last = jnp.broadcast_to(b3[:, dec_seq - 1:dec_seq, :], b3.shape).reshape(CHUNK, GK_W)
    decay_t = jnp.exp(jnp.transpose(b_last))
    scores = _gla_intra(gq, gk, log_a, gmat_ref[2 * CHUNK:, :], levels, masks_ref)
    lane_k = lax.broadcasted_iota(jnp.int32, (dec_seq, GK_W), 1)
    lane_t = lax.broadcasted_iota(jnp.int32, (GLA_DK, LANES), 1)
    o_inter = []
    for s in range(n_seqs):
        qd = seq_rows(q_dec, s)
        lhs = jnp.concatenate(
            [jnp.where((lane_k >= GLA_DK * h) & (lane_k < GLA_DK * (h + 1)), qd, 0.0)
             for h in range(GLA_HEADS)], axis=0)
        o_inter.append(_dot(lhs, st_ref[s]))
    g_gla = g_gla_ref[...]
    for h in range(GLA_HEADS):
        cs = slice(GLA_DV * h, GLA_DV * (h + 1))
        hs = slice(GLA_DK * h, GLA_DK * (h + 1))
        v_h = gv[:, cs].astype(BF16)
        o_h = _dot(scores[:, CHUNK * h:CHUNK * (h + 1)].astype(BF16), v_h) + scatter_heads(o_inter, h)
        mix_cols.append(_gla_out_gate(o_h, gr[:, cs], g_gla).astype(BF16))
        kd_h = k_dec_t[hs, :]
        lhs = jnp.concatenate(
            [jnp.where((lane_t >= dec_seq * s) & (lane_t < dec_seq * (s + 1)), kd_h, 0.0).astype(BF16)
             for s in range(n_seqs)], axis=0)
        upd = _dot(lhs, v_h)
        for s in range(n_seqs):
            a_col = jnp.broadcast_to(decay_t[hs, dec_seq * s:dec_seq * s + 1], (GLA_DK, GLA_DV))
            sout_ref[s, hs, :] = a_col * st_ref[s, hs, :] + upd[GLA_DK * s:GLA_DK * (s + 1)]

    y_ref[...] = _post_stage(x, jnp.concatenate(mix_cols, axis=1), g_post_ref[...], g_fpre_ref[...],
                             g_fpost_ref[...], w_out_ref, w_up_ref, w_down_ref)


def _gla_constants(seg, levels):
    t = np.arange(CHUNK)[:, None]
    u = np.arange(CHUNK)[None, :]
    same = (t // seg) == (u // seg)
    blocks = [same & (u <= t), same & (u > t)]
    masks = [np.eye(CHUNK, dtype=bool)]
    for m in levels:
        ref = (t // (2 * m)) * (2 * m) + m - 1
        second = (t & m) != 0
        blocks.append(np.where(second, (u > ref) & (u <= t), (u > t) & (u <= ref)))
        masks.append(((t // (2 * m)) == (u // (2 * m))) & second & ((u & m) == 0))
    masks.append(blocks[0])
    gmat = jnp.asarray(np.concatenate(blocks, axis=0).astype(np.float32), dtype=BF16)
    return gmat, jnp.asarray(np.stack(masks).astype(np.float32))


def _attention_bias():
    t = (np.arange(2 * CHUNK) % CHUNK)[:, None]
    j = (np.arange(4 * CHUNK) % (2 * CHUNK))[None, :]
    band = (j >= t + 1) & (j <= t + WINDOW)
    first = band & (j >= CHUNK)
    neg = np.float32(-np.inf)
    return jnp.asarray(np.stack([np.where(band, np.float32(0), neg), np.where(first, np.float32(0), neg)]))


def _sample_attention_bias(dec_seq):
    r = np.arange(CHUNK)[:, None]
    c = np.arange(CHUNK)[None, :]
    t = r % dec_seq
    win = c > t
    new = ((r // dec_seq) == (c // dec_seq)) & ((c % dec_seq) <= t)
    neg = np.float32(-np.inf)
    zero = np.float32(0)
    return jnp.asarray(np.where(win, zero, neg)), jnp.asarray(np.where(new, zero, neg))


def _rope_tables(pos):
    half = HEAD_DIM // 2
    inv = ROPE_THETA ** (-jnp.arange(half, dtype=F32) / half)
    ang = pos.astype(F32)[:, None] * inv[None, :]
    return jnp.cos(ang), jnp.sin(ang)


def _const_spec(shape):
    zeros = (0,) * len(shape)
    return pl.BlockSpec(shape, lambda *_: zeros, pipeline_mode=pl.Buffered(1))


def kernel(x_prompt, x_sample, cache_k, cache_v, state_gla, w_in, w_gk2, b_gk, g_gla, sinks, w_out, g_mix_pre, g_mix_post, g_ffn_pre, g_ffn_post, w_up, w_down):
    depth = w_in.shape[0]
    assert depth == 1, "single trunk layer"
    batch, seq, d_model = x_prompt.shape
    dec_batch, dec_seq, _ = x_sample.shape
    d_ff = w_up.shape[2]
    assert seq % PROMPT_TILE == 0 and dec_batch % SAMPLE_SEQS == 0
    assert SAMPLE_SEQS * dec_seq == CHUNK and CHUNK % dec_seq == 0 and d_ff % FF_CHUNK == 0
    assert w_in.shape[2] == MAIN_W + GLA_RANK

    w_in_b = w_in[0].astype(BF16)
    w_gk2_b = w_gk2[0].astype(BF16)
    w_out_b = w_out[0].astype(BF16)
    w_up_b = w_up[0].astype(BF16)
    w_down_b = w_down[0].astype(BF16)
    g_pre = g_mix_pre[0][None, :]
    g_post = g_mix_post[0][None, :]
    g_fpre = g_ffn_pre[0][None, :]
    g_fpost = g_ffn_post[0][None, :]
    g_gla2 = g_gla[0][None, :]
    b_gk2 = b_gk[0][None, :]
    sinks1 = sinks[0]

    weights = (g_pre, g_post, g_fpre, g_fpost, g_gla2, b_gk2, w_in_b, w_gk2_b, w_out_b, w_up_b, w_down_b)
    weight_specs = [_const_spec(w.shape) for w in weights]
    smem_spec = pl.BlockSpec(memory_space=pltpu.SMEM)

    cos_p, sin_p = _rope_tables(jnp.arange(seq, dtype=jnp.int32))
    gmat_p, masks_p = _gla_constants(CHUNK, PROMPT_LEVELS)
    bias = _attention_bias()
    n_t = seq // PROMPT_TILE
    tile = PROMPT_TILE
    n_tiles = batch * n_t

    def cur_tile(j):
        return jnp.minimum(j, n_tiles - 1)

    def mix_tile(j):
        return jnp.clip(j - 1, 0, n_tiles - 1)

    def done_tile(j):
        return jnp.clip(j - 2, 0, n_tiles - 1)

    tab_spec = pl.BlockSpec((tile, HEAD_DIM // 2), lambda j: (cur_tile(j) % n_t, 0))
    seq_out = lambda j: (cur_tile(j) // n_t, 0, 0)
    yp, kp, vp, sp = pl.pallas_call(
        functools.partial(_prompt_kernel, tiles_per_seq=n_t, n_tiles=n_tiles),
        grid=(n_tiles + 2,),
        in_specs=[smem_spec,
                  pl.BlockSpec((1, tile, d_model), lambda j: (cur_tile(j) // n_t, cur_tile(j) % n_t, 0)),
                  tab_spec, tab_spec,
                  _const_spec(bias.shape), _const_spec(gmat_p.shape), _const_spec(masks_p.shape)] + weight_specs,
        out_specs=[pl.BlockSpec((1, tile, d_model), lambda j: (done_tile(j) // n_t, done_tile(j) % n_t, 0)),
                   pl.BlockSpec((1, WINDOW, KV_W), seq_out),
                   pl.BlockSpec((1, WINDOW, KV_W), seq_out),
                   pl.BlockSpec((1, GK_W, GLA_DV), lambda j: (mix_tile(j) // n_t, 0, 0))],
        out_shape=[jax.ShapeDtypeStruct((batch, seq, d_model), F32),
                   jax.ShapeDtypeStruct((batch, WINDOW, KV_W), F32),
                   jax.ShapeDtypeStruct((batch, WINDOW, KV_W), F32),
                   jax.ShapeDtypeStruct((batch, GK_W, GLA_DV), F32)],
        scratch_shapes=[pltpu.VMEM((tile, ATT_W), BF16),
                        pltpu.VMEM((4, CHUNK + tile, LANES), BF16),
                        pltpu.VMEM((4, CHUNK + tile, LANES), BF16),
                        pltpu.VMEM((tile, GK_W), F32),
                        pltpu.VMEM((tile, GK_W), F32),
                        pltpu.VMEM((tile, GV_W), F32),
                        pltpu.VMEM((tile, GV_W), F32),
                        pltpu.VMEM((tile, GK_W), F32),
                        pltpu.VMEM((GK_W, GLA_DV), F32),
                        pltpu.VMEM((tile, d_model), BF16),
                        pltpu.VMEM((tile, d_model), BF16),
                        pltpu.VMEM((tile, d_model), F32),
                        pltpu.VMEM((tile, d_model), F32),
                        pltpu.SMEM((1,), jnp.int32)],
        compiler_params=pltpu.CompilerParams(
            dimension_semantics=("arbitrary",), vmem_limit_bytes=VMEM_LIMIT_BYTES),
        name="prompt_layer",
    )(sinks1, x_prompt, cos_p, sin_p, bias, gmat_p, masks_p, *weights)

    levels_s = tuple(m for m in PROMPT_LEVELS if m < dec_seq)
    pos_s = PAST_LEN + jnp.arange(dec_seq, dtype=jnp.int32)
    cos_s, sin_s = (jnp.tile(a, (SAMPLE_SEQS, 1)) for a in _rope_tables(pos_s))
    gmat_s, masks_s = _gla_constants(dec_seq, levels_s)
    bias_win, bias_new = _sample_attention_bias(dec_seq)
    rows = SAMPLE_SEQS * dec_seq
    n_g = dec_batch // SAMPLE_SEQS
    xs = x_sample.reshape(dec_batch * dec_seq, d_model)
    ck = cache_k[0].reshape(dec_batch, WINDOW, KV_W)
    cv = cache_v[0].reshape(dec_batch, WINDOW, KV_W)
    st = state_gla[0].reshape(dec_batch, GK_W, GLA_DV)
    seq_spec = pl.BlockSpec((SAMPLE_SEQS, WINDOW, KV_W), lambda i: (i, 0, 0))
    st_spec = pl.BlockSpec((SAMPLE_SEQS, GK_W, GLA_DV), lambda i: (i, 0, 0))
    row_spec = pl.BlockSpec((rows, d_model), lambda i: (i, 0))
    ys, ks, vs, ss = pl.pallas_call(
        functools.partial(_sample_kernel, dec_seq=dec_seq, levels=levels_s),
        grid=(n_g,),
        in_specs=[smem_spec, row_spec,
                  _const_spec(cos_s.shape), _const_spec(sin_s.shape),
                  _const_spec(gmat_s.shape), _const_spec(masks_s.shape),
                  _const_spec(bias_win.shape), _const_spec(bias_new.shape),
                  seq_spec, seq_spec, st_spec] + weight_specs,
        out_specs=[row_spec, seq_spec, seq_spec, st_spec],
        out_shape=[jax.ShapeDtypeStruct((dec_batch * dec_seq, d_model), F32),
                   jax.ShapeDtypeStruct((dec_batch, WINDOW, KV_W), F32),
                   jax.ShapeDtypeStruct((dec_batch, WINDOW, KV_W), F32),
                   jax.ShapeDtypeStruct((dec_batch, GK_W, GLA_DV), F32)],
        compiler_params=pltpu.CompilerParams(
            dimension_semantics=("arbitrary",), vmem_limit_bytes=VMEM_LIMIT_BYTES),
        name="sample_layer",
    )(sinks1, xs, cos_s, sin_s, gmat_s, masks_s, bias_win, bias_new, ck, cv, st, *weights)

    head_shape = (ATT_KV_HEADS, HEAD_DIM)
    state_shape = (GLA_HEADS, GLA_DK, GLA_DV)
    return (yp,
            ys.reshape(dec_batch, dec_seq, d_model),
            kp.reshape((1, batch, WINDOW) + head_shape),
            vp.reshape((1, batch, WINDOW) + head_shape),
            sp.reshape((1, batch) + state_shape),
            ks.reshape((1, dec_batch, WINDOW) + head_shape),
            vs.reshape((1, dec_batch, WINDOW) + head_shape),
            ss.reshape((1, dec_batch) + state_shape))
```

```python
import functools
import math

import numpy as np
import jax
import jax.numpy as jnp
from jax import lax
from jax.experimental import pallas as pl
from jax.experimental.pallas import tpu as pltpu

F32 = jnp.float32
BF16 = jnp.bfloat16

HEAD_DIM = 64
ATT_HEADS = 8
ATT_KV_HEADS = 2
WINDOW = 128
PAST_LEN = 16384
ROPE_THETA = 10000.0
GLA_HEADS = 4
GLA_DK = 64
GLA_DV = 128
GLA_RANK = 16
GLA_GATE_TEMP = 16.0
EPS = 1e-6
ATT_W = ATT_HEADS * HEAD_DIM
KV_W = ATT_KV_HEADS * HEAD_DIM
GK_W = GLA_HEADS * GLA_DK
GV_W = GLA_HEADS * GLA_DV
COL_Q = 0
COL_K = COL_Q + ATT_W
COL_V = COL_K + KV_W
COL_GQ = COL_V + KV_W
COL_GK = COL_GQ + GK_W
COL_GV = COL_GK + GK_W
COL_GR = COL_GV + GV_W
MAIN_W = COL_GR + GV_W
QK_SCALE = HEAD_DIM ** -0.5
GLA_SCALE = GLA_DK ** -0.5

CHUNK = 128
LANES = 128
PROMPT_TILE = 256
SAMPLE_SEQS = 16
FF_CHUNK = 1024
PROMPT_LEVELS = (1, 2, 4, 8, 16, 32, 64)
BOUNDED_LOG_DECAY = 30.0
VMEM_LIMIT_BYTES = 56 * 1024 * 1024


def _dot(a, b):
    return jnp.dot(a, b, preferred_element_type=F32)


def _dot_nt(a, b):
    return lax.dot_general(a, b, (((1,), (1,)), ((), ())), preferred_element_type=F32)


def _dot_tn(a, b):
    return lax.dot_general(a, b, (((0,), (0,)), ((), ())), preferred_element_type=F32)


def _rms(x, g):
    ms = jnp.mean(x * x, axis=-1, keepdims=True)
    return x * lax.rsqrt(ms + EPS) * g


def _rope(x, cos, sin_a, sin_b):
    return x * cos + pltpu.roll(x, 96, 1) * sin_a + pltpu.roll(x, 32, 1) * sin_b


def _rope_lanes(cos, sin):
    reps = LANES // cos.shape[0]
    cos_l = jnp.transpose(jnp.concatenate([cos] * reps, axis=0))
    sin_l = jnp.transpose(jnp.concatenate([sin] * reps, axis=0))
    lane = lax.broadcasted_iota(jnp.int32, cos_l.shape, 1)
    first_half = (lane & (HEAD_DIM // 2)) == 0
    return cos_l, jnp.where(first_half, -sin_l, 0.0), jnp.where(first_half, 0.0, sin_l)


def _split3(x):
    hi = x.astype(BF16)
    r1 = x - hi.astype(F32)
    mid = r1.astype(BF16)
    lo = (r1 - mid.astype(F32)).astype(BF16)
    return hi, mid, lo


def _pre_stage(x, cos, sin_a, sin_b, g_pre, w_in, w_gk2, b_gk):
    h = _rms(x, g_pre).astype(BF16)
    proj = _dot(h, w_in[:, 0:MAIN_W])
    glr = _dot(h, w_in[:, MAIN_W:])
    z = _dot(glr.astype(BF16), w_gk2[...]) + b_gk
    log_a = (jnp.minimum(z, 0.0) - jnp.log1p(jnp.exp(-jnp.abs(z)))) * (1.0 / GLA_GATE_TEMP)
    q = jnp.concatenate(
        [_rope(proj[:, COL_Q + LANES * j:COL_Q + LANES * (j + 1)], cos, sin_a, sin_b) * QK_SCALE
         for j in range(ATT_W // LANES)], axis=1)
    k = _rope(proj[:, COL_K:COL_V], cos, sin_a, sin_b)
    v = proj[:, COL_V:COL_GQ]
    gq = proj[:, COL_GQ:COL_GK] * GLA_SCALE
    gk = proj[:, COL_GK:COL_GV]
    gv = proj[:, COL_GV:COL_GR]
    gr = proj[:, COL_GR:MAIN_W]
    return q, k, v, gq, gk, gv, gr, log_a


def _post_stage(x, mix_b, g_mix_post, g_ffn_pre, g_ffn_post, w_out, w_up, w_down):
    mix = _dot(mix_b, w_out[...])
    x1 = x + _rms(mix, g_mix_post)
    h2 = _rms(x1, g_ffn_pre).astype(BF16)
    f = None
    for j in range(w_up.shape[1] // FF_CHUNK):
        u = jnp.maximum(_dot(h2, w_up[:, FF_CHUNK * j:FF_CHUNK * (j + 1)]), 0.0)
        part = _dot((u * u).astype(BF16), w_down[FF_CHUNK * j:FF_CHUNK * (j + 1), :])
        f = part if f is None else f + part
    return x1 + _rms(f, g_ffn_post)


def _lo_hi_forms(a):
    lo = lax.broadcasted_iota(jnp.int32, a.shape, 1) < HEAD_DIM
    r = pltpu.roll(a, HEAD_DIM, 1)
    zero = jnp.zeros_like(a)
    return ((jnp.where(lo, a, zero), jnp.where(lo, zero, r)),
            (jnp.where(lo, r, zero), jnp.where(lo, zero, a)))


def _gla_exponents(log_a, gmat):
    hi, mid, lo = _split3(log_a)
    return _dot(gmat, hi) + _dot(gmat, mid) + _dot(gmat, lo)


def _head_lanes(h, rows):
    lane = lax.broadcasted_iota(jnp.int32, (rows, LANES), 1)
    e = h % 2
    return (lane >= GLA_DK * e) & (lane < GLA_DK * (e + 1))


def _gla_intra(gq, gk, log_a, gmat_lv, levels, masks_ref):
    expo = _gla_exponents(log_a, gmat_lv)
    row = lax.broadcasted_iota(jnp.int32, gq.shape, 0)
    terms = [(gq.astype(BF16), gk.astype(BF16), 0)]
    for li, m in enumerate(levels):
        ex = jnp.exp(expo[CHUNK * li:CHUNK * (li + 1)])
        z = (jnp.where((row & m) != 0, gq, gk) * ex).astype(BF16)
        terms.append((z, z, 1 + li))
    scores = []
    for h in range(GLA_HEADS):
        p = h // 2
        in_head = _head_lanes(h, gq.shape[0])
        acc = None
        for zq, zk, mi in terms:
            zq_p = zq[:, LANES * p:LANES * (p + 1)]
            zk_p = zk[:, LANES * p:LANES * (p + 1)]
            zk_h = jnp.where(in_head, zk_p, jnp.zeros_like(zk_p))
            t = _dot_nt(zq_p, zk_h) * masks_ref[mi]
            acc = t if acc is None else acc + t
        scores.append(acc)
    return jnp.concatenate(scores, axis=1)


def _gla_intra_bounded(q_dec, gk, b, causal):
    k_grow_t = jnp.transpose(gk * jnp.exp(-b)).astype(BF16)
    scores = []
    for h in range(GLA_HEADS):
        p = h // 2
        qd_p = q_dec[:, LANES * p:LANES * (p + 1)]
        qd_h = jnp.where(_head_lanes(h, q_dec.shape[0]), qd_p, jnp.zeros_like(qd_p))
        scores.append(_dot(qd_h, k_grow_t[LANES * p:LANES * (p + 1), :]) * causal)
    return jnp.concatenate(scores, axis=1)


def _gla_out_gate(o_h, gr_h, g_gla):
    on = _rms(o_h, g_gla)
    return on * (gr_h * (1.0 / (1.0 + jnp.exp(-gr_h))))


def _prompt_kernel(sinks_ref, x_ref, cos_ref, sin_ref, bias_ref, gmat_ref, masks_ref,
                   g_pre_ref, g_post_ref, g_fpre_ref, g_fpost_ref, g_gla_ref, b_gk_ref,
                   w_in_ref, w_gk2_ref, w_out_ref, w_up_ref, w_down_ref,
                   y_ref, kwin_ref, vwin_ref, sout_ref,
                   q_scr, kbd_scr, vbd_scr, gq_scr, gk_scr, gv_scr, gr_scr, la_scr, s_scr,
                   mix_scr, mixp_scr, xp1_scr, xp2_scr, bounded_scr, *, tiles_per_seq, n_tiles):
    j = pl.program_id(0)
    t_step = lax.rem(jnp.clip(j - 1, 0, n_tiles - 1), tiles_per_seq)
    tile = x_ref.shape[1]
    n_chunks = tile // CHUNK

    @pl.when(j == 0)
    def _():
        for ref in (s_scr, q_scr, kbd_scr, vbd_scr, gq_scr, gk_scr, gv_scr, gr_scr, la_scr,
                    mixp_scr, xp1_scr, xp2_scr):
            ref[...] = jnp.zeros_like(ref)
        bounded_scr[0] = 1

    branch = jnp.where(j == n_tiles + 1, 2, jnp.where(bounded_scr[0] == 1, 0, 1))

    lo_lane = lax.broadcasted_iota(jnp.int32, (CHUNK, LANES), 1) < HEAD_DIM
    g_gla = g_gla_ref[...]

    def input_stage(results):
        x = x_ref[0]
        h = _rms(x, g_pre_ref[...]).astype(BF16)
        proj = _dot(h, w_in_ref[:, 0:MAIN_W])
        glr = _dot(h, w_in_ref[:, MAIN_W:])
        yield
        cos, sin_a, sin_b = _rope_lanes(cos_ref[...], sin_ref[...])
        z = _dot(glr.astype(BF16), w_gk2_ref[...]) + b_gk_ref[...]
        log_a = (jnp.minimum(z, 0.0) - jnp.log1p(jnp.exp(-jnp.abs(z)))) * (1.0 / GLA_GATE_TEMP)
        q = jnp.concatenate(
            [_rope(proj[:, COL_Q + LANES * i:COL_Q + LANES * (i + 1)], cos, sin_a, sin_b) * QK_SCALE
             for i in range(ATT_W // LANES)], axis=1)
        k = _rope(proj[:, COL_K:COL_V], cos, sin_a, sin_b)
        v = proj[:, COL_V:COL_GQ]
        chunk_decay = jnp.concatenate(
            [jnp.sum(log_a[CHUNK * c:CHUNK * (c + 1)], axis=0, keepdims=True) for c in range(n_chunks)], axis=0)
        results.update(
            x=x, q=q.astype(BF16), k=k, v=v, k_forms=_lo_hi_forms(k), v_forms=_lo_hi_forms(v),
            gq=proj[:, COL_GQ:COL_GK] * GLA_SCALE, gk=proj[:, COL_GK:COL_GV],
            gv=proj[:, COL_GV:COL_GR], gr=proj[:, COL_GR:MAIN_W], log_a=log_a,
            bounded=jnp.min(chunk_decay) >= -BOUNDED_LOG_DECAY)
        yield

    def hand_over(results):
        kbd_scr[:, 0:CHUNK, :] = kbd_scr[:, tile:tile + CHUNK, :]
        vbd_scr[:, 0:CHUNK, :] = vbd_scr[:, tile:tile + CHUNK, :]
        for g in range(ATT_KV_HEADS):
            for e in range(2):
                kbd_scr[2 * g + e, CHUNK:CHUNK + tile, :] = results["k_forms"][g][e].astype(BF16)
                vbd_scr[2 * g + e, CHUNK:CHUNK + tile, :] = results["v_forms"][g][e].astype(BF16)
        q_scr[...] = results["q"]
        gq_scr[...] = results["gq"]
        gk_scr[...] = results["gk"]
        gv_scr[...] = results["gv"]
        gr_scr[...] = results["gr"]
        la_scr[...] = results["log_a"]
        bounded_scr[0] = jnp.where(results["bounded"], 1, 0)
        kwin_ref[0] = results["k"][tile - WINDOW:tile, :]
        vwin_ref[0] = results["v"][tile - WINDOW:tile, :]
        xp2_scr[...] = xp1_scr[...]
        xp1_scr[...] = results["x"]
        mixp_scr[...] = mix_scr[...]

    def attention_chunk(c):
        rows = slice(CHUNK * c, CHUNK * (c + 1))
        rows2 = slice(CHUNK * c, CHUNK * (c + 2))
        bias = bias_ref[jnp.where(t_step == 0, 1, 0)] if c == 0 else bias_ref[0]
        scores = []
        for g in range(ATT_KV_HEADS):
            lhs = jnp.concatenate([q_scr[rows, 2 * LANES * g:2 * LANES * g + LANES],
                                   q_scr[rows, 2 * LANES * g + LANES:2 * LANES * (g + 1)]], axis=0)
            kcat = jnp.concatenate([kbd_scr[2 * g, rows2, :], kbd_scr[2 * g + 1, rows2, :]], axis=0)
            scores.append(_dot_nt(lhs, kcat) + bias)
        yield
        for g in range(ATT_KV_HEADS):
            s = scores[g]
            p_blocks = []
            inv = []
            for blk in range(2):
                p_row = []
                inv_row = []
                for e in range(2):
                    sub = s[CHUNK * blk:CHUNK * (blk + 1), 2 * CHUNK * e:2 * CHUNK * (e + 1)]
                    sink = sinks_ref[4 * g + 2 * blk + e]
                    m = jnp.maximum(jnp.max(sub, axis=-1, keepdims=True), sink)
                    p = jnp.exp(sub - m)
                    den = jnp.sum(p, axis=-1, keepdims=True) + jnp.exp(sink - m)
                    p_row.append(p.astype(BF16))
                    inv_row.append(1.0 / den)
                p_blocks.append(jnp.concatenate(p_row, axis=1))
                inv.append(inv_row)
            p_all = jnp.concatenate(p_blocks, axis=0)
            vcat = jnp.concatenate([vbd_scr[2 * g, rows2, :], vbd_scr[2 * g + 1, rows2, :]], axis=0)
            o = _dot(p_all, vcat)
            for blk in range(2):
                scale = jnp.where(lo_lane, inv[blk][0], inv[blk][1])
                c0 = 2 * LANES * g + LANES * blk
                mix_scr[rows, c0:c0 + LANES] = (o[CHUNK * blk:CHUNK * (blk + 1)] * scale).astype(BF16)
        yield

    def gla_chunk(c, state, bounded_decay):
        rows = slice(CHUNK * c, CHUNK * (c + 1))
        gq_c = gq_scr[rows, :]
        gk_c = gk_scr[rows, :]
        la_c = la_scr[rows, :]
        expo = _gla_exponents(la_c, gmat_ref[0:2 * CHUNK, :])
        b = expo[0:CHUNK]
        q_dec_f = gq_c * jnp.exp(b)
        q_dec = q_dec_f.astype(BF16)
        k_dec_t = jnp.transpose(gk_c * jnp.exp(expo[CHUNK:2 * CHUNK])).astype(BF16)
        decay_t = jnp.exp(jnp.transpose(jnp.broadcast_to(b[CHUNK - 1:CHUNK, :], b.shape)))
        yield
        if bounded_decay:
            scores = _gla_intra_bounded(q_dec_f, gk_c, b, masks_ref[len(PROMPT_LEVELS) + 1])
        else:
            scores = _gla_intra(gq_c, gk_c, la_c, gmat_ref[2 * CHUNK:, :], PROMPT_LEVELS, masks_ref)
        yield
        for h in range(GLA_HEADS):
            p_ = h // 2
            v_h = gv_scr[rows, GLA_DV * h:GLA_DV * (h + 1)].astype(BF16)
            qd_p = q_dec[:, LANES * p_:LANES * (p_ + 1)]
            qd_h = jnp.where(_head_lanes(h, CHUNK), qd_p, jnp.zeros_like(qd_p))
            s0_pair = jnp.concatenate([state[2 * p_], state[2 * p_ + 1]], axis=0).astype(BF16)
            o_h = _dot(scores[:, CHUNK * h:CHUNK * (h + 1)].astype(BF16), v_h) + _dot(qd_h, s0_pair)
            out_h = _gla_out_gate(o_h, gr_scr[rows, GLA_DV * h:GLA_DV * (h + 1)], g_gla)
            mix_scr[rows, ATT_W + GLA_DV * h:ATT_W + GLA_DV * (h + 1)] = out_h.astype(BF16)
        new_state = [decay_t[GLA_DK * h:GLA_DK * (h + 1), :] * state[h]
                     + _dot(k_dec_t[GLA_DK * h:GLA_DK * (h + 1), :],
                            gv_scr[rows, GLA_DV * h:GLA_DV * (h + 1)].astype(BF16))
                     for h in range(GLA_HEADS)]
        state[:] = new_state
        yield

    def output_stage():
        mix = _dot(mixp_scr[...], w_out_ref[...])
        yield
        x1 = xp2_scr[...] + _rms(mix, g_post_ref[...])
        h2 = _rms(x1, g_fpre_ref[...]).astype(BF16)
        f = None
        for i in range(w_up_ref.shape[1] // FF_CHUNK):
            u = jnp.maximum(_dot(h2, w_up_ref[:, FF_CHUNK * i:FF_CHUNK * (i + 1)]), 0.0)
            yield
            part = _dot((u * u).astype(BF16), w_down_ref[FF_CHUNK * i:FF_CHUNK * (i + 1), :])
            f = part if f is None else f + part
            yield
        y_ref[0] = x1 + _rms(f, g_fpost_ref[...])

    def run_step(bounded_decay):
        keep = jnp.where(t_step == 0, 0.0, 1.0)
        state = [s_scr[GLA_DK * h:GLA_DK * (h + 1), :] * keep for h in range(GLA_HEADS)]
        results = {}
        out = output_stage()
        inp = input_stage(results)
        att = [attention_chunk(c) for c in range(n_chunks)]
        gla = [gla_chunk(c, state, bounded_decay) for c in range(n_chunks)]
        next(out)
        for c in range(n_chunks):
            next(att[c])
            next(gla[c])
        next(inp)
        for c in range(n_chunks):
            next(att[c])
            next(gla[c])
        next(out)
        for c in range(n_chunks):
            next(out)
            next(gla[c])
        next(inp)
        for _ in out:
            pass
        s_all = jnp.concatenate(state, axis=0)
        s_scr[...] = s_all
        sout_ref[0] = s_all
        hand_over(results)

    def finish_last_tile():
        for _ in output_stage():
            pass

    lax.switch(branch, [lambda: run_step(True), lambda: run_step(False), finish_last_tile])


def _sample_kernel(sinks_ref, x_ref, cos_ref, sin_ref, gmat_ref, masks_ref, bias_win_ref, bias_new_ref,
                   ck_ref, cv_ref, st_ref,
                   g_pre_ref, g_post_ref, g_fpre_ref, g_fpost_ref, g_gla_ref, b_gk_ref,
                   w_in_ref, w_gk2_ref, w_out_ref, w_up_ref, w_down_ref,
                   y_ref, kwin_ref, vwin_ref, sout_ref,
                   *, dec_seq, levels):
    n_seqs = ck_ref.shape[0]
    x = x_ref[...]
    q, k, v, gq, gk, gv, gr, log_a = _pre_stage(
        x, *_rope_lanes(cos_ref[...], sin_ref[...]), g_pre_ref[...],
        w_in_ref, w_gk2_ref, b_gk_ref[...])

    kwin_ref[:, 0:WINDOW - dec_seq, :] = ck_ref[:, dec_seq:WINDOW, :]
    vwin_ref[:, 0:WINDOW - dec_seq, :] = cv_ref[:, dec_seq:WINDOW, :]
    kwin_ref[:, WINDOW - dec_seq:WINDOW, :] = k.reshape(n_seqs, dec_seq, KV_W)
    vwin_ref[:, WINDOW - dec_seq:WINDOW, :] = v.reshape(n_seqs, dec_seq, KV_W)

    def seq_rows(a, s):
        return a[dec_seq * s:dec_seq * (s + 1)]

    def gather_seq(per_head, s):
        return jnp.concatenate([seq_rows(a, s) for a in per_head], axis=0)

    def scatter_heads(per_seq, h):
        return jnp.concatenate([seq_rows(a, h) for a in per_seq], axis=0)

    lo_lane = lax.broadcasted_iota(jnp.int32, (CHUNK, LANES), 1) < HEAD_DIM
    q_heads = []
    for j in range(ATT_W // LANES):
        blk = q[:, LANES * j:LANES * (j + 1)]
        rolled = pltpu.roll(blk, HEAD_DIM, 1)
        g = (2 * j) // (ATT_HEADS // ATT_KV_HEADS)
        keep = lo_lane if g == 0 else jnp.logical_not(lo_lane)
        for e in range(2):
            q_heads.append(jnp.where(keep, blk if e == g else rolled, 0.0))
    s_win = [_dot_nt(gather_seq(q_heads, s), ck_ref[s]) for s in range(n_seqs)]
    k_b = k.astype(BF16)
    v_b = v.astype(BF16)
    bias_win = bias_win_ref[...]
    bias_new = bias_new_ref[...]
    p_win = []
    o_new = []
    inv_den = []
    for h in range(ATT_HEADS):
        sw = scatter_heads(s_win, h) + bias_win
        sn = _dot_nt(q_heads[h].astype(BF16), k_b) + bias_new
        sink = sinks_ref[h]
        m = jnp.maximum(jnp.maximum(jnp.max(sw, axis=-1, keepdims=True),
                                    jnp.max(sn, axis=-1, keepdims=True)), sink)
        pw = jnp.exp(sw - m)
        pn = jnp.exp(sn - m)
        den = (jnp.sum(pw, axis=-1, keepdims=True) + jnp.sum(pn, axis=-1, keepdims=True)
               + jnp.exp(sink - m))
        p_win.append(pw)
        o_new.append(_dot(pn.astype(BF16), v_b))
        inv_den.append(1.0 / den)
    o_win = [_dot(gather_seq(p_win, s), cv_ref[s]) for s in range(n_seqs)]
    o_heads = [(scatter_heads(o_win, h) + o_new[h]) * inv_den[h] for h in range(ATT_HEADS)]
    mix_cols = []
    for j in range(ATT_W // LANES):
        g = (2 * j) // (ATT_HEADS // ATT_KV_HEADS)
        first = o_heads[2 * j] if g == 0 else pltpu.roll(o_heads[2 * j], HEAD_DIM, 1)
        second = o_heads[2 * j + 1] if g == 1 else pltpu.roll(o_heads[2 * j + 1], HEAD_DIM, 1)
        mix_cols.append(jnp.where(lo_lane, first, second).astype(BF16))

    expo = _gla_exponents(log_a, gmat_ref[0:2 * CHUNK, :])
    b = expo[0:CHUNK]
    q_dec = gq * jnp.exp(b)
    k_dec_t = jnp.transpose(gk * jnp.exp(expo[CHUNK:2 * CHUNK]))
    b3 = b.reshape(n_seqs, dec_seq, GK_W)
    b_last = jnp.broadcast_to(b3[:, dec_seq - 1:dec_seq, :], b3.shape).reshape(CHUNK, GK_W)
    decay_t = jnp.exp(jnp.transpose(b_last))
    scores = _gla_intra(gq, gk, log_a, gmat_ref[2 * CHUNK:, :], levels, masks_ref)
    lane_k = lax.broadcasted_iota(jnp.int32, (dec_seq, GK_W), 1)
    lane_t = lax.broadcasted_iota(jnp.int32, (GLA_DK, LANES), 1)
    o_inter = []
    for s in range(n_seqs):
        qd = seq_rows(q_dec, s)
        lhs = jnp.concatenate(
            [jnp.where((lane_k >= GLA_DK * h) & (lane_k < GLA_DK * (h + 1)), qd, 0.0)
             for h in range(GLA_HEADS)], axis=0)
        o_inter.append(_dot(lhs, st_ref[s]))
    g_gla = g_gla_ref[...]
    for h in range(GLA_HEADS):
        cs = slice(GLA_DV * h, GLA_DV * (h + 1))
        hs = slice(GLA_DK * h, GLA_DK * (h + 1))
        v_h = gv[:, cs].astype(BF16)
        o_h = _dot(scores[:, CHUNK * h:CHUNK * (h + 1)].astype(BF16), v_h) + scatter_heads(o_inter, h)
        mix_cols.append(_gla_out_gate(o_h, gr[:, cs], g_gla).astype(BF16))
        kd_h = k_dec_t[hs, :]
        lhs = jnp.concatenate(
            [jnp.where((lane_t >= dec_seq * s) & (lane_t < dec_seq * (s + 1)), kd_h, 0.0).astype(BF16)
             for s in range(n_seqs)], axis=0)
        upd = _dot(lhs, v_h)
        for s in range(n_seqs):
            a_col = jnp.broadcast_to(decay_t[hs, dec_seq * s:dec_seq * s + 1], (GLA_DK, GLA_DV))
            sout_ref[s, hs, :] = a_col * st_ref[s, hs, :] + upd[GLA_DK * s:GLA_DK * (s + 1)]

    y_ref[...] = _post_stage(x, jnp.concatenate(mix_cols, axis=1), g_post_ref[...], g_fpre_ref[...],
                             g_fpost_ref[...], w_out_ref, w_up_ref, w_down_ref)


def _gla_constants(seg, levels):
    t = np.arange(CHUNK)[:, None]
    u = np.arange(CHUNK)[None, :]
    same = (t // seg) == (u // seg)
    blocks = [same & (u <= t), same & (u > t)]
    masks = [np.eye(CHUNK, dtype=bool)]
    for m in levels:
        ref = (t // (2 * m)) * (2 * m) + m - 1
        second = (t & m) != 0
        blocks.append(np.where(second, (u > ref) & (u <= t), (u > t) & (u <= ref)))
        masks.append(((t // (2 * m)) == (u // (2 * m))) & second & ((u & m) == 0))
    masks.append(blocks[0])
    gmat = jnp.asarray(np.concatenate(blocks, axis=0).astype(np.float32), dtype=BF16)
    return gmat, jnp.asarray(np.stack(masks).astype(np.float32))


def _attention_bias():
    t = (np.arange(2 * CHUNK) % CHUNK)[:, None]
    j = (np.arange(4 * CHUNK) % (2 * CHUNK))[None, :]
    band = (j >= t + 1) & (j <= t + WINDOW)
    first = band & (j >= CHUNK)
    neg = np.float32(-np.inf)
    return jnp.asarray(np.stack([np.where(band, np.float32(0), neg), np.where(first, np.float32(0), neg)]))


def _sample_attention_bias(dec_seq):
    r = np.arange(CHUNK)[:, None]
    c = np.arange(CHUNK)[None, :]
    t = r % dec_seq
    win = c > t
    new = ((r // dec_seq) == (c // dec_seq)) & ((c % dec_seq) <= t)
    neg = np.float32(-np.inf)
    zero = np.float32(0)
    return jnp.asarray(np.where(win, zero, neg)), jnp.asarray(np.where(new, zero, neg))


def _rope_tables(pos):
    half = HEAD_DIM // 2
    inv = ROPE_THETA ** (-jnp.arange(half, dtype=F32) / half)
    ang = inv[:, None] * pos.astype(F32)[None, :]
    return jnp.cos(ang), jnp.sin(ang)


def _const_spec(shape):
    zeros = (0,) * len(shape)
    return pl.BlockSpec(shape, lambda *_: zeros, pipeline_mode=pl.Buffered(1))


def kernel(x_prompt, x_sample, cache_k, cache_v, state_gla, w_in, w_gk2, b_gk, g_gla, sinks, w_out, g_mix_pre, g_mix_post, g_ffn_pre, g_ffn_post, w_up, w_down):
    depth = w_in.shape[0]
    assert depth == 1, "single trunk layer"
    batch, seq, d_model = x_prompt.shape
    dec_batch, dec_seq, _ = x_sample.shape
    d_ff = w_up.shape[2]
    assert seq % PROMPT_TILE == 0 and dec_batch % SAMPLE_SEQS == 0
    assert SAMPLE_SEQS * dec_seq == CHUNK and CHUNK % dec_seq == 0 and d_ff % FF_CHUNK == 0
    assert w_in.shape[2] == MAIN_W + GLA_RANK

    w_in_b = w_in[0].astype(BF16)
    w_gk2_b = w_gk2[0].astype(BF16)
    w_out_b = w_out[0].astype(BF16)
    w_up_b = w_up[0].astype(BF16)
    w_down_b = w_down[0].astype(BF16)
    g_pre = g_mix_pre[0][None, :]
    g_post = g_mix_post[0][None, :]
    g_fpre = g_ffn_pre[0][None, :]
    g_fpost = g_ffn_post[0][None, :]
    g_gla2 = g_gla[0][None, :]
    b_gk2 = b_gk[0][None, :]
    sinks1 = sinks[0]

    weights = (g_pre, g_post, g_fpre, g_fpost, g_gla2, b_gk2, w_in_b, w_gk2_b, w_out_b, w_up_b, w_down_b)
    weight_specs = [_const_spec(w.shape) for w in weights]
    smem_spec = pl.BlockSpec(memory_space=pltpu.SMEM)

    cos_p, sin_p = _rope_tables(jnp.arange(seq, dtype=jnp.int32))
    gmat_p, masks_p = _gla_constants(CHUNK, PROMPT_LEVELS)
    bias = _attention_bias()
    n_t = seq // PROMPT_TILE
    tile = PROMPT_TILE
    n_tiles = batch * n_t

    def cur_tile(j):
        return jnp.minimum(j, n_tiles - 1)

    def mix_tile(j):
        return jnp.clip(j - 1, 0, n_tiles - 1)

    def done_tile(j):
        return jnp.clip(j - 2, 0, n_tiles - 1)

    tab_spec = pl.BlockSpec((HEAD_DIM // 2, tile), lambda j: (0, cur_tile(j) % n_t))
    seq_out = lambda j: (cur_tile(j) // n_t, 0, 0)
    yp, kp, vp, sp = pl.pallas_call(
        functools.partial(_prompt_kernel, tiles_per_seq=n_t, n_tiles=n_tiles),
        grid=(n_tiles + 2,),
        in_specs=[smem_spec,
                  pl.BlockSpec((1, tile, d_model), lambda j: (cur_tile(j) // n_t, cur_tile(j) % n_t, 0)),
                  tab_spec, tab_spec,
                  _const_spec(bias.shape), _const_spec(gmat_p.shape), _const_spec(masks_p.shape)] + weight_specs,
        out_specs=[pl.BlockSpec((1, tile, d_model), lambda j: (done_tile(j) // n_t, done_tile(j) % n_t, 0)),
                   pl.BlockSpec((1, WINDOW, KV_W), seq_out),
                   pl.BlockSpec((1, WINDOW, KV_W), seq_out),
                   pl.BlockSpec((1, GK_W, GLA_DV), lambda j: (mix_tile(j) // n_t, 0, 0))],
        out_shape=[jax.ShapeDtypeStruct((batch, seq, d_model), F32),
                   jax.ShapeDtypeStruct((batch, WINDOW, KV_W), F32),
                   jax.ShapeDtypeStruct((batch, WINDOW, KV_W), F32),
                   jax.ShapeDtypeStruct((batch, GK_W, GLA_DV), F32)],
        scratch_shapes=[pltpu.VMEM((tile, ATT_W), BF16),
                        pltpu.VMEM((4, CHUNK + tile, LANES), BF16),
                        pltpu.VMEM((4, CHUNK + tile, LANES), BF16),
                        pltpu.VMEM((tile, GK_W), F32),
                        pltpu.VMEM((tile, GK_W), F32),
                        pltpu.VMEM((tile, GV_W), F32),
                        pltpu.VMEM((tile, GV_W), F32),
                        pltpu.VMEM((tile, GK_W), F32),
                        pltpu.VMEM((GK_W, GLA_DV), F32),
                        pltpu.VMEM((tile, d_model), BF16),
                        pltpu.VMEM((tile, d_model), BF16),
                        pltpu.VMEM((tile, d_model), F32),
                        pltpu.VMEM((tile, d_model), F32),
                        pltpu.SMEM((1,), jnp.int32)],
        compiler_params=pltpu.CompilerParams(
            dimension_semantics=("arbitrary",), vmem_limit_bytes=VMEM_LIMIT_BYTES),
        name="prompt_layer",
    )(sinks1, x_prompt, cos_p, sin_p, bias, gmat_p, masks_p, *weights)

    levels_s = tuple(m for m in PROMPT_LEVELS if m < dec_seq)
    pos_s = PAST_LEN + jnp.arange(dec_seq, dtype=jnp.int32)
    cos_s, sin_s = (jnp.tile(a, (1, SAMPLE_SEQS)) for a in _rope_tables(pos_s))
    gmat_s, masks_s = _gla_constants(dec_seq, levels_s)
    bias_win, bias_new = _sample_attention_bias(dec_seq)
    rows = SAMPLE_SEQS * dec_seq
    n_g = dec_batch // SAMPLE_SEQS
    xs = x_sample.reshape(dec_batch * dec_seq, d_model)
    ck = cache_k[0].reshape(dec_batch, WINDOW, KV_W)
    cv = cache_v[0].reshape(dec_batch, WINDOW, KV_W)
    st = state_gla[0].reshape(dec_batch, GK_W, GLA_DV)
    seq_spec = pl.BlockSpec((SAMPLE_SEQS, WINDOW, KV_W), lambda i: (i, 0, 0))
    st_spec = pl.BlockSpec((SAMPLE_SEQS, GK_W, GLA_DV), lambda i: (i, 0, 0))
    row_spec = pl.BlockSpec((rows, d_model), lambda i: (i, 0))
    ys, ks, vs, ss = pl.pallas_call(
        functools.partial(_sample_kernel, dec_seq=dec_seq, levels=levels_s),
        grid=(n_g,),
        in_specs=[smem_spec, row_spec,
                  _const_spec(cos_s.shape), _const_spec(sin_s.shape),
                  _const_spec(gmat_s.shape), _const_spec(masks_s.shape),
                  _const_spec(bias_win.shape), _const_spec(bias_new.shape),
                  seq_spec, seq_spec, st_spec] + weight_specs,
        out_specs=[row_spec, seq_spec, seq_spec, st_spec],
        out_shape=[jax.ShapeDtypeStruct((dec_batch * dec_seq, d_model), F32),
                   jax.ShapeDtypeStruct((dec_batch, WINDOW, KV_W), F32),
                   jax.ShapeDtypeStruct((dec_batch, WINDOW, KV_W), F32),
                   jax.ShapeDtypeStruct((dec_batch, GK_W, GLA_DV), F32)],
        compiler_params=pltpu.CompilerParams(
            dimension_semantics=("arbitrary",), vmem_limit_bytes=VMEM_LIMIT_BYTES),
        name="sample_layer",
    )(sinks1, xs, cos_s, sin_s, gmat_s, masks_s, bias_win, bias_new, ck, cv, st, *weights)

    head_shape = (ATT_KV_HEADS, HEAD_DIM)
    state_shape = (GLA_HEADS, GLA_DK, GLA_DV)
    return (yp,
            ys.reshape(dec_batch, dec_seq, d_model),
            kp.reshape((1, batch, WINDOW) + head_shape),
            vp.reshape((1, batch, WINDOW) + head_shape),
            sp.reshape((1, batch) + state_shape),
            ks.reshape((1, dec_batch, WINDOW) + head_shape),
            vs.reshape((1, dec_batch, WINDOW) + head_shape),
            ss.reshape((1, dec_batch) + state_shape))
```

```python
import functools
import math

import numpy as np
import jax
import jax.numpy as jnp
from jax import lax
from jax.experimental import pallas as pl
from jax.experimental.pallas import tpu as pltpu

F32 = jnp.float32
BF16 = jnp.bfloat16

HEAD_DIM = 64
ATT_HEADS = 8
ATT_KV_HEADS = 2
WINDOW = 128
PAST_LEN = 16384
ROPE_THETA = 10000.0
GLA_HEADS = 4
GLA_DK = 64
GLA_DV = 128
GLA_RANK = 16
GLA_GATE_TEMP = 16.0
EPS = 1e-6
ATT_W = ATT_HEADS * HEAD_DIM
KV_W = ATT_KV_HEADS * HEAD_DIM
GK_W = GLA_HEADS * GLA_DK
GV_W = GLA_HEADS * GLA_DV
COL_Q = 0
COL_K = COL_Q + ATT_W
COL_V = COL_K + KV_W
COL_GQ = COL_V + KV_W
COL_GK = COL_GQ + GK_W
COL_GV = COL_GK + GK_W
COL_GR = COL_GV + GV_W
MAIN_W = COL_GR + GV_W
QK_SCALE = HEAD_DIM ** -0.5
GLA_SCALE = GLA_DK ** -0.5

CHUNK = 128
LANES = 128
PROMPT_TILE = 256
SAMPLE_SEQS = 16
FF_CHUNK = 1024
PROMPT_LEVELS = (1, 2, 4, 8, 16, 32, 64)
BOUNDED_LOG_DECAY = 30.0
VMEM_LIMIT_BYTES = 56 * 1024 * 1024


def _dot(a, b):
    return jnp.dot(a, b, preferred_element_type=F32)


def _dot_nt(a, b):
    return lax.dot_general(a, b, (((1,), (1,)), ((), ())), preferred_element_type=F32)


def _rms(x, g):
    ms = jnp.mean(x * x, axis=-1, keepdims=True)
    return x * lax.rsqrt(ms + EPS) * g


def _rope(x, cos, sin_a, sin_b):
    return x * cos + pltpu.roll(x, 96, 1) * sin_a + pltpu.roll(x, 32, 1) * sin_b


def _rope_lanes(cos, sin):
    reps = LANES // cos.shape[0]
    cos_l = jnp.transpose(jnp.concatenate([cos] * reps, axis=0))
    sin_l = jnp.transpose(jnp.concatenate([sin] * reps, axis=0))
    lane = lax.broadcasted_iota(jnp.int32, cos_l.shape, 1)
    first_half = (lane & (HEAD_DIM // 2)) == 0
    return cos_l, jnp.where(first_half, -sin_l, 0.0), jnp.where(first_half, 0.0, sin_l)


def _split3(x):
    hi = x.astype(BF16)
    r1 = x - hi.astype(F32)
    mid = r1.astype(BF16)
    lo = (r1 - mid.astype(F32)).astype(BF16)
    return hi, mid, lo


def _project(x, g_pre, w_in_ref):
    h = _rms(x, g_pre).astype(BF16)
    return _dot(h, w_in_ref[:, 0:MAIN_W]), _dot(h, w_in_ref[:, MAIN_W:])


def _split_projection(proj, glr, cos_ref, sin_ref, w_gk2_ref, b_gk_ref):
    cos, sin_a, sin_b = _rope_lanes(cos_ref[...], sin_ref[...])
    z = _dot(glr.astype(BF16), w_gk2_ref[...]) + b_gk_ref[...]
    log_a = (jnp.minimum(z, 0.0) - jnp.log1p(jnp.exp(-jnp.abs(z)))) * (1.0 / GLA_GATE_TEMP)
    q = jnp.concatenate(
        [_rope(proj[:, COL_Q + LANES * j:COL_Q + LANES * (j + 1)], cos, sin_a, sin_b) * QK_SCALE
         for j in range(ATT_W // LANES)], axis=1)
    k = _rope(proj[:, COL_K:COL_V], cos, sin_a, sin_b)
    v = proj[:, COL_V:COL_GQ]
    gq = proj[:, COL_GQ:COL_GK] * GLA_SCALE
    gk = proj[:, COL_GK:COL_GV]
    gv = proj[:, COL_GV:COL_GR]
    gr = proj[:, COL_GR:MAIN_W]
    return q, k, v, gq, gk, gv, gr, log_a


def _output_stage(x_prev, mix_prev, g_post_ref, g_fpre_ref, g_fpost_ref, w_out_ref, w_up_ref, w_down_ref, write):
    mix = _dot(mix_prev[...], w_out_ref[...])
    yield
    x1 = x_prev[...] + _rms(mix, g_post_ref[...])
    h2 = _rms(x1, g_fpre_ref[...]).astype(BF16)
    f = None
    for i in range(w_up_ref.shape[1] // FF_CHUNK):
        u = jnp.maximum(_dot(h2, w_up_ref[:, FF_CHUNK * i:FF_CHUNK * (i + 1)]), 0.0)
        yield
        part = _dot((u * u).astype(BF16), w_down_ref[FF_CHUNK * i:FF_CHUNK * (i + 1), :])
        f = part if f is None else f + part
        yield
    write(x1 + _rms(f, g_fpost_ref[...]))


def _lo_hi_forms(a):
    lo = lax.broadcasted_iota(jnp.int32, a.shape, 1) < HEAD_DIM
    r = pltpu.roll(a, HEAD_DIM, 1)
    zero = jnp.zeros_like(a)
    return ((jnp.where(lo, a, zero), jnp.where(lo, zero, r)),
            (jnp.where(lo, r, zero), jnp.where(lo, zero, a)))


def _gla_exponents(log_a, gmat):
    hi, mid, lo = _split3(log_a)
    return _dot(gmat, hi) + _dot(gmat, mid) + _dot(gmat, lo)


def _head_lanes(h, rows):
    lane = lax.broadcasted_iota(jnp.int32, (rows, LANES), 1)
    e = h % 2
    return (lane >= GLA_DK * e) & (lane < GLA_DK * (e + 1))


def _gla_intra(gq, gk, log_a, gmat_lv, levels, masks_ref):
    expo = _gla_exponents(log_a, gmat_lv)
    row = lax.broadcasted_iota(jnp.int32, gq.shape, 0)
    terms = [(gq.astype(BF16), gk.astype(BF16), 0)]
    for li, m in enumerate(levels):
        ex = jnp.exp(expo[CHUNK * li:CHUNK * (li + 1)])
        z = (jnp.where((row & m) != 0, gq, gk) * ex).astype(BF16)
        terms.append((z, z, 1 + li))
    scores = []
    for h in range(GLA_HEADS):
        p = h // 2
        in_head = _head_lanes(h, gq.shape[0])
        acc = None
        for zq, zk, mi in terms:
            zq_p = zq[:, LANES * p:LANES * (p + 1)]
            zk_p = zk[:, LANES * p:LANES * (p + 1)]
            zk_h = jnp.where(in_head, zk_p, jnp.zeros_like(zk_p))
            t = _dot_nt(zq_p, zk_h) * masks_ref[mi]
            acc = t if acc is None else acc + t
        scores.append(acc)
    return jnp.concatenate(scores, axis=1)


def _gla_intra_bounded(q_dec, gk, b, causal):
    k_grow_t = jnp.transpose(gk * jnp.exp(-b)).astype(BF16)
    scores = []
    for h in range(GLA_HEADS):
        p = h // 2
        qd_p = q_dec[:, LANES * p:LANES * (p + 1)]
        qd_h = jnp.where(_head_lanes(h, q_dec.shape[0]), qd_p, jnp.zeros_like(qd_p))
        scores.append(_dot(qd_h, k_grow_t[LANES * p:LANES * (p + 1), :]) * causal)
    return jnp.concatenate(scores, axis=1)


def _gla_out_gate(o_h, gr_h, g_gla):
    on = _rms(o_h, g_gla)
    return on * (gr_h * (1.0 / (1.0 + jnp.exp(-gr_h))))


def _prompt_kernel(sinks_ref, x_ref, cos_ref, sin_ref, bias_ref, gmat_ref, masks_ref,
                   g_pre_ref, g_post_ref, g_fpre_ref, g_fpost_ref, g_gla_ref, b_gk_ref,
                   w_in_ref, w_gk2_ref, w_out_ref, w_up_ref, w_down_ref,
                   y_ref, kwin_ref, vwin_ref, sout_ref,
                   q_scr, kbd_scr, vbd_scr, gq_scr, gk_scr, gv_scr, gr_scr, la_scr, s_scr,
                   mix_scr, mixp_scr, xp1_scr, xp2_scr, bounded_scr, *, tiles_per_seq, n_tiles):
    j = pl.program_id(0)
    t_step = lax.rem(jnp.clip(j - 1, 0, n_tiles - 1), tiles_per_seq)
    tile = x_ref.shape[1]
    n_chunks = tile // CHUNK

    @pl.when(j == 0)
    def _():
        for ref in (s_scr, q_scr, kbd_scr, vbd_scr, gq_scr, gk_scr, gv_scr, gr_scr, la_scr,
                    mixp_scr, xp1_scr, xp2_scr):
            ref[...] = jnp.zeros_like(ref)
        bounded_scr[0] = 1

    branch = jnp.where(j == 0, 3, jnp.where(j == n_tiles + 1, 2, jnp.where(bounded_scr[0] == 1, 0, 1)))

    lo_lane = lax.broadcasted_iota(jnp.int32, (CHUNK, LANES), 1) < HEAD_DIM
    g_gla = g_gla_ref[...]

    def input_stage(results):
        x = x_ref[0]
        proj, glr = _project(x, g_pre_ref[...], w_in_ref)
        yield
        q, k, v, gq, gk, gv, gr, log_a = _split_projection(proj, glr, cos_ref, sin_ref, w_gk2_ref, b_gk_ref)
        chunk_decay = jnp.concatenate(
            [jnp.sum(log_a[CHUNK * c:CHUNK * (c + 1)], axis=0, keepdims=True) for c in range(n_chunks)], axis=0)
        results.update(
            x=x, q=q.astype(BF16), k=k, v=v, k_forms=_lo_hi_forms(k), v_forms=_lo_hi_forms(v),
            gq=gq, gk=gk, gv=gv, gr=gr, log_a=log_a,
            bounded=jnp.min(chunk_decay) >= -BOUNDED_LOG_DECAY)
        yield

    def hand_over(results, mixed=True):
        kbd_scr[:, 0:CHUNK, :] = kbd_scr[:, tile:tile + CHUNK, :]
        vbd_scr[:, 0:CHUNK, :] = vbd_scr[:, tile:tile + CHUNK, :]
        for g in range(ATT_KV_HEADS):
            for e in range(2):
                kbd_scr[2 * g + e, CHUNK:CHUNK + tile, :] = results["k_forms"][g][e].astype(BF16)
                vbd_scr[2 * g + e, CHUNK:CHUNK + tile, :] = results["v_forms"][g][e].astype(BF16)
        q_scr[...] = results["q"]
        gq_scr[...] = results["gq"]
        gk_scr[...] = results["gk"]
        gv_scr[...] = results["gv"]
        gr_scr[...] = results["gr"]
        la_scr[...] = results["log_a"]
        bounded_scr[0] = jnp.where(results["bounded"], 1, 0)
        kwin_ref[0] = results["k"][tile - WINDOW:tile, :]
        vwin_ref[0] = results["v"][tile - WINDOW:tile, :]
        xp2_scr[...] = xp1_scr[...]
        xp1_scr[...] = results["x"]
        if mixed:
            mixp_scr[...] = mix_scr[...]

    def attention_chunk(c):
        rows = slice(CHUNK * c, CHUNK * (c + 1))
        rows2 = slice(CHUNK * c, CHUNK * (c + 2))
        bias = bias_ref[jnp.where(t_step == 0, 1, 0)] if c == 0 else bias_ref[0]
        scores = []
        for g in range(ATT_KV_HEADS):
            lhs = jnp.concatenate([q_scr[rows, 2 * LANES * g:2 * LANES * g + LANES],
                                   q_scr[rows, 2 * LANES * g + LANES:2 * LANES * (g + 1)]], axis=0)
            kcat = jnp.concatenate([kbd_scr[2 * g, rows2, :], kbd_scr[2 * g + 1, rows2, :]], axis=0)
            scores.append(_dot_nt(lhs, kcat) + bias)
        yield
        for g in range(ATT_KV_HEADS):
            s = scores[g]
            p_blocks = []
            inv = []
            for blk in range(2):
                p_row = []
                inv_row = []
                for e in range(2):
                    sub = s[CHUNK * blk:CHUNK * (blk + 1), 2 * CHUNK * e:2 * CHUNK * (e + 1)]
                    sink = sinks_ref[4 * g + 2 * blk + e]
                    m = jnp.maximum(jnp.max(sub, axis=-1, keepdims=True), sink)
                    p = jnp.exp(sub - m)
                    den = jnp.sum(p, axis=-1, keepdims=True) + jnp.exp(sink - m)
                    p_row.append(p.astype(BF16))
                    inv_row.append(1.0 / den)
                p_blocks.append(jnp.concatenate(p_row, axis=1))
                inv.append(inv_row)
            p_all = jnp.concatenate(p_blocks, axis=0)
            vcat = jnp.concatenate([vbd_scr[2 * g, rows2, :], vbd_scr[2 * g + 1, rows2, :]], axis=0)
            o = _dot(p_all, vcat)
            for blk in range(2):
                scale = jnp.where(lo_lane, inv[blk][0], inv[blk][1])
                c0 = 2 * LANES * g + LANES * blk
                mix_scr[rows, c0:c0 + LANES] = (o[CHUNK * blk:CHUNK * (blk + 1)] * scale).astype(BF16)
        yield

    def gla_chunk(c, state, bounded_decay):
        rows = slice(CHUNK * c, CHUNK * (c + 1))
        gq_c = gq_scr[rows, :]
        gk_c = gk_scr[rows, :]
        la_c = la_scr[rows, :]
        b = _gla_exponents(la_c, gmat_ref[0:CHUNK, :])
        b_last = jnp.broadcast_to(b[CHUNK - 1:CHUNK, :], b.shape)
        q_dec_f = gq_c * jnp.exp(b)
        q_dec = q_dec_f.astype(BF16)
        k_dec_t = jnp.transpose(gk_c * jnp.exp(b_last - b)).astype(BF16)
        decay_t = jnp.exp(jnp.transpose(b_last))
        yield
        if bounded_decay:
            scores = _gla_intra_bounded(q_dec_f, gk_c, b, masks_ref[len(PROMPT_LEVELS) + 1])
        else:
            scores = _gla_intra(gq_c, gk_c, la_c, gmat_ref[CHUNK:, :], PROMPT_LEVELS, masks_ref)
        yield
        for h in range(GLA_HEADS):
            p_ = h // 2
            v_h = gv_scr[rows, GLA_DV * h:GLA_DV * (h + 1)].astype(BF16)
            qd_p = q_dec[:, LANES * p_:LANES * (p_ + 1)]
            qd_h = jnp.where(_head_lanes(h, CHUNK), qd_p, jnp.zeros_like(qd_p))
            s0_pair = jnp.concatenate([state[2 * p_], state[2 * p_ + 1]], axis=0).astype(BF16)
            o_h = _dot(scores[:, CHUNK * h:CHUNK * (h + 1)].astype(BF16), v_h) + _dot(qd_h, s0_pair)
            out_h = _gla_out_gate(o_h, gr_scr[rows, GLA_DV * h:GLA_DV * (h + 1)], g_gla)
            mix_scr[rows, ATT_W + GLA_DV * h:ATT_W + GLA_DV * (h + 1)] = out_h.astype(BF16)
        new_state = [decay_t[GLA_DK * h:GLA_DK * (h + 1), :] * state[h]
                     + _dot(k_dec_t[GLA_DK * h:GLA_DK * (h + 1), :],
                            gv_scr[rows, GLA_DV * h:GLA_DV * (h + 1)].astype(BF16))
                     for h in range(GLA_HEADS)]
        state[:] = new_state
        yield

    def write_y(y):
        y_ref[0] = y

    def output_stage():
        return _output_stage(xp2_scr, mixp_scr, g_post_ref, g_fpre_ref, g_fpost_ref,
                             w_out_ref, w_up_ref, w_down_ref, write_y)

    def run_step(bounded_decay):
        keep = jnp.where(t_step == 0, 0.0, 1.0)
        state = [s_scr[GLA_DK * h:GLA_DK * (h + 1), :] * keep for h in range(GLA_HEADS)]
        results = {}
        out = output_stage()
        inp = input_stage(results)
        att = [attention_chunk(c) for c in range(n_chunks)]
        gla = [gla_chunk(c, state, bounded_decay) for c in range(n_chunks)]
        next(out)
        for c in range(n_chunks):
            next(att[c])
            next(gla[c])
        next(inp)
        for c in range(n_chunks):
            next(att[c])
            next(gla[c])
        next(out)
        for c in range(n_chunks):
            next(out)
            next(gla[c])
        next(inp)
        for _ in out:
            pass
        s_all = jnp.concatenate(state, axis=0)
        s_scr[...] = s_all
        sout_ref[0] = s_all
        hand_over(results)

    def finish_last_tile():
        for _ in output_stage():
            pass

    def start_first_tile():
        results = {}
        for _ in input_stage(results):
            pass
        hand_over(results, mixed=False)

    lax.switch(branch, [lambda: run_step(True), lambda: run_step(False), finish_last_tile, start_first_tile])


def _sample_kernel(sinks_ref, x_ref, cos_ref, sin_ref, gmat_ref, masks_ref, bias_win_ref, bias_new_ref,
                   ck_ref, cv_ref, st_ref,
                   g_pre_ref, g_post_ref, g_fpre_ref, g_fpost_ref, g_gla_ref, b_gk_ref,
                   w_in_ref, w_gk2_ref, w_out_ref, w_up_ref, w_down_ref,
                   y_ref, kwin_ref, vwin_ref, sout_ref,
                   *, dec_seq, levels):
    results = {}
    for _ in _sample_tile_stages(sinks_ref, x_ref, cos_ref, sin_ref, gmat_ref, masks_ref, bias_win_ref,
                                 bias_new_ref, ck_ref, cv_ref, st_ref, g_pre_ref, g_gla_ref, b_gk_ref,
                                 w_in_ref, w_gk2_ref, kwin_ref, vwin_ref, sout_ref, results,
                                 dec_seq=dec_seq, levels=levels):
        pass

    def write_y(y):
        y_ref[...] = y

    for _ in _output_stage(results["x"], results["mix"], g_post_ref, g_fpre_ref, g_fpost_ref,
                           w_out_ref, w_up_ref, w_down_ref, write_y):
        pass


def _sample_tile_stages(sinks_ref, x_ref, cos_ref, sin_ref, gmat_ref, masks_ref, bias_win_ref, bias_new_ref,
                        ck_ref, cv_ref, st_ref, g_pre_ref, g_gla_ref, b_gk_ref, w_in_ref, w_gk2_ref,
                        kwin_ref, vwin_ref, sout_ref, results, *, dec_seq, levels):
    n_seqs = ck_ref.shape[0]
    x = x_ref[...]
    proj, glr = _project(x, g_pre_ref[...], w_in_ref)
    yield
    q, k, v, gq, gk, gv, gr, log_a = _split_projection(proj, glr, cos_ref, sin_ref, w_gk2_ref, b_gk_ref)

    kwin_ref[:, 0:WINDOW - dec_seq, :] = ck_ref[:, dec_seq:WINDOW, :]
    vwin_ref[:, 0:WINDOW - dec_seq, :] = cv_ref[:, dec_seq:WINDOW, :]
    kwin_ref[:, WINDOW - dec_seq:WINDOW, :] = k.reshape(n_seqs, dec_seq, KV_W)
    vwin_ref[:, WINDOW - dec_seq:WINDOW, :] = v.reshape(n_seqs, dec_seq, KV_W)

    def seq_rows(a, s):
        return a[dec_seq * s:dec_seq * (s + 1)]

    def gather_seq(per_head, s):
        return jnp.concatenate([seq_rows(a, s) for a in per_head], axis=0)

    def scatter_heads(per_seq, h):
        return jnp.concatenate([seq_rows(a, h) for a in per_seq], axis=0)

    lo_lane = lax.broadcasted_iota(jnp.int32, (CHUNK, LANES), 1) < HEAD_DIM
    q_heads = []
    for j in range(ATT_W // LANES):
        blk = q[:, LANES * j:LANES * (j + 1)]
        rolled = pltpu.roll(blk, HEAD_DIM, 1)
        g = (2 * j) // (ATT_HEADS // ATT_KV_HEADS)
        keep = lo_lane if g == 0 else jnp.logical_not(lo_lane)
        for e in range(2):
            q_heads.append(jnp.where(keep, blk if e == g else rolled, 0.0))
    s_win = [_dot_nt(gather_seq(q_heads, s), ck_ref[s]) for s in range(n_seqs)]
    k_b = k.astype(BF16)
    v_b = v.astype(BF16)
    s_new = [_dot_nt(q_heads[h].astype(BF16), k_b) for h in range(ATT_HEADS)]
    b = _gla_exponents(log_a, gmat_ref[0:CHUNK, :])
    yield
    bias_win = bias_win_ref[...]
    bias_new = bias_new_ref[...]
    p_win = []
    o_new = []
    inv_den = []
    for h in range(ATT_HEADS):
        sw = scatter_heads(s_win, h) + bias_win
        sn = s_new[h] + bias_new
        sink = sinks_ref[h]
        m = jnp.maximum(jnp.maximum(jnp.max(sw, axis=-1, keepdims=True),
                                    jnp.max(sn, axis=-1, keepdims=True)), sink)
        pw = jnp.exp(sw - m)
        pn = jnp.exp(sn - m)
        den = (jnp.sum(pw, axis=-1, keepdims=True) + jnp.sum(pn, axis=-1, keepdims=True)
               + jnp.exp(sink - m))
        p_win.append(pw)
        o_new.append(_dot(pn.astype(BF16), v_b))
        inv_den.append(1.0 / den)
    o_win = [_dot(gather_seq(p_win, s), cv_ref[s]) for s in range(n_seqs)]
    o_heads = [(scatter_heads(o_win, h) + o_new[h]) * inv_den[h] for h in range(ATT_HEADS)]
    mix_cols = []
    for j in range(ATT_W // LANES):
        g = (2 * j) // (ATT_HEADS // ATT_KV_HEADS)
        first = o_heads[2 * j] if g == 0 else pltpu.roll(o_heads[2 * j], HEAD_DIM, 1)
        second = o_heads[2 * j + 1] if g == 1 else pltpu.roll(o_heads[2 * j + 1], HEAD_DIM, 1)
        mix_cols.append(jnp.where(lo_lane, first, second).astype(BF16))

    b3 = b.reshape(n_seqs, dec_seq, GK_W)
    b_last = jnp.broadcast_to(b3[:, dec_seq - 1:dec_seq, :], b3.shape).reshape(CHUNK, GK_W)
    q_dec = gq * jnp.exp(b)
    k_dec_t = jnp.transpose(gk * jnp.exp(b_last - b))
    decay_t = jnp.exp(jnp.transpose(b_last))
    scores = _gla_intra(gq, gk, log_a, gmat_ref[CHUNK:, :], levels, masks_ref)
    yield
    lane_k = lax.broadcasted_iota(jnp.int32, (dec_seq, GK_W), 1)
    lane_t = lax.broadcasted_iota(jnp.int32, (GLA_DK, LANES), 1)
    o_inter = []
    for s in range(n_seqs):
        qd = seq_rows(q_dec, s)
        lhs = jnp.concatenate(
            [jnp.where((lane_k >= GLA_DK * h) & (lane_k < GLA_DK * (h + 1)), qd, 0.0)
             for h in range(GLA_HEADS)], axis=0)
        o_inter.append(_dot(lhs, st_ref[s]))
    g_gla = g_gla_ref[...]
    for h in range(GLA_HEADS):
        cs = slice(GLA_DV * h, GLA_DV * (h + 1))
        hs = slice(GLA_DK * h, GLA_DK * (h + 1))
        v_h = gv[:, cs].astype(BF16)
        o_h = _dot(scores[:, CHUNK * h:CHUNK * (h + 1)].astype(BF16), v_h) + scatter_heads(o_inter, h)
        mix_cols.append(_gla_out_gate(o_h, gr[:, cs], g_gla).astype(BF16))
        kd_h = k_dec_t[hs, :]
        lhs = jnp.concatenate(
            [jnp.where((lane_t >= dec_seq * s) & (lane_t < dec_seq * (s + 1)), kd_h, 0.0).astype(BF16)
             for s in range(n_seqs)], axis=0)
        upd = _dot(lhs, v_h)
        for s in range(n_seqs):
            a_col = jnp.broadcast_to(decay_t[hs, dec_seq * s:dec_seq * s + 1], (GLA_DK, GLA_DV))
            sout_ref[s, hs, :] = a_col * st_ref[s, hs, :] + upd[GLA_DK * s:GLA_DK * (s + 1)]
    results.update(x=x, mix=jnp.concatenate(mix_cols, axis=1))


def _gla_constants(seg, levels):
    t = np.arange(CHUNK)[:, None]
    u = np.arange(CHUNK)[None, :]
    same = (t // seg) == (u // seg)
    blocks = [same & (u <= t)]
    masks = [np.eye(CHUNK, dtype=bool)]
    for m in levels:
        ref = (t // (2 * m)) * (2 * m) + m - 1
        second = (t & m) != 0
        blocks.append(np.where(second, (u > ref) & (u <= t), (u > t) & (u <= ref)))
        masks.append(((t // (2 * m)) == (u // (2 * m))) & second & ((u & m) == 0))
    masks.append(blocks[0])
    gmat = jnp.asarray(np.concatenate(blocks, axis=0).astype(np.float32), dtype=BF16)
    return gmat, jnp.asarray(np.stack(masks).astype(np.float32))


def _attention_bias():
    t = (np.arange(2 * CHUNK) % CHUNK)[:, None]
    j = (np.arange(4 * CHUNK) % (2 * CHUNK))[None, :]
    band = (j >= t + 1) & (j <= t + WINDOW)
    first = band & (j >= CHUNK)
    neg = np.float32(-np.inf)
    return jnp.asarray(np.stack([np.where(band, np.float32(0), neg), np.where(first, np.float32(0), neg)]))


def _sample_attention_bias(dec_seq):
    r = np.arange(CHUNK)[:, None]
    c = np.arange(CHUNK)[None, :]
    t = r % dec_seq
    win = c > t
    new = ((r // dec_seq) == (c // dec_seq)) & ((c % dec_seq) <= t)
    neg = np.float32(-np.inf)
    zero = np.float32(0)
    return jnp.asarray(np.where(win, zero, neg)), jnp.asarray(np.where(new, zero, neg))


def _rope_tables(pos):
    half = HEAD_DIM // 2
    inv = ROPE_THETA ** (-jnp.arange(half, dtype=F32) / half)
    ang = inv[:, None] * pos.astype(F32)[None, :]
    return jnp.cos(ang), jnp.sin(ang)


def _const_spec(shape):
    zeros = (0,) * len(shape)
    return pl.BlockSpec(shape, lambda *_: zeros, pipeline_mode=pl.Buffered(1))


def kernel(x_prompt, x_sample, cache_k, cache_v, state_gla, w_in, w_gk2, b_gk, g_gla, sinks, w_out, g_mix_pre, g_mix_post, g_ffn_pre, g_ffn_post, w_up, w_down):
    depth = w_in.shape[0]
    assert depth == 1, "single trunk layer"
    batch, seq, d_model = x_prompt.shape
    dec_batch, dec_seq, _ = x_sample.shape
    d_ff = w_up.shape[2]
    assert seq % PROMPT_TILE == 0 and dec_batch % SAMPLE_SEQS == 0
    assert SAMPLE_SEQS * dec_seq == CHUNK and CHUNK % dec_seq == 0 and d_ff % FF_CHUNK == 0
    assert w_in.shape[2] == MAIN_W + GLA_RANK

    w_in_b = w_in[0].astype(BF16)
    w_gk2_b = w_gk2[0].astype(BF16)
    w_out_b = w_out[0].astype(BF16)
    w_up_b = w_up[0].astype(BF16)
    w_down_b = w_down[0].astype(BF16)
    g_pre = g_mix_pre[0][None, :]
    g_post = g_mix_post[0][None, :]
    g_fpre = g_ffn_pre[0][None, :]
    g_fpost = g_ffn_post[0][None, :]
    g_gla2 = g_gla[0][None, :]
    b_gk2 = b_gk[0][None, :]
    sinks1 = sinks[0]

    weights = (g_pre, g_post, g_fpre, g_fpost, g_gla2, b_gk2, w_in_b, w_gk2_b, w_out_b, w_up_b, w_down_b)
    weight_specs = [_const_spec(w.shape) for w in weights]
    smem_spec = pl.BlockSpec(memory_space=pltpu.SMEM)

    cos_p, sin_p = _rope_tables(jnp.arange(seq, dtype=jnp.int32))
    gmat_p, masks_p = _gla_constants(CHUNK, PROMPT_LEVELS)
    bias = _attention_bias()
    n_t = seq // PROMPT_TILE
    tile = PROMPT_TILE
    n_tiles = batch * n_t

    def cur_tile(j):
        return jnp.minimum(j, n_tiles - 1)

    def mix_tile(j):
        return jnp.clip(j - 1, 0, n_tiles - 1)

    def done_tile(j):
        return jnp.clip(j - 2, 0, n_tiles - 1)

    tab_spec = pl.BlockSpec((HEAD_DIM // 2, tile), lambda j: (0, cur_tile(j) % n_t))
    seq_out = lambda j: (cur_tile(j) // n_t, 0, 0)
    yp, kp, vp, sp = pl.pallas_call(
        functools.partial(_prompt_kernel, tiles_per_seq=n_t, n_tiles=n_tiles),
        grid=(n_tiles + 2,),
        in_specs=[smem_spec,
                  pl.BlockSpec((1, tile, d_model), lambda j: (cur_tile(j) // n_t, cur_tile(j) % n_t, 0)),
                  tab_spec, tab_spec,
                  _const_spec(bias.shape), _const_spec(gmat_p.shape), _const_spec(masks_p.shape)] + weight_specs,
        out_specs=[pl.BlockSpec((1, tile, d_model), lambda j: (done_tile(j) // n_t, done_tile(j) % n_t, 0)),
                   pl.BlockSpec((1, WINDOW, KV_W), seq_out),
                   pl.BlockSpec((1, WINDOW, KV_W), seq_out),
                   pl.BlockSpec((1, GK_W, GLA_DV), lambda j: (mix_tile(j) // n_t, 0, 0))],
        out_shape=[jax.ShapeDtypeStruct((batch, seq, d_model), F32),
                   jax.ShapeDtypeStruct((batch, WINDOW, KV_W), F32),
                   jax.ShapeDtypeStruct((batch, WINDOW, KV_W), F32),
                   jax.ShapeDtypeStruct((batch, GK_W, GLA_DV), F32)],
        scratch_shapes=[pltpu.VMEM((tile, ATT_W), BF16),
                        pltpu.VMEM((4, CHUNK + tile, LANES), BF16),
                        pltpu.VMEM((4, CHUNK + tile, LANES), BF16),
                        pltpu.VMEM((tile, GK_W), F32),
                        pltpu.VMEM((tile, GK_W), F32),
                        pltpu.VMEM((tile, GV_W), F32),
                        pltpu.VMEM((tile, GV_W), F32),
                        pltpu.VMEM((tile, GK_W), F32),
                        pltpu.VMEM((GK_W, GLA_DV), F32),
                        pltpu.VMEM((tile, d_model), BF16),
                        pltpu.VMEM((tile, d_model), BF16),
                        pltpu.VMEM((tile, d_model), F32),
                        pltpu.VMEM((tile, d_model), F32),
                        pltpu.SMEM((1,), jnp.int32)],
        compiler_params=pltpu.CompilerParams(
            dimension_semantics=("arbitrary",), vmem_limit_bytes=VMEM_LIMIT_BYTES),
        name="prompt_layer",
    )(sinks1, x_prompt, cos_p, sin_p, bias, gmat_p, masks_p, *weights)

    levels_s = tuple(m for m in PROMPT_LEVELS if m < dec_seq)
    pos_s = PAST_LEN + jnp.arange(dec_seq, dtype=jnp.int32)
    cos_s, sin_s = (jnp.tile(a, (1, SAMPLE_SEQS)) for a in _rope_tables(pos_s))
    gmat_s, masks_s = _gla_constants(dec_seq, levels_s)
    bias_win, bias_new = _sample_attention_bias(dec_seq)
    rows = SAMPLE_SEQS * dec_seq
    n_g = dec_batch // SAMPLE_SEQS
    xs = x_sample.reshape(dec_batch * dec_seq, d_model)
    ck = cache_k[0].reshape(dec_batch, WINDOW, KV_W)
    cv = cache_v[0].reshape(dec_batch, WINDOW, KV_W)
    st = state_gla[0].reshape(dec_batch, GK_W, GLA_DV)
    seq_spec = pl.BlockSpec((SAMPLE_SEQS, WINDOW, KV_W), lambda i: (i, 0, 0))
    st_spec = pl.BlockSpec((SAMPLE_SEQS, GK_W, GLA_DV), lambda i: (i, 0, 0))
    row_spec = pl.BlockSpec((rows, d_model), lambda i: (i, 0))
    ys, ks, vs, ss = pl.pallas_call(
        functools.partial(_sample_kernel, dec_seq=dec_seq, levels=levels_s),
        grid=(n_g,),
        in_specs=[smem_spec, row_spec,
                  _const_spec(cos_s.shape), _const_spec(sin_s.shape),
                  _const_spec(gmat_s.shape), _const_spec(masks_s.shape),
                  _const_spec(bias_win.shape), _const_spec(bias_new.shape),
                  seq_spec, seq_spec, st_spec] + weight_specs,
        out_specs=[row_spec, seq_spec, seq_spec, st_spec],
        out_shape=[jax.ShapeDtypeStruct((dec_batch * dec_seq, d_model), F32),
                   jax.ShapeDtypeStruct((dec_batch, WINDOW, KV_W), F32),
                   jax.ShapeDtypeStruct((dec_batch, WINDOW, KV_W), F32),
                   jax.ShapeDtypeStruct((dec_batch, GK_W, GLA_DV), F32)],
        compiler_params=pltpu.CompilerParams(
            dimension_semantics=("arbitrary",), vmem_limit_bytes=VMEM_LIMIT_BYTES),
        name="sample_layer",
    )(sinks1, xs, cos_s, sin_s, gmat_s, masks_s, bias_win, bias_new, ck, cv, st, *weights)

    head_shape = (ATT_KV_HEADS, HEAD_DIM)
    state_shape = (GLA_HEADS, GLA_DK, GLA_DV)
    return (yp,
            ys.reshape(dec_batch, dec_seq, d_model),
            kp.reshape((1, batch, WINDOW) + head_shape),
            vp.reshape((1, batch, WINDOW) + head_shape),
            sp.reshape((1, batch) + state_shape),
            ks.reshape((1, dec_batch, WINDOW) + head_shape),
            vs.reshape((1, dec_batch, WINDOW) + head_shape),
            ss.reshape((1, dec_batch) + state_shape))
```

```python
import functools
import math

import numpy as np
import jax
import jax.numpy as jnp
from jax import lax
from jax.experimental import pallas as pl
from jax.experimental.pallas import tpu as pltpu

F32 = jnp.float32
BF16 = jnp.bfloat16

HEAD_DIM = 64
ATT_HEADS = 8
ATT_KV_HEADS = 2
WINDOW = 128
PAST_LEN = 16384
ROPE_THETA = 10000.0
GLA_HEADS = 4
GLA_DK = 64
GLA_DV = 128
GLA_RANK = 16
GLA_GATE_TEMP = 16.0
EPS = 1e-6
ATT_W = ATT_HEADS * HEAD_DIM
KV_W = ATT_KV_HEADS * HEAD_DIM
GK_W = GLA_HEADS * GLA_DK
GV_W = GLA_HEADS * GLA_DV
COL_Q = 0
COL_K = COL_Q + ATT_W
COL_V = COL_K + KV_W
COL_GQ = COL_V + KV_W
COL_GK = COL_GQ + GK_W
COL_GV = COL_GK + GK_W
COL_GR = COL_GV + GV_W
MAIN_W = COL_GR + GV_W
QK_SCALE = HEAD_DIM ** -0.5
GLA_SCALE = GLA_DK ** -0.5

CHUNK = 128
LANES = 128
PROMPT_TILE = 256
SAMPLE_SEQS = 16
FF_CHUNK = 1024
PROMPT_LEVELS = (1, 2, 4, 8, 16, 32, 64)
BOUNDED_LOG_DECAY = 30.0
VMEM_LIMIT_BYTES = 56 * 1024 * 1024


def _dot(a, b):
    return jnp.dot(a, b, preferred_element_type=F32)


def _dot_nt(a, b):
    return lax.dot_general(a, b, (((1,), (1,)), ((), ())), preferred_element_type=F32)


def _rms(x, g):
    ms = jnp.mean(x * x, axis=-1, keepdims=True)
    return x * lax.rsqrt(ms + EPS) * g


def _rope(x, cos, sin_a, sin_b):
    return x * cos + pltpu.roll(x, 96, 1) * sin_a + pltpu.roll(x, 32, 1) * sin_b


def _rope_lanes(cos, sin):
    reps = LANES // cos.shape[0]
    cos_l = jnp.transpose(jnp.concatenate([cos] * reps, axis=0))
    sin_l = jnp.transpose(jnp.concatenate([sin] * reps, axis=0))
    lane = lax.broadcasted_iota(jnp.int32, cos_l.shape, 1)
    first_half = (lane & (HEAD_DIM // 2)) == 0
    return cos_l, jnp.where(first_half, -sin_l, 0.0), jnp.where(first_half, 0.0, sin_l)


def _split3(x):
    hi = x.astype(BF16)
    r1 = x - hi.astype(F32)
    mid = r1.astype(BF16)
    lo = (r1 - mid.astype(F32)).astype(BF16)
    return hi, mid, lo


def _project(x, g_pre, w_in_ref):
    h = _rms(x, g_pre).astype(BF16)
    return _dot(h, w_in_ref[:, 0:MAIN_W]), _dot(h, w_in_ref[:, MAIN_W:])


def _split_projection(proj, glr, cos_ref, sin_ref, w_gk2_ref, b_gk_ref):
    cos, sin_a, sin_b = _rope_lanes(cos_ref[...], sin_ref[...])
    z = _dot(glr.astype(BF16), w_gk2_ref[...]) + b_gk_ref[...]
    log_a = (jnp.minimum(z, 0.0) - jnp.log1p(jnp.exp(-jnp.abs(z)))) * (1.0 / GLA_GATE_TEMP)
    q = jnp.concatenate(
        [_rope(proj[:, COL_Q + LANES * j:COL_Q + LANES * (j + 1)], cos, sin_a, sin_b) * QK_SCALE
         for j in range(ATT_W // LANES)], axis=1)
    k = _rope(proj[:, COL_K:COL_V], cos, sin_a, sin_b)
    v = proj[:, COL_V:COL_GQ]
    gq = proj[:, COL_GQ:COL_GK] * GLA_SCALE
    gk = proj[:, COL_GK:COL_GV]
    gv = proj[:, COL_GV:COL_GR]
    gr = proj[:, COL_GR:MAIN_W]
    return q, k, v, gq, gk, gv, gr, log_a


def _output_stage(x_prev, mix_prev, g_post_ref, g_fpre_ref, g_fpost_ref, w_out_ref, w_up_ref, w_down_ref, write):
    mix = _dot(mix_prev[...], w_out_ref[...])
    yield
    x1 = x_prev[...] + _rms(mix, g_post_ref[...])
    h2 = _rms(x1, g_fpre_ref[...]).astype(BF16)
    f = None
    for i in range(w_up_ref.shape[1] // FF_CHUNK):
        u = jnp.maximum(_dot(h2, w_up_ref[:, FF_CHUNK * i:FF_CHUNK * (i + 1)]), 0.0)
        yield
        part = _dot((u * u).astype(BF16), w_down_ref[FF_CHUNK * i:FF_CHUNK * (i + 1), :])
        f = part if f is None else f + part
        yield
    write(x1 + _rms(f, g_fpost_ref[...]))


def _lo_hi_forms(a):
    lo = lax.broadcasted_iota(jnp.int32, a.shape, 1) < HEAD_DIM
    r = pltpu.roll(a, HEAD_DIM, 1)
    zero = jnp.zeros_like(a)
    return ((jnp.where(lo, a, zero), jnp.where(lo, zero, r)),
            (jnp.where(lo, r, zero), jnp.where(lo, zero, a)))


def _gla_exponents(log_a, gmat):
    hi, mid, lo = _split3(log_a)
    return _dot(gmat, hi) + _dot(gmat, mid) + _dot(gmat, lo)


def _head_lanes(h, rows):
    lane = lax.broadcasted_iota(jnp.int32, (rows, LANES), 1)
    e = h % 2
    return (lane >= GLA_DK * e) & (lane < GLA_DK * (e + 1))


def _gla_intra(gq, gk, log_a, gmat_lv, levels, masks_ref):
    expo = _gla_exponents(log_a, gmat_lv)
    row = lax.broadcasted_iota(jnp.int32, gq.shape, 0)
    terms = [(gq.astype(BF16), gk.astype(BF16), 0)]
    for li, m in enumerate(levels):
        ex = jnp.exp(expo[CHUNK * li:CHUNK * (li + 1)])
        z = (jnp.where((row & m) != 0, gq, gk) * ex).astype(BF16)
        terms.append((z, z, 1 + li))
    scores = []
    for h in range(GLA_HEADS):
        p = h // 2
        in_head = _head_lanes(h, gq.shape[0])
        acc = None
        for zq, zk, mi in terms:
            zq_p = zq[:, LANES * p:LANES * (p + 1)]
            zk_p = zk[:, LANES * p:LANES * (p + 1)]
            zk_h = jnp.where(in_head, zk_p, jnp.zeros_like(zk_p))
            t = _dot_nt(zq_p, zk_h) * masks_ref[mi]
            acc = t if acc is None else acc + t
        scores.append(acc)
    return jnp.concatenate(scores, axis=1)


def _gla_intra_bounded(q_dec, gk, b, causal):
    k_grow_t = jnp.transpose(gk * jnp.exp(-b)).astype(BF16)
    scores = []
    for h in range(GLA_HEADS):
        p = h // 2
        qd_p = q_dec[:, LANES * p:LANES * (p + 1)]
        qd_h = jnp.where(_head_lanes(h, q_dec.shape[0]), qd_p, jnp.zeros_like(qd_p))
        scores.append(_dot(qd_h, k_grow_t[LANES * p:LANES * (p + 1), :]) * causal)
    return jnp.concatenate(scores, axis=1)


def _gla_out_gate(o_h, gr_h, g_gla):
    on = _rms(o_h, g_gla)
    return on * (gr_h * (1.0 / (1.0 + jnp.exp(-gr_h))))


def _prompt_kernel(sinks_ref, x_ref, cos_ref, sin_ref, bias_ref, gmat_ref, masks_ref,
                   g_pre_ref, g_post_ref, g_fpre_ref, g_fpost_ref, g_gla_ref, b_gk_ref,
                   w_in_ref, w_gk2_ref, w_out_ref, w_up_ref, w_down_ref,
                   y_ref, kwin_ref, vwin_ref, sout_ref,
                   q_scr, kbd_scr, vbd_scr, gq_scr, gk_scr, gv_scr, gr_scr, la_scr, s_scr,
                   mix_scr, mixp_scr, xp1_scr, xp2_scr, bounded_scr, *, tiles_per_seq, n_tiles):
    j = pl.program_id(0)
    t_step = lax.rem(jnp.clip(j - 1, 0, n_tiles - 1), tiles_per_seq)
    tile = x_ref.shape[1]
    n_chunks = tile // CHUNK

    @pl.when(j == 0)
    def _():
        for ref in (s_scr, q_scr, kbd_scr, vbd_scr, gq_scr, gk_scr, gv_scr, gr_scr, la_scr,
                    mixp_scr, xp1_scr, xp2_scr):
            ref[...] = jnp.zeros_like(ref)
        bounded_scr[0] = 1

    branch = jnp.where(j == 0, 3, jnp.where(j == n_tiles + 1, 2, jnp.where(bounded_scr[0] == 1, 0, 1)))

    lo_lane = lax.broadcasted_iota(jnp.int32, (CHUNK, LANES), 1) < HEAD_DIM
    g_gla = g_gla_ref[...]

    def input_stage(results):
        x = x_ref[0]
        proj, glr = _project(x, g_pre_ref[...], w_in_ref)
        yield
        q, k, v, gq, gk, gv, gr, log_a = _split_projection(proj, glr, cos_ref, sin_ref, w_gk2_ref, b_gk_ref)
        chunk_decay = jnp.concatenate(
            [jnp.sum(log_a[CHUNK * c:CHUNK * (c + 1)], axis=0, keepdims=True) for c in range(n_chunks)], axis=0)
        results.update(
            x=x, q=q.astype(BF16), k=k, v=v, k_forms=_lo_hi_forms(k), v_forms=_lo_hi_forms(v),
            gq=gq, gk=gk, gv=gv, gr=gr, log_a=log_a,
            bounded=jnp.min(chunk_decay) >= -BOUNDED_LOG_DECAY)
        yield

    def hand_over(results, mixed=True):
        kbd_scr[:, 0:CHUNK, :] = kbd_scr[:, tile:tile + CHUNK, :]
        vbd_scr[:, 0:CHUNK, :] = vbd_scr[:, tile:tile + CHUNK, :]
        for g in range(ATT_KV_HEADS):
            for e in range(2):
                kbd_scr[2 * g + e, CHUNK:CHUNK + tile, :] = results["k_forms"][g][e].astype(BF16)
                vbd_scr[2 * g + e, CHUNK:CHUNK + tile, :] = results["v_forms"][g][e].astype(BF16)
        q_scr[...] = results["q"]
        gq_scr[...] = results["gq"]
        gk_scr[...] = results["gk"]
        gv_scr[...] = results["gv"]
        gr_scr[...] = results["gr"]
        la_scr[...] = results["log_a"]
        bounded_scr[0] = jnp.where(results["bounded"], 1, 0)
        kwin_ref[0] = jnp.transpose(results["k"][tile - WINDOW:tile, :])
        vwin_ref[0] = jnp.transpose(results["v"][tile - WINDOW:tile, :])
        xp2_scr[...] = xp1_scr[...]
        xp1_scr[...] = results["x"]
        if mixed:
            mixp_scr[...] = mix_scr[...]

    def attention_chunk(c):
        rows = slice(CHUNK * c, CHUNK * (c + 1))
        rows2 = slice(CHUNK * c, CHUNK * (c + 2))
        bias = bias_ref[jnp.where(t_step == 0, 1, 0)] if c == 0 else bias_ref[0]
        scores = []
        for g in range(ATT_KV_HEADS):
            lhs = jnp.concatenate([q_scr[rows, 2 * LANES * g:2 * LANES * g + LANES],
                                   q_scr[rows, 2 * LANES * g + LANES:2 * LANES * (g + 1)]], axis=0)
            kcat = jnp.concatenate([kbd_scr[2 * g, rows2, :], kbd_scr[2 * g + 1, rows2, :]], axis=0)
            scores.append(_dot_nt(lhs, kcat) + bias)
        yield
        for g in range(ATT_KV_HEADS):
            s = scores[g]
            p_blocks = []
            inv = []
            for blk in range(2):
                p_row = []
                inv_row = []
                for e in range(2):
                    sub = s[CHUNK * blk:CHUNK * (blk + 1), 2 * CHUNK * e:2 * CHUNK * (e + 1)]
                    sink = sinks_ref[4 * g + 2 * blk + e]
                    m = jnp.maximum(jnp.max(sub, axis=-1, keepdims=True), sink)
                    p = jnp.exp(sub - m)
                    den = jnp.sum(p, axis=-1, keepdims=True) + jnp.exp(sink - m)
                    p_row.append(p.astype(BF16))
                    inv_row.append(1.0 / den)
                p_blocks.append(jnp.concatenate(p_row, axis=1))
                inv.append(inv_row)
            p_all = jnp.concatenate(p_blocks, axis=0)
            vcat = jnp.concatenate([vbd_scr[2 * g, rows2, :], vbd_scr[2 * g + 1, rows2, :]], axis=0)
            o = _dot(p_all, vcat)
            for blk in range(2):
                scale = jnp.where(lo_lane, inv[blk][0], inv[blk][1])
                c0 = 2 * LANES * g + LANES * blk
                mix_scr[rows, c0:c0 + LANES] = (o[CHUNK * blk:CHUNK * (blk + 1)] * scale).astype(BF16)
        yield

    def gla_chunk(c, state, bounded_decay):
        rows = slice(CHUNK * c, CHUNK * (c + 1))
        gq_c = gq_scr[rows, :]
        gk_c = gk_scr[rows, :]
        la_c = la_scr[rows, :]
        b = _gla_exponents(la_c, gmat_ref[0:CHUNK, :])
        b_last = jnp.broadcast_to(b[CHUNK - 1:CHUNK, :], b.shape)
        q_dec_f = gq_c * jnp.exp(b)
        q_dec = q_dec_f.astype(BF16)
        k_dec_t = jnp.transpose(gk_c * jnp.exp(b_last - b)).astype(BF16)
        decay_t = jnp.exp(jnp.transpose(b_last))
        yield
        if bounded_decay:
            scores = _gla_intra_bounded(q_dec_f, gk_c, b, masks_ref[len(PROMPT_LEVELS) + 1])
        else:
            scores = _gla_intra(gq_c, gk_c, la_c, gmat_ref[CHUNK:, :], PROMPT_LEVELS, masks_ref)
        yield
        for h in range(GLA_HEADS):
            p_ = h // 2
            v_h = gv_scr[rows, GLA_DV * h:GLA_DV * (h + 1)].astype(BF16)
            qd_p = q_dec[:, LANES * p_:LANES * (p_ + 1)]
            qd_h = jnp.where(_head_lanes(h, CHUNK), qd_p, jnp.zeros_like(qd_p))
            s0_pair = jnp.concatenate([state[2 * p_], state[2 * p_ + 1]], axis=0).astype(BF16)
            o_h = _dot(scores[:, CHUNK * h:CHUNK * (h + 1)].astype(BF16), v_h) + _dot(qd_h, s0_pair)
            out_h = _gla_out_gate(o_h, gr_scr[rows, GLA_DV * h:GLA_DV * (h + 1)], g_gla)
            mix_scr[rows, ATT_W + GLA_DV * h:ATT_W + GLA_DV * (h + 1)] = out_h.astype(BF16)
        new_state = [decay_t[GLA_DK * h:GLA_DK * (h + 1), :] * state[h]
                     + _dot(k_dec_t[GLA_DK * h:GLA_DK * (h + 1), :],
                            gv_scr[rows, GLA_DV * h:GLA_DV * (h + 1)].astype(BF16))
                     for h in range(GLA_HEADS)]
        state[:] = new_state
        yield

    def write_y(y):
        y_ref[0] = y

    def output_stage():
        return _output_stage(xp2_scr, mixp_scr, g_post_ref, g_fpre_ref, g_fpost_ref,
                             w_out_ref, w_up_ref, w_down_ref, write_y)

    def run_step(bounded_decay):
        keep = jnp.where(t_step == 0, 0.0, 1.0)
        state = [s_scr[GLA_DK * h:GLA_DK * (h + 1), :] * keep for h in range(GLA_HEADS)]
        results = {}
        out = output_stage()
        inp = input_stage(results)
        att = [attention_chunk(c) for c in range(n_chunks)]
        gla = [gla_chunk(c, state, bounded_decay) for c in range(n_chunks)]
        next(out)
        for c in range(n_chunks):
            next(att[c])
            next(gla[c])
        next(inp)
        for c in range(n_chunks):
            next(att[c])
            next(gla[c])
        next(out)
        for c in range(n_chunks):
            next(out)
            next(gla[c])
        next(inp)
        for _ in out:
            pass
        s_all = jnp.concatenate(state, axis=0)
        s_scr[...] = s_all
        sout_ref[0] = s_all
        hand_over(results)

    def finish_last_tile():
        for _ in output_stage():
            pass

    def start_first_tile():
        results = {}
        for _ in input_stage(results):
            pass
        hand_over(results, mixed=False)

    lax.switch(branch, [lambda: run_step(True), lambda: run_step(False), finish_last_tile, start_first_tile])


def _sample_kernel(sinks_ref, x_ref, cos_ref, sin_ref, gmat_ref, masks_ref, bias_win_ref, bias_new_ref,
                   ck_ref, cv_ref, st_ref,
                   g_pre_ref, g_post_ref, g_fpre_ref, g_fpost_ref, g_gla_ref, b_gk_ref,
                   w_in_ref, w_gk2_ref, w_out_ref, w_up_ref, w_down_ref,
                   y_ref, kwin_ref, vwin_ref, sout_ref,
                   *, dec_seq, levels):
    results = {}
    for _ in _sample_tile_stages(sinks_ref, x_ref, cos_ref, sin_ref, gmat_ref, masks_ref, bias_win_ref,
                                 bias_new_ref, ck_ref, cv_ref, st_ref, g_pre_ref, g_gla_ref, b_gk_ref,
                                 w_in_ref, w_gk2_ref, kwin_ref, vwin_ref, sout_ref, results,
                                 dec_seq=dec_seq, levels=levels):
        pass

    def write_y(y):
        y_ref[...] = y

    for _ in _output_stage(results["x"], results["mix"], g_post_ref, g_fpre_ref, g_fpost_ref,
                           w_out_ref, w_up_ref, w_down_ref, write_y):
        pass


def _sample_tile_stages(sinks_ref, x_ref, cos_ref, sin_ref, gmat_ref, masks_ref, bias_win_ref, bias_new_ref,
                        ck_ref, cv_ref, st_ref, g_pre_ref, g_gla_ref, b_gk_ref, w_in_ref, w_gk2_ref,
                        kwin_ref, vwin_ref, sout_ref, results, *, dec_seq, levels):
    n_seqs = ck_ref.shape[0]
    x = x_ref[...]
    proj, glr = _project(x, g_pre_ref[...], w_in_ref)
    yield
    q, k, v, gq, gk, gv, gr, log_a = _split_projection(proj, glr, cos_ref, sin_ref, w_gk2_ref, b_gk_ref)

    is_new = lax.broadcasted_iota(jnp.int32, (KV_W, WINDOW), 1) >= WINDOW - dec_seq
    for new, old_ref, out_ref in ((jnp.transpose(k), ck_ref, kwin_ref), (jnp.transpose(v), cv_ref, vwin_ref)):
        for s in range(n_seqs):
            shifted = pltpu.roll(old_ref[s], WINDOW - dec_seq, 1)
            tail = pltpu.roll(new, (WINDOW - dec_seq * (s + 1)) % CHUNK, 1)
            out_ref[s] = jnp.where(is_new, tail, shifted)

    def seq_rows(a, s):
        return a[dec_seq * s:dec_seq * (s + 1)]

    def gather_seq(per_head, s):
        return jnp.concatenate([seq_rows(a, s) for a in per_head], axis=0)

    def scatter_heads(per_seq, h):
        return jnp.concatenate([seq_rows(a, h) for a in per_seq], axis=0)

    lo_lane = lax.broadcasted_iota(jnp.int32, (CHUNK, LANES), 1) < HEAD_DIM
    q_heads = []
    for j in range(ATT_W // LANES):
        blk = q[:, LANES * j:LANES * (j + 1)]
        rolled = pltpu.roll(blk, HEAD_DIM, 1)
        g = (2 * j) // (ATT_HEADS // ATT_KV_HEADS)
        keep = lo_lane if g == 0 else jnp.logical_not(lo_lane)
        for e in range(2):
            q_heads.append(jnp.where(keep, blk if e == g else rolled, 0.0))
    s_win = [_dot(gather_seq(q_heads, s), ck_ref[s]) for s in range(n_seqs)]
    k_b = k.astype(BF16)
    v_b = v.astype(BF16)
    s_new = [_dot_nt(q_heads[h].astype(BF16), k_b) for h in range(ATT_HEADS)]
    b = _gla_exponents(log_a, gmat_ref[0:CHUNK, :])
    yield
    bias_win = bias_win_ref[...]
    bias_new = bias_new_ref[...]
    p_win = []
    o_new = []
    inv_den = []
    for h in range(ATT_HEADS):
        sw = scatter_heads(s_win, h) + bias_win
        sn = s_new[h] + bias_new
        sink = sinks_ref[h]
        m = jnp.maximum(jnp.maximum(jnp.max(sw, axis=-1, keepdims=True),
                                    jnp.max(sn, axis=-1, keepdims=True)), sink)
        pw = jnp.exp(sw - m)
        pn = jnp.exp(sn - m)
        den = (jnp.sum(pw, axis=-1, keepdims=True) + jnp.sum(pn, axis=-1, keepdims=True)
               + jnp.exp(sink - m))
        p_win.append(pw)
        o_new.append(_dot(pn.astype(BF16), v_b))
        inv_den.append(1.0 / den)
    o_win = [_dot_nt(gather_seq(p_win, s), cv_ref[s]) for s in range(n_seqs)]
    o_heads = [(scatter_heads(o_win, h) + o_new[h]) * inv_den[h] for h in range(ATT_HEADS)]
    mix_cols = []
    for j in range(ATT_W // LANES):
        g = (2 * j) // (ATT_HEADS // ATT_KV_HEADS)
        first = o_heads[2 * j] if g == 0 else pltpu.roll(o_heads[2 * j], HEAD_DIM, 1)
        second = o_heads[2 * j + 1] if g == 1 else pltpu.roll(o_heads[2 * j + 1], HEAD_DIM, 1)
        mix_cols.append(jnp.where(lo_lane, first, second).astype(BF16))

    b3 = b.reshape(n_seqs, dec_seq, GK_W)
    b_last = jnp.broadcast_to(b3[:, dec_seq - 1:dec_seq, :], b3.shape).reshape(CHUNK, GK_W)
    q_dec = gq * jnp.exp(b)
    k_dec_t = jnp.transpose(gk * jnp.exp(b_last - b))
    decay_t = jnp.exp(jnp.transpose(b_last))
    scores = _gla_intra(gq, gk, log_a, gmat_ref[CHUNK:, :], levels, masks_ref)
    yield
    lane_k = lax.broadcasted_iota(jnp.int32, (dec_seq, GK_W), 1)
    lane_t = lax.broadcasted_iota(jnp.int32, (GLA_DK, LANES), 1)
    o_inter = []
    for s in range(n_seqs):
        qd = seq_rows(q_dec, s)
        lhs = jnp.concatenate(
            [jnp.where((lane_k >= GLA_DK * h) & (lane_k < GLA_DK * (h + 1)), qd, 0.0)
             for h in range(GLA_HEADS)], axis=0)
        o_inter.append(_dot(lhs, st_ref[s]))
    g_gla = g_gla_ref[...]
    for h in range(GLA_HEADS):
        cs = slice(GLA_DV * h, GLA_DV * (h + 1))
        hs = slice(GLA_DK * h, GLA_DK * (h + 1))
        v_h = gv[:, cs].astype(BF16)
        o_h = _dot(scores[:, CHUNK * h:CHUNK * (h + 1)].astype(BF16), v_h) + scatter_heads(o_inter, h)
        mix_cols.append(_gla_out_gate(o_h, gr[:, cs], g_gla).astype(BF16))
        kd_h = k_dec_t[hs, :]
        lhs = jnp.concatenate(
            [jnp.where((lane_t >= dec_seq * s) & (lane_t < dec_seq * (s + 1)), kd_h, 0.0).astype(BF16)
             for s in range(n_seqs)], axis=0)
        upd = _dot(lhs, v_h)
        for s in range(n_seqs):
            a_col = jnp.broadcast_to(decay_t[hs, dec_seq * s:dec_seq * s + 1], (GLA_DK, GLA_DV))
            sout_ref[s, hs, :] = a_col * st_ref[s, hs, :] + upd[GLA_DK * s:GLA_DK * (s + 1)]
    results.update(x=x, mix=jnp.concatenate(mix_cols, axis=1))


def _gla_constants(seg, levels):
    t = np.arange(CHUNK)[:, None]
    u = np.arange(CHUNK)[None, :]
    same = (t // seg) == (u // seg)
    blocks = [same & (u <= t)]
    masks = [np.eye(CHUNK, dtype=bool)]
    for m in levels:
        ref = (t // (2 * m)) * (2 * m) + m - 1
        second = (t & m) != 0
        blocks.append(np.where(second, (u > ref) & (u <= t), (u > t) & (u <= ref)))
        masks.append(((t // (2 * m)) == (u // (2 * m))) & second & ((u & m) == 0))
    masks.append(blocks[0])
    gmat = jnp.asarray(np.concatenate(blocks, axis=0).astype(np.float32), dtype=BF16)
    return gmat, jnp.asarray(np.stack(masks).astype(np.float32))


def _attention_bias():
    t = (np.arange(2 * CHUNK) % CHUNK)[:, None]
    j = (np.arange(4 * CHUNK) % (2 * CHUNK))[None, :]
    band = (j >= t + 1) & (j <= t + WINDOW)
    first = band & (j >= CHUNK)
    neg = np.float32(-np.inf)
    return jnp.asarray(np.stack([np.where(band, np.float32(0), neg), np.where(first, np.float32(0), neg)]))


def _sample_attention_bias(dec_seq):
    r = np.arange(CHUNK)[:, None]
    c = np.arange(CHUNK)[None, :]
    t = r % dec_seq
    win = c > t
    new = ((r // dec_seq) == (c // dec_seq)) & ((c % dec_seq) <= t)
    neg = np.float32(-np.inf)
    zero = np.float32(0)
    return jnp.asarray(np.where(win, zero, neg)), jnp.asarray(np.where(new, zero, neg))


def _rope_tables(pos):
    half = HEAD_DIM // 2
    inv = ROPE_THETA ** (-jnp.arange(half, dtype=F32) / half)
    ang = inv[:, None] * pos.astype(F32)[None, :]
    return jnp.cos(ang), jnp.sin(ang)


def _const_spec(shape):
    zeros = (0,) * len(shape)
    return pl.BlockSpec(shape, lambda *_: zeros, pipeline_mode=pl.Buffered(1))


def kernel(x_prompt, x_sample, cache_k, cache_v, state_gla, w_in, w_gk2, b_gk, g_gla, sinks, w_out, g_mix_pre, g_mix_post, g_ffn_pre, g_ffn_post, w_up, w_down):
    depth = w_in.shape[0]
    assert depth == 1, "single trunk layer"
    batch, seq, d_model = x_prompt.shape
    dec_batch, dec_seq, _ = x_sample.shape
    d_ff = w_up.shape[2]
    assert seq % PROMPT_TILE == 0 and dec_batch % SAMPLE_SEQS == 0
    assert SAMPLE_SEQS * dec_seq == CHUNK and CHUNK % dec_seq == 0 and d_ff % FF_CHUNK == 0
    assert w_in.shape[2] == MAIN_W + GLA_RANK

    w_in_b = w_in[0].astype(BF16)
    w_gk2_b = w_gk2[0].astype(BF16)
    w_out_b = w_out[0].astype(BF16)
    w_up_b = w_up[0].astype(BF16)
    w_down_b = w_down[0].astype(BF16)
    g_pre = g_mix_pre[0][None, :]
    g_post = g_mix_post[0][None, :]
    g_fpre = g_ffn_pre[0][None, :]
    g_fpost = g_ffn_post[0][None, :]
    g_gla2 = g_gla[0][None, :]
    b_gk2 = b_gk[0][None, :]
    sinks1 = sinks[0]

    weights = (g_pre, g_post, g_fpre, g_fpost, g_gla2, b_gk2, w_in_b, w_gk2_b, w_out_b, w_up_b, w_down_b)
    weight_specs = [_const_spec(w.shape) for w in weights]
    smem_spec = pl.BlockSpec(memory_space=pltpu.SMEM)

    cos_p, sin_p = _rope_tables(jnp.arange(seq, dtype=jnp.int32))
    gmat_p, masks_p = _gla_constants(CHUNK, PROMPT_LEVELS)
    bias = _attention_bias()
    n_t = seq // PROMPT_TILE
    tile = PROMPT_TILE
    n_tiles = batch * n_t

    def cur_tile(j):
        return jnp.minimum(j, n_tiles - 1)

    def mix_tile(j):
        return jnp.clip(j - 1, 0, n_tiles - 1)

    def done_tile(j):
        return jnp.clip(j - 2, 0, n_tiles - 1)

    tab_spec = pl.BlockSpec((HEAD_DIM // 2, tile), lambda j: (0, cur_tile(j) % n_t))
    seq_out = lambda j: (cur_tile(j) // n_t, 0, 0)
    yp, kp, vp, sp = pl.pallas_call(
        functools.partial(_prompt_kernel, tiles_per_seq=n_t, n_tiles=n_tiles),
        grid=(n_tiles + 2,),
        in_specs=[smem_spec,
                  pl.BlockSpec((1, tile, d_model), lambda j: (cur_tile(j) // n_t, cur_tile(j) % n_t, 0)),
                  tab_spec, tab_spec,
                  _const_spec(bias.shape), _const_spec(gmat_p.shape), _const_spec(masks_p.shape)] + weight_specs,
        out_specs=[pl.BlockSpec((1, tile, d_model), lambda j: (done_tile(j) // n_t, done_tile(j) % n_t, 0)),
                   pl.BlockSpec((1, WINDOW, KV_W), seq_out),
                   pl.BlockSpec((1, WINDOW, KV_W), seq_out),
                   pl.BlockSpec((1, GK_W, GLA_DV), lambda j: (mix_tile(j) // n_t, 0, 0))],
        out_shape=[jax.ShapeDtypeStruct((batch, seq, d_model), F32),
                   jax.ShapeDtypeStruct((batch, WINDOW, KV_W), F32),
                   jax.ShapeDtypeStruct((batch, WINDOW, KV_W), F32),
                   jax.ShapeDtypeStruct((batch, GK_W, GLA_DV), F32)],
        scratch_shapes=[pltpu.VMEM((tile, ATT_W), BF16),
                        pltpu.VMEM((4, CHUNK + tile, LANES), BF16),
                        pltpu.VMEM((4, CHUNK + tile, LANES), BF16),
                        pltpu.VMEM((tile, GK_W), F32),
                        pltpu.VMEM((tile, GK_W), F32),
                        pltpu.VMEM((tile, GV_W), F32),
                        pltpu.VMEM((tile, GV_W), F32),
                        pltpu.VMEM((tile, GK_W), F32),
                        pltpu.VMEM((GK_W, GLA_DV), F32),
                        pltpu.VMEM((tile, d_model), BF16),
                        pltpu.VMEM((tile, d_model), BF16),
                        pltpu.VMEM((tile, d_model), F32),
                        pltpu.VMEM((tile, d_model), F32),
                        pltpu.SMEM((1,), jnp.int32)],
        compiler_params=pltpu.CompilerParams(
            dimension_semantics=("arbitrary",), vmem_limit_bytes=VMEM_LIMIT_BYTES),
        name="prompt_layer",
    )(sinks1, x_prompt, cos_p, sin_p, bias, gmat_p, masks_p, *weights)

    levels_s = tuple(m for m in PROMPT_LEVELS if m < dec_seq)
    pos_s = PAST_LEN + jnp.arange(dec_seq, dtype=jnp.int32)
    cos_s, sin_s = (jnp.tile(a, (1, SAMPLE_SEQS)) for a in _rope_tables(pos_s))
    gmat_s, masks_s = _gla_constants(dec_seq, levels_s)
    bias_win, bias_new = _sample_attention_bias(dec_seq)
    rows = SAMPLE_SEQS * dec_seq
    n_g = dec_batch // SAMPLE_SEQS
    xs = x_sample.reshape(dec_batch * dec_seq, d_model)
    ck = jnp.transpose(cache_k[0].reshape(dec_batch, WINDOW, KV_W), (0, 2, 1))
    cv = jnp.transpose(cache_v[0].reshape(dec_batch, WINDOW, KV_W), (0, 2, 1))
    st = state_gla[0].reshape(dec_batch, GK_W, GLA_DV)
    seq_spec = pl.BlockSpec((SAMPLE_SEQS, WINDOW, KV_W), lambda i: (i, 0, 0))
    st_spec = pl.BlockSpec((SAMPLE_SEQS, GK_W, GLA_DV), lambda i: (i, 0, 0))
    row_spec = pl.BlockSpec((rows, d_model), lambda i: (i, 0))
    ys, ks, vs, ss = pl.pallas_call(
        functools.partial(_sample_kernel, dec_seq=dec_seq, levels=levels_s),
        grid=(n_g,),
        in_specs=[smem_spec, row_spec,
                  _const_spec(cos_s.shape), _const_spec(sin_s.shape),
                  _const_spec(gmat_s.shape), _const_spec(masks_s.shape),
                  _const_spec(bias_win.shape), _const_spec(bias_new.shape),
                  seq_spec, seq_spec, st_spec] + weight_specs,
        out_specs=[row_spec, seq_spec, seq_spec, st_spec],
        out_shape=[jax.ShapeDtypeStruct((dec_batch * dec_seq, d_model), F32),
                   jax.ShapeDtypeStruct((dec_batch, WINDOW, KV_W), F32),
                   jax.ShapeDtypeStruct((dec_batch, WINDOW, KV_W), F32),
                   jax.ShapeDtypeStruct((dec_batch, GK_W, GLA_DV), F32)],
        compiler_params=pltpu.CompilerParams(
            dimension_semantics=("arbitrary",), vmem_limit_bytes=VMEM_LIMIT_BYTES),
        name="sample_layer",
    )(sinks1, xs, cos_s, sin_s, gmat_s, masks_s, bias_win, bias_new, ck, cv, st, *weights)

    head_shape = (ATT_KV_HEADS, HEAD_DIM)
    state_shape = (GLA_HEADS, GLA_DK, GLA_DV)
    return (yp,
            ys.reshape(dec_batch, dec_seq, d_model),
            jnp.transpose(kp, (0, 2, 1)).reshape((1, batch, WINDOW) + head_shape),
            jnp.transpose(vp, (0, 2, 1)).reshape((1, batch, WINDOW) + head_shape),
            sp.reshape((1, batch) + state_shape),
            jnp.transpose(ks, (0, 2, 1)).reshape((1, dec_batch, WINDOW) + head_shape),
            jnp.transpose(vs, (0, 2, 1)).reshape((1, dec_batch, WINDOW) + head_shape),
            ss.reshape((1, dec_batch) + state_shape))
```

```python
import functools

import numpy as np
import jax
import jax.numpy as jnp
from jax import lax
from jax.experimental import pallas as pl
from jax.experimental.pallas import tpu as pltpu

F32 = jnp.float32
BF16 = jnp.bfloat16

HEAD_DIM = 64
ATT_HEADS = 8
ATT_KV_HEADS = 2
WINDOW = 128
PAST_LEN = 16384
ROPE_THETA = 10000.0
GLA_HEADS = 4
GLA_DK = 64
GLA_DV = 128
GLA_RANK = 16
GLA_GATE_TEMP = 16.0
EPS = 1e-6
ATT_W = ATT_HEADS * HEAD_DIM
KV_W = ATT_KV_HEADS * HEAD_DIM
GK_W = GLA_HEADS * GLA_DK
GV_W = GLA_HEADS * GLA_DV
COL_Q = 0
COL_K = COL_Q + ATT_W
COL_V = COL_K + KV_W
COL_GQ = COL_V + KV_W
COL_GK = COL_GQ + GK_W
COL_GV = COL_GK + GK_W
COL_GR = COL_GV + GV_W
MAIN_W = COL_GR + GV_W
QK_SCALE = HEAD_DIM ** -0.5
GLA_SCALE = GLA_DK ** -0.5

CHUNK = 128
LANES = 128
PROMPT_TILE = 256
SAMPLE_SEQS = 16
FF_CHUNK = 1024
PROMPT_LEVELS = (1, 2, 4, 8, 16, 32, 64)
BOUNDED_LOG_DECAY = 30.0
VMEM_LIMIT_BYTES = 56 * 1024 * 1024


def _dot(a, b):
    return jnp.dot(a, b, preferred_element_type=F32)


def _dot_nt(a, b):
    return lax.dot_general(a, b, (((1,), (1,)), ((), ())), preferred_element_type=F32)


def _rms(x, g):
    ms = jnp.mean(x * x, axis=-1, keepdims=True)
    return x * lax.rsqrt(ms + EPS) * g


def _rope(x, cos, sin_a, sin_b):
    return x * cos + pltpu.roll(x, 96, 1) * sin_a + pltpu.roll(x, 32, 1) * sin_b


def _rope_lanes(cos, sin):
    reps = LANES // cos.shape[0]
    cos_l = jnp.transpose(jnp.concatenate([cos] * reps, axis=0))
    sin_l = jnp.transpose(jnp.concatenate([sin] * reps, axis=0))
    lane = lax.broadcasted_iota(jnp.int32, cos_l.shape, 1)
    first_half = (lane & (HEAD_DIM // 2)) == 0
    return cos_l, jnp.where(first_half, -sin_l, 0.0), jnp.where(first_half, 0.0, sin_l)


def _split3(x):
    hi = x.astype(BF16)
    r1 = x - hi.astype(F32)
    mid = r1.astype(BF16)
    lo = (r1 - mid.astype(F32)).astype(BF16)
    return hi, mid, lo


def _project(x, g_pre, w_in_ref):
    h = _rms(x, g_pre).astype(BF16)
    return _dot(h, w_in_ref[:, 0:MAIN_W]), _dot(h, w_in_ref[:, MAIN_W:])


def _split_projection(proj, glr, cos_ref, sin_ref, w_gk2_ref, b_gk_ref):
    cos, sin_a, sin_b = _rope_lanes(cos_ref[...], sin_ref[...])
    z = _dot(glr.astype(BF16), w_gk2_ref[...]) + b_gk_ref[...]
    log_a = (jnp.minimum(z, 0.0) - jnp.log1p(jnp.exp(-jnp.abs(z)))) * (1.0 / GLA_GATE_TEMP)
    q = jnp.concatenate(
        [_rope(proj[:, COL_Q + LANES * j:COL_Q + LANES * (j + 1)], cos, sin_a, sin_b) * QK_SCALE
         for j in range(ATT_W // LANES)], axis=1)
    k = _rope(proj[:, COL_K:COL_V], cos, sin_a, sin_b)
    v = proj[:, COL_V:COL_GQ]
    gq = proj[:, COL_GQ:COL_GK] * GLA_SCALE
    gk = proj[:, COL_GK:COL_GV]
    gv = proj[:, COL_GV:COL_GR]
    gr = proj[:, COL_GR:MAIN_W]
    return q, k, v, gq, gk, gv, gr, log_a


def _output_stage(x_prev, mix_prev, g_post_ref, g_fpre_ref, g_fpost_ref, w_out_ref, w_up_ref, w_down_ref, write):
    mix = _dot(mix_prev[...], w_out_ref[...])
    yield
    x1 = x_prev[...] + _rms(mix, g_post_ref[...])
    h2 = _rms(x1, g_fpre_ref[...]).astype(BF16)
    f = None
    for i in range(w_up_ref.shape[1] // FF_CHUNK):
        u = jnp.maximum(_dot(h2, w_up_ref[:, FF_CHUNK * i:FF_CHUNK * (i + 1)]), 0.0)
        yield
        part = _dot((u * u).astype(BF16), w_down_ref[FF_CHUNK * i:FF_CHUNK * (i + 1), :])
        f = part if f is None else f + part
        yield
    write(x1 + _rms(f, g_fpost_ref[...]))


def _lo_hi_forms(a):
    lo = lax.broadcasted_iota(jnp.int32, a.shape, 1) < HEAD_DIM
    r = pltpu.roll(a, HEAD_DIM, 1)
    zero = jnp.zeros_like(a)
    return ((jnp.where(lo, a, zero), jnp.where(lo, zero, r)),
            (jnp.where(lo, r, zero), jnp.where(lo, zero, a)))


def _gla_exponents(log_a, gmat):
    hi, mid, lo = _split3(log_a)
    return _dot(gmat, hi) + _dot(gmat, mid) + _dot(gmat, lo)


def _head_lanes(h, rows):
    lane = lax.broadcasted_iota(jnp.int32, (rows, LANES), 1)
    e = h % 2
    return (lane >= GLA_DK * e) & (lane < GLA_DK * (e + 1))


def _gla_intra(gq, gk, log_a, gmat_lv, levels, masks_ref):
    expo = _gla_exponents(log_a, gmat_lv)
    row = lax.broadcasted_iota(jnp.int32, gq.shape, 0)
    terms = [(gq.astype(BF16), gk.astype(BF16), 0)]
    for li, m in enumerate(levels):
        ex = jnp.exp(expo[CHUNK * li:CHUNK * (li + 1)])
        z = (jnp.where((row & m) != 0, gq, gk) * ex).astype(BF16)
        terms.append((z, z, 1 + li))
    scores = []
    for h in range(GLA_HEADS):
        p = h // 2
        in_head = _head_lanes(h, gq.shape[0])
        acc = None
        for zq, zk, mi in terms:
            zq_p = zq[:, LANES * p:LANES * (p + 1)]
            zk_p = zk[:, LANES * p:LANES * (p + 1)]
            zk_h = jnp.where(in_head, zk_p, jnp.zeros_like(zk_p))
            t = _dot_nt(zq_p, zk_h) * masks_ref[mi]
            acc = t if acc is None else acc + t
        scores.append(acc)
    return jnp.concatenate(scores, axis=1)


def _gla_intra_bounded(q_dec, gk, b, causal):
    k_grow_t = jnp.transpose(gk * jnp.exp(-b)).astype(BF16)
    scores = []
    for h in range(GLA_HEADS):
        p = h // 2
        qd_p = q_dec[:, LANES * p:LANES * (p + 1)]
        qd_h = jnp.where(_head_lanes(h, q_dec.shape[0]), qd_p, jnp.zeros_like(qd_p))
        scores.append(_dot(qd_h, k_grow_t[LANES * p:LANES * (p + 1), :]) * causal)
    return jnp.concatenate(scores, axis=1)


def _gla_out_gate(o_h, gr_h, g_gla):
    on = _rms(o_h, g_gla)
    return on * (gr_h * (1.0 / (1.0 + jnp.exp(-gr_h))))


def _prompt_kernel(sinks_ref, x_ref, cos_ref, sin_ref, bias_ref, gmat_ref, masks_ref,
                   g_pre_ref, g_post_ref, g_fpre_ref, g_fpost_ref, g_gla_ref, b_gk_ref,
                   w_in_ref, w_gk2_ref, w_out_ref, w_up_ref, w_down_ref,
                   y_ref, kwin_ref, vwin_ref, sout_ref,
                   q_scr, kbd_scr, vbd_scr, gq_scr, gk_scr, gv_scr, gr_scr, la_scr, s_scr,
                   mix_scr, mixp_scr, xp1_scr, xp2_scr, bounded_scr, *, tiles_per_seq, n_tiles):
    j = pl.program_id(0)
    t_step = lax.rem(jnp.clip(j - 1, 0, n_tiles - 1), tiles_per_seq)
    tile = x_ref.shape[1]
    n_chunks = tile // CHUNK

    @pl.when(j == 0)
    def _():
        for ref in (s_scr, q_scr, kbd_scr, vbd_scr, gq_scr, gk_scr, gv_scr, gr_scr, la_scr,
                    mixp_scr, xp1_scr, xp2_scr):
            ref[...] = jnp.zeros_like(ref)
        bounded_scr[0] = 1

    branch = jnp.where(j == 0, 3, jnp.where(j == n_tiles + 1, 2, jnp.where(bounded_scr[0] == 1, 0, 1)))

    lo_lane = lax.broadcasted_iota(jnp.int32, (CHUNK, LANES), 1) < HEAD_DIM
    g_gla = g_gla_ref[...]

    def input_stage(results):
        x = x_ref[0]
        proj, glr = _project(x, g_pre_ref[...], w_in_ref)
        yield
        q, k, v, gq, gk, gv, gr, log_a = _split_projection(proj, glr, cos_ref, sin_ref, w_gk2_ref, b_gk_ref)
        chunk_decay = jnp.concatenate(
            [jnp.sum(log_a[CHUNK * c:CHUNK * (c + 1)], axis=0, keepdims=True) for c in range(n_chunks)], axis=0)
        results.update(
            x=x, q=q.astype(BF16), k=k, v=v, k_forms=_lo_hi_forms(k), v_forms=_lo_hi_forms(v),
            gq=gq, gk=gk, gv=gv, gr=gr, log_a=log_a,
            bounded=jnp.min(chunk_decay) >= -BOUNDED_LOG_DECAY)
        yield

    def hand_over(results, mixed=True):
        kbd_scr[:, 0:CHUNK, :] = kbd_scr[:, tile:tile + CHUNK, :]
        vbd_scr[:, 0:CHUNK, :] = vbd_scr[:, tile:tile + CHUNK, :]
        for g in range(ATT_KV_HEADS):
            for e in range(2):
                kbd_scr[2 * g + e, CHUNK:CHUNK + tile, :] = results["k_forms"][g][e].astype(BF16)
                vbd_scr[2 * g + e, CHUNK:CHUNK + tile, :] = results["v_forms"][g][e].astype(BF16)
        q_scr[...] = results["q"]
        gq_scr[...] = results["gq"]
        gk_scr[...] = results["gk"]
        gv_scr[...] = results["gv"]
        gr_scr[...] = results["gr"]
        la_scr[...] = results["log_a"]
        bounded_scr[0] = jnp.where(results["bounded"], 1, 0)
        kwin_ref[0] = jnp.transpose(results["k"][tile - WINDOW:tile, :])
        vwin_ref[0] = jnp.transpose(results["v"][tile - WINDOW:tile, :])
        xp2_scr[...] = xp1_scr[...]
        xp1_scr[...] = results["x"]
        if mixed:
            mixp_scr[...] = mix_scr[...]

    def attention_chunk(c):
        rows = slice(CHUNK * c, CHUNK * (c + 1))
        rows2 = slice(CHUNK * c, CHUNK * (c + 2))
        bias = bias_ref[jnp.where(t_step == 0, 1, 0)] if c == 0 else bias_ref[0]
        scores = []
        for g in range(ATT_KV_HEADS):
            lhs = jnp.concatenate([q_scr[rows, 2 * LANES * g:2 * LANES * g + LANES],
                                   q_scr[rows, 2 * LANES * g + LANES:2 * LANES * (g + 1)]], axis=0)
            kcat = jnp.concatenate([kbd_scr[2 * g, rows2, :], kbd_scr[2 * g + 1, rows2, :]], axis=0)
            scores.append(_dot_nt(lhs, kcat) + bias)
        yield
        for g in range(ATT_KV_HEADS):
            s = scores[g]
            p_blocks = []
            inv = []
            for blk in range(2):
                p_row = []
                inv_row = []
                for e in range(2):
                    sub = s[CHUNK * blk:CHUNK * (blk + 1), 2 * CHUNK * e:2 * CHUNK * (e + 1)]
                    sink = sinks_ref[4 * g + 2 * blk + e]
                    m = jnp.maximum(jnp.max(sub, axis=-1, keepdims=True), sink)
                    p = jnp.exp(sub - m)
                    den = jnp.sum(p, axis=-1, keepdims=True) + jnp.exp(sink - m)
                    p_row.append(p.astype(BF16))
                    inv_row.append(1.0 / den)
                p_blocks.append(jnp.concatenate(p_row, axis=1))
                inv.append(inv_row)
            p_all = jnp.concatenate(p_blocks, axis=0)
            vcat = jnp.concatenate([vbd_scr[2 * g, rows2, :], vbd_scr[2 * g + 1, rows2, :]], axis=0)
            o = _dot(p_all, vcat)
            for blk in range(2):
                scale = jnp.where(lo_lane, inv[blk][0], inv[blk][1])
                c0 = 2 * LANES * g + LANES * blk
                mix_scr[rows, c0:c0 + LANES] = (o[CHUNK * blk:CHUNK * (blk + 1)] * scale).astype(BF16)
        yield

    def gla_chunk(c, state, bounded_decay):
        rows = slice(CHUNK * c, CHUNK * (c + 1))
        gq_c = gq_scr[rows, :]
        gk_c = gk_scr[rows, :]
        la_c = la_scr[rows, :]
        b = _gla_exponents(la_c, gmat_ref[0:CHUNK, :])
        b_last = jnp.broadcast_to(b[CHUNK - 1:CHUNK, :], b.shape)
        q_dec_f = gq_c * jnp.exp(b)
        q_dec = q_dec_f.astype(BF16)
        k_dec_t = jnp.transpose(gk_c * jnp.exp(b_last - b)).astype(BF16)
        decay_t = jnp.exp(jnp.transpose(b_last))
        yield
        if bounded_decay:
            scores = _gla_intra_bounded(q_dec_f, gk_c, b, masks_ref[len(PROMPT_LEVELS) + 1])
        else:
            scores = _gla_intra(gq_c, gk_c, la_c, gmat_ref[CHUNK:, :], PROMPT_LEVELS, masks_ref)
        yield
        for h in range(GLA_HEADS):
            p_ = h // 2
            v_h = gv_scr[rows, GLA_DV * h:GLA_DV * (h + 1)].astype(BF16)
            qd_p = q_dec[:, LANES * p_:LANES * (p_ + 1)]
            qd_h = jnp.where(_head_lanes(h, CHUNK), qd_p, jnp.zeros_like(qd_p))
            s0_pair = jnp.concatenate([state[2 * p_], state[2 * p_ + 1]], axis=0).astype(BF16)
            o_h = _dot(scores[:, CHUNK * h:CHUNK * (h + 1)].astype(BF16), v_h) + _dot(qd_h, s0_pair)
            out_h = _gla_out_gate(o_h, gr_scr[rows, GLA_DV * h:GLA_DV * (h + 1)], g_gla)
            mix_scr[rows, ATT_W + GLA_DV * h:ATT_W + GLA_DV * (h + 1)] = out_h.astype(BF16)
        new_state = [decay_t[GLA_DK * h:GLA_DK * (h + 1), :] * state[h]
                     + _dot(k_dec_t[GLA_DK * h:GLA_DK * (h + 1), :],
                            gv_scr[rows, GLA_DV * h:GLA_DV * (h + 1)].astype(BF16))
                     for h in range(GLA_HEADS)]
        state[:] = new_state
        yield

    def write_y(y):
        y_ref[0] = y

    def output_stage():
        return _output_stage(xp2_scr, mixp_scr, g_post_ref, g_fpre_ref, g_fpost_ref,
                             w_out_ref, w_up_ref, w_down_ref, write_y)

    def run_step(bounded_decay):
        keep = jnp.where(t_step == 0, 0.0, 1.0)
        state = [s_scr[GLA_DK * h:GLA_DK * (h + 1), :] * keep for h in range(GLA_HEADS)]
        results = {}
        out = output_stage()
        inp = input_stage(results)
        att = [attention_chunk(c) for c in range(n_chunks)]
        gla = [gla_chunk(c, state, bounded_decay) for c in range(n_chunks)]
        next(out)
        for c in range(n_chunks):
            next(att[c])
            next(gla[c])
        next(inp)
        for c in range(n_chunks):
            next(att[c])
            next(gla[c])
        next(out)
        for c in range(n_chunks):
            next(out)
            next(gla[c])
        next(inp)
        for _ in out:
            pass
        s_all = jnp.concatenate(state, axis=0)
        s_scr[...] = s_all
        sout_ref[0] = s_all
        hand_over(results)

    def finish_last_tile():
        for _ in output_stage():
            pass

    def start_first_tile():
        results = {}
        for _ in input_stage(results):
            pass
        hand_over(results, mixed=False)

    lax.switch(branch, [lambda: run_step(True), lambda: run_step(False), finish_last_tile, start_first_tile])


def _sample_kernel(sinks_ref, x_ref, cos_ref, sin_ref, gmat_ref, masks_ref, bias_win_ref, bias_new_ref,
                   ck_ref, cv_ref, st_ref,
                   g_pre_ref, g_post_ref, g_fpre_ref, g_fpost_ref, g_gla_ref, b_gk_ref,
                   w_in_ref, w_gk2_ref, w_out_ref, w_up_ref, w_down_ref,
                   y_ref, kwin_ref, vwin_ref, sout_ref,
                   *, dec_seq, levels):
    results = {}
    for _ in _sample_tile_stages(sinks_ref, x_ref, cos_ref, sin_ref, gmat_ref, masks_ref, bias_win_ref,
                                 bias_new_ref, ck_ref, cv_ref, st_ref, g_pre_ref, g_gla_ref, b_gk_ref,
                                 w_in_ref, w_gk2_ref, kwin_ref, vwin_ref, sout_ref, results,
                                 dec_seq=dec_seq, levels=levels):
        pass

    def write_y(y):
        y_ref[...] = y

    for _ in _output_stage(results["x"], results["mix"], g_post_ref, g_fpre_ref, g_fpost_ref,
                           w_out_ref, w_up_ref, w_down_ref, write_y):
        pass


def _sample_tile_stages(sinks_ref, x_ref, cos_ref, sin_ref, gmat_ref, masks_ref, bias_win_ref, bias_new_ref,
                        ck_ref, cv_ref, st_ref, g_pre_ref, g_gla_ref, b_gk_ref, w_in_ref, w_gk2_ref,
                        kwin_ref, vwin_ref, sout_ref, results, *, dec_seq, levels):
    n_seqs = ck_ref.shape[0]
    x = x_ref[...]
    proj, glr = _project(x, g_pre_ref[...], w_in_ref)
    yield
    q, k, v, gq, gk, gv, gr, log_a = _split_projection(proj, glr, cos_ref, sin_ref, w_gk2_ref, b_gk_ref)

    is_new = lax.broadcasted_iota(jnp.int32, (KV_W, WINDOW), 1) >= WINDOW - dec_seq
    for new, old_ref, out_ref in ((jnp.transpose(k), ck_ref, kwin_ref), (jnp.transpose(v), cv_ref, vwin_ref)):
        for s in range(n_seqs):
            shifted = pltpu.roll(old_ref[s], WINDOW - dec_seq, 1)
            tail = pltpu.roll(new, (WINDOW - dec_seq * (s + 1)) % CHUNK, 1)
            out_ref[s] = jnp.where(is_new, tail, shifted)

    def seq_rows(a, s):
        return a[dec_seq * s:dec_seq * (s + 1)]

    def gather_seq(per_head, s):
        return jnp.concatenate([seq_rows(a, s) for a in per_head], axis=0)

    def scatter_heads(per_seq, h):
        return jnp.concatenate([seq_rows(a, h) for a in per_seq], axis=0)

    lo_lane = lax.broadcasted_iota(jnp.int32, (CHUNK, LANES), 1) < HEAD_DIM
    q_heads = []
    for j in range(ATT_W // LANES):
        blk = q[:, LANES * j:LANES * (j + 1)]
        rolled = pltpu.roll(blk, HEAD_DIM, 1)
        g = (2 * j) // (ATT_HEADS // ATT_KV_HEADS)
        keep = lo_lane if g == 0 else jnp.logical_not(lo_lane)
        for e in range(2):
            q_heads.append(jnp.where(keep, blk if e == g else rolled, 0.0))
    s_win = [_dot(gather_seq(q_heads, s), ck_ref[s]) for s in range(n_seqs)]
    k_b = k.astype(BF16)
    v_b = v.astype(BF16)
    s_new = [_dot_nt(q_heads[h].astype(BF16), k_b) for h in range(ATT_HEADS)]
    b = _gla_exponents(log_a, gmat_ref[0:CHUNK, :])
    yield
    bias_win = bias_win_ref[...]
    bias_new = bias_new_ref[...]
    p_win = []
    o_new = []
    inv_den = []
    for h in range(ATT_HEADS):
        sw = scatter_heads(s_win, h) + bias_win
        sn = s_new[h] + bias_new
        sink = sinks_ref[h]
        m = jnp.maximum(jnp.maximum(jnp.max(sw, axis=-1, keepdims=True),
                                    jnp.max(sn, axis=-1, keepdims=True)), sink)
        pw = jnp.exp(sw - m)
        pn = jnp.exp(sn - m)
        den = (jnp.sum(pw, axis=-1, keepdims=True) + jnp.sum(pn, axis=-1, keepdims=True)
               + jnp.exp(sink - m))
        p_win.append(pw)
        o_new.append(_dot(pn.astype(BF16), v_b))
        inv_den.append(1.0 / den)
    o_win = [_dot_nt(gather_seq(p_win, s), cv_ref[s]) for s in range(n_seqs)]
    o_heads = [(scatter_heads(o_win, h) + o_new[h]) * inv_den[h] for h in range(ATT_HEADS)]
    mix_cols = []
    for j in range(ATT_W // LANES):
        g = (2 * j) // (ATT_HEADS // ATT_KV_HEADS)
        first = o_heads[2 * j] if g == 0 else pltpu.roll(o_heads[2 * j], HEAD_DIM, 1)
        second = o_heads[2 * j + 1] if g == 1 else pltpu.roll(o_heads[2 * j + 1], HEAD_DIM, 1)
        mix_cols.append(jnp.where(lo_lane, first, second).astype(BF16))

    b3 = b.reshape(n_seqs, dec_seq, GK_W)
    b_last = jnp.broadcast_to(b3[:, dec_seq - 1:dec_seq, :], b3.shape).reshape(CHUNK, GK_W)
    q_dec = gq * jnp.exp(b)
    k_dec_t = jnp.transpose(gk * jnp.exp(b_last - b))
    decay_t = jnp.exp(jnp.transpose(b_last))
    scores = _gla_intra(gq, gk, log_a, gmat_ref[CHUNK:, :], levels, masks_ref)
    yield
    lane_k = lax.broadcasted_iota(jnp.int32, (dec_seq, GK_W), 1)
    lane_t = lax.broadcasted_iota(jnp.int32, (GLA_DK, LANES), 1)
    o_inter = []
    for s in range(n_seqs):
        qd = seq_rows(q_dec, s)
        lhs = jnp.concatenate(
            [jnp.where((lane_k >= GLA_DK * h) & (lane_k < GLA_DK * (h + 1)), qd, 0.0)
             for h in range(GLA_HEADS)], axis=0)
        o_inter.append(_dot(lhs, st_ref[s]))
    g_gla = g_gla_ref[...]
    for h in range(GLA_HEADS):
        cs = slice(GLA_DV * h, GLA_DV * (h + 1))
        hs = slice(GLA_DK * h, GLA_DK * (h + 1))
        v_h = gv[:, cs].astype(BF16)
        o_h = _dot(scores[:, CHUNK * h:CHUNK * (h + 1)].astype(BF16), v_h) + scatter_heads(o_inter, h)
        mix_cols.append(_gla_out_gate(o_h, gr[:, cs], g_gla).astype(BF16))
        kd_h = k_dec_t[hs, :]
        lhs = jnp.concatenate(
            [jnp.where((lane_t >= dec_seq * s) & (lane_t < dec_seq * (s + 1)), kd_h, 0.0).astype(BF16)
             for s in range(n_seqs)], axis=0)
        upd = _dot(lhs, v_h)
        for s in range(n_seqs):
            a_col = jnp.broadcast_to(decay_t[hs, dec_seq * s:dec_seq * s + 1], (GLA_DK, GLA_DV))
            sout_ref[s, hs, :] = a_col * st_ref[s, hs, :] + upd[GLA_DK * s:GLA_DK * (s + 1)]
    results.update(x=x, mix=jnp.concatenate(mix_cols, axis=1))


def _gla_constants(seg, levels):
    t = np.arange(CHUNK)[:, None]
    u = np.arange(CHUNK)[None, :]
    same = (t // seg) == (u // seg)
    blocks = [same & (u <= t)]
    masks = [np.eye(CHUNK, dtype=bool)]
    for m in levels:
        ref = (t // (2 * m)) * (2 * m) + m - 1
        second = (t & m) != 0
        blocks.append(np.where(second, (u > ref) & (u <= t), (u > t) & (u <= ref)))
        masks.append(((t // (2 * m)) == (u // (2 * m))) & second & ((u & m) == 0))
    masks.append(blocks[0])
    gmat = jnp.asarray(np.concatenate(blocks, axis=0).astype(np.float32), dtype=BF16)
    return gmat, jnp.asarray(np.stack(masks).astype(np.float32))


def _attention_bias():
    t = (np.arange(2 * CHUNK) % CHUNK)[:, None]
    j = (np.arange(4 * CHUNK) % (2 * CHUNK))[None, :]
    band = (j >= t + 1) & (j <= t + WINDOW)
    first = band & (j >= CHUNK)
    neg = np.float32(-np.inf)
    return jnp.asarray(np.stack([np.where(band, np.float32(0), neg), np.where(first, np.float32(0), neg)]))


def _sample_attention_bias(dec_seq):
    r = np.arange(CHUNK)[:, None]
    c = np.arange(CHUNK)[None, :]
    t = r % dec_seq
    win = c > t
    new = ((r // dec_seq) == (c // dec_seq)) & ((c % dec_seq) <= t)
    neg = np.float32(-np.inf)
    zero = np.float32(0)
    return jnp.asarray(np.where(win, zero, neg)), jnp.asarray(np.where(new, zero, neg))


def _rope_tables(pos):
    half = HEAD_DIM // 2
    inv = ROPE_THETA ** (-jnp.arange(half, dtype=F32) / half)
    ang = inv[:, None] * pos.astype(F32)[None, :]
    return jnp.cos(ang), jnp.sin(ang)


def _const_spec(shape):
    zeros = (0,) * len(shape)
    return pl.BlockSpec(shape, lambda *_: zeros, pipeline_mode=pl.Buffered(1))


def kernel(x_prompt, x_sample, cache_k, cache_v, state_gla, w_in, w_gk2, b_gk, g_gla, sinks, w_out, g_mix_pre, g_mix_post, g_ffn_pre, g_ffn_post, w_up, w_down):
    depth = w_in.shape[0]
    assert depth == 1, "single trunk layer"
    batch, seq, d_model = x_prompt.shape
    dec_batch, dec_seq, _ = x_sample.shape
    d_ff = w_up.shape[2]
    assert seq % PROMPT_TILE == 0 and dec_batch % SAMPLE_SEQS == 0
    assert SAMPLE_SEQS * dec_seq == CHUNK and CHUNK % dec_seq == 0 and d_ff % FF_CHUNK == 0
    assert w_in.shape[2] == MAIN_W + GLA_RANK

    w_in_b = w_in[0].astype(BF16)
    w_gk2_b = w_gk2[0].astype(BF16)
    w_out_b = w_out[0].astype(BF16)
    w_up_b = w_up[0].astype(BF16)
    w_down_b = w_down[0].astype(BF16)
    g_pre = g_mix_pre[0][None, :]
    g_post = g_mix_post[0][None, :]
    g_fpre = g_ffn_pre[0][None, :]
    g_fpost = g_ffn_post[0][None, :]
    g_gla2 = g_gla[0][None, :]
    b_gk2 = b_gk[0][None, :]
    sinks1 = sinks[0]

    weights = (g_pre, g_post, g_fpre, g_fpost, g_gla2, b_gk2, w_in_b, w_gk2_b, w_out_b, w_up_b, w_down_b)
    weight_specs = [_const_spec(w.shape) for w in weights]
    smem_spec = pl.BlockSpec(memory_space=pltpu.SMEM)

    cos_p, sin_p = _rope_tables(jnp.arange(seq, dtype=jnp.int32))
    gmat_p, masks_p = _gla_constants(CHUNK, PROMPT_LEVELS)
    bias = _attention_bias()
    n_t = seq // PROMPT_TILE
    tile = PROMPT_TILE
    n_tiles = batch * n_t

    def cur_tile(j):
        return jnp.minimum(j, n_tiles - 1)

    def mix_tile(j):
        return jnp.clip(j - 1, 0, n_tiles - 1)

    def done_tile(j):
        return jnp.clip(j - 2, 0, n_tiles - 1)

    tab_spec = pl.BlockSpec((HEAD_DIM // 2, tile), lambda j: (0, cur_tile(j) % n_t))
    seq_out = lambda j: (cur_tile(j) // n_t, 0, 0)
    yp, kp, vp, sp = pl.pallas_call(
        functools.partial(_prompt_kernel, tiles_per_seq=n_t, n_tiles=n_tiles),
        grid=(n_tiles + 2,),
        in_specs=[smem_spec,
                  pl.BlockSpec((1, tile, d_model), lambda j: (cur_tile(j) // n_t, cur_tile(j) % n_t, 0)),
                  tab_spec, tab_spec,
                  _const_spec(bias.shape), _const_spec(gmat_p.shape), _const_spec(masks_p.shape)] + weight_specs,
        out_specs=[pl.BlockSpec((1, tile, d_model), lambda j: (done_tile(j) // n_t, done_tile(j) % n_t, 0)),
                   pl.BlockSpec((1, WINDOW, KV_W), seq_out),
                   pl.BlockSpec((1, WINDOW, KV_W), seq_out),
                   pl.BlockSpec((1, GK_W, GLA_DV), lambda j: (mix_tile(j) // n_t, 0, 0))],
        out_shape=[jax.ShapeDtypeStruct((batch, seq, d_model), F32),
                   jax.ShapeDtypeStruct((batch, WINDOW, KV_W), F32),
                   jax.ShapeDtypeStruct((batch, WINDOW, KV_W), F32),
                   jax.ShapeDtypeStruct((batch, GK_W, GLA_DV), F32)],
        scratch_shapes=[pltpu.VMEM((tile, ATT_W), BF16),
                        pltpu.VMEM((2 * ATT_KV_HEADS, CHUNK + tile, LANES), BF16),
                        pltpu.VMEM((2 * ATT_KV_HEADS, CHUNK + tile, LANES), BF16),
                        pltpu.VMEM((tile, GK_W), F32),
                        pltpu.VMEM((tile, GK_W), F32),
                        pltpu.VMEM((tile, GV_W), F32),
                        pltpu.VMEM((tile, GV_W), F32),
                        pltpu.VMEM((tile, GK_W), F32),
                        pltpu.VMEM((GK_W, GLA_DV), F32),
                        pltpu.VMEM((tile, d_model), BF16),
                        pltpu.VMEM((tile, d_model), BF16),
                        pltpu.VMEM((tile, d_model), F32),
                        pltpu.VMEM((tile, d_model), F32),
                        pltpu.SMEM((1,), jnp.int32)],
        compiler_params=pltpu.CompilerParams(
            dimension_semantics=("arbitrary",), vmem_limit_bytes=VMEM_LIMIT_BYTES),
        name="prompt_layer",
    )(sinks1, x_prompt, cos_p, sin_p, bias, gmat_p, masks_p, *weights)

    levels_s = tuple(m for m in PROMPT_LEVELS if m < dec_seq)
    pos_s = PAST_LEN + jnp.arange(dec_seq, dtype=jnp.int32)
    cos_s, sin_s = (jnp.tile(a, (1, SAMPLE_SEQS)) for a in _rope_tables(pos_s))
    gmat_s, masks_s = _gla_constants(dec_seq, levels_s)
    bias_win, bias_new = _sample_attention_bias(dec_seq)
    rows = SAMPLE_SEQS * dec_seq
    n_g = dec_batch // SAMPLE_SEQS
    xs = x_sample.reshape(dec_batch * dec_seq, d_model)
    ck = jnp.transpose(cache_k[0].reshape(dec_batch, WINDOW, KV_W), (0, 2, 1))
    cv = jnp.transpose(cache_v[0].reshape(dec_batch, WINDOW, KV_W), (0, 2, 1))
    st = state_gla[0].reshape(dec_batch, GK_W, GLA_DV)
    seq_spec = pl.BlockSpec((SAMPLE_SEQS, WINDOW, KV_W), lambda i: (i, 0, 0))
    st_spec = pl.BlockSpec((SAMPLE_SEQS, GK_W, GLA_DV), lambda i: (i, 0, 0))
    row_spec = pl.BlockSpec((rows, d_model), lambda i: (i, 0))
    ys, ks, vs, ss = pl.pallas_call(
        functools.partial(_sample_kernel, dec_seq=dec_seq, levels=levels_s),
        grid=(n_g,),
        in_specs=[smem_spec, row_spec,
                  _const_spec(cos_s.shape), _const_spec(sin_s.shape),
                  _const_spec(gmat_s.shape), _const_spec(masks_s.shape),
                  _const_spec(bias_win.shape), _const_spec(bias_new.shape),
                  seq_spec, seq_spec, st_spec] + weight_specs,
        out_specs=[row_spec, seq_spec, seq_spec, st_spec],
        out_shape=[jax.ShapeDtypeStruct((dec_batch * dec_seq, d_model), F32),
                   jax.ShapeDtypeStruct((dec_batch, WINDOW, KV_W), F32),
                   jax.ShapeDtypeStruct((dec_batch, WINDOW, KV_W), F32),
                   jax.ShapeDtypeStruct((dec_batch, GK_W, GLA_DV), F32)],
        compiler_params=pltpu.CompilerParams(
            dimension_semantics=("arbitrary",), vmem_limit_bytes=VMEM_LIMIT_BYTES),
        name="sample_layer",
    )(sinks1, xs, cos_s, sin_s, gmat_s, masks_s, bias_win, bias_new, ck, cv, st, *weights)

    head_shape = (ATT_KV_HEADS, HEAD_DIM)
    state_shape = (GLA_HEADS, GLA_DK, GLA_DV)
    return (yp,
            ys.reshape(dec_batch, dec_seq, d_model),
            jnp.transpose(kp, (0, 2, 1)).reshape((1, batch, WINDOW) + head_shape),
            jnp.transpose(vp, (0, 2, 1)).reshape((1, batch, WINDOW) + head_shape),
            sp.reshape((1, batch) + state_shape),
            jnp.transpose(ks, (0, 2, 1)).reshape((1, dec_batch, WINDOW) + head_shape),
            jnp.transpose(vs, (0, 2, 1)).reshape((1, dec_batch, WINDOW) + head_shape),
            ss.reshape((1, dec_batch) + state_shape))
```

```python
import functools

import numpy as np
import jax
import jax.numpy as jnp
from jax import lax
from jax.experimental import pallas as pl
from jax.experimental.pallas import tpu as pltpu

F32 = jnp.float32
BF16 = jnp.bfloat16

HEAD_DIM = 64
ATT_HEADS = 8
ATT_KV_HEADS = 2
WINDOW = 128
PAST_LEN = 16384
ROPE_THETA = 10000.0
GLA_HEADS = 4
GLA_DK = 64
GLA_DV = 128
GLA_RANK = 16
GLA_GATE_TEMP = 16.0
EPS = 1e-6
ATT_W = ATT_HEADS * HEAD_DIM
KV_W = ATT_KV_HEADS * HEAD_DIM
GK_W = GLA_HEADS * GLA_DK
GV_W = GLA_HEADS * GLA_DV
COL_Q = 0
COL_K = COL_Q + ATT_W
COL_V = COL_K + KV_W
COL_GQ = COL_V + KV_W
COL_GK = COL_GQ + GK_W
COL_GV = COL_GK + GK_W
COL_GR = COL_GV + GV_W
MAIN_W = COL_GR + GV_W
QK_SCALE = HEAD_DIM ** -0.5
GLA_SCALE = GLA_DK ** -0.5

CHUNK = 128
LANES = 128
PROMPT_TILE = 256
SAMPLE_SEQS = 16
FF_CHUNK = 1024
PROMPT_LEVELS = (1, 2, 4, 8, 16, 32, 64)
BOUNDED_LOG_DECAY = 30.0
VMEM_LIMIT_BYTES = 56 * 1024 * 1024


def _dot(a, b):
    return jnp.dot(a, b, preferred_element_type=F32)


def _dot_nt(a, b):
    return lax.dot_general(a, b, (((1,), (1,)), ((), ())), preferred_element_type=F32)


def _rms(x, g):
    ms = jnp.mean(x * x, axis=-1, keepdims=True)
    return x * lax.rsqrt(ms + EPS) * g


def _rope(x, cos, sin_a, sin_b):
    return x * cos + pltpu.roll(x, 96, 1) * sin_a + pltpu.roll(x, 32, 1) * sin_b


def _rope_lanes(cos, sin):
    reps = LANES // cos.shape[0]
    cos_l = jnp.transpose(jnp.concatenate([cos] * reps, axis=0))
    sin_l = jnp.transpose(jnp.concatenate([sin] * reps, axis=0))
    lane = lax.broadcasted_iota(jnp.int32, cos_l.shape, 1)
    first_half = (lane & (HEAD_DIM // 2)) == 0
    return cos_l, jnp.where(first_half, -sin_l, 0.0), jnp.where(first_half, 0.0, sin_l)


def _split3(x):
    hi = x.astype(BF16)
    r1 = x - hi.astype(F32)
    mid = r1.astype(BF16)
    lo = (r1 - mid.astype(F32)).astype(BF16)
    return hi, mid, lo


def _project(x, g_pre, w_in_ref):
    h = _rms(x, g_pre).astype(BF16)
    return _dot(h, w_in_ref[:, 0:MAIN_W]), _dot(h, w_in_ref[:, MAIN_W:])


def _split_projection(proj, glr, cos_ref, sin_ref, w_gk2_ref, b_gk_ref):
    cos, sin_a, sin_b = _rope_lanes(cos_ref[...], sin_ref[...])
    z = _dot(glr.astype(BF16), w_gk2_ref[...]) + b_gk_ref[...]
    log_a = (jnp.minimum(z, 0.0) - jnp.log1p(jnp.exp(-jnp.abs(z)))) * (1.0 / GLA_GATE_TEMP)
    q = jnp.concatenate(
        [_rope(proj[:, COL_Q + LANES * j:COL_Q + LANES * (j + 1)], cos, sin_a, sin_b) * QK_SCALE
         for j in range(ATT_W // LANES)], axis=1)
    k = _rope(proj[:, COL_K:COL_V], cos, sin_a, sin_b)
    v = proj[:, COL_V:COL_GQ]
    gq = proj[:, COL_GQ:COL_GK] * GLA_SCALE
    gk = proj[:, COL_GK:COL_GV]
    gv = proj[:, COL_GV:COL_GR]
    gr = proj[:, COL_GR:MAIN_W]
    return q, k, v, gq, gk, gv, gr, log_a


def _output_stage(x_prev, mix_prev, g_post_ref, g_fpre_ref, g_fpost_ref, w_out_ref, w_up_ref, w_down_ref, write):
    mix = _dot(mix_prev[...], w_out_ref[...])
    yield
    x1 = x_prev[...] + _rms(mix, g_post_ref[...])
    h2 = _rms(x1, g_fpre_ref[...]).astype(BF16)
    f = None
    for i in range(w_up_ref.shape[1] // FF_CHUNK):
        u = jnp.maximum(_dot(h2, w_up_ref[:, FF_CHUNK * i:FF_CHUNK * (i + 1)]), 0.0)
        yield
        part = _dot((u * u).astype(BF16), w_down_ref[FF_CHUNK * i:FF_CHUNK * (i + 1), :])
        f = part if f is None else f + part
        yield
    write(x1 + _rms(f, g_fpost_ref[...]))


def _lo_hi_forms(a):
    lo = lax.broadcasted_iota(jnp.int32, a.shape, 1) < HEAD_DIM
    r = pltpu.roll(a, HEAD_DIM, 1)
    zero = jnp.zeros_like(a)
    return ((jnp.where(lo, a, zero), jnp.where(lo, zero, r)),
            (jnp.where(lo, r, zero), jnp.where(lo, zero, a)))


def _gla_exponents(log_a, gmat):
    hi, mid, lo = _split3(log_a)
    return _dot(gmat, hi) + _dot(gmat, mid) + _dot(gmat, lo)


def _head_lanes(h, rows):
    lane = lax.broadcasted_iota(jnp.int32, (rows, LANES), 1)
    e = h % 2
    return (lane >= GLA_DK * e) & (lane < GLA_DK * (e + 1))


def _gla_intra(gq, gk, log_a, gmat_lv, levels, masks_ref):
    expo = _gla_exponents(log_a, gmat_lv)
    row = lax.broadcasted_iota(jnp.int32, gq.shape, 0)
    terms = [(gq.astype(BF16), gk.astype(BF16), 0)]
    for li, m in enumerate(levels):
        ex = jnp.exp(expo[CHUNK * li:CHUNK * (li + 1)])
        z = (jnp.where((row & m) != 0, gq, gk) * ex).astype(BF16)
        terms.append((z, z, 1 + li))
    scores = []
    for h in range(GLA_HEADS):
        p = h // 2
        in_head = _head_lanes(h, gq.shape[0])
        acc = None
        for zq, zk, mi in terms:
            zq_p = zq[:, LANES * p:LANES * (p + 1)]
            zk_p = zk[:, LANES * p:LANES * (p + 1)]
            zk_h = jnp.where(in_head, zk_p, jnp.zeros_like(zk_p))
            t = _dot_nt(zq_p, zk_h) * masks_ref[mi]
            acc = t if acc is None else acc + t
        scores.append(acc)
    return jnp.concatenate(scores, axis=1)


def _gla_intra_bounded(q_dec, gk, b, causal):
    k_grow_t = jnp.transpose(gk * jnp.exp(-b)).astype(BF16)
    scores = []
    for h in range(GLA_HEADS):
        p = h // 2
        qd_p = q_dec[:, LANES * p:LANES * (p + 1)]
        qd_h = jnp.where(_head_lanes(h, q_dec.shape[0]), qd_p, jnp.zeros_like(qd_p))
        scores.append(_dot(qd_h, k_grow_t[LANES * p:LANES * (p + 1), :]) * causal)
    return jnp.concatenate(scores, axis=1)


def _gla_out_gate(o_h, gr_h, g_gla):
    on = _rms(o_h, g_gla)
    return on * (gr_h * (1.0 / (1.0 + jnp.exp(-gr_h))))


def _prompt_kernel(sinks_ref, x_ref, cos_ref, sin_ref, bias_ref, gmat_ref, masks_ref,
                   g_pre_ref, g_post_ref, g_fpre_ref, g_fpost_ref, g_gla_ref, b_gk_ref,
                   w_in_ref, w_gk2_ref, w_out_ref, w_up_ref, w_down_ref,
                   y_ref, kwin_ref, vwin_ref, sout_ref,
                   q_scr, kbd_scr, vbd_scr, gq_scr, gk_scr, gv_scr, gr_scr, la_scr, s_scr,
                   mix_scr, mixp_scr, xp1_scr, xp2_scr, bounded_scr, *, tiles_per_seq, n_tiles):
    j = pl.program_id(0)
    t_step = lax.rem(jnp.clip(j - 1, 0, n_tiles - 1), tiles_per_seq)
    tile = x_ref.shape[1]
    n_chunks = tile // CHUNK

    @pl.when(j == 0)
    def _():
        for ref in (s_scr, q_scr, kbd_scr, vbd_scr, gq_scr, gk_scr, gv_scr, gr_scr, la_scr,
                    mixp_scr, xp1_scr, xp2_scr):
            ref[...] = jnp.zeros_like(ref)
        bounded_scr[0] = 1

    branch = jnp.where(j == 0, 3, jnp.where(j == n_tiles + 1, 2, jnp.where(bounded_scr[0] == 1, 0, 1)))

    lo_lane = lax.broadcasted_iota(jnp.int32, (CHUNK, LANES), 1) < HEAD_DIM
    g_gla = g_gla_ref[...]

    def input_stage(results):
        x = x_ref[0]
        proj, glr = _project(x, g_pre_ref[...], w_in_ref)
        yield
        q, k, v, gq, gk, gv, gr, log_a = _split_projection(proj, glr, cos_ref, sin_ref, w_gk2_ref, b_gk_ref)
        chunk_decay = jnp.concatenate(
            [jnp.sum(log_a[CHUNK * c:CHUNK * (c + 1)], axis=0, keepdims=True) for c in range(n_chunks)], axis=0)
        results.update(
            x=x, q=q.astype(BF16), k=k, v=v, k_forms=_lo_hi_forms(k), v_forms=_lo_hi_forms(v),
            gq=gq, gk=gk, gv=gv.astype(BF16), gr=gr, log_a=log_a,
            bounded=jnp.min(chunk_decay) >= -BOUNDED_LOG_DECAY)
        yield

    def hand_over(results, mixed=True):
        kbd_scr[:, 0:CHUNK, :] = kbd_scr[:, tile:tile + CHUNK, :]
        vbd_scr[:, 0:CHUNK, :] = vbd_scr[:, tile:tile + CHUNK, :]
        for g in range(ATT_KV_HEADS):
            for e in range(2):
                kbd_scr[2 * g + e, CHUNK:CHUNK + tile, :] = results["k_forms"][g][e].astype(BF16)
                vbd_scr[2 * g + e, CHUNK:CHUNK + tile, :] = results["v_forms"][g][e].astype(BF16)
        q_scr[...] = results["q"]
        gq_scr[...] = results["gq"]
        gk_scr[...] = results["gk"]
        gv_scr[...] = results["gv"]
        gr_scr[...] = results["gr"]
        la_scr[...] = results["log_a"]
        bounded_scr[0] = jnp.where(results["bounded"], 1, 0)
        kwin_ref[0] = jnp.transpose(results["k"][tile - WINDOW:tile, :])
        vwin_ref[0] = jnp.transpose(results["v"][tile - WINDOW:tile, :])
        xp2_scr[...] = xp1_scr[...]
        xp1_scr[...] = results["x"]
        if mixed:
            mixp_scr[...] = mix_scr[...]

    def attention_chunk(c):
        rows = slice(CHUNK * c, CHUNK * (c + 1))
        rows2 = slice(CHUNK * c, CHUNK * (c + 2))
        bias = bias_ref[jnp.where(t_step == 0, 1, 0)] if c == 0 else bias_ref[0]
        scores = []
        for g in range(ATT_KV_HEADS):
            lhs = jnp.concatenate([q_scr[rows, 2 * LANES * g:2 * LANES * g + LANES],
                                   q_scr[rows, 2 * LANES * g + LANES:2 * LANES * (g + 1)]], axis=0)
            kcat = jnp.concatenate([kbd_scr[2 * g, rows2, :], kbd_scr[2 * g + 1, rows2, :]], axis=0)
            scores.append(_dot_nt(lhs, kcat) + bias)
        yield
        for g in range(ATT_KV_HEADS):
            s = scores[g]
            p_blocks = []
            inv = []
            for blk in range(2):
                p_row = []
                inv_row = []
                for e in range(2):
                    sub = s[CHUNK * blk:CHUNK * (blk + 1), 2 * CHUNK * e:2 * CHUNK * (e + 1)]
                    sink = sinks_ref[4 * g + 2 * blk + e]
                    m = jnp.maximum(jnp.max(sub, axis=-1, keepdims=True), sink)
                    p = jnp.exp(sub - m)
                    den = jnp.sum(p, axis=-1, keepdims=True) + jnp.exp(sink - m)
                    p_row.append(p.astype(BF16))
                    inv_row.append(1.0 / den)
                p_blocks.append(jnp.concatenate(p_row, axis=1))
                inv.append(inv_row)
            p_all = jnp.concatenate(p_blocks, axis=0)
            vcat = jnp.concatenate([vbd_scr[2 * g, rows2, :], vbd_scr[2 * g + 1, rows2, :]], axis=0)
            o = _dot(p_all, vcat)
            for blk in range(2):
                scale = jnp.where(lo_lane, inv[blk][0], inv[blk][1])
                c0 = 2 * LANES * g + LANES * blk
                mix_scr[rows, c0:c0 + LANES] = (o[CHUNK * blk:CHUNK * (blk + 1)] * scale).astype(BF16)
        yield

    def gla_chunk(c, state, bounded_decay):
        rows = slice(CHUNK * c, CHUNK * (c + 1))
        gq_c = gq_scr[rows, :]
        gk_c = gk_scr[rows, :]
        la_c = la_scr[rows, :]
        b = _gla_exponents(la_c, gmat_ref[0:CHUNK, :])
        b_last = jnp.broadcast_to(b[CHUNK - 1:CHUNK, :], b.shape)
        q_dec_f = gq_c * jnp.exp(b)
        q_dec = q_dec_f.astype(BF16)
        k_dec_t = jnp.transpose(gk_c * jnp.exp(b_last - b)).astype(BF16)
        decay_t = jnp.exp(jnp.transpose(b_last))
        yield
        if bounded_decay:
            scores = _gla_intra_bounded(q_dec_f, gk_c, b, masks_ref[len(PROMPT_LEVELS) + 1])
        else:
            scores = _gla_intra(gq_c, gk_c, la_c, gmat_ref[CHUNK:, :], PROMPT_LEVELS, masks_ref)
        yield
        for h in range(GLA_HEADS):
            p_ = h // 2
            v_h = gv_scr[rows, GLA_DV * h:GLA_DV * (h + 1)]
            qd_p = q_dec[:, LANES * p_:LANES * (p_ + 1)]
            qd_h = jnp.where(_head_lanes(h, CHUNK), qd_p, jnp.zeros_like(qd_p))
            s0_pair = jnp.concatenate([state[2 * p_], state[2 * p_ + 1]], axis=0).astype(BF16)
            o_h = _dot(scores[:, CHUNK * h:CHUNK * (h + 1)].astype(BF16), v_h) + _dot(qd_h, s0_pair)
            out_h = _gla_out_gate(o_h, gr_scr[rows, GLA_DV * h:GLA_DV * (h + 1)], g_gla)
            mix_scr[rows, ATT_W + GLA_DV * h:ATT_W + GLA_DV * (h + 1)] = out_h.astype(BF16)
        new_state = [decay_t[GLA_DK * h:GLA_DK * (h + 1), :] * state[h]
                     + _dot(k_dec_t[GLA_DK * h:GLA_DK * (h + 1), :],
                            gv_scr[rows, GLA_DV * h:GLA_DV * (h + 1)])
                     for h in range(GLA_HEADS)]
        state[:] = new_state
        yield

    def write_y(y):
        y_ref[0] = y

    def output_stage():
        return _output_stage(xp2_scr, mixp_scr, g_post_ref, g_fpre_ref, g_fpost_ref,
                             w_out_ref, w_up_ref, w_down_ref, write_y)

    def run_step(bounded_decay):
        keep = jnp.where(t_step == 0, 0.0, 1.0)
        state = [s_scr[GLA_DK * h:GLA_DK * (h + 1), :] * keep for h in range(GLA_HEADS)]
        results = {}
        out = output_stage()
        inp = input_stage(results)
        att = [attention_chunk(c) for c in range(n_chunks)]
        gla = [gla_chunk(c, state, bounded_decay) for c in range(n_chunks)]
        next(out)
        for c in range(n_chunks):
            next(att[c])
            next(gla[c])
        next(inp)
        spread = [(att[c], gla[c]) for c in range(n_chunks)]
        spread += [(gla[c],) for c in range(n_chunks)]
        spread.append((inp,))
        for stage in spread:
            next(out)
            for gen in stage:
                next(gen)
        for _ in out:
            pass
        s_all = jnp.concatenate(state, axis=0)
        s_scr[...] = s_all
        sout_ref[0] = s_all
        hand_over(results)

    def finish_last_tile():
        for _ in output_stage():
            pass

    def start_first_tile():
        results = {}
        for _ in input_stage(results):
            pass
        hand_over(results, mixed=False)

    lax.switch(branch, [lambda: run_step(True), lambda: run_step(False), finish_last_tile, start_first_tile])


def _sample_kernel(sinks_ref, x_ref, cos_ref, sin_ref, gmat_ref, masks_ref, bias_win_ref, bias_new_ref,
                   ck_ref, cv_ref, st_ref,
                   g_pre_ref, g_post_ref, g_fpre_ref, g_fpost_ref, g_gla_ref, b_gk_ref,
                   w_in_ref, w_gk2_ref, w_out_ref, w_up_ref, w_down_ref,
                   y_ref, kwin_ref, vwin_ref, sout_ref,
                   *, dec_seq, levels):
    results = {}
    for _ in _sample_tile_stages(sinks_ref, x_ref, cos_ref, sin_ref, gmat_ref, masks_ref, bias_win_ref,
                                 bias_new_ref, ck_ref, cv_ref, st_ref, g_pre_ref, g_gla_ref, b_gk_ref,
                                 w_in_ref, w_gk2_ref, kwin_ref, vwin_ref, sout_ref, results,
                                 dec_seq=dec_seq, levels=levels):
        pass

    def write_y(y):
        y_ref[...] = y

    for _ in _output_stage(results["x"], results["mix"], g_post_ref, g_fpre_ref, g_fpost_ref,
                           w_out_ref, w_up_ref, w_down_ref, write_y):
        pass


def _sample_tile_stages(sinks_ref, x_ref, cos_ref, sin_ref, gmat_ref, masks_ref, bias_win_ref, bias_new_ref,
                        ck_ref, cv_ref, st_ref, g_pre_ref, g_gla_ref, b_gk_ref, w_in_ref, w_gk2_ref,
                        kwin_ref, vwin_ref, sout_ref, results, *, dec_seq, levels):
    n_seqs = ck_ref.shape[0]
    x = x_ref[...]
    proj, glr = _project(x, g_pre_ref[...], w_in_ref)
    yield
    q, k, v, gq, gk, gv, gr, log_a = _split_projection(proj, glr, cos_ref, sin_ref, w_gk2_ref, b_gk_ref)

    is_new = lax.broadcasted_iota(jnp.int32, (KV_W, WINDOW), 1) >= WINDOW - dec_seq
    for new, old_ref, out_ref in ((jnp.transpose(k), ck_ref, kwin_ref), (jnp.transpose(v), cv_ref, vwin_ref)):
        for s in range(n_seqs):
            shifted = pltpu.roll(old_ref[s], WINDOW - dec_seq, 1)
            tail = pltpu.roll(new, (WINDOW - dec_seq * (s + 1)) % CHUNK, 1)
            out_ref[s] = jnp.where(is_new, tail, shifted)

    def seq_rows(a, s):
        return a[dec_seq * s:dec_seq * (s + 1)]

    def gather_seq(per_head, s):
        return jnp.concatenate([seq_rows(a, s) for a in per_head], axis=0)

    def scatter_heads(per_seq, h):
        return jnp.concatenate([seq_rows(a, h) for a in per_seq], axis=0)

    lo_lane = lax.broadcasted_iota(jnp.int32, (CHUNK, LANES), 1) < HEAD_DIM
    q_heads = []
    for j in range(ATT_W // LANES):
        blk = q[:, LANES * j:LANES * (j + 1)]
        rolled = pltpu.roll(blk, HEAD_DIM, 1)
        g = (2 * j) // (ATT_HEADS // ATT_KV_HEADS)
        keep = lo_lane if g == 0 else jnp.logical_not(lo_lane)
        for e in range(2):
            q_heads.append(jnp.where(keep, blk if e == g else rolled, 0.0))
    s_win = [_dot(gather_seq(q_heads, s), ck_ref[s]) for s in range(n_seqs)]
    k_b = k.astype(BF16)
    v_b = v.astype(BF16)
    s_new = [_dot_nt(q_heads[h].astype(BF16), k_b) for h in range(ATT_HEADS)]
    b = _gla_exponents(log_a, gmat_ref[0:CHUNK, :])
    yield
    bias_win = bias_win_ref[...]
    bias_new = bias_new_ref[...]
    p_win = []
    o_new = []
    inv_den = []
    for h in range(ATT_HEADS):
        sw = scatter_heads(s_win, h) + bias_win
        sn = s_new[h] + bias_new
        sink = sinks_ref[h]
        m = jnp.maximum(jnp.maximum(jnp.max(sw, axis=-1, keepdims=True),
                                    jnp.max(sn, axis=-1, keepdims=True)), sink)
        pw = jnp.exp(sw - m)
        pn = jnp.exp(sn - m)
        den = (jnp.sum(pw, axis=-1, keepdims=True) + jnp.sum(pn, axis=-1, keepdims=True)
               + jnp.exp(sink - m))
        p_win.append(pw)
        o_new.append(_dot(pn.astype(BF16), v_b))
        inv_den.append(1.0 / den)
    o_win = [_dot_nt(gather_seq(p_win, s), cv_ref[s]) for s in range(n_seqs)]
    o_heads = [(scatter_heads(o_win, h) + o_new[h]) * inv_den[h] for h in range(ATT_HEADS)]
    mix_cols = []
    for j in range(ATT_W // LANES):
        g = (2 * j) // (ATT_HEADS // ATT_KV_HEADS)
        first = o_heads[2 * j] if g == 0 else pltpu.roll(o_heads[2 * j], HEAD_DIM, 1)
        second = o_heads[2 * j + 1] if g == 1 else pltpu.roll(o_heads[2 * j + 1], HEAD_DIM, 1)
        mix_cols.append(jnp.where(lo_lane, first, second).astype(BF16))

    b3 = b.reshape(n_seqs, dec_seq, GK_W)
    b_last = jnp.broadcast_to(b3[:, dec_seq - 1:dec_seq, :], b3.shape).reshape(CHUNK, GK_W)
    q_dec = gq * jnp.exp(b)
    k_dec_t = jnp.transpose(gk * jnp.exp(b_last - b))
    decay_t = jnp.exp(jnp.transpose(b_last))
    scores = _gla_intra(gq, gk, log_a, gmat_ref[CHUNK:, :], levels, masks_ref)
    yield
    lane_k = lax.broadcasted_iota(jnp.int32, (dec_seq, GK_W), 1)
    lane_t = lax.broadcasted_iota(jnp.int32, (GLA_DK, LANES), 1)
    o_inter = []
    for s in range(n_seqs):
        qd = seq_rows(q_dec, s)
        lhs = jnp.concatenate(
            [jnp.where((lane_k >= GLA_DK * h) & (lane_k < GLA_DK * (h + 1)), qd, 0.0)
             for h in range(GLA_HEADS)], axis=0)
        o_inter.append(_dot(lhs, st_ref[s]))
    g_gla = g_gla_ref[...]
    for h in range(GLA_HEADS):
        cs = slice(GLA_DV * h, GLA_DV * (h + 1))
        hs = slice(GLA_DK * h, GLA_DK * (h + 1))
        v_h = gv[:, cs].astype(BF16)
        o_h = _dot(scores[:, CHUNK * h:CHUNK * (h + 1)].astype(BF16), v_h) + scatter_heads(o_inter, h)
        mix_cols.append(_gla_out_gate(o_h, gr[:, cs], g_gla).astype(BF16))
        kd_h = k_dec_t[hs, :]
        lhs = jnp.concatenate(
            [jnp.where((lane_t >= dec_seq * s) & (lane_t < dec_seq * (s + 1)), kd_h, 0.0).astype(BF16)
             for s in range(n_seqs)], axis=0)
        upd = _dot(lhs, v_h)
        for s in range(n_seqs):
            a_col = jnp.broadcast_to(decay_t[hs, dec_seq * s:dec_seq * s + 1], (GLA_DK, GLA_DV))
            sout_ref[s, hs, :] = a_col * st_ref[s, hs, :] + upd[GLA_DK * s:GLA_DK * (s + 1)]
    results.update(x=x, mix=jnp.concatenate(mix_cols, axis=1))


def _gla_constants(seg, levels):
    t = np.arange(CHUNK)[:, None]
    u = np.arange(CHUNK)[None, :]
    same = (t // seg) == (u // seg)
    blocks = [same & (u <= t)]
    masks = [np.eye(CHUNK, dtype=bool)]
    for m in levels:
        ref = (t // (2 * m)) * (2 * m) + m - 1
        second = (t & m) != 0
        blocks.append(np.where(second, (u > ref) & (u <= t), (u > t) & (u <= ref)))
        masks.append(((t // (2 * m)) == (u // (2 * m))) & second & ((u & m) == 0))
    masks.append(blocks[0])
    gmat = jnp.asarray(np.concatenate(blocks, axis=0).astype(np.float32), dtype=BF16)
    return gmat, jnp.asarray(np.stack(masks).astype(np.float32))


def _attention_bias():
    t = (np.arange(2 * CHUNK) % CHUNK)[:, None]
    j = (np.arange(4 * CHUNK) % (2 * CHUNK))[None, :]
    band = (j >= t + 1) & (j <= t + WINDOW)
    first = band & (j >= CHUNK)
    neg = np.float32(-np.inf)
    return jnp.asarray(np.stack([np.where(band, np.float32(0), neg), np.where(first, np.float32(0), neg)]))


def _sample_attention_bias(dec_seq):
    r = np.arange(CHUNK)[:, None]
    c = np.arange(CHUNK)[None, :]
    t = r % dec_seq
    win = c > t
    new = ((r // dec_seq) == (c // dec_seq)) & ((c % dec_seq) <= t)
    neg = np.float32(-np.inf)
    zero = np.float32(0)
    return jnp.asarray(np.where(win, zero, neg)), jnp.asarray(np.where(new, zero, neg))


def _rope_tables(pos):
    half = HEAD_DIM // 2
    inv = ROPE_THETA ** (-jnp.arange(half, dtype=F32) / half)
    ang = inv[:, None] * pos.astype(F32)[None, :]
    return jnp.cos(ang), jnp.sin(ang)


def _const_spec(shape):
    zeros = (0,) * len(shape)
    return pl.BlockSpec(shape, lambda *_: zeros, pipeline_mode=pl.Buffered(1))


def kernel(x_prompt, x_sample, cache_k, cache_v, state_gla, w_in, w_gk2, b_gk, g_gla, sinks, w_out, g_mix_pre, g_mix_post, g_ffn_pre, g_ffn_post, w_up, w_down):
    depth = w_in.shape[0]
    assert depth == 1, "single trunk layer"
    batch, seq, d_model = x_prompt.shape
    dec_batch, dec_seq, _ = x_sample.shape
    d_ff = w_up.shape[2]
    assert seq % PROMPT_TILE == 0 and dec_batch % SAMPLE_SEQS == 0
    assert SAMPLE_SEQS * dec_seq == CHUNK and CHUNK % dec_seq == 0 and d_ff % FF_CHUNK == 0
    assert w_in.shape[2] == MAIN_W + GLA_RANK

    w_in_b = w_in[0].astype(BF16)
    w_gk2_b = w_gk2[0].astype(BF16)
    w_out_b = w_out[0].astype(BF16)
    w_up_b = w_up[0].astype(BF16)
    w_down_b = w_down[0].astype(BF16)
    g_pre = g_mix_pre[0][None, :]
    g_post = g_mix_post[0][None, :]
    g_fpre = g_ffn_pre[0][None, :]
    g_fpost = g_ffn_post[0][None, :]
    g_gla2 = g_gla[0][None, :]
    b_gk2 = b_gk[0][None, :]
    sinks1 = sinks[0]

    weights = (g_pre, g_post, g_fpre, g_fpost, g_gla2, b_gk2, w_in_b, w_gk2_b, w_out_b, w_up_b, w_down_b)
    weight_specs = [_const_spec(w.shape) for w in weights]
    smem_spec = pl.BlockSpec(memory_space=pltpu.SMEM)

    cos_p, sin_p = _rope_tables(jnp.arange(seq, dtype=jnp.int32))
    gmat_p, masks_p = _gla_constants(CHUNK, PROMPT_LEVELS)
    bias = _attention_bias()
    n_t = seq // PROMPT_TILE
    tile = PROMPT_TILE
    n_tiles = batch * n_t

    def cur_tile(j):
        return jnp.minimum(j, n_tiles - 1)

    def mix_tile(j):
        return jnp.clip(j - 1, 0, n_tiles - 1)

    def done_tile(j):
        return jnp.clip(j - 2, 0, n_tiles - 1)

    tab_spec = pl.BlockSpec((HEAD_DIM // 2, tile), lambda j: (0, cur_tile(j) % n_t))
    seq_out = lambda j: (cur_tile(j) // n_t, 0, 0)
    yp, kp, vp, sp = pl.pallas_call(
        functools.partial(_prompt_kernel, tiles_per_seq=n_t, n_tiles=n_tiles),
        grid=(n_tiles + 2,),
        in_specs=[smem_spec,
                  pl.BlockSpec((1, tile, d_model), lambda j: (cur_tile(j) // n_t, cur_tile(j) % n_t, 0)),
                  tab_spec, tab_spec,
                  _const_spec(bias.shape), _const_spec(gmat_p.shape), _const_spec(masks_p.shape)] + weight_specs,
        out_specs=[pl.BlockSpec((1, tile, d_model), lambda j: (done_tile(j) // n_t, done_tile(j) % n_t, 0)),
                   pl.BlockSpec((1, WINDOW, KV_W), seq_out),
                   pl.BlockSpec((1, WINDOW, KV_W), seq_out),
                   pl.BlockSpec((1, GK_W, GLA_DV), lambda j: (mix_tile(j) // n_t, 0, 0))],
        out_shape=[jax.ShapeDtypeStruct((batch, seq, d_model), F32),
                   jax.ShapeDtypeStruct((batch, WINDOW, KV_W), F32),
                   jax.ShapeDtypeStruct((batch, WINDOW, KV_W), F32),
                   jax.ShapeDtypeStruct((batch, GK_W, GLA_DV), F32)],
        scratch_shapes=[pltpu.VMEM((tile, ATT_W), BF16),
                        pltpu.VMEM((2 * ATT_KV_HEADS, CHUNK + tile, LANES), BF16),
                        pltpu.VMEM((2 * ATT_KV_HEADS, CHUNK + tile, LANES), BF16),
                        pltpu.VMEM((tile, GK_W), F32),
                        pltpu.VMEM((tile, GK_W), F32),
                        pltpu.VMEM((tile, GV_W), BF16),
                        pltpu.VMEM((tile, GV_W), F32),
                        pltpu.VMEM((tile, GK_W), F32),
                        pltpu.VMEM((GK_W, GLA_DV), F32),
                        pltpu.VMEM((tile, d_model), BF16),
                        pltpu.VMEM((tile, d_model), BF16),
                        pltpu.VMEM((tile, d_model), F32),
                        pltpu.VMEM((tile, d_model), F32),
                        pltpu.SMEM((1,), jnp.int32)],
        compiler_params=pltpu.CompilerParams(
            dimension_semantics=("arbitrary",), vmem_limit_bytes=VMEM_LIMIT_BYTES),
        name="prompt_layer",
    )(sinks1, x_prompt, cos_p, sin_p, bias, gmat_p, masks_p, *weights)

    levels_s = tuple(m for m in PROMPT_LEVELS if m < dec_seq)
    pos_s = PAST_LEN + jnp.arange(dec_seq, dtype=jnp.int32)
    cos_s, sin_s = (jnp.tile(a, (1, SAMPLE_SEQS)) for a in _rope_tables(pos_s))
    gmat_s, masks_s = _gla_constants(dec_seq, levels_s)
    bias_win, bias_new = _sample_attention_bias(dec_seq)
    rows = SAMPLE_SEQS * dec_seq
    n_g = dec_batch // SAMPLE_SEQS
    xs = x_sample.reshape(dec_batch * dec_seq, d_model)
    ck = jnp.transpose(cache_k[0].reshape(dec_batch, WINDOW, KV_W), (0, 2, 1))
    cv = jnp.transpose(cache_v[0].reshape(dec_batch, WINDOW, KV_W), (0, 2, 1))
    st = state_gla[0].reshape(dec_batch, GK_W, GLA_DV)
    seq_spec = pl.BlockSpec((SAMPLE_SEQS, WINDOW, KV_W), lambda i: (i, 0, 0))
    st_spec = pl.BlockSpec((SAMPLE_SEQS, GK_W, GLA_DV), lambda i: (i, 0, 0))
    row_spec = pl.BlockSpec((rows, d_model), lambda i: (i, 0))
    ys, ks, vs, ss = pl.pallas_call(
        functools.partial(_sample_kernel, dec_seq=dec_seq, levels=levels_s),
        grid=(n_g,),
        in_specs=[smem_spec, row_spec,
                  _const_spec(cos_s.shape), _const_spec(sin_s.shape),
                  _const_spec(gmat_s.shape), _const_spec(masks_s.shape),
                  _const_spec(bias_win.shape), _const_spec(bias_new.shape),
                  seq_spec, seq_spec, st_spec] + weight_specs,
        out_specs=[row_spec, seq_spec, seq_spec, st_spec],
        out_shape=[jax.ShapeDtypeStruct((dec_batch * dec_seq, d_model), F32),
                   jax.ShapeDtypeStruct((dec_batch, WINDOW, KV_W), F32),
                   jax.ShapeDtypeStruct((dec_batch, WINDOW, KV_W), F32),
                   jax.ShapeDtypeStruct((dec_batch, GK_W, GLA_DV), F32)],
        compiler_params=pltpu.CompilerParams(
            dimension_semantics=("arbitrary",), vmem_limit_bytes=VMEM_LIMIT_BYTES),
        name="sample_layer",
    )(sinks1, xs, cos_s, sin_s, gmat_s, masks_s, bias_win, bias_new, ck, cv, st, *weights)

    head_shape = (ATT_KV_HEADS, HEAD_DIM)
    state_shape = (GLA_HEADS, GLA_DK, GLA_DV)
    return (yp,
            ys.reshape(dec_batch, dec_seq, d_model),
            jnp.transpose(kp, (0, 2, 1)).reshape((1, batch, WINDOW) + head_shape),
            jnp.transpose(vp, (0, 2, 1)).reshape((1, batch, WINDOW) + head_shape),
            sp.reshape((1, batch) + state_shape),
            jnp.transpose(ks, (0, 2, 1)).reshape((1, dec_batch, WINDOW) + head_shape),
            jnp.transpose(vs, (0, 2, 1)).reshape((1, dec_batch, WINDOW) + head_shape),
            ss.reshape((1, dec_batch) + state_shape))
```

```python
import functools

import numpy as np
import jax
import jax.numpy as jnp
from jax import lax
from jax.experimental import pallas as pl
from jax.experimental.pallas import tpu as pltpu

F32 = jnp.float32
BF16 = jnp.bfloat16

HEAD_DIM = 64
ATT_HEADS = 8
ATT_KV_HEADS = 2
WINDOW = 128
PAST_LEN = 16384
ROPE_THETA = 10000.0
GLA_HEADS = 4
GLA_DK = 64
GLA_DV = 128
GLA_RANK = 16
GLA_GATE_TEMP = 16.0
EPS = 1e-6
ATT_W = ATT_HEADS * HEAD_DIM
KV_W = ATT_KV_HEADS * HEAD_DIM
GK_W = GLA_HEADS * GLA_DK
GV_W = GLA_HEADS * GLA_DV
COL_Q = 0
COL_K = COL_Q + ATT_W
COL_V = COL_K + KV_W
COL_GQ = COL_V + KV_W
COL_GK = COL_GQ + GK_W
COL_GV = COL_GK + GK_W
COL_GR = COL_GV + GV_W
MAIN_W = COL_GR + GV_W
QK_SCALE = HEAD_DIM ** -0.5
GLA_SCALE = GLA_DK ** -0.5

CHUNK = 128
LANES = 128
PROMPT_TILE = 256
SAMPLE_SEQS = 16
FF_CHUNK = 1024
PROMPT_LEVELS = (1, 2, 4, 8, 16, 32, 64)
BOUNDED_LOG_DECAY = 30.0
VMEM_LIMIT_BYTES = 56 * 1024 * 1024


def _dot(a, b):
    return jnp.dot(a, b, preferred_element_type=F32)


def _dot_nt(a, b):
    return lax.dot_general(a, b, (((1,), (1,)), ((), ())), preferred_element_type=F32)


def _rms(x, g):
    ms = jnp.mean(x * x, axis=-1, keepdims=True)
    return x * lax.rsqrt(ms + EPS) * g


def _rope(x, cos, sin_a, sin_b):
    return x * cos + pltpu.roll(x, 96, 1) * sin_a + pltpu.roll(x, 32, 1) * sin_b


def _rope_lanes(cos, sin):
    reps = LANES // cos.shape[0]
    cos_l = jnp.transpose(jnp.concatenate([cos] * reps, axis=0))
    sin_l = jnp.transpose(jnp.concatenate([sin] * reps, axis=0))
    lane = lax.broadcasted_iota(jnp.int32, cos_l.shape, 1)
    first_half = (lane & (HEAD_DIM // 2)) == 0
    return cos_l, jnp.where(first_half, -sin_l, 0.0), jnp.where(first_half, 0.0, sin_l)


def _split3(x):
    hi = x.astype(BF16)
    r1 = x - hi.astype(F32)
    mid = r1.astype(BF16)
    lo = (r1 - mid.astype(F32)).astype(BF16)
    return hi, mid, lo


def _project(x, g_pre, w_in_ref):
    h = _rms(x, g_pre).astype(BF16)
    return _dot(h, w_in_ref[:, 0:MAIN_W]), _dot(h, w_in_ref[:, MAIN_W:])


def _split_projection(proj, glr, cos_ref, sin_ref, w_gk2_ref, b_gk_ref):
    cos, sin_a, sin_b = _rope_lanes(cos_ref[...], sin_ref[...])
    z = _dot(glr.astype(BF16), w_gk2_ref[...]) + b_gk_ref[...]
    log_a = (jnp.minimum(z, 0.0) - jnp.log1p(jnp.exp(-jnp.abs(z)))) * (1.0 / GLA_GATE_TEMP)
    q = jnp.concatenate(
        [_rope(proj[:, COL_Q + LANES * j:COL_Q + LANES * (j + 1)], cos, sin_a, sin_b) * QK_SCALE
         for j in range(ATT_W // LANES)], axis=1)
    k = _rope(proj[:, COL_K:COL_V], cos, sin_a, sin_b)
    v = proj[:, COL_V:COL_GQ]
    gq = proj[:, COL_GQ:COL_GK] * GLA_SCALE
    gk = proj[:, COL_GK:COL_GV]
    gv = proj[:, COL_GV:COL_GR]
    gr = proj[:, COL_GR:MAIN_W]
    return q, k, v, gq, gk, gv, gr, log_a


def _output_stage(x_prev, mix_prev, g_post_ref, g_fpre_ref, g_fpost_ref, w_out_ref, w_up_ref, w_down_ref, write):
    mix = _dot(mix_prev[...], w_out_ref[...])
    yield
    x1 = x_prev[...] + _rms(mix, g_post_ref[...])
    h2 = _rms(x1, g_fpre_ref[...]).astype(BF16)
    f = None
    for i in range(w_up_ref.shape[1] // FF_CHUNK):
        u = jnp.maximum(_dot(h2, w_up_ref[:, FF_CHUNK * i:FF_CHUNK * (i + 1)]), 0.0)
        yield
        part = _dot((u * u).astype(BF16), w_down_ref[FF_CHUNK * i:FF_CHUNK * (i + 1), :])
        f = part if f is None else f + part
        yield
    write(x1 + _rms(f, g_fpost_ref[...]))


def _lo_hi_forms(a):
    lo = lax.broadcasted_iota(jnp.int32, a.shape, 1) < HEAD_DIM
    r = pltpu.roll(a, HEAD_DIM, 1)
    zero = jnp.zeros_like(a)
    return ((jnp.where(lo, a, zero), jnp.where(lo, zero, r)),
            (jnp.where(lo, r, zero), jnp.where(lo, zero, a)))


def _gla_exponents(log_a, gmat):
    hi, mid, lo = _split3(log_a)
    return _dot(gmat, hi) + _dot(gmat, mid) + _dot(gmat, lo)


def _head_lanes(h, rows):
    lane = lax.broadcasted_iota(jnp.int32, (rows, LANES), 1)
    e = h % 2
    return (lane >= GLA_DK * e) & (lane < GLA_DK * (e + 1))


def _gla_intra(gq, gk, log_a, gmat_lv, levels, masks_ref):
    expo = _gla_exponents(log_a, gmat_lv)
    row = lax.broadcasted_iota(jnp.int32, gq.shape, 0)
    terms = [(gq.astype(BF16), gk.astype(BF16), 0)]
    for li, m in enumerate(levels):
        ex = jnp.exp(expo[CHUNK * li:CHUNK * (li + 1)])
        z = (jnp.where((row & m) != 0, gq, gk) * ex).astype(BF16)
        terms.append((z, z, 1 + li))
    scores = []
    for h in range(GLA_HEADS):
        p = h // 2
        in_head = _head_lanes(h, gq.shape[0])
        acc = None
        for zq, zk, mi in terms:
            zq_p = zq[:, LANES * p:LANES * (p + 1)]
            zk_p = zk[:, LANES * p:LANES * (p + 1)]
            zk_h = jnp.where(in_head, zk_p, jnp.zeros_like(zk_p))
            t = _dot_nt(zq_p, zk_h) * masks_ref[mi]
            acc = t if acc is None else acc + t
        scores.append(acc)
    return jnp.concatenate(scores, axis=1)


def _gla_intra_bounded(q_dec, gk, b, causal):
    k_grow_t = jnp.transpose(gk * jnp.exp(-b)).astype(BF16)
    scores = []
    for h in range(GLA_HEADS):
        p = h // 2
        qd_p = q_dec[:, LANES * p:LANES * (p + 1)]
        qd_h = jnp.where(_head_lanes(h, q_dec.shape[0]), qd_p, jnp.zeros_like(qd_p))
        scores.append(_dot(qd_h, k_grow_t[LANES * p:LANES * (p + 1), :]) * causal)
    return jnp.concatenate(scores, axis=1)


def _gla_out_gate(o_h, gr_h, g_gla):
    on = _rms(o_h, g_gla)
    return on * (gr_h * (1.0 / (1.0 + jnp.exp(-gr_h))))


def _prompt_kernel(sinks_ref, x_ref, xo_ref, cos_ref, sin_ref, bias_ref, gmat_ref, masks_ref,
                   g_pre_ref, g_post_ref, g_fpre_ref, g_fpost_ref, g_gla_ref, b_gk_ref,
                   w_in_ref, w_gk2_ref, w_out_ref, w_up_ref, w_down_ref,
                   y_ref, kwin_ref, vwin_ref, sout_ref,
                   q_scr, kbd_scr, vbd_scr, gq_scr, gk_scr, gv_scr, gr_scr, la_scr, s_scr,
                   mix_scr, bounded_scr, *, tiles_per_seq, n_tiles):
    j = pl.program_id(0)
    t_step = lax.rem(jnp.clip(j - 1, 0, n_tiles - 1), tiles_per_seq)
    slot = lax.rem(j, 2)
    tile = x_ref.shape[1]
    n_chunks = tile // CHUNK

    @pl.when(j == 0)
    def _():
        for ref in (s_scr, q_scr, kbd_scr, vbd_scr, gq_scr, gk_scr, gv_scr, gr_scr, la_scr,
                    mix_scr):
            ref[...] = jnp.zeros_like(ref)
        bounded_scr[0] = 1

    branch = jnp.where(j == 0, 3, jnp.where(j == n_tiles + 1, 2, jnp.where(bounded_scr[0] == 1, 0, 1)))

    lo_lane = lax.broadcasted_iota(jnp.int32, (CHUNK, LANES), 1) < HEAD_DIM
    g_gla = g_gla_ref[...]

    def input_stage(results):
        x = x_ref[0]
        proj, glr = _project(x, g_pre_ref[...], w_in_ref)
        yield
        q, k, v, gq, gk, gv, gr, log_a = _split_projection(proj, glr, cos_ref, sin_ref, w_gk2_ref, b_gk_ref)
        chunk_decay = jnp.concatenate(
            [jnp.sum(log_a[CHUNK * c:CHUNK * (c + 1)], axis=0, keepdims=True) for c in range(n_chunks)], axis=0)
        results.update(
            q=q.astype(BF16), k=k, v=v, k_forms=_lo_hi_forms(k), v_forms=_lo_hi_forms(v),
            gq=gq, gk=gk, gv=gv.astype(BF16), gr=gr, log_a=log_a,
            bounded=jnp.min(chunk_decay) >= -BOUNDED_LOG_DECAY)
        yield

    def hand_over(results):
        kbd_scr[:, 0:CHUNK, :] = kbd_scr[:, tile:tile + CHUNK, :]
        vbd_scr[:, 0:CHUNK, :] = vbd_scr[:, tile:tile + CHUNK, :]
        for g in range(ATT_KV_HEADS):
            for e in range(2):
                kbd_scr[2 * g + e, CHUNK:CHUNK + tile, :] = results["k_forms"][g][e].astype(BF16)
                vbd_scr[2 * g + e, CHUNK:CHUNK + tile, :] = results["v_forms"][g][e].astype(BF16)
        q_scr[...] = results["q"]
        gq_scr[...] = results["gq"]
        gk_scr[...] = results["gk"]
        gv_scr[...] = results["gv"]
        gr_scr[...] = results["gr"]
        la_scr[...] = results["log_a"]
        bounded_scr[0] = jnp.where(results["bounded"], 1, 0)
        kwin_ref[0] = jnp.transpose(results["k"][tile - WINDOW:tile, :])
        vwin_ref[0] = jnp.transpose(results["v"][tile - WINDOW:tile, :])

    def attention_chunk(c):
        rows = slice(CHUNK * c, CHUNK * (c + 1))
        rows2 = slice(CHUNK * c, CHUNK * (c + 2))
        bias = bias_ref[jnp.where(t_step == 0, 1, 0)] if c == 0 else bias_ref[0]
        scores = []
        for g in range(ATT_KV_HEADS):
            lhs = jnp.concatenate([q_scr[rows, 2 * LANES * g:2 * LANES * g + LANES],
                                   q_scr[rows, 2 * LANES * g + LANES:2 * LANES * (g + 1)]], axis=0)
            kcat = jnp.concatenate([kbd_scr[2 * g, rows2, :], kbd_scr[2 * g + 1, rows2, :]], axis=0)
            scores.append(_dot_nt(lhs, kcat) + bias)
        yield
        for g in range(ATT_KV_HEADS):
            s = scores[g]
            p_blocks = []
            inv = []
            for blk in range(2):
                p_row = []
                inv_row = []
                for e in range(2):
                    sub = s[CHUNK * blk:CHUNK * (blk + 1), 2 * CHUNK * e:2 * CHUNK * (e + 1)]
                    sink = sinks_ref[4 * g + 2 * blk + e]
                    m = jnp.maximum(jnp.max(sub, axis=-1, keepdims=True), sink)
                    p = jnp.exp(sub - m)
                    den = jnp.sum(p, axis=-1, keepdims=True) + jnp.exp(sink - m)
                    p_row.append(p.astype(BF16))
                    inv_row.append(1.0 / den)
                p_blocks.append(jnp.concatenate(p_row, axis=1))
                inv.append(inv_row)
            p_all = jnp.concatenate(p_blocks, axis=0)
            vcat = jnp.concatenate([vbd_scr[2 * g, rows2, :], vbd_scr[2 * g + 1, rows2, :]], axis=0)
            o = _dot(p_all, vcat)
            for blk in range(2):
                scale = jnp.where(lo_lane, inv[blk][0], inv[blk][1])
                c0 = 2 * LANES * g + LANES * blk
                mix_scr[slot, rows, c0:c0 + LANES] = (o[CHUNK * blk:CHUNK * (blk + 1)] * scale).astype(BF16)
        yield

    def gla_chunk(c, state, bounded_decay):
        rows = slice(CHUNK * c, CHUNK * (c + 1))
        gq_c = gq_scr[rows, :]
        gk_c = gk_scr[rows, :]
        la_c = la_scr[rows, :]
        b = _gla_exponents(la_c, gmat_ref[0:CHUNK, :])
        b_last = jnp.broadcast_to(b[CHUNK - 1:CHUNK, :], b.shape)
        q_dec_f = gq_c * jnp.exp(b)
        q_dec = q_dec_f.astype(BF16)
        k_dec_t = jnp.transpose(gk_c * jnp.exp(b_last - b)).astype(BF16)
        decay_t = jnp.exp(jnp.transpose(b_last))
        yield
        if bounded_decay:
            scores = _gla_intra_bounded(q_dec_f, gk_c, b, masks_ref[len(PROMPT_LEVELS) + 1])
        else:
            scores = _gla_intra(gq_c, gk_c, la_c, gmat_ref[CHUNK:, :], PROMPT_LEVELS, masks_ref)
        yield
        for h in range(GLA_HEADS):
            p_ = h // 2
            v_h = gv_scr[rows, GLA_DV * h:GLA_DV * (h + 1)]
            qd_p = q_dec[:, LANES * p_:LANES * (p_ + 1)]
            qd_h = jnp.where(_head_lanes(h, CHUNK), qd_p, jnp.zeros_like(qd_p))
            s0_pair = jnp.concatenate([state[2 * p_], state[2 * p_ + 1]], axis=0).astype(BF16)
            o_h = _dot(scores[:, CHUNK * h:CHUNK * (h + 1)].astype(BF16), v_h) + _dot(qd_h, s0_pair)
            out_h = _gla_out_gate(o_h, gr_scr[rows, GLA_DV * h:GLA_DV * (h + 1)], g_gla)
            mix_scr[slot, rows, ATT_W + GLA_DV * h:ATT_W + GLA_DV * (h + 1)] = out_h.astype(BF16)
        new_state = [decay_t[GLA_DK * h:GLA_DK * (h + 1), :] * state[h]
                     + _dot(k_dec_t[GLA_DK * h:GLA_DK * (h + 1), :],
                            gv_scr[rows, GLA_DV * h:GLA_DV * (h + 1)])
                     for h in range(GLA_HEADS)]
        state[:] = new_state
        yield

    def write_y(y):
        y_ref[0] = y

    def output_stage():
        return _output_stage(xo_ref.at[0], mix_scr.at[1 - slot], g_post_ref, g_fpre_ref, g_fpost_ref,
                             w_out_ref, w_up_ref, w_down_ref, write_y)

    def run_step(bounded_decay):
        keep = jnp.where(t_step == 0, 0.0, 1.0)
        state = [s_scr[GLA_DK * h:GLA_DK * (h + 1), :] * keep for h in range(GLA_HEADS)]
        results = {}
        out = output_stage()
        inp = input_stage(results)
        att = [attention_chunk(c) for c in range(n_chunks)]
        gla = [gla_chunk(c, state, bounded_decay) for c in range(n_chunks)]
        next(out)
        for c in range(n_chunks):
            next(att[c])
            next(gla[c])
        next(inp)
        spread = [(att[c], gla[c]) for c in range(n_chunks)]
        spread += [(gla[c],) for c in range(n_chunks)]
        spread.append((inp,))
        for stage in spread:
            next(out)
            for gen in stage:
                next(gen)
        for _ in out:
            pass
        s_all = jnp.concatenate(state, axis=0)
        s_scr[...] = s_all
        sout_ref[0] = s_all
        hand_over(results)

    def finish_last_tile():
        for _ in output_stage():
            pass

    def start_first_tile():
        results = {}
        for _ in input_stage(results):
            pass
        hand_over(results)

    lax.switch(branch, [lambda: run_step(True), lambda: run_step(False), finish_last_tile, start_first_tile])


def _sample_kernel(sinks_ref, x_ref, cos_ref, sin_ref, gmat_ref, masks_ref, bias_win_ref, bias_new_ref,
                   ck_ref, cv_ref, st_ref,
                   g_pre_ref, g_post_ref, g_fpre_ref, g_fpost_ref, g_gla_ref, b_gk_ref,
                   w_in_ref, w_gk2_ref, w_out_ref, w_up_ref, w_down_ref,
                   y_ref, kwin_ref, vwin_ref, sout_ref,
                   *, dec_seq, levels):
    results = {}
    for _ in _sample_tile_stages(sinks_ref, x_ref, cos_ref, sin_ref, gmat_ref, masks_ref, bias_win_ref,
                                 bias_new_ref, ck_ref, cv_ref, st_ref, g_pre_ref, g_gla_ref, b_gk_ref,
                                 w_in_ref, w_gk2_ref, kwin_ref, vwin_ref, sout_ref, results,
                                 dec_seq=dec_seq, levels=levels):
        pass

    def write_y(y):
        y_ref[...] = y

    for _ in _output_stage(results["x"], results["mix"], g_post_ref, g_fpre_ref, g_fpost_ref,
                           w_out_ref, w_up_ref, w_down_ref, write_y):
        pass


def _sample_tile_stages(sinks_ref, x_ref, cos_ref, sin_ref, gmat_ref, masks_ref, bias_win_ref, bias_new_ref,
                        ck_ref, cv_ref, st_ref, g_pre_ref, g_gla_ref, b_gk_ref, w_in_ref, w_gk2_ref,
                        kwin_ref, vwin_ref, sout_ref, results, *, dec_seq, levels):
    n_seqs = ck_ref.shape[0]
    x = x_ref[...]
    proj, glr = _project(x, g_pre_ref[...], w_in_ref)
    yield
    q, k, v, gq, gk, gv, gr, log_a = _split_projection(proj, glr, cos_ref, sin_ref, w_gk2_ref, b_gk_ref)

    is_new = lax.broadcasted_iota(jnp.int32, (KV_W, WINDOW), 1) >= WINDOW - dec_seq
    for new, old_ref, out_ref in ((jnp.transpose(k), ck_ref, kwin_ref), (jnp.transpose(v), cv_ref, vwin_ref)):
        for s in range(n_seqs):
            shifted = pltpu.roll(old_ref[s], WINDOW - dec_seq, 1)
            tail = pltpu.roll(new, (WINDOW - dec_seq * (s + 1)) % CHUNK, 1)
            out_ref[s] = jnp.where(is_new, tail, shifted)

    def seq_rows(a, s):
        return a[dec_seq * s:dec_seq * (s + 1)]

    def gather_seq(per_head, s):
        return jnp.concatenate([seq_rows(a, s) for a in per_head], axis=0)

    def scatter_heads(per_seq, h):
        return jnp.concatenate([seq_rows(a, h) for a in per_seq], axis=0)

    lo_lane = lax.broadcasted_iota(jnp.int32, (CHUNK, LANES), 1) < HEAD_DIM
    q_heads = []
    for j in range(ATT_W // LANES):
        blk = q[:, LANES * j:LANES * (j + 1)]
        rolled = pltpu.roll(blk, HEAD_DIM, 1)
        g = (2 * j) // (ATT_HEADS // ATT_KV_HEADS)
        keep = lo_lane if g == 0 else jnp.logical_not(lo_lane)
        for e in range(2):
            q_heads.append(jnp.where(keep, blk if e == g else rolled, 0.0))
    s_win = [_dot(gather_seq(q_heads, s), ck_ref[s]) for s in range(n_seqs)]
    k_b = k.astype(BF16)
    v_b = v.astype(BF16)
    s_new = [_dot_nt(q_heads[h].astype(BF16), k_b) for h in range(ATT_HEADS)]
    b = _gla_exponents(log_a, gmat_ref[0:CHUNK, :])
    yield
    bias_win = bias_win_ref[...]
    bias_new = bias_new_ref[...]
    p_win = []
    o_new = []
    inv_den = []
    for h in range(ATT_HEADS):
        sw = scatter_heads(s_win, h) + bias_win
        sn = s_new[h] + bias_new
        sink = sinks_ref[h]
        m = jnp.maximum(jnp.maximum(jnp.max(sw, axis=-1, keepdims=True),
                                    jnp.max(sn, axis=-1, keepdims=True)), sink)
        pw = jnp.exp(sw - m)
        pn = jnp.exp(sn - m)
        den = (jnp.sum(pw, axis=-1, keepdims=True) + jnp.sum(pn, axis=-1, keepdims=True)
               + jnp.exp(sink - m))
        p_win.append(pw)
        o_new.append(_dot(pn.astype(BF16), v_b))
        inv_den.append(1.0 / den)
    o_win = [_dot_nt(gather_seq(p_win, s), cv_ref[s]) for s in range(n_seqs)]
    o_heads = [(scatter_heads(o_win, h) + o_new[h]) * inv_den[h] for h in range(ATT_HEADS)]
    mix_cols = []
    for j in range(ATT_W // LANES):
        g = (2 * j) // (ATT_HEADS // ATT_KV_HEADS)
        first = o_heads[2 * j] if g == 0 else pltpu.roll(o_heads[2 * j], HEAD_DIM, 1)
        second = o_heads[2 * j + 1] if g == 1 else pltpu.roll(o_heads[2 * j + 1], HEAD_DIM, 1)
        mix_cols.append(jnp.where(lo_lane, first, second).astype(BF16))

    b3 = b.reshape(n_seqs, dec_seq, GK_W)
    b_last = jnp.broadcast_to(b3[:, dec_seq - 1:dec_seq, :], b3.shape).reshape(CHUNK, GK_W)
    q_dec = gq * jnp.exp(b)
    k_dec_t = jnp.transpose(gk * jnp.exp(b_last - b))
    decay_t = jnp.exp(jnp.transpose(b_last))
    scores = _gla_intra(gq, gk, log_a, gmat_ref[CHUNK:, :], levels, masks_ref)
    yield
    lane_k = lax.broadcasted_iota(jnp.int32, (dec_seq, GK_W), 1)
    lane_t = lax.broadcasted_iota(jnp.int32, (GLA_DK, LANES), 1)
    o_inter = []
    for s in range(n_seqs):
        qd = seq_rows(q_dec, s)
        lhs = jnp.concatenate(
            [jnp.where((lane_k >= GLA_DK * h) & (lane_k < GLA_DK * (h + 1)), qd, 0.0)
             for h in range(GLA_HEADS)], axis=0)
        o_inter.append(_dot(lhs, st_ref[s]))
    g_gla = g_gla_ref[...]
    for h in range(GLA_HEADS):
        cs = slice(GLA_DV * h, GLA_DV * (h + 1))
        hs = slice(GLA_DK * h, GLA_DK * (h + 1))
        v_h = gv[:, cs].astype(BF16)
        o_h = _dot(scores[:, CHUNK * h:CHUNK * (h + 1)].astype(BF16), v_h) + scatter_heads(o_inter, h)
        mix_cols.append(_gla_out_gate(o_h, gr[:, cs], g_gla).astype(BF16))
        kd_h = k_dec_t[hs, :]
        lhs = jnp.concatenate(
            [jnp.where((lane_t >= dec_seq * s) & (lane_t < dec_seq * (s + 1)), kd_h, 0.0).astype(BF16)
             for s in range(n_seqs)], axis=0)
        upd = _dot(lhs, v_h)
        for s in range(n_seqs):
            a_col = jnp.broadcast_to(decay_t[hs, dec_seq * s:dec_seq * s + 1], (GLA_DK, GLA_DV))
            sout_ref[s, hs, :] = a_col * st_ref[s, hs, :] + upd[GLA_DK * s:GLA_DK * (s + 1)]
    results.update(x=x, mix=jnp.concatenate(mix_cols, axis=1))


def _gla_constants(seg, levels):
    t = np.arange(CHUNK)[:, None]
    u = np.arange(CHUNK)[None, :]
    same = (t // seg) == (u // seg)
    blocks = [same & (u <= t)]
    masks = [np.eye(CHUNK, dtype=bool)]
    for m in levels:
        ref = (t // (2 * m)) * (2 * m) + m - 1
        second = (t & m) != 0
        blocks.append(np.where(second, (u > ref) & (u <= t), (u > t) & (u <= ref)))
        masks.append(((t // (2 * m)) == (u // (2 * m))) & second & ((u & m) == 0))
    masks.append(blocks[0])
    gmat = jnp.asarray(np.concatenate(blocks, axis=0).astype(np.float32), dtype=BF16)
    return gmat, jnp.asarray(np.stack(masks).astype(np.float32))


def _attention_bias():
    t = (np.arange(2 * CHUNK) % CHUNK)[:, None]
    j = (np.arange(4 * CHUNK) % (2 * CHUNK))[None, :]
    band = (j >= t + 1) & (j <= t + WINDOW)
    first = band & (j >= CHUNK)
    neg = np.float32(-np.inf)
    return jnp.asarray(np.stack([np.where(band, np.float32(0), neg), np.where(first, np.float32(0), neg)]))


def _sample_attention_bias(dec_seq):
    r = np.arange(CHUNK)[:, None]
    c = np.arange(CHUNK)[None, :]
    t = r % dec_seq
    win = c > t
    new = ((r // dec_seq) == (c // dec_seq)) & ((c % dec_seq) <= t)
    neg = np.float32(-np.inf)
    zero = np.float32(0)
    return jnp.asarray(np.where(win, zero, neg)), jnp.asarray(np.where(new, zero, neg))


def _rope_tables(pos):
    half = HEAD_DIM // 2
    inv = ROPE_THETA ** (-jnp.arange(half, dtype=F32) / half)
    ang = inv[:, None] * pos.astype(F32)[None, :]
    return jnp.cos(ang), jnp.sin(ang)


def _const_spec(shape):
    zeros = (0,) * len(shape)
    return pl.BlockSpec(shape, lambda *_: zeros, pipeline_mode=pl.Buffered(1))


def kernel(x_prompt, x_sample, cache_k, cache_v, state_gla, w_in, w_gk2, b_gk, g_gla, sinks, w_out, g_mix_pre, g_mix_post, g_ffn_pre, g_ffn_post, w_up, w_down):
    depth = w_in.shape[0]
    assert depth == 1, "single trunk layer"
    batch, seq, d_model = x_prompt.shape
    dec_batch, dec_seq, _ = x_sample.shape
    d_ff = w_up.shape[2]
    assert seq % PROMPT_TILE == 0 and dec_batch % SAMPLE_SEQS == 0
    assert SAMPLE_SEQS * dec_seq == CHUNK and CHUNK % dec_seq == 0 and d_ff % FF_CHUNK == 0
    assert w_in.shape[2] == MAIN_W + GLA_RANK

    w_in_b = w_in[0].astype(BF16)
    w_gk2_b = w_gk2[0].astype(BF16)
    w_out_b = w_out[0].astype(BF16)
    w_up_b = w_up[0].astype(BF16)
    w_down_b = w_down[0].astype(BF16)
    g_pre = g_mix_pre[0][None, :]
    g_post = g_mix_post[0][None, :]
    g_fpre = g_ffn_pre[0][None, :]
    g_fpost = g_ffn_post[0][None, :]
    g_gla2 = g_gla[0][None, :]
    b_gk2 = b_gk[0][None, :]
    sinks1 = sinks[0]

    weights = (g_pre, g_post, g_fpre, g_fpost, g_gla2, b_gk2, w_in_b, w_gk2_b, w_out_b, w_up_b, w_down_b)
    weight_specs = [_const_spec(w.shape) for w in weights]
    smem_spec = pl.BlockSpec(memory_space=pltpu.SMEM)

    cos_p, sin_p = _rope_tables(jnp.arange(seq, dtype=jnp.int32))
    gmat_p, masks_p = _gla_constants(CHUNK, PROMPT_LEVELS)
    bias = _attention_bias()
    n_t = seq // PROMPT_TILE
    tile = PROMPT_TILE
    n_tiles = batch * n_t

    def cur_tile(j):
        return jnp.minimum(j, n_tiles - 1)

    def mix_tile(j):
        return jnp.clip(j - 1, 0, n_tiles - 1)

    def done_tile(j):
        return jnp.clip(j - 2, 0, n_tiles - 1)

    tab_spec = pl.BlockSpec((HEAD_DIM // 2, tile), lambda j: (0, cur_tile(j) % n_t))
    seq_out = lambda j: (cur_tile(j) // n_t, 0, 0)
    yp, kp, vp, sp = pl.pallas_call(
        functools.partial(_prompt_kernel, tiles_per_seq=n_t, n_tiles=n_tiles),
        grid=(n_tiles + 2,),
        in_specs=[smem_spec,
                  pl.BlockSpec((1, tile, d_model), lambda j: (cur_tile(j) // n_t, cur_tile(j) % n_t, 0)),
                  pl.BlockSpec((1, tile, d_model), lambda j: (done_tile(j) // n_t, done_tile(j) % n_t, 0)),
                  tab_spec, tab_spec,
                  _const_spec(bias.shape), _const_spec(gmat_p.shape), _const_spec(masks_p.shape)] + weight_specs,
        out_specs=[pl.BlockSpec((1, tile, d_model), lambda j: (done_tile(j) // n_t, done_tile(j) % n_t, 0)),
                   pl.BlockSpec((1, WINDOW, KV_W), seq_out),
                   pl.BlockSpec((1, WINDOW, KV_W), seq_out),
                   pl.BlockSpec((1, GK_W, GLA_DV), lambda j: (mix_tile(j) // n_t, 0, 0))],
        out_shape=[jax.ShapeDtypeStruct((batch, seq, d_model), F32),
                   jax.ShapeDtypeStruct((batch, WINDOW, KV_W), F32),
                   jax.ShapeDtypeStruct((batch, WINDOW, KV_W), F32),
                   jax.ShapeDtypeStruct((batch, GK_W, GLA_DV), F32)],
        scratch_shapes=[pltpu.VMEM((tile, ATT_W), BF16),
                        pltpu.VMEM((2 * ATT_KV_HEADS, CHUNK + tile, LANES), BF16),
                        pltpu.VMEM((2 * ATT_KV_HEADS, CHUNK + tile, LANES), BF16),
                        pltpu.VMEM((tile, GK_W), F32),
                        pltpu.VMEM((tile, GK_W), F32),
                        pltpu.VMEM((tile, GV_W), BF16),
                        pltpu.VMEM((tile, GV_W), F32),
                        pltpu.VMEM((tile, GK_W), F32),
                        pltpu.VMEM((GK_W, GLA_DV), F32),
                        pltpu.VMEM((2, tile, d_model), BF16),
                        pltpu.SMEM((1,), jnp.int32)],
        compiler_params=pltpu.CompilerParams(
            dimension_semantics=("arbitrary",), vmem_limit_bytes=VMEM_LIMIT_BYTES),
        name="prompt_layer",
    )(sinks1, x_prompt, x_prompt, cos_p, sin_p, bias, gmat_p, masks_p, *weights)

    levels_s = tuple(m for m in PROMPT_LEVELS if m < dec_seq)
    pos_s = PAST_LEN + jnp.arange(dec_seq, dtype=jnp.int32)
    cos_s, sin_s = (jnp.tile(a, (1, SAMPLE_SEQS)) for a in _rope_tables(pos_s))
    gmat_s, masks_s = _gla_constants(dec_seq, levels_s)
    bias_win, bias_new = _sample_attention_bias(dec_seq)
    rows = SAMPLE_SEQS * dec_seq
    n_g = dec_batch // SAMPLE_SEQS
    xs = x_sample.reshape(dec_batch * dec_seq, d_model)
    ck = jnp.transpose(cache_k[0].reshape(dec_batch, WINDOW, KV_W), (0, 2, 1))
    cv = jnp.transpose(cache_v[0].reshape(dec_batch, WINDOW, KV_W), (0, 2, 1))
    st = state_gla[0].reshape(dec_batch, GK_W, GLA_DV)
    seq_spec = pl.BlockSpec((SAMPLE_SEQS, WINDOW, KV_W), lambda i: (i, 0, 0))
    st_spec = pl.BlockSpec((SAMPLE_SEQS, GK_W, GLA_DV), lambda i: (i, 0, 0))
    row_spec = pl.BlockSpec((rows, d_model), lambda i: (i, 0))
    ys, ks, vs, ss = pl.pallas_call(
        functools.partial(_sample_kernel, dec_seq=dec_seq, levels=levels_s),
        grid=(n_g,),
        in_specs=[smem_spec, row_spec,
                  _const_spec(cos_s.shape), _const_spec(sin_s.shape),
                  _const_spec(gmat_s.shape), _const_spec(masks_s.shape),
                  _const_spec(bias_win.shape), _const_spec(bias_new.shape),
                  seq_spec, seq_spec, st_spec] + weight_specs,
        out_specs=[row_spec, seq_spec, seq_spec, st_spec],
        out_shape=[jax.ShapeDtypeStruct((dec_batch * dec_seq, d_model), F32),
                   jax.ShapeDtypeStruct((dec_batch, WINDOW, KV_W), F32),
                   jax.ShapeDtypeStruct((dec_batch, WINDOW, KV_W), F32),
                   jax.ShapeDtypeStruct((dec_batch, GK_W, GLA_DV), F32)],
        compiler_params=pltpu.CompilerParams(
            dimension_semantics=("arbitrary",), vmem_limit_bytes=VMEM_LIMIT_BYTES),
        name="sample_layer",
    )(sinks1, xs, cos_s, sin_s, gmat_s, masks_s, bias_win, bias_new, ck, cv, st, *weights)

    head_shape = (ATT_KV_HEADS, HEAD_DIM)
    state_shape = (GLA_HEADS, GLA_DK, GLA_DV)
    return (yp,
            ys.reshape(dec_batch, dec_seq, d_model),
            jnp.transpose(kp, (0, 2, 1)).reshape((1, batch, WINDOW) + head_shape),
            jnp.transpose(vp, (0, 2, 1)).reshape((1, batch, WINDOW) + head_shape),
            sp.reshape((1, batch) + state_shape),
            jnp.transpose(ks, (0, 2, 1)).reshape((1, dec_batch, WINDOW) + head_shape),
            jnp.transpose(vs, (0, 2, 1)).reshape((1, dec_batch, WINDOW) + head_shape),
            ss.reshape((1, dec_batch) + state_shape))
```

```python
import functools

import numpy as np
import jax
import jax.numpy as jnp
from jax import lax
from jax.experimental import pallas as pl
from jax.experimental.pallas import tpu as pltpu

F32 = jnp.float32
BF16 = jnp.bfloat16

HEAD_DIM = 64
ATT_HEADS = 8
ATT_KV_HEADS = 2
WINDOW = 128
PAST_LEN = 16384
ROPE_THETA = 10000.0
GLA_HEADS = 4
GLA_DK = 64
GLA_DV = 128
GLA_RANK = 16
GLA_GATE_TEMP = 16.0
EPS = 1e-6
ATT_W = ATT_HEADS * HEAD_DIM
KV_W = ATT_KV_HEADS * HEAD_DIM
GK_W = GLA_HEADS * GLA_DK
GV_W = GLA_HEADS * GLA_DV
COL_Q = 0
COL_K = COL_Q + ATT_W
COL_V = COL_K + KV_W
COL_GQ = COL_V + KV_W
COL_GK = COL_GQ + GK_W
COL_GV = COL_GK + GK_W
COL_GR = COL_GV + GV_W
MAIN_W = COL_GR + GV_W
QK_SCALE = HEAD_DIM ** -0.5
GLA_SCALE = GLA_DK ** -0.5

CHUNK = 128
LANES = 128
PROMPT_TILE = 256
SAMPLE_SEQS = 16
FF_CHUNK = 1024
PROMPT_LEVELS = (1, 2, 4, 8, 16, 32, 64)
BOUNDED_LOG_DECAY = 30.0
VMEM_LIMIT_BYTES = 56 * 1024 * 1024


def _dot(a, b):
    return jnp.dot(a, b, preferred_element_type=F32)


def _dot_nt(a, b):
    return lax.dot_general(a, b, (((1,), (1,)), ((), ())), preferred_element_type=F32)


def _rms(x, g):
    ms = jnp.mean(x * x, axis=-1, keepdims=True)
    return x * lax.rsqrt(ms + EPS) * g


def _rope(x, cos, sin_a, sin_b):
    return x * cos + pltpu.roll(x, 96, 1) * sin_a + pltpu.roll(x, 32, 1) * sin_b


def _rope_lanes(cos, sin):
    reps = LANES // cos.shape[0]
    cos_l = jnp.transpose(jnp.concatenate([cos] * reps, axis=0))
    sin_l = jnp.transpose(jnp.concatenate([sin] * reps, axis=0))
    lane = lax.broadcasted_iota(jnp.int32, cos_l.shape, 1)
    first_half = (lane & (HEAD_DIM // 2)) == 0
    return cos_l, jnp.where(first_half, -sin_l, 0.0), jnp.where(first_half, 0.0, sin_l)


def _split3(x):
    hi = x.astype(BF16)
    r1 = x - hi.astype(F32)
    mid = r1.astype(BF16)
    lo = (r1 - mid.astype(F32)).astype(BF16)
    return hi, mid, lo


def _project(x, g_pre, w_in_ref):
    h = _rms(x, g_pre).astype(BF16)
    return _dot(h, w_in_ref[:, 0:MAIN_W]), _dot(h, w_in_ref[:, MAIN_W:])


def _split_projection(proj, glr, cos_ref, sin_ref, w_gk2_ref, b_gk_ref):
    cos, sin_a, sin_b = _rope_lanes(cos_ref[...], sin_ref[...])
    z = _dot(glr.astype(BF16), w_gk2_ref[...]) + b_gk_ref[...]
    log_a = (jnp.minimum(z, 0.0) - jnp.log1p(jnp.exp(-jnp.abs(z)))) * (1.0 / GLA_GATE_TEMP)
    q = jnp.concatenate(
        [_rope(proj[:, COL_Q + LANES * j:COL_Q + LANES * (j + 1)], cos, sin_a, sin_b) * QK_SCALE
         for j in range(ATT_W // LANES)], axis=1)
    k = _rope(proj[:, COL_K:COL_V], cos, sin_a, sin_b)
    v = proj[:, COL_V:COL_GQ]
    gq = proj[:, COL_GQ:COL_GK] * GLA_SCALE
    gk = proj[:, COL_GK:COL_GV]
    gv = proj[:, COL_GV:COL_GR]
    gr = proj[:, COL_GR:MAIN_W]
    return q, k, v, gq, gk, gv, gr, log_a


def _output_stage(x_prev, mix_prev, g_post_ref, g_fpre_ref, g_fpost_ref, w_out_ref, w_up_ref, w_down_ref, write):
    mix = _dot(mix_prev[...], w_out_ref[...])
    yield
    x1 = x_prev[...] + _rms(mix, g_post_ref[...])
    h2 = _rms(x1, g_fpre_ref[...]).astype(BF16)
    f = None
    for i in range(w_up_ref.shape[1] // FF_CHUNK):
        u = jnp.maximum(_dot(h2, w_up_ref[:, FF_CHUNK * i:FF_CHUNK * (i + 1)]), 0.0)
        yield
        part = _dot((u * u).astype(BF16), w_down_ref[FF_CHUNK * i:FF_CHUNK * (i + 1), :])
        f = part if f is None else f + part
        yield
    write(x1 + _rms(f, g_fpost_ref[...]))


def _lo_hi_forms(a):
    lo = lax.broadcasted_iota(jnp.int32, a.shape, 1) < HEAD_DIM
    r = pltpu.roll(a, HEAD_DIM, 1)
    zero = jnp.zeros_like(a)
    return ((jnp.where(lo, a, zero), jnp.where(lo, zero, r)),
            (jnp.where(lo, r, zero), jnp.where(lo, zero, a)))


def _gla_exponents(log_a, gmat):
    hi, mid, lo = _split3(log_a)
    return _dot(gmat, hi) + _dot(gmat, mid) + _dot(gmat, lo)


def _head_lanes(h, rows):
    lane = lax.broadcasted_iota(jnp.int32, (rows, LANES), 1)
    e = h % 2
    return (lane >= GLA_DK * e) & (lane < GLA_DK * (e + 1))


def _gla_intra(gq, gk, log_a, gmat_lv, levels, masks_ref):
    expo = _gla_exponents(log_a, gmat_lv)
    row = lax.broadcasted_iota(jnp.int32, gq.shape, 0)
    terms = [(gq.astype(BF16), gk.astype(BF16), 0)]
    for li, m in enumerate(levels):
        ex = jnp.exp(expo[CHUNK * li:CHUNK * (li + 1)])
        z = (jnp.where((row & m) != 0, gq, gk) * ex).astype(BF16)
        terms.append((z, z, 1 + li))
    scores = []
    for h in range(GLA_HEADS):
        p = h // 2
        in_head = _head_lanes(h, gq.shape[0])
        acc = None
        for zq, zk, mi in terms:
            zq_p = zq[:, LANES * p:LANES * (p + 1)]
            zk_p = zk[:, LANES * p:LANES * (p + 1)]
            zk_h = jnp.where(in_head, zk_p, jnp.zeros_like(zk_p))
            t = _dot_nt(zq_p, zk_h) * masks_ref[mi]
            acc = t if acc is None else acc + t
        scores.append(acc)
    return jnp.concatenate(scores, axis=1)


def _gla_intra_bounded(q_dec, gk, b, causal):
    k_grow_t = jnp.transpose(gk * jnp.exp(-b)).astype(BF16)
    scores = []
    for h in range(GLA_HEADS):
        p = h // 2
        qd_p = q_dec[:, LANES * p:LANES * (p + 1)]
        qd_h = jnp.where(_head_lanes(h, q_dec.shape[0]), qd_p, jnp.zeros_like(qd_p))
        scores.append(_dot(qd_h, k_grow_t[LANES * p:LANES * (p + 1), :]) * causal)
    return jnp.concatenate(scores, axis=1)


def _gla_out_gate(o_h, gr_h, g_gla):
    on = _rms(o_h, g_gla)
    return on * (gr_h * (1.0 / (1.0 + jnp.exp(-gr_h))))


def _prompt_kernel(sinks_ref, x_ref, xo_ref, cos_ref, sin_ref, bias_ref, gmat_ref, masks_ref,
                   g_pre_ref, g_post_ref, g_fpre_ref, g_fpost_ref, g_gla_ref, b_gk_ref,
                   w_in_ref, w_gk2_ref, w_out_ref, w_up_ref, w_down_ref,
                   y_ref, kwin_ref, vwin_ref, sout_ref,
                   q_scr, kbd_scr, vbd_scr, gq_scr, gk_scr, gv_scr, gr_scr, la_scr, s_scr,
                   mix_scr, bounded_scr, *, tiles_per_seq, n_tiles):
    j = pl.program_id(0)
    t_step = lax.rem(jnp.clip(j - 1, 0, n_tiles - 1), tiles_per_seq)
    slot = lax.rem(j, 2)
    tile = x_ref.shape[1]
    n_chunks = tile // CHUNK

    @pl.when(j == 0)
    def _():
        for ref in (s_scr, q_scr, kbd_scr, vbd_scr, gq_scr, gk_scr, gv_scr, gr_scr, la_scr,
                    mix_scr):
            ref[...] = jnp.zeros_like(ref)
        bounded_scr[0] = 1

    branch = jnp.where(j == 0, 3, jnp.where(j == n_tiles + 1, 2, jnp.where(bounded_scr[0] == 1, 0, 1)))

    lo_lane = lax.broadcasted_iota(jnp.int32, (CHUNK, LANES), 1) < HEAD_DIM
    g_gla = g_gla_ref[...]

    def input_stage(results):
        x = x_ref[0]
        h = _rms(x, g_pre_ref[...]).astype(BF16)
        proj_att = _dot(h, w_in_ref[:, COL_Q:COL_GQ])
        glr = _dot(h, w_in_ref[:, MAIN_W:])
        yield
        proj = jnp.concatenate([proj_att, _dot(h, w_in_ref[:, COL_GQ:MAIN_W])], axis=1)
        q, k, v, gq, gk, gv, gr, log_a = _split_projection(proj, glr, cos_ref, sin_ref, w_gk2_ref, b_gk_ref)
        chunk_decay = jnp.concatenate(
            [jnp.sum(log_a[CHUNK * c:CHUNK * (c + 1)], axis=0, keepdims=True) for c in range(n_chunks)], axis=0)
        results.update(
            q=q.astype(BF16), k=k, v=v, k_forms=_lo_hi_forms(k), v_forms=_lo_hi_forms(v),
            gq=gq, gk=gk, gv=gv.astype(BF16), gr=gr, log_a=log_a,
            bounded=jnp.min(chunk_decay) >= -BOUNDED_LOG_DECAY)
        yield

    def hand_over(results):
        kbd_scr[:, 0:CHUNK, :] = kbd_scr[:, tile:tile + CHUNK, :]
        vbd_scr[:, 0:CHUNK, :] = vbd_scr[:, tile:tile + CHUNK, :]
        for g in range(ATT_KV_HEADS):
            for e in range(2):
                kbd_scr[2 * g + e, CHUNK:CHUNK + tile, :] = results["k_forms"][g][e].astype(BF16)
                vbd_scr[2 * g + e, CHUNK:CHUNK + tile, :] = results["v_forms"][g][e].astype(BF16)
        q_scr[...] = results["q"]
        gq_scr[...] = results["gq"]
        gk_scr[...] = results["gk"]
        gv_scr[...] = results["gv"]
        gr_scr[...] = results["gr"]
        la_scr[...] = results["log_a"]
        bounded_scr[0] = jnp.where(results["bounded"], 1, 0)
        kwin_ref[0] = jnp.transpose(results["k"][tile - WINDOW:tile, :])
        vwin_ref[0] = jnp.transpose(results["v"][tile - WINDOW:tile, :])

    def attention_chunk(c):
        rows = slice(CHUNK * c, CHUNK * (c + 1))
        rows2 = slice(CHUNK * c, CHUNK * (c + 2))
        bias = bias_ref[jnp.where(t_step == 0, 1, 0)] if c == 0 else bias_ref[0]
        scores = []
        for g in range(ATT_KV_HEADS):
            lhs = jnp.concatenate([q_scr[rows, 2 * LANES * g:2 * LANES * g + LANES],
                                   q_scr[rows, 2 * LANES * g + LANES:2 * LANES * (g + 1)]], axis=0)
            kcat = jnp.concatenate([kbd_scr[2 * g, rows2, :], kbd_scr[2 * g + 1, rows2, :]], axis=0)
            scores.append(_dot_nt(lhs, kcat) + bias)
        yield
        for g in range(ATT_KV_HEADS):
            s = scores[g]
            p_blocks = []
            inv = []
            for blk in range(2):
                p_row = []
                inv_row = []
                for e in range(2):
                    sub = s[CHUNK * blk:CHUNK * (blk + 1), 2 * CHUNK * e:2 * CHUNK * (e + 1)]
                    sink = sinks_ref[4 * g + 2 * blk + e]
                    m = jnp.maximum(jnp.max(sub, axis=-1, keepdims=True), sink)
                    p = jnp.exp(sub - m)
                    den = jnp.sum(p, axis=-1, keepdims=True) + jnp.exp(sink - m)
                    p_row.append(p.astype(BF16))
                    inv_row.append(1.0 / den)
                p_blocks.append(jnp.concatenate(p_row, axis=1))
                inv.append(inv_row)
            p_all = jnp.concatenate(p_blocks, axis=0)
            vcat = jnp.concatenate([vbd_scr[2 * g, rows2, :], vbd_scr[2 * g + 1, rows2, :]], axis=0)
            o = _dot(p_all, vcat)
            for blk in range(2):
                scale = jnp.where(lo_lane, inv[blk][0], inv[blk][1])
                c0 = 2 * LANES * g + LANES * blk
                mix_scr[slot, rows, c0:c0 + LANES] = (o[CHUNK * blk:CHUNK * (blk + 1)] * scale).astype(BF16)
        yield

    def gla_chunk(c, state, bounded_decay):
        rows = slice(CHUNK * c, CHUNK * (c + 1))
        gq_c = gq_scr[rows, :]
        gk_c = gk_scr[rows, :]
        la_c = la_scr[rows, :]
        b = _gla_exponents(la_c, gmat_ref[0:CHUNK, :])
        b_last = jnp.broadcast_to(b[CHUNK - 1:CHUNK, :], b.shape)
        q_dec_f = gq_c * jnp.exp(b)
        q_dec = q_dec_f.astype(BF16)
        k_dec_t = jnp.transpose(gk_c * jnp.exp(b_last - b)).astype(BF16)
        decay_t = jnp.exp(jnp.transpose(b_last))
        yield
        if bounded_decay:
            scores = _gla_intra_bounded(q_dec_f, gk_c, b, masks_ref[len(PROMPT_LEVELS) + 1])
        else:
            scores = _gla_intra(gq_c, gk_c, la_c, gmat_ref[CHUNK:, :], PROMPT_LEVELS, masks_ref)
        yield
        for h in range(GLA_HEADS):
            p_ = h // 2
            v_h = gv_scr[rows, GLA_DV * h:GLA_DV * (h + 1)]
            qd_p = q_dec[:, LANES * p_:LANES * (p_ + 1)]
            qd_h = jnp.where(_head_lanes(h, CHUNK), qd_p, jnp.zeros_like(qd_p))
            s0_pair = jnp.concatenate([state[2 * p_], state[2 * p_ + 1]], axis=0).astype(BF16)
            o_h = _dot(scores[:, CHUNK * h:CHUNK * (h + 1)].astype(BF16), v_h) + _dot(qd_h, s0_pair)
            out_h = _gla_out_gate(o_h, gr_scr[rows, GLA_DV * h:GLA_DV * (h + 1)], g_gla)
            mix_scr[slot, rows, ATT_W + GLA_DV * h:ATT_W + GLA_DV * (h + 1)] = out_h.astype(BF16)
        new_state = [decay_t[GLA_DK * h:GLA_DK * (h + 1), :] * state[h]
                     + _dot(k_dec_t[GLA_DK * h:GLA_DK * (h + 1), :],
                            gv_scr[rows, GLA_DV * h:GLA_DV * (h + 1)])
                     for h in range(GLA_HEADS)]
        state[:] = new_state
        yield

    def write_y(y):
        y_ref[0] = y

    def output_stage():
        return _output_stage(xo_ref.at[0], mix_scr.at[1 - slot], g_post_ref, g_fpre_ref, g_fpost_ref,
                             w_out_ref, w_up_ref, w_down_ref, write_y)

    def run_step(bounded_decay):
        keep = jnp.where(t_step == 0, 0.0, 1.0)
        state = [s_scr[GLA_DK * h:GLA_DK * (h + 1), :] * keep for h in range(GLA_HEADS)]
        results = {}
        out = output_stage()
        inp = input_stage(results)
        att = [attention_chunk(c) for c in range(n_chunks)]
        gla = [gla_chunk(c, state, bounded_decay) for c in range(n_chunks)]
        next(out)
        for c in range(n_chunks):
            next(att[c])
            next(gla[c])
        next(inp)
        spread = [(att[c], gla[c]) for c in range(n_chunks)]
        spread += [(gla[c],) for c in range(n_chunks)]
        spread.append((inp,))
        for stage in spread:
            next(out)
            for gen in stage:
                next(gen)
        for _ in out:
            pass
        s_all = jnp.concatenate(state, axis=0)
        s_scr[...] = s_all
        sout_ref[0] = s_all
        hand_over(results)

    def finish_last_tile():
        for _ in output_stage():
            pass

    def start_first_tile():
        results = {}
        for _ in input_stage(results):
            pass
        hand_over(results)

    lax.switch(branch, [lambda: run_step(True), lambda: run_step(False), finish_last_tile, start_first_tile])


def _sample_kernel(sinks_ref, x_ref, cos_ref, sin_ref, gmat_ref, masks_ref, bias_win_ref, bias_new_ref,
                   ck_ref, cv_ref, st_ref,
                   g_pre_ref, g_post_ref, g_fpre_ref, g_fpost_ref, g_gla_ref, b_gk_ref,
                   w_in_ref, w_gk2_ref, w_out_ref, w_up_ref, w_down_ref,
                   y_ref, kwin_ref, vwin_ref, sout_ref,
                   *, dec_seq, levels):
    results = {}
    for _ in _sample_tile_stages(sinks_ref, x_ref, cos_ref, sin_ref, gmat_ref, masks_ref, bias_win_ref,
                                 bias_new_ref, ck_ref, cv_ref, st_ref, g_pre_ref, g_gla_ref, b_gk_ref,
                                 w_in_ref, w_gk2_ref, kwin_ref, vwin_ref, sout_ref, results,
                                 dec_seq=dec_seq, levels=levels):
        pass

    def write_y(y):
        y_ref[...] = y

    for _ in _output_stage(results["x"], results["mix"], g_post_ref, g_fpre_ref, g_fpost_ref,
                           w_out_ref, w_up_ref, w_down_ref, write_y):
        pass


def _sample_tile_stages(sinks_ref, x_ref, cos_ref, sin_ref, gmat_ref, masks_ref, bias_win_ref, bias_new_ref,
                        ck_ref, cv_ref, st_ref, g_pre_ref, g_gla_ref, b_gk_ref, w_in_ref, w_gk2_ref,
                        kwin_ref, vwin_ref, sout_ref, results, *, dec_seq, levels):
    n_seqs = ck_ref.shape[0]
    x = x_ref[...]
    proj, glr = _project(x, g_pre_ref[...], w_in_ref)
    yield
    q, k, v, gq, gk, gv, gr, log_a = _split_projection(proj, glr, cos_ref, sin_ref, w_gk2_ref, b_gk_ref)

    is_new = lax.broadcasted_iota(jnp.int32, (KV_W, WINDOW), 1) >= WINDOW - dec_seq
    for new, old_ref, out_ref in ((jnp.transpose(k), ck_ref, kwin_ref), (jnp.transpose(v), cv_ref, vwin_ref)):
        for s in range(n_seqs):
            shifted = pltpu.roll(old_ref[s], WINDOW - dec_seq, 1)
            tail = pltpu.roll(new, (WINDOW - dec_seq * (s + 1)) % CHUNK, 1)
            out_ref[s] = jnp.where(is_new, tail, shifted)

    def seq_rows(a, s):
        return a[dec_seq * s:dec_seq * (s + 1)]

    def gather_seq(per_head, s):
        return jnp.concatenate([seq_rows(a, s) for a in per_head], axis=0)

    def scatter_heads(per_seq, h):
        return jnp.concatenate([seq_rows(a, h) for a in per_seq], axis=0)

    lo_lane = lax.broadcasted_iota(jnp.int32, (CHUNK, LANES), 1) < HEAD_DIM
    q_heads = []
    for j in range(ATT_W // LANES):
        blk = q[:, LANES * j:LANES * (j + 1)]
        rolled = pltpu.roll(blk, HEAD_DIM, 1)
        g = (2 * j) // (ATT_HEADS // ATT_KV_HEADS)
        keep = lo_lane if g == 0 else jnp.logical_not(lo_lane)
        for e in range(2):
            q_heads.append(jnp.where(keep, blk if e == g else rolled, 0.0))
    s_win = [_dot(gather_seq(q_heads, s), ck_ref[s]) for s in range(n_seqs)]
    k_b = k.astype(BF16)
    v_b = v.astype(BF16)
    s_new = [_dot_nt(q_heads[h].astype(BF16), k_b) for h in range(ATT_HEADS)]
    b = _gla_exponents(log_a, gmat_ref[0:CHUNK, :])
    yield
    bias_win = bias_win_ref[...]
    bias_new = bias_new_ref[...]
    p_win = []
    o_new = []
    inv_den = []
    for h in range(ATT_HEADS):
        sw = scatter_heads(s_win, h) + bias_win
        sn = s_new[h] + bias_new
        sink = sinks_ref[h]
        m = jnp.maximum(jnp.maximum(jnp.max(sw, axis=-1, keepdims=True),
                                    jnp.max(sn, axis=-1, keepdims=True)), sink)
        pw = jnp.exp(sw - m)
        pn = jnp.exp(sn - m)
        den = (jnp.sum(pw, axis=-1, keepdims=True) + jnp.sum(pn, axis=-1, keepdims=True)
               + jnp.exp(sink - m))
        p_win.append(pw)
        o_new.append(_dot(pn.astype(BF16), v_b))
        inv_den.append(1.0 / den)
    o_win = [_dot_nt(gather_seq(p_win, s), cv_ref[s]) for s in range(n_seqs)]
    o_heads = [(scatter_heads(o_win, h) + o_new[h]) * inv_den[h] for h in range(ATT_HEADS)]
    mix_cols = []
    for j in range(ATT_W // LANES):
        g = (2 * j) // (ATT_HEADS // ATT_KV_HEADS)
        first = o_heads[2 * j] if g == 0 else pltpu.roll(o_heads[2 * j], HEAD_DIM, 1)
        second = o_heads[2 * j + 1] if g == 1 else pltpu.roll(o_heads[2 * j + 1], HEAD_DIM, 1)
        mix_cols.append(jnp.where(lo_lane, first, second).astype(BF16))

    b3 = b.reshape(n_seqs, dec_seq, GK_W)
    b_last = jnp.broadcast_to(b3[:, dec_seq - 1:dec_seq, :], b3.shape).reshape(CHUNK, GK_W)
    q_dec = gq * jnp.exp(b)
    k_dec_t = jnp.transpose(gk * jnp.exp(b_last - b))
    decay_t = jnp.exp(jnp.transpose(b_last))
    scores = _gla_intra(gq, gk, log_a, gmat_ref[CHUNK:, :], levels, masks_ref)
    yield
    lane_k = lax.broadcasted_iota(jnp.int32, (dec_seq, GK_W), 1)
    lane_t = lax.broadcasted_iota(jnp.int32, (GLA_DK, LANES), 1)
    o_inter = []
    for s in range(n_seqs):
        qd = seq_rows(q_dec, s)
        lhs = jnp.concatenate(
            [jnp.where((lane_k >= GLA_DK * h) & (lane_k < GLA_DK * (h + 1)), qd, 0.0)
             for h in range(GLA_HEADS)], axis=0)
        o_inter.append(_dot(lhs, st_ref[s]))
    g_gla = g_gla_ref[...]
    for h in range(GLA_HEADS):
        cs = slice(GLA_DV * h, GLA_DV * (h + 1))
        hs = slice(GLA_DK * h, GLA_DK * (h + 1))
        v_h = gv[:, cs].astype(BF16)
        o_h = _dot(scores[:, CHUNK * h:CHUNK * (h + 1)].astype(BF16), v_h) + scatter_heads(o_inter, h)
        mix_cols.append(_gla_out_gate(o_h, gr[:, cs], g_gla).astype(BF16))
        kd_h = k_dec_t[hs, :]
        lhs = jnp.concatenate(
            [jnp.where((lane_t >= dec_seq * s) & (lane_t < dec_seq * (s + 1)), kd_h, 0.0).astype(BF16)
             for s in range(n_seqs)], axis=0)
        upd = _dot(lhs, v_h)
        for s in range(n_seqs):
            a_col = jnp.broadcast_to(decay_t[hs, dec_seq * s:dec_seq * s + 1], (GLA_DK, GLA_DV))
            sout_ref[s, hs, :] = a_col * st_ref[s, hs, :] + upd[GLA_DK * s:GLA_DK * (s + 1)]
    results.update(x=x, mix=jnp.concatenate(mix_cols, axis=1))


def _gla_constants(seg, levels):
    t = np.arange(CHUNK)[:, None]
    u = np.arange(CHUNK)[None, :]
    same = (t // seg) == (u // seg)
    blocks = [same & (u <= t)]
    masks = [np.eye(CHUNK, dtype=bool)]
    for m in levels:
        ref = (t // (2 * m)) * (2 * m) + m - 1
        second = (t & m) != 0
        blocks.append(np.where(second, (u > ref) & (u <= t), (u > t) & (u <= ref)))
        masks.append(((t // (2 * m)) == (u // (2 * m))) & second & ((u & m) == 0))
    masks.append(blocks[0])
    gmat = jnp.asarray(np.concatenate(blocks, axis=0).astype(np.float32), dtype=BF16)
    return gmat, jnp.asarray(np.stack(masks).astype(np.float32))


def _attention_bias():
    t = (np.arange(2 * CHUNK) % CHUNK)[:, None]
    j = (np.arange(4 * CHUNK) % (2 * CHUNK))[None, :]
    band = (j >= t + 1) & (j <= t + WINDOW)
    first = band & (j >= CHUNK)
    neg = np.float32(-np.inf)
    return jnp.asarray(np.stack([np.where(band, np.float32(0), neg), np.where(first, np.float32(0), neg)]))


def _sample_attention_bias(dec_seq):
    r = np.arange(CHUNK)[:, None]
    c = np.arange(CHUNK)[None, :]
    t = r % dec_seq
    win = c > t
    new = ((r // dec_seq) == (c // dec_seq)) & ((c % dec_seq) <= t)
    neg = np.float32(-np.inf)
    zero = np.float32(0)
    return jnp.asarray(np.where(win, zero, neg)), jnp.asarray(np.where(new, zero, neg))


def _rope_tables(pos):
    half = HEAD_DIM // 2
    inv = ROPE_THETA ** (-jnp.arange(half, dtype=F32) / half)
    ang = inv[:, None] * pos.astype(F32)[None, :]
    return jnp.cos(ang), jnp.sin(ang)


def _const_spec(shape):
    zeros = (0,) * len(shape)
    return pl.BlockSpec(shape, lambda *_: zeros, pipeline_mode=pl.Buffered(1))


def kernel(x_prompt, x_sample, cache_k, cache_v, state_gla, w_in, w_gk2, b_gk, g_gla, sinks, w_out, g_mix_pre, g_mix_post, g_ffn_pre, g_ffn_post, w_up, w_down):
    depth = w_in.shape[0]
    assert depth == 1, "single trunk layer"
    batch, seq, d_model = x_prompt.shape
    dec_batch, dec_seq, _ = x_sample.shape
    d_ff = w_up.shape[2]
    assert seq % PROMPT_TILE == 0 and dec_batch % SAMPLE_SEQS == 0
    assert SAMPLE_SEQS * dec_seq == CHUNK and CHUNK % dec_seq == 0 and d_ff % FF_CHUNK == 0
    assert w_in.shape[2] == MAIN_W + GLA_RANK

    w_in_b = w_in[0].astype(BF16)
    w_gk2_b = w_gk2[0].astype(BF16)
    w_out_b = w_out[0].astype(BF16)
    w_up_b = w_up[0].astype(BF16)
    w_down_b = w_down[0].astype(BF16)
    g_pre = g_mix_pre[0][None, :]
    g_post = g_mix_post[0][None, :]
    g_fpre = g_ffn_pre[0][None, :]
    g_fpost = g_ffn_post[0][None, :]
    g_gla2 = g_gla[0][None, :]
    b_gk2 = b_gk[0][None, :]
    sinks1 = sinks[0]

    weights = (g_pre, g_post, g_fpre, g_fpost, g_gla2, b_gk2, w_in_b, w_gk2_b, w_out_b, w_up_b, w_down_b)
    weight_specs = [_const_spec(w.shape) for w in weights]
    smem_spec = pl.BlockSpec(memory_space=pltpu.SMEM)

    cos_p, sin_p = _rope_tables(jnp.arange(seq, dtype=jnp.int32))
    gmat_p, masks_p = _gla_constants(CHUNK, PROMPT_LEVELS)
    bias = _attention_bias()
    n_t = seq // PROMPT_TILE
    tile = PROMPT_TILE
    n_tiles = batch * n_t

    def cur_tile(j):
        return jnp.minimum(j, n_tiles - 1)

    def mix_tile(j):
        return jnp.clip(j - 1, 0, n_tiles - 1)

    def done_tile(j):
        return jnp.clip(j - 2, 0, n_tiles - 1)

    tab_spec = pl.BlockSpec((HEAD_DIM // 2, tile), lambda j: (0, cur_tile(j) % n_t))
    seq_out = lambda j: (cur_tile(j) // n_t, 0, 0)
    yp, kp, vp, sp = pl.pallas_call(
        functools.partial(_prompt_kernel, tiles_per_seq=n_t, n_tiles=n_tiles),
        grid=(n_tiles + 2,),
        in_specs=[smem_spec,
                  pl.BlockSpec((1, tile, d_model), lambda j: (cur_tile(j) // n_t, cur_tile(j) % n_t, 0)),
                  pl.BlockSpec((1, tile, d_model), lambda j: (done_tile(j) // n_t, done_tile(j) % n_t, 0)),
                  tab_spec, tab_spec,
                  _const_spec(bias.shape), _const_spec(gmat_p.shape), _const_spec(masks_p.shape)] + weight_specs,
        out_specs=[pl.BlockSpec((1, tile, d_model), lambda j: (done_tile(j) // n_t, done_tile(j) % n_t, 0)),
                   pl.BlockSpec((1, WINDOW, KV_W), seq_out),
                   pl.BlockSpec((1, WINDOW, KV_W), seq_out),
                   pl.BlockSpec((1, GK_W, GLA_DV), lambda j: (mix_tile(j) // n_t, 0, 0))],
        out_shape=[jax.ShapeDtypeStruct((batch, seq, d_model), F32),
                   jax.ShapeDtypeStruct((batch, WINDOW, KV_W), F32),
                   jax.ShapeDtypeStruct((batch, WINDOW, KV_W), F32),
                   jax.ShapeDtypeStruct((batch, GK_W, GLA_DV), F32)],
        scratch_shapes=[pltpu.VMEM((tile, ATT_W), BF16),
                        pltpu.VMEM((2 * ATT_KV_HEADS, CHUNK + tile, LANES), BF16),
                        pltpu.VMEM((2 * ATT_KV_HEADS, CHUNK + tile, LANES), BF16),
                        pltpu.VMEM((tile, GK_W), F32),
                        pltpu.VMEM((tile, GK_W), F32),
                        pltpu.VMEM((tile, GV_W), BF16),
                        pltpu.VMEM((tile, GV_W), F32),
                        pltpu.VMEM((tile, GK_W), F32),
                        pltpu.VMEM((GK_W, GLA_DV), F32),
                        pltpu.VMEM((2, tile, d_model), BF16),
                        pltpu.SMEM((1,), jnp.int32)],
        compiler_params=pltpu.CompilerParams(
            dimension_semantics=("arbitrary",), vmem_limit_bytes=VMEM_LIMIT_BYTES),
        name="prompt_layer",
    )(sinks1, x_prompt, x_prompt, cos_p, sin_p, bias, gmat_p, masks_p, *weights)

    levels_s = tuple(m for m in PROMPT_LEVELS if m < dec_seq)
    pos_s = PAST_LEN + jnp.arange(dec_seq, dtype=jnp.int32)
    cos_s, sin_s = (jnp.tile(a, (1, SAMPLE_SEQS)) for a in _rope_tables(pos_s))
    gmat_s, masks_s = _gla_constants(dec_seq, levels_s)
    bias_win, bias_new = _sample_attention_bias(dec_seq)
    rows = SAMPLE_SEQS * dec_seq
    n_g = dec_batch // SAMPLE_SEQS
    xs = x_sample.reshape(dec_batch * dec_seq, d_model)
    ck = jnp.transpose(cache_k[0].reshape(dec_batch, WINDOW, KV_W), (0, 2, 1))
    cv = jnp.transpose(cache_v[0].reshape(dec_batch, WINDOW, KV_W), (0, 2, 1))
    st = state_gla[0].reshape(dec_batch, GK_W, GLA_DV)
    seq_spec = pl.BlockSpec((SAMPLE_SEQS, WINDOW, KV_W), lambda i: (i, 0, 0))
    st_spec = pl.BlockSpec((SAMPLE_SEQS, GK_W, GLA_DV), lambda i: (i, 0, 0))
    row_spec = pl.BlockSpec((rows, d_model), lambda i: (i, 0))
    ys, ks, vs, ss = pl.pallas_call(
        functools.partial(_sample_kernel, dec_seq=dec_seq, levels=levels_s),
        grid=(n_g,),
        in_specs=[smem_spec, row_spec,
                  _const_spec(cos_s.shape), _const_spec(sin_s.shape),
                  _const_spec(gmat_s.shape), _const_spec(masks_s.shape),
                  _const_spec(bias_win.shape), _const_spec(bias_new.shape),
                  seq_spec, seq_spec, st_spec] + weight_specs,
        out_specs=[row_spec, seq_spec, seq_spec, st_spec],
        out_shape=[jax.ShapeDtypeStruct((dec_batch * dec_seq, d_model), F32),
                   jax.ShapeDtypeStruct((dec_batch, WINDOW, KV_W), F32),
                   jax.ShapeDtypeStruct((dec_batch, WINDOW, KV_W), F32),
                   jax.ShapeDtypeStruct((dec_batch, GK_W, GLA_DV), F32)],
        compiler_params=pltpu.CompilerParams(
            dimension_semantics=("arbitrary",), vmem_limit_bytes=VMEM_LIMIT_BYTES),
        name="sample_layer",
    )(sinks1, xs, cos_s, sin_s, gmat_s, masks_s, bias_win, bias_new, ck, cv, st, *weights)

    head_shape = (ATT_KV_HEADS, HEAD_DIM)
    state_shape = (GLA_HEADS, GLA_DK, GLA_DV)
    return (yp,
            ys.reshape(dec_batch, dec_seq, d_model),
            jnp.transpose(kp, (0, 2, 1)).reshape((1, batch, WINDOW) + head_shape),
            jnp.transpose(vp, (0, 2, 1)).reshape((1, batch, WINDOW) + head_shape),
            sp.reshape((1, batch) + state_shape),
            jnp.transpose(ks, (0, 2, 1)).reshape((1, dec_batch, WINDOW) + head_shape),
            jnp.transpose(vs, (0, 2, 1)).reshape((1, dec_batch, WINDOW) + head_shape),
            ss.reshape((1, dec_batch) + state_shape))
```

```python
import functools

import numpy as np
import jax
import jax.numpy as jnp
from jax import lax
from jax.experimental import pallas as pl
from jax.experimental.pallas import tpu as pltpu

F32 = jnp.float32
BF16 = jnp.bfloat16

HEAD_DIM = 64
ATT_HEADS = 8
ATT_KV_HEADS = 2
WINDOW = 128
PAST_LEN = 16384
ROPE_THETA = 10000.0
GLA_HEADS = 4
GLA_DK = 64
GLA_DV = 128
GLA_RANK = 16
GLA_GATE_TEMP = 16.0
EPS = 1e-6
ATT_W = ATT_HEADS * HEAD_DIM
KV_W = ATT_KV_HEADS * HEAD_DIM
GK_W = GLA_HEADS * GLA_DK
GV_W = GLA_HEADS * GLA_DV
COL_Q = 0
COL_K = COL_Q + ATT_W
COL_V = COL_K + KV_W
COL_GQ = COL_V + KV_W
COL_GK = COL_GQ + GK_W
COL_GV = COL_GK + GK_W
COL_GR = COL_GV + GV_W
MAIN_W = COL_GR + GV_W
QK_SCALE = HEAD_DIM ** -0.5
GLA_SCALE = GLA_DK ** -0.5

CHUNK = 128
LANES = 128
PROMPT_TILE = 256
SAMPLE_SEQS = 16
FF_CHUNK = 1024
PROMPT_LEVELS = (1, 2, 4, 8, 16, 32, 64)
BOUNDED_LOG_DECAY = 30.0
VMEM_LIMIT_BYTES = 56 * 1024 * 1024


def _dot(a, b):
    return jnp.dot(a, b, preferred_element_type=F32)


def _dot_nt(a, b):
    return lax.dot_general(a, b, (((1,), (1,)), ((), ())), preferred_element_type=F32)


def _rms(x, g):
    ms = jnp.mean(x * x, axis=-1, keepdims=True)
    return x * lax.rsqrt(ms + EPS) * g


def _rope(x, cos, sin_a, sin_b):
    return x * cos + pltpu.roll(x, 96, 1) * sin_a + pltpu.roll(x, 32, 1) * sin_b


def _rope_lanes(cos, sin):
    reps = LANES // cos.shape[0]
    cos_l = jnp.transpose(jnp.concatenate([cos] * reps, axis=0))
    sin_l = jnp.transpose(jnp.concatenate([sin] * reps, axis=0))
    lane = lax.broadcasted_iota(jnp.int32, cos_l.shape, 1)
    first_half = (lane & (HEAD_DIM // 2)) == 0
    return cos_l, jnp.where(first_half, -sin_l, 0.0), jnp.where(first_half, 0.0, sin_l)


def _split3(x):
    hi = x.astype(BF16)
    r1 = x - hi.astype(F32)
    mid = r1.astype(BF16)
    lo = (r1 - mid.astype(F32)).astype(BF16)
    return hi, mid, lo


def _project(x, g_pre, w_in_ref):
    h = _rms(x, g_pre).astype(BF16)
    return _dot(h, w_in_ref[:, 0:MAIN_W]), _dot(h, w_in_ref[:, MAIN_W:])


def _split_projection(proj, glr, cos_ref, sin_ref, w_gk2_ref, b_gk_ref):
    cos, sin_a, sin_b = _rope_lanes(cos_ref[...], sin_ref[...])
    z = _dot(glr.astype(BF16), w_gk2_ref[...]) + b_gk_ref[...]
    log_a = (jnp.minimum(z, 0.0) - jnp.log1p(jnp.exp(-jnp.abs(z)))) * (1.0 / GLA_GATE_TEMP)
    q = jnp.concatenate(
        [_rope(proj[:, COL_Q + LANES * j:COL_Q + LANES * (j + 1)], cos, sin_a, sin_b) * QK_SCALE
         for j in range(ATT_W // LANES)], axis=1)
    k = _rope(proj[:, COL_K:COL_V], cos, sin_a, sin_b)
    v = proj[:, COL_V:COL_GQ]
    gq = proj[:, COL_GQ:COL_GK] * GLA_SCALE
    gk = proj[:, COL_GK:COL_GV]
    gv = proj[:, COL_GV:COL_GR]
    gr = proj[:, COL_GR:MAIN_W]
    return q, k, v, gq, gk, gv, gr, log_a


def _output_stage(x_prev, mix_prev, g_post_ref, g_fpre_ref, g_fpost_ref, w_out_ref, w_up_ref, w_down_ref, write):
    mix = _dot(mix_prev[...], w_out_ref[...])
    yield
    x1 = x_prev[...] + _rms(mix, g_post_ref[...])
    h2 = _rms(x1, g_fpre_ref[...]).astype(BF16)
    f = None
    for i in range(w_up_ref.shape[1] // FF_CHUNK):
        u = jnp.maximum(_dot(h2, w_up_ref[:, FF_CHUNK * i:FF_CHUNK * (i + 1)]), 0.0)
        yield
        part = _dot((u * u).astype(BF16), w_down_ref[FF_CHUNK * i:FF_CHUNK * (i + 1), :])
        f = part if f is None else f + part
        yield
    write(x1 + _rms(f, g_fpost_ref[...]))


def _lo_hi_forms(a):
    lo = lax.broadcasted_iota(jnp.int32, a.shape, 1) < HEAD_DIM
    r = pltpu.roll(a, HEAD_DIM, 1)
    zero = jnp.zeros_like(a)
    return ((jnp.where(lo, a, zero), jnp.where(lo, zero, r)),
            (jnp.where(lo, r, zero), jnp.where(lo, zero, a)))


def _gla_exponents(log_a, gmat):
    hi, mid, lo = _split3(log_a)
    return _dot(gmat, hi) + _dot(gmat, mid) + _dot(gmat, lo)


def _head_lanes(h, rows):
    lane = lax.broadcasted_iota(jnp.int32, (rows, LANES), 1)
    e = h % 2
    return (lane >= GLA_DK * e) & (lane < GLA_DK * (e + 1))


def _gla_intra(gq, gk, log_a, gmat_lv, levels, masks_ref):
    expo = _gla_exponents(log_a, gmat_lv)
    row = lax.broadcasted_iota(jnp.int32, gq.shape, 0)
    terms = [(gq.astype(BF16), gk.astype(BF16), 0)]
    for li, m in enumerate(levels):
        ex = jnp.exp(expo[CHUNK * li:CHUNK * (li + 1)])
        z = (jnp.where((row & m) != 0, gq, gk) * ex).astype(BF16)
        terms.append((z, z, 1 + li))
    scores = []
    for h in range(GLA_HEADS):
        p = h // 2
        in_head = _head_lanes(h, gq.shape[0])
        acc = None
        for zq, zk, mi in terms:
            zq_p = zq[:, LANES * p:LANES * (p + 1)]
            zk_p = zk[:, LANES * p:LANES * (p + 1)]
            zk_h = jnp.where(in_head, zk_p, jnp.zeros_like(zk_p))
            t = _dot_nt(zq_p, zk_h) * masks_ref[mi]
            acc = t if acc is None else acc + t
        scores.append(acc)
    return jnp.concatenate(scores, axis=1)


def _gla_intra_bounded(q_dec, gk, b, causal):
    k_grow_t = jnp.transpose(gk * jnp.exp(-b)).astype(BF16)
    scores = []
    for h in range(GLA_HEADS):
        p = h // 2
        qd_p = q_dec[:, LANES * p:LANES * (p + 1)]
        qd_h = jnp.where(_head_lanes(h, q_dec.shape[0]), qd_p, jnp.zeros_like(qd_p))
        scores.append(_dot(qd_h, k_grow_t[LANES * p:LANES * (p + 1), :]) * causal)
    return jnp.concatenate(scores, axis=1)


def _gla_out_gate(o_h, gr_h, g_gla):
    on = _rms(o_h, g_gla)
    return on * (gr_h * (1.0 / (1.0 + jnp.exp(-gr_h))))


def _prompt_kernel(sinks_ref, x_ref, xo_ref, cos_ref, sin_ref, bias_ref, gmat_ref, masks_ref,
                   g_pre_ref, g_post_ref, g_fpre_ref, g_fpost_ref, g_gla_ref, b_gk_ref,
                   w_in_ref, w_gk2_ref, w_out_ref, w_up_ref, w_down_ref,
                   y_ref, kwin_ref, vwin_ref, sout_ref,
                   q_scr, kbd_scr, vbd_scr, gq_scr, gk_scr, gv_scr, gr_scr, la_scr, s_scr,
                   mix_scr, bounded_scr, *, tiles_per_seq, n_tiles):
    j = pl.program_id(0)
    t_step = lax.rem(jnp.clip(j - 1, 0, n_tiles - 1), tiles_per_seq)
    slot = lax.rem(j, 2)
    tile = x_ref.shape[1]
    n_chunks = tile // CHUNK

    @pl.when(j == 0)
    def _():
        for ref in (s_scr, q_scr, kbd_scr, vbd_scr, gq_scr, gk_scr, gv_scr, gr_scr, la_scr,
                    mix_scr):
            ref[...] = jnp.zeros_like(ref)
        bounded_scr[0] = 1

    branch = jnp.where(j == 0, 3, jnp.where(j == n_tiles + 1, 2, jnp.where(bounded_scr[0] == 1, 0, 1)))

    lo_lane = lax.broadcasted_iota(jnp.int32, (CHUNK, LANES), 1) < HEAD_DIM
    g_gla = g_gla_ref[...]

    def input_stage(results):
        x = x_ref[0]
        proj, glr = _project(x, g_pre_ref[...], w_in_ref)
        yield
        q, k, v, gq, gk, gv, gr, log_a = _split_projection(proj, glr, cos_ref, sin_ref, w_gk2_ref, b_gk_ref)
        chunk_decay = jnp.concatenate(
            [jnp.sum(log_a[CHUNK * c:CHUNK * (c + 1)], axis=0, keepdims=True) for c in range(n_chunks)], axis=0)
        results.update(
            q=q.astype(BF16), k=k, v=v, k_forms=_lo_hi_forms(k), v_forms=_lo_hi_forms(v),
            gq=gq, gk=gk, gv=gv.astype(BF16), gr=gr, log_a=log_a,
            bounded=jnp.min(chunk_decay) >= -BOUNDED_LOG_DECAY)
        yield

    def hand_over(results):
        kbd_scr[:, 0:CHUNK, :] = kbd_scr[:, tile:tile + CHUNK, :]
        vbd_scr[:, 0:CHUNK, :] = vbd_scr[:, tile:tile + CHUNK, :]
        for g in range(ATT_KV_HEADS):
            for e in range(2):
                kbd_scr[2 * g + e, CHUNK:CHUNK + tile, :] = results["k_forms"][g][e].astype(BF16)
                vbd_scr[2 * g + e, CHUNK:CHUNK + tile, :] = results["v_forms"][g][e].astype(BF16)
        q_scr[...] = results["q"]
        gq_scr[...] = results["gq"]
        gk_scr[...] = results["gk"]
        gv_scr[...] = results["gv"]
        gr_scr[...] = results["gr"]
        la_scr[...] = results["log_a"]
        bounded_scr[0] = jnp.where(results["bounded"], 1, 0)
        kwin_ref[0] = jnp.transpose(results["k"][tile - WINDOW:tile, :])
        vwin_ref[0] = jnp.transpose(results["v"][tile - WINDOW:tile, :])

    def attention_chunk(c):
        rows = slice(CHUNK * c, CHUNK * (c + 1))
        rows2 = slice(CHUNK * c, CHUNK * (c + 2))
        bias = bias_ref[jnp.where(t_step == 0, 1, 0)] if c == 0 else bias_ref[0]
        scores = []
        for g in range(ATT_KV_HEADS):
            lhs = jnp.concatenate([q_scr[rows, 2 * LANES * g:2 * LANES * g + LANES],
                                   q_scr[rows, 2 * LANES * g + LANES:2 * LANES * (g + 1)]], axis=0)
            kcat = jnp.concatenate([kbd_scr[2 * g, rows2, :], kbd_scr[2 * g + 1, rows2, :]], axis=0)
            scores.append(_dot_nt(lhs, kcat) + bias)
        yield
        for g in range(ATT_KV_HEADS):
            s = scores[g]
            p_blocks = []
            inv = []
            for blk in range(2):
                p_row = []
                inv_row = []
                for e in range(2):
                    sub = s[CHUNK * blk:CHUNK * (blk + 1), 2 * CHUNK * e:2 * CHUNK * (e + 1)]
                    sink = sinks_ref[4 * g + 2 * blk + e]
                    m = jnp.maximum(jnp.max(sub, axis=-1, keepdims=True), sink)
                    p = jnp.exp(sub - m)
                    den = jnp.sum(p, axis=-1, keepdims=True) + jnp.exp(sink - m)
                    p_row.append(p.astype(BF16))
                    inv_row.append(1.0 / den)
                p_blocks.append(jnp.concatenate(p_row, axis=1))
                inv.append(inv_row)
            p_all = jnp.concatenate(p_blocks, axis=0)
            vcat = jnp.concatenate([vbd_scr[2 * g, rows2, :], vbd_scr[2 * g + 1, rows2, :]], axis=0)
            o = _dot(p_all, vcat)
            for blk in range(2):
                scale = jnp.where(lo_lane, inv[blk][0], inv[blk][1])
                c0 = 2 * LANES * g + LANES * blk
                mix_scr[slot, rows, c0:c0 + LANES] = (o[CHUNK * blk:CHUNK * (blk + 1)] * scale).astype(BF16)
        yield

    def gla_chunk(c, state, bounded_decay):
        rows = slice(CHUNK * c, CHUNK * (c + 1))
        gq_c = gq_scr[rows, :]
        gk_c = gk_scr[rows, :]
        la_c = la_scr[rows, :]
        b = _gla_exponents(la_c, gmat_ref[0:CHUNK, :])
        b_last = jnp.broadcast_to(b[CHUNK - 1:CHUNK, :], b.shape)
        q_dec_f = gq_c * jnp.exp(b)
        q_dec = q_dec_f.astype(BF16)
        k_dec_t = jnp.transpose(gk_c * jnp.exp(b_last - b)).astype(BF16)
        decay_t = jnp.exp(jnp.transpose(b_last))
        yield
        if bounded_decay:
            scores = _gla_intra_bounded(q_dec_f, gk_c, b, masks_ref[len(PROMPT_LEVELS) + 1])
        else:
            scores = _gla_intra(gq_c, gk_c, la_c, gmat_ref[CHUNK:, :], PROMPT_LEVELS, masks_ref)
        yield
        for h in range(GLA_HEADS):
            p_ = h // 2
            v_h = gv_scr[rows, GLA_DV * h:GLA_DV * (h + 1)]
            qd_p = q_dec[:, LANES * p_:LANES * (p_ + 1)]
            qd_h = jnp.where(_head_lanes(h, CHUNK), qd_p, jnp.zeros_like(qd_p))
            s0_pair = jnp.concatenate([state[2 * p_], state[2 * p_ + 1]], axis=0).astype(BF16)
            o_h = _dot(scores[:, CHUNK * h:CHUNK * (h + 1)].astype(BF16), v_h) + _dot(qd_h, s0_pair)
            out_h = _gla_out_gate(o_h, gr_scr[rows, GLA_DV * h:GLA_DV * (h + 1)], g_gla)
            mix_scr[slot, rows, ATT_W + GLA_DV * h:ATT_W + GLA_DV * (h + 1)] = out_h.astype(BF16)
        new_state = [decay_t[GLA_DK * h:GLA_DK * (h + 1), :] * state[h]
                     + _dot(k_dec_t[GLA_DK * h:GLA_DK * (h + 1), :],
                            gv_scr[rows, GLA_DV * h:GLA_DV * (h + 1)])
                     for h in range(GLA_HEADS)]
        state[:] = new_state
        yield

    def write_y(y):
        y_ref[0] = y

    def output_stage():
        return _output_stage(xo_ref.at[0], mix_scr.at[1 - slot], g_post_ref, g_fpre_ref, g_fpost_ref,
                             w_out_ref, w_up_ref, w_down_ref, write_y)

    def run_step(bounded_decay):
        keep = jnp.where(t_step == 0, 0.0, 1.0)
        state = [s_scr[GLA_DK * h:GLA_DK * (h + 1), :] * keep for h in range(GLA_HEADS)]
        results = {}
        out = output_stage()
        inp = input_stage(results)
        att = [attention_chunk(c) for c in range(n_chunks)]
        gla = [gla_chunk(c, state, bounded_decay) for c in range(n_chunks)]
        next(out)
        for c in range(n_chunks):
            next(att[c])
            next(gla[c])
        next(inp)
        spread = [(att[c], gla[c]) for c in range(n_chunks)]
        spread += [(gla[c],) for c in range(n_chunks)]
        spread.append((inp,))
        for stage in spread:
            next(out)
            for gen in stage:
                next(gen)
        for _ in out:
            pass
        s_all = jnp.concatenate(state, axis=0)
        s_scr[...] = s_all
        sout_ref[0] = s_all
        hand_over(results)

    def finish_last_tile():
        for _ in output_stage():
            pass

    def start_first_tile():
        results = {}
        for _ in input_stage(results):
            pass
        hand_over(results)

    lax.switch(branch, [lambda: run_step(True), lambda: run_step(False), finish_last_tile, start_first_tile])


def _sample_kernel(sinks_ref, x_ref, cos_ref, sin_ref, gmat_ref, masks_ref, bias_win_ref, bias_new_ref,
                   ck_ref, cv_ref, st_ref,
                   g_pre_ref, g_post_ref, g_fpre_ref, g_fpost_ref, g_gla_ref, b_gk_ref,
                   w_in_ref, w_gk2_ref, w_out_ref, w_up_ref, w_down_ref,
                   y_ref, kwin_ref, vwin_ref, sout_ref,
                   *, dec_seq, levels):
    results = {}
    for _ in _sample_tile_stages(sinks_ref, x_ref, cos_ref, sin_ref, gmat_ref, masks_ref, bias_win_ref,
                                 bias_new_ref, ck_ref, cv_ref, st_ref, g_pre_ref, g_gla_ref, b_gk_ref,
                                 w_in_ref, w_gk2_ref, kwin_ref, vwin_ref, sout_ref, results,
                                 dec_seq=dec_seq, levels=levels):
        pass

    def write_y(y):
        y_ref[...] = y

    for _ in _output_stage(results["x"], results["mix"], g_post_ref, g_fpre_ref, g_fpost_ref,
                           w_out_ref, w_up_ref, w_down_ref, write_y):
        pass


def _sample_tile_stages(sinks_ref, x_ref, cos_ref, sin_ref, gmat_ref, masks_ref, bias_win_ref, bias_new_ref,
                        ck_ref, cv_ref, st_ref, g_pre_ref, g_gla_ref, b_gk_ref, w_in_ref, w_gk2_ref,
                        kwin_ref, vwin_ref, sout_ref, results, *, dec_seq, levels):
    n_seqs = ck_ref.shape[0]
    x = x_ref[...]
    proj, glr = _project(x, g_pre_ref[...], w_in_ref)
    yield
    q, k, v, gq, gk, gv, gr, log_a = _split_projection(proj, glr, cos_ref, sin_ref, w_gk2_ref, b_gk_ref)

    is_new = lax.broadcasted_iota(jnp.int32, (KV_W, WINDOW), 1) >= WINDOW - dec_seq
    for new, old_ref, out_ref in ((jnp.transpose(k), ck_ref, kwin_ref), (jnp.transpose(v), cv_ref, vwin_ref)):
        for s in range(n_seqs):
            shifted = pltpu.roll(old_ref[s], WINDOW - dec_seq, 1)
            tail = pltpu.roll(new, (WINDOW - dec_seq * (s + 1)) % CHUNK, 1)
            out_ref[s] = jnp.where(is_new, tail, shifted)

    def seq_rows(a, s):
        return a[dec_seq * s:dec_seq * (s + 1)]

    def gather_seq(per_head, s):
        return jnp.concatenate([seq_rows(a, s) for a in per_head], axis=0)

    def scatter_heads(per_seq, h):
        return jnp.concatenate([seq_rows(a, h) for a in per_seq], axis=0)

    lo_lane = lax.broadcasted_iota(jnp.int32, (CHUNK, LANES), 1) < HEAD_DIM
    q_heads = []
    for j in range(ATT_W // LANES):
        blk = q[:, LANES * j:LANES * (j + 1)]
        rolled = pltpu.roll(blk, HEAD_DIM, 1)
        g = (2 * j) // (ATT_HEADS // ATT_KV_HEADS)
        keep = lo_lane if g == 0 else jnp.logical_not(lo_lane)
        for e in range(2):
            q_heads.append(jnp.where(keep, blk if e == g else rolled, 0.0))
    s_win = [_dot(gather_seq(q_heads, s), ck_ref[s]) for s in range(n_seqs)]
    k_b = k.astype(BF16)
    v_b = v.astype(BF16)
    s_new = [_dot_nt(q_heads[h].astype(BF16), k_b) for h in range(ATT_HEADS)]
    b = _gla_exponents(log_a, gmat_ref[0:CHUNK, :])
    yield

    b3 = b.reshape(n_seqs, dec_seq, GK_W)
    b_last = jnp.broadcast_to(b3[:, dec_seq - 1:dec_seq, :], b3.shape).reshape(CHUNK, GK_W)
    q_dec = gq * jnp.exp(b)
    k_dec_t = jnp.transpose(gk * jnp.exp(b_last - b))
    decay_t = jnp.exp(jnp.transpose(b_last))
    scores = _gla_intra(gq, gk, log_a, gmat_ref[CHUNK:, :], levels, masks_ref)
    lane_k = lax.broadcasted_iota(jnp.int32, (dec_seq, GK_W), 1)
    lane_t = lax.broadcasted_iota(jnp.int32, (GLA_DK, LANES), 1)
    o_inter = []
    for s in range(n_seqs):
        qd = seq_rows(q_dec, s)
        lhs = jnp.concatenate(
            [jnp.where((lane_k >= GLA_DK * h) & (lane_k < GLA_DK * (h + 1)), qd, 0.0)
             for h in range(GLA_HEADS)], axis=0)
        o_inter.append(_dot(lhs, st_ref[s]))
    g_gla = g_gla_ref[...]
    gla_cols = []
    for h in range(GLA_HEADS):
        cs = slice(GLA_DV * h, GLA_DV * (h + 1))
        v_h = gv[:, cs].astype(BF16)
        o_h = _dot(scores[:, CHUNK * h:CHUNK * (h + 1)].astype(BF16), v_h) + scatter_heads(o_inter, h)
        gla_cols.append(_gla_out_gate(o_h, gr[:, cs], g_gla).astype(BF16))
    yield

    bias_win = bias_win_ref[...]
    bias_new = bias_new_ref[...]
    p_win = []
    p_new = []
    inv_den = []
    for h in range(ATT_HEADS):
        sw = scatter_heads(s_win, h) + bias_win
        sn = s_new[h] + bias_new
        sink = sinks_ref[h]
        m = jnp.maximum(jnp.maximum(jnp.max(sw, axis=-1, keepdims=True),
                                    jnp.max(sn, axis=-1, keepdims=True)), sink)
        pw = jnp.exp(sw - m)
        pn = jnp.exp(sn - m)
        den = (jnp.sum(pw, axis=-1, keepdims=True) + jnp.sum(pn, axis=-1, keepdims=True)
               + jnp.exp(sink - m))
        p_win.append(pw)
        p_new.append(pn.astype(BF16))
        inv_den.append(1.0 / den)
    o_new = [_dot(p_new[h], v_b) for h in range(ATT_HEADS)]
    o_win = [_dot_nt(gather_seq(p_win, s), cv_ref[s]) for s in range(n_seqs)]
    for h in range(GLA_HEADS):
        hs = slice(GLA_DK * h, GLA_DK * (h + 1))
        kd_h = k_dec_t[hs, :]
        lhs = jnp.concatenate(
            [jnp.where((lane_t >= dec_seq * s) & (lane_t < dec_seq * (s + 1)), kd_h, 0.0).astype(BF16)
             for s in range(n_seqs)], axis=0)
        upd = _dot(lhs, gv[:, GLA_DV * h:GLA_DV * (h + 1)].astype(BF16))
        for s in range(n_seqs):
            a_col = jnp.broadcast_to(decay_t[hs, dec_seq * s:dec_seq * s + 1], (GLA_DK, GLA_DV))
            sout_ref[s, hs, :] = a_col * st_ref[s, hs, :] + upd[GLA_DK * s:GLA_DK * (s + 1)]
    o_heads = [(scatter_heads(o_win, h) + o_new[h]) * inv_den[h] for h in range(ATT_HEADS)]
    mix_cols = []
    for j in range(ATT_W // LANES):
        g = (2 * j) // (ATT_HEADS // ATT_KV_HEADS)
        first = o_heads[2 * j] if g == 0 else pltpu.roll(o_heads[2 * j], HEAD_DIM, 1)
        second = o_heads[2 * j + 1] if g == 1 else pltpu.roll(o_heads[2 * j + 1], HEAD_DIM, 1)
        mix_cols.append(jnp.where(lo_lane, first, second).astype(BF16))
    results.update(x=x, mix=jnp.concatenate(mix_cols + gla_cols, axis=1))


def _gla_constants(seg, levels):
    t = np.arange(CHUNK)[:, None]
    u = np.arange(CHUNK)[None, :]
    same = (t // seg) == (u // seg)
    blocks = [same & (u <= t)]
    masks = [np.eye(CHUNK, dtype=bool)]
    for m in levels:
        ref = (t // (2 * m)) * (2 * m) + m - 1
        second = (t & m) != 0
        blocks.append(np.where(second, (u > ref) & (u <= t), (u > t) & (u <= ref)))
        masks.append(((t // (2 * m)) == (u // (2 * m))) & second & ((u & m) == 0))
    masks.append(blocks[0])
    gmat = jnp.asarray(np.concatenate(blocks, axis=0).astype(np.float32), dtype=BF16)
    return gmat, jnp.asarray(np.stack(masks).astype(np.float32))


def _attention_bias():
    t = (np.arange(2 * CHUNK) % CHUNK)[:, None]
    j = (np.arange(4 * CHUNK) % (2 * CHUNK))[None, :]
    band = (j >= t + 1) & (j <= t + WINDOW)
    first = band & (j >= CHUNK)
    neg = np.float32(-np.inf)
    return jnp.asarray(np.stack([np.where(band, np.float32(0), neg), np.where(first, np.float32(0), neg)]))


def _sample_attention_bias(dec_seq):
    r = np.arange(CHUNK)[:, None]
    c = np.arange(CHUNK)[None, :]
    t = r % dec_seq
    win = c > t
    new = ((r // dec_seq) == (c // dec_seq)) & ((c % dec_seq) <= t)
    neg = np.float32(-np.inf)
    zero = np.float32(0)
    return jnp.asarray(np.where(win, zero, neg)), jnp.asarray(np.where(new, zero, neg))


def _rope_tables(pos):
    half = HEAD_DIM // 2
    inv = ROPE_THETA ** (-jnp.arange(half, dtype=F32) / half)
    ang = inv[:, None] * pos.astype(F32)[None, :]
    return jnp.cos(ang), jnp.sin(ang)


def _const_spec(shape):
    zeros = (0,) * len(shape)
    return pl.BlockSpec(shape, lambda *_: zeros, pipeline_mode=pl.Buffered(1))


def kernel(x_prompt, x_sample, cache_k, cache_v, state_gla, w_in, w_gk2, b_gk, g_gla, sinks, w_out, g_mix_pre, g_mix_post, g_ffn_pre, g_ffn_post, w_up, w_down):
    depth = w_in.shape[0]
    assert depth == 1, "single trunk layer"
    batch, seq, d_model = x_prompt.shape
    dec_batch, dec_seq, _ = x_sample.shape
    d_ff = w_up.shape[2]
    assert seq % PROMPT_TILE == 0 and dec_batch % SAMPLE_SEQS == 0
    assert SAMPLE_SEQS * dec_seq == CHUNK and CHUNK % dec_seq == 0 and d_ff % FF_CHUNK == 0
    assert w_in.shape[2] == MAIN_W + GLA_RANK

    w_in_b = w_in[0].astype(BF16)
    w_gk2_b = w_gk2[0].astype(BF16)
    w_out_b = w_out[0].astype(BF16)
    w_up_b = w_up[0].astype(BF16)
    w_down_b = w_down[0].astype(BF16)
    g_pre = g_mix_pre[0][None, :]
    g_post = g_mix_post[0][None, :]
    g_fpre = g_ffn_pre[0][None, :]
    g_fpost = g_ffn_post[0][None, :]
    g_gla2 = g_gla[0][None, :]
    b_gk2 = b_gk[0][None, :]
    sinks1 = sinks[0]

    weights = (g_pre, g_post, g_fpre, g_fpost, g_gla2, b_gk2, w_in_b, w_gk2_b, w_out_b, w_up_b, w_down_b)
    weight_specs = [_const_spec(w.shape) for w in weights]
    smem_spec = pl.BlockSpec(memory_space=pltpu.SMEM)

    cos_p, sin_p = _rope_tables(jnp.arange(seq, dtype=jnp.int32))
    gmat_p, masks_p = _gla_constants(CHUNK, PROMPT_LEVELS)
    bias = _attention_bias()
    n_t = seq // PROMPT_TILE
    tile = PROMPT_TILE
    n_tiles = batch * n_t

    def cur_tile(j):
        return jnp.minimum(j, n_tiles - 1)

    def mix_tile(j):
        return jnp.clip(j - 1, 0, n_tiles - 1)

    def done_tile(j):
        return jnp.clip(j - 2, 0, n_tiles - 1)

    tab_spec = pl.BlockSpec((HEAD_DIM // 2, tile), lambda j: (0, cur_tile(j) % n_t))
    seq_out = lambda j: (cur_tile(j) // n_t, 0, 0)
    yp, kp, vp, sp = pl.pallas_call(
        functools.partial(_prompt_kernel, tiles_per_seq=n_t, n_tiles=n_tiles),
        grid=(n_tiles + 2,),
        in_specs=[smem_spec,
                  pl.BlockSpec((1, tile, d_model), lambda j: (cur_tile(j) // n_t, cur_tile(j) % n_t, 0)),
                  pl.BlockSpec((1, tile, d_model), lambda j: (done_tile(j) // n_t, done_tile(j) % n_t, 0)),
                  tab_spec, tab_spec,
                  _const_spec(bias.shape), _const_spec(gmat_p.shape), _const_spec(masks_p.shape)] + weight_specs,
        out_specs=[pl.BlockSpec((1, tile, d_model), lambda j: (done_tile(j) // n_t, done_tile(j) % n_t, 0)),
                   pl.BlockSpec((1, WINDOW, KV_W), seq_out),
                   pl.BlockSpec((1, WINDOW, KV_W), seq_out),
                   pl.BlockSpec((1, GK_W, GLA_DV), lambda j: (mix_tile(j) // n_t, 0, 0))],
        out_shape=[jax.ShapeDtypeStruct((batch, seq, d_model), F32),
                   jax.ShapeDtypeStruct((batch, WINDOW, KV_W), F32),
                   jax.ShapeDtypeStruct((batch, WINDOW, KV_W), F32),
                   jax.ShapeDtypeStruct((batch, GK_W, GLA_DV), F32)],
        scratch_shapes=[pltpu.VMEM((tile, ATT_W), BF16),
                        pltpu.VMEM((2 * ATT_KV_HEADS, CHUNK + tile, LANES), BF16),
                        pltpu.VMEM((2 * ATT_KV_HEADS, CHUNK + tile, LANES), BF16),
                        pltpu.VMEM((tile, GK_W), F32),
                        pltpu.VMEM((tile, GK_W), F32),
                        pltpu.VMEM((tile, GV_W), BF16),
                        pltpu.VMEM((tile, GV_W), F32),
                        pltpu.VMEM((tile, GK_W), F32),
                        pltpu.VMEM((GK_W, GLA_DV), F32),
                        pltpu.VMEM((2, tile, d_model), BF16),
                        pltpu.SMEM((1,), jnp.int32)],
        compiler_params=pltpu.CompilerParams(
            dimension_semantics=("arbitrary",), vmem_limit_bytes=VMEM_LIMIT_BYTES),
        name="prompt_layer",
    )(sinks1, x_prompt, x_prompt, cos_p, sin_p, bias, gmat_p, masks_p, *weights)

    levels_s = tuple(m for m in PROMPT_LEVELS if m < dec_seq)
    pos_s = PAST_LEN + jnp.arange(dec_seq, dtype=jnp.int32)
    cos_s, sin_s = (jnp.tile(a, (1, SAMPLE_SEQS)) for a in _rope_tables(pos_s))
    gmat_s, masks_s = _gla_constants(dec_seq, levels_s)
    bias_win, bias_new = _sample_attention_bias(dec_seq)
    rows = SAMPLE_SEQS * dec_seq
    n_g = dec_batch // SAMPLE_SEQS
    xs = x_sample.reshape(dec_batch * dec_seq, d_model)
    ck = jnp.transpose(cache_k[0].reshape(dec_batch, WINDOW, KV_W), (0, 2, 1))
    cv = jnp.transpose(cache_v[0].reshape(dec_batch, WINDOW, KV_W), (0, 2, 1))
    st = state_gla[0].reshape(dec_batch, GK_W, GLA_DV)
    seq_spec = pl.BlockSpec((SAMPLE_SEQS, WINDOW, KV_W), lambda i: (i, 0, 0))
    st_spec = pl.BlockSpec((SAMPLE_SEQS, GK_W, GLA_DV), lambda i: (i, 0, 0))
    row_spec = pl.BlockSpec((rows, d_model), lambda i: (i, 0))
    ys, ks, vs, ss = pl.pallas_call(
        functools.partial(_sample_kernel, dec_seq=dec_seq, levels=levels_s),
        grid=(n_g,),
        in_specs=[smem_spec, row_spec,
                  _const_spec(cos_s.shape), _const_spec(sin_s.shape),
                  _const_spec(gmat_s.shape), _const_spec(masks_s.shape),
                  _const_spec(bias_win.shape), _const_spec(bias_new.shape),
                  seq_spec, seq_spec, st_spec] + weight_specs,
        out_specs=[row_spec, seq_spec, seq_spec, st_spec],
        out_shape=[jax.ShapeDtypeStruct((dec_batch * dec_seq, d_model), F32),
                   jax.ShapeDtypeStruct((dec_batch, WINDOW, KV_W), F32),
                   jax.ShapeDtypeStruct((dec_batch, WINDOW, KV_W), F32),
                   jax.ShapeDtypeStruct((dec_batch, GK_W, GLA_DV), F32)],
        compiler_params=pltpu.CompilerParams(
            dimension_semantics=("arbitrary",), vmem_limit_bytes=VMEM_LIMIT_BYTES),
        name="sample_layer",
    )(sinks1, xs, cos_s, sin_s, gmat_s, masks_s, bias_win, bias_new, ck, cv, st, *weights)

    head_shape = (ATT_KV_HEADS, HEAD_DIM)
    state_shape = (GLA_HEADS, GLA_DK, GLA_DV)
    return (yp,
            ys.reshape(dec_batch, dec_seq, d_model),
            jnp.transpose(kp, (0, 2, 1)).reshape((1, batch, WINDOW) + head_shape),
            jnp.transpose(vp, (0, 2, 1)).reshape((1, batch, WINDOW) + head_shape),
            sp.reshape((1, batch) + state_shape),
            jnp.transpose(ks, (0, 2, 1)).reshape((1, dec_batch, WINDOW) + head_shape),
            jnp.transpose(vs, (0, 2, 1)).reshape((1, dec_batch, WINDOW) + head_shape),
            ss.reshape((1, dec_batch) + state_shape))
```

```python
import functools

import numpy as np
import jax
import jax.numpy as jnp
from jax import lax
from jax.experimental import pallas as pl
from jax.experimental.pallas import tpu as pltpu

F32 = jnp.float32
BF16 = jnp.bfloat16

HEAD_DIM = 64
ATT_HEADS = 8
ATT_KV_HEADS = 2
WINDOW = 128
PAST_LEN = 16384
ROPE_THETA = 10000.0
GLA_HEADS = 4
GLA_DK = 64
GLA_DV = 128
GLA_RANK = 16
GLA_GATE_TEMP = 16.0
EPS = 1e-6
ATT_W = ATT_HEADS * HEAD_DIM
KV_W = ATT_KV_HEADS * HEAD_DIM
GK_W = GLA_HEADS * GLA_DK
GV_W = GLA_HEADS * GLA_DV
COL_Q = 0
COL_K = COL_Q + ATT_W
COL_V = COL_K + KV_W
COL_GQ = COL_V + KV_W
COL_GK = COL_GQ + GK_W
COL_GV = COL_GK + GK_W
COL_GR = COL_GV + GV_W
MAIN_W = COL_GR + GV_W
QK_SCALE = HEAD_DIM ** -0.5
GLA_SCALE = GLA_DK ** -0.5

CHUNK = 128
LANES = 128
PROMPT_TILE = 256
SAMPLE_SEQS = 16
FF_CHUNK = 1024
PROMPT_LEVELS = (1, 2, 4, 8, 16, 32, 64)
BOUNDED_LOG_DECAY = 30.0
VMEM_LIMIT_BYTES = 56 * 1024 * 1024


def _dot(a, b):
    return jnp.dot(a, b, preferred_element_type=F32)


def _dot_nt(a, b):
    return lax.dot_general(a, b, (((1,), (1,)), ((), ())), preferred_element_type=F32)


def _rms(x, g):
    ms = jnp.mean(x * x, axis=-1, keepdims=True)
    return x * lax.rsqrt(ms + EPS) * g


def _rope(x, cos, sin_a, sin_b):
    return x * cos + pltpu.roll(x, 96, 1) * sin_a + pltpu.roll(x, 32, 1) * sin_b


def _rope_lanes(cos, sin):
    reps = LANES // cos.shape[0]
    cos_l = jnp.transpose(jnp.concatenate([cos] * reps, axis=0))
    sin_l = jnp.transpose(jnp.concatenate([sin] * reps, axis=0))
    lane = lax.broadcasted_iota(jnp.int32, cos_l.shape, 1)
    first_half = (lane & (HEAD_DIM // 2)) == 0
    return cos_l, jnp.where(first_half, -sin_l, 0.0), jnp.where(first_half, 0.0, sin_l)


def _split3(x):
    hi = x.astype(BF16)
    r1 = x - hi.astype(F32)
    mid = r1.astype(BF16)
    lo = (r1 - mid.astype(F32)).astype(BF16)
    return hi, mid, lo


def _project(x, g_pre, w_in_ref):
    h = _rms(x, g_pre).astype(BF16)
    return _dot(h, w_in_ref[:, 0:MAIN_W]), _dot(h, w_in_ref[:, MAIN_W:])


def _split_projection(proj, glr, cos_ref, sin_ref, w_gk2_ref, b_gk_ref):
    cos, sin_a, sin_b = _rope_lanes(cos_ref[...], sin_ref[...])
    z = _dot(glr.astype(BF16), w_gk2_ref[...]) + b_gk_ref[...]
    log_a = (jnp.minimum(z, 0.0) - jnp.log1p(jnp.exp(-jnp.abs(z)))) * (1.0 / GLA_GATE_TEMP)
    q = jnp.concatenate(
        [_rope(proj[:, COL_Q + LANES * j:COL_Q + LANES * (j + 1)], cos, sin_a, sin_b) * QK_SCALE
         for j in range(ATT_W // LANES)], axis=1)
    k = _rope(proj[:, COL_K:COL_V], cos, sin_a, sin_b)
    v = proj[:, COL_V:COL_GQ]
    gq = proj[:, COL_GQ:COL_GK] * GLA_SCALE
    gk = proj[:, COL_GK:COL_GV]
    gv = proj[:, COL_GV:COL_GR]
    gr = proj[:, COL_GR:MAIN_W]
    return q, k, v, gq, gk, gv, gr, log_a


def _output_stage(x_prev, mix_prev, g_post_ref, g_fpre_ref, g_fpost_ref, w_out_ref, w_up_ref, w_down_ref, write):
    mix = _dot(mix_prev[...], w_out_ref[...])
    yield
    x1 = x_prev[...] + _rms(mix, g_post_ref[...])
    h2 = _rms(x1, g_fpre_ref[...]).astype(BF16)
    f = None
    for i in range(w_up_ref.shape[1] // FF_CHUNK):
        u = jnp.maximum(_dot(h2, w_up_ref[:, FF_CHUNK * i:FF_CHUNK * (i + 1)]), 0.0)
        yield
        part = _dot((u * u).astype(BF16), w_down_ref[FF_CHUNK * i:FF_CHUNK * (i + 1), :])
        f = part if f is None else f + part
        yield
    write(x1 + _rms(f, g_fpost_ref[...]))


def _lo_hi_forms(a):
    lo = lax.broadcasted_iota(jnp.int32, a.shape, 1) < HEAD_DIM
    r = pltpu.roll(a, HEAD_DIM, 1)
    zero = jnp.zeros_like(a)
    return ((jnp.where(lo, a, zero), jnp.where(lo, zero, r)),
            (jnp.where(lo, r, zero), jnp.where(lo, zero, a)))


def _gla_exponents(log_a, gmat):
    hi, mid, lo = _split3(log_a)
    return _dot(gmat, hi) + _dot(gmat, mid) + _dot(gmat, lo)


def _head_lanes(h, rows):
    lane = lax.broadcasted_iota(jnp.int32, (rows, LANES), 1)
    e = h % 2
    return (lane >= GLA_DK * e) & (lane < GLA_DK * (e + 1))


def _gla_intra(gq, gk, log_a, gmat_lv, levels, masks_ref):
    expo = _gla_exponents(log_a, gmat_lv)
    row = lax.broadcasted_iota(jnp.int32, gq.shape, 0)
    terms = [(gq.astype(BF16), gk.astype(BF16), 0)]
    for li, m in enumerate(levels):
        ex = jnp.exp(expo[CHUNK * li:CHUNK * (li + 1)])
        z = (jnp.where((row & m) != 0, gq, gk) * ex).astype(BF16)
        terms.append((z, z, 1 + li))
    scores = []
    for h in range(GLA_HEADS):
        p = h // 2
        in_head = _head_lanes(h, gq.shape[0])
        acc = None
        for zq, zk, mi in terms:
            zq_p = zq[:, LANES * p:LANES * (p + 1)]
            zk_p = zk[:, LANES * p:LANES * (p + 1)]
            zk_h = jnp.where(in_head, zk_p, jnp.zeros_like(zk_p))
            t = _dot_nt(zq_p, zk_h) * masks_ref[mi]
            acc = t if acc is None else acc + t
        scores.append(acc)
    return jnp.concatenate(scores, axis=1)


def _gla_intra_bounded(q_dec, gk, b, causal):
    k_grow_t = jnp.transpose(gk * jnp.exp(-b)).astype(BF16)
    scores = []
    for h in range(GLA_HEADS):
        p = h // 2
        qd_p = q_dec[:, LANES * p:LANES * (p + 1)]
        qd_h = jnp.where(_head_lanes(h, q_dec.shape[0]), qd_p, jnp.zeros_like(qd_p))
        scores.append(_dot(qd_h, k_grow_t[LANES * p:LANES * (p + 1), :]) * causal)
    return jnp.concatenate(scores, axis=1)


def _gla_out_gate(o_h, gr_h, g_gla):
    on = _rms(o_h, g_gla)
    return on * (gr_h * (1.0 / (1.0 + jnp.exp(-gr_h))))


def _prompt_kernel(sinks_ref, x_ref, xo_ref, cos_ref, sin_ref, bias_ref, gmat_ref, masks_ref,
                   g_pre_ref, g_post_ref, g_fpre_ref, g_fpost_ref, g_gla_ref, b_gk_ref,
                   w_in_ref, w_gk2_ref, w_out_ref, w_up_ref, w_down_ref,
                   y_ref, kwin_ref, vwin_ref, sout_ref,
                   q_scr, kbd_scr, vbd_scr, gq_scr, gk_scr, gv_scr, gr_scr, la_scr, s_scr,
                   mix_scr, bounded_scr, *, tiles_per_seq, n_tiles):
    j = pl.program_id(0)
    t_step = lax.rem(jnp.clip(j - 1, 0, n_tiles - 1), tiles_per_seq)
    slot = lax.rem(j, 2)
    tile = x_ref.shape[1]
    n_chunks = tile // CHUNK

    @pl.when(j == 0)
    def _():
        for ref in (s_scr, q_scr, kbd_scr, vbd_scr, gq_scr, gk_scr, gv_scr, gr_scr, la_scr,
                    mix_scr):
            ref[...] = jnp.zeros_like(ref)
        bounded_scr[0] = 1

    branch = jnp.where(j == 0, 3, jnp.where(j == n_tiles + 1, 2, jnp.where(bounded_scr[0] == 1, 0, 1)))

    lo_lane = lax.broadcasted_iota(jnp.int32, (CHUNK, LANES), 1) < HEAD_DIM
    g_gla = g_gla_ref[...]

    def input_stage(results):
        x = x_ref[0]
        proj, glr = _project(x, g_pre_ref[...], w_in_ref)
        yield
        q, k, v, gq, gk, gv, gr, log_a = _split_projection(proj, glr, cos_ref, sin_ref, w_gk2_ref, b_gk_ref)
        chunk_decay = jnp.concatenate(
            [jnp.sum(log_a[CHUNK * c:CHUNK * (c + 1)], axis=0, keepdims=True) for c in range(n_chunks)], axis=0)
        results.update(
            q=q.astype(BF16), k=k, v=v, k_forms=_lo_hi_forms(k), v_forms=_lo_hi_forms(v),
            gq=gq, gk=gk, gv=gv.astype(BF16), gr=gr, log_a=log_a,
            bounded=jnp.min(chunk_decay) >= -BOUNDED_LOG_DECAY)
        yield

    def hand_over(results):
        kbd_scr[:, 0:CHUNK, :] = kbd_scr[:, tile:tile + CHUNK, :]
        vbd_scr[:, 0:CHUNK, :] = vbd_scr[:, tile:tile + CHUNK, :]
        for g in range(ATT_KV_HEADS):
            for e in range(2):
                kbd_scr[2 * g + e, CHUNK:CHUNK + tile, :] = results["k_forms"][g][e].astype(BF16)
                vbd_scr[2 * g + e, CHUNK:CHUNK + tile, :] = results["v_forms"][g][e].astype(BF16)
        q_scr[...] = results["q"]
        gq_scr[...] = results["gq"]
        gk_scr[...] = results["gk"]
        gv_scr[...] = results["gv"]
        gr_scr[...] = results["gr"]
        la_scr[...] = results["log_a"]
        bounded_scr[0] = jnp.where(results["bounded"], 1, 0)
        kwin_ref[0] = jnp.transpose(results["k"][tile - WINDOW:tile, :])
        vwin_ref[0] = jnp.transpose(results["v"][tile - WINDOW:tile, :])

    def attention_chunk(c):
        rows = slice(CHUNK * c, CHUNK * (c + 1))
        rows2 = slice(CHUNK * c, CHUNK * (c + 2))
        bias = bias_ref[jnp.where(t_step == 0, 1, 0)] if c == 0 else bias_ref[0]
        scores = []
        for g in range(ATT_KV_HEADS):
            lhs = jnp.concatenate([q_scr[rows, 2 * LANES * g:2 * LANES * g + LANES],
                                   q_scr[rows, 2 * LANES * g + LANES:2 * LANES * (g + 1)]], axis=0)
            kcat = jnp.concatenate([kbd_scr[2 * g, rows2, :], kbd_scr[2 * g + 1, rows2, :]], axis=0)
            scores.append(_dot_nt(lhs, kcat) + bias)
        yield
        for g in range(ATT_KV_HEADS):
            s = scores[g]
            p_blocks = []
            inv = []
            for blk in range(2):
                p_row = []
                inv_row = []
                for e in range(2):
                    sub = s[CHUNK * blk:CHUNK * (blk + 1), 2 * CHUNK * e:2 * CHUNK * (e + 1)]
                    sink = sinks_ref[4 * g + 2 * blk + e]
                    m = jnp.maximum(jnp.max(sub, axis=-1, keepdims=True), sink)
                    p = jnp.exp(sub - m)
                    den = jnp.sum(p, axis=-1, keepdims=True) + jnp.exp(sink - m)
                    p_row.append(p.astype(BF16))
                    inv_row.append(1.0 / den)
                p_blocks.append(jnp.concatenate(p_row, axis=1))
                inv.append(inv_row)
            p_all = jnp.concatenate(p_blocks, axis=0)
            vcat = jnp.concatenate([vbd_scr[2 * g, rows2, :], vbd_scr[2 * g + 1, rows2, :]], axis=0)
            o = _dot(p_all, vcat)
            for blk in range(2):
                scale = jnp.where(lo_lane, inv[blk][0], inv[blk][1])
                c0 = 2 * LANES * g + LANES * blk
                mix_scr[slot, rows, c0:c0 + LANES] = (o[CHUNK * blk:CHUNK * (blk + 1)] * scale).astype(BF16)
        yield

    def gla_chunk(c, state, bounded_decay):
        rows = slice(CHUNK * c, CHUNK * (c + 1))
        gq_c = gq_scr[rows, :]
        gk_c = gk_scr[rows, :]
        la_c = la_scr[rows, :]
        b = _gla_exponents(la_c, gmat_ref[0:CHUNK, :])
        b_last = jnp.broadcast_to(b[CHUNK - 1:CHUNK, :], b.shape)
        q_dec_f = gq_c * jnp.exp(b)
        q_dec = q_dec_f.astype(BF16)
        k_dec_t = jnp.transpose(gk_c * jnp.exp(b_last - b)).astype(BF16)
        decay_t = jnp.exp(jnp.transpose(b_last))
        yield
        if bounded_decay:
            scores = _gla_intra_bounded(q_dec_f, gk_c, b, masks_ref[len(PROMPT_LEVELS) + 1])
        else:
            scores = _gla_intra(gq_c, gk_c, la_c, gmat_ref[CHUNK:, :], PROMPT_LEVELS, masks_ref)
        yield
        for h in range(GLA_HEADS):
            p_ = h // 2
            v_h = gv_scr[rows, GLA_DV * h:GLA_DV * (h + 1)]
            qd_p = q_dec[:, LANES * p_:LANES * (p_ + 1)]
            qd_h = jnp.where(_head_lanes(h, CHUNK), qd_p, jnp.zeros_like(qd_p))
            s0_pair = jnp.concatenate([state[2 * p_], state[2 * p_ + 1]], axis=0).astype(BF16)
            o_h = _dot(scores[:, CHUNK * h:CHUNK * (h + 1)].astype(BF16), v_h) + _dot(qd_h, s0_pair)
            out_h = _gla_out_gate(o_h, gr_scr[rows, GLA_DV * h:GLA_DV * (h + 1)], g_gla)
            mix_scr[slot, rows, ATT_W + GLA_DV * h:ATT_W + GLA_DV * (h + 1)] = out_h.astype(BF16)
        new_state = [decay_t[GLA_DK * h:GLA_DK * (h + 1), :] * state[h]
                     + _dot(k_dec_t[GLA_DK * h:GLA_DK * (h + 1), :],
                            gv_scr[rows, GLA_DV * h:GLA_DV * (h + 1)])
                     for h in range(GLA_HEADS)]
        state[:] = new_state
        yield

    def write_y(y):
        y_ref[0] = y

    def output_stage():
        return _output_stage(xo_ref.at[0], mix_scr.at[1 - slot], g_post_ref, g_fpre_ref, g_fpost_ref,
                             w_out_ref, w_up_ref, w_down_ref, write_y)

    def run_step(bounded_decay):
        keep = jnp.where(t_step == 0, 0.0, 1.0)
        state = [s_scr[GLA_DK * h:GLA_DK * (h + 1), :] * keep for h in range(GLA_HEADS)]
        results = {}
        out = output_stage()
        inp = input_stage(results)
        att = [attention_chunk(c) for c in range(n_chunks)]
        gla = [gla_chunk(c, state, bounded_decay) for c in range(n_chunks)]
        next(out)
        for c in range(n_chunks):
            next(att[c])
            next(gla[c])
        next(inp)
        spread = [(gla[c], att[c]) for c in range(n_chunks)]
        spread += [(gla[c],) for c in range(n_chunks)]
        spread.append((inp,))
        for stage in spread:
            next(out)
            for gen in stage:
                next(gen)
        for _ in out:
            pass
        s_all = jnp.concatenate(state, axis=0)
        s_scr[...] = s_all
        sout_ref[0] = s_all
        hand_over(results)

    def finish_last_tile():
        for _ in output_stage():
            pass

    def start_first_tile():
        results = {}
        for _ in input_stage(results):
            pass
        hand_over(results)

    lax.switch(branch, [lambda: run_step(True), lambda: run_step(False), finish_last_tile, start_first_tile])


def _sample_kernel(sinks_ref, x_ref, cos_ref, sin_ref, gmat_ref, masks_ref, bias_win_ref, bias_new_ref,
                   ck_ref, cv_ref, st_ref,
                   g_pre_ref, g_post_ref, g_fpre_ref, g_fpost_ref, g_gla_ref, b_gk_ref,
                   w_in_ref, w_gk2_ref, w_out_ref, w_up_ref, w_down_ref,
                   y_ref, kwin_ref, vwin_ref, sout_ref,
                   *, dec_seq, levels):
    results = {}
    for _ in _sample_tile_stages(sinks_ref, x_ref, cos_ref, sin_ref, gmat_ref, masks_ref, bias_win_ref,
                                 bias_new_ref, ck_ref, cv_ref, st_ref, g_pre_ref, g_gla_ref, b_gk_ref,
                                 w_in_ref, w_gk2_ref, kwin_ref, vwin_ref, sout_ref, results,
                                 dec_seq=dec_seq, levels=levels):
        pass

    def write_y(y):
        y_ref[...] = y

    for _ in _output_stage(results["x"], results["mix"], g_post_ref, g_fpre_ref, g_fpost_ref,
                           w_out_ref, w_up_ref, w_down_ref, write_y):
        pass


def _sample_tile_stages(sinks_ref, x_ref, cos_ref, sin_ref, gmat_ref, masks_ref, bias_win_ref, bias_new_ref,
                        ck_ref, cv_ref, st_ref, g_pre_ref, g_gla_ref, b_gk_ref, w_in_ref, w_gk2_ref,
                        kwin_ref, vwin_ref, sout_ref, results, *, dec_seq, levels):
    n_seqs = ck_ref.shape[0]
    x = x_ref[...]
    proj, glr = _project(x, g_pre_ref[...], w_in_ref)
    yield
    q, k, v, gq, gk, gv, gr, log_a = _split_projection(proj, glr, cos_ref, sin_ref, w_gk2_ref, b_gk_ref)

    is_new = lax.broadcasted_iota(jnp.int32, (KV_W, WINDOW), 1) >= WINDOW - dec_seq
    for new, old_ref, out_ref in ((jnp.transpose(k), ck_ref, kwin_ref), (jnp.transpose(v), cv_ref, vwin_ref)):
        for s in range(n_seqs):
            shifted = pltpu.roll(old_ref[s], WINDOW - dec_seq, 1)
            tail = pltpu.roll(new, (WINDOW - dec_seq * (s + 1)) % CHUNK, 1)
            out_ref[s] = jnp.where(is_new, tail, shifted)

    def seq_rows(a, s):
        return a[dec_seq * s:dec_seq * (s + 1)]

    def gather_seq(per_head, s):
        return jnp.concatenate([seq_rows(a, s) for a in per_head], axis=0)

    def scatter_heads(per_seq, h):
        return jnp.concatenate([seq_rows(a, h) for a in per_seq], axis=0)

    lo_lane = lax.broadcasted_iota(jnp.int32, (CHUNK, LANES), 1) < HEAD_DIM
    q_heads = []
    for j in range(ATT_W // LANES):
        blk = q[:, LANES * j:LANES * (j + 1)]
        rolled = pltpu.roll(blk, HEAD_DIM, 1)
        g = (2 * j) // (ATT_HEADS // ATT_KV_HEADS)
        keep = lo_lane if g == 0 else jnp.logical_not(lo_lane)
        for e in range(2):
            q_heads.append(jnp.where(keep, blk if e == g else rolled, 0.0))
    s_win = [_dot(gather_seq(q_heads, s), ck_ref[s]) for s in range(n_seqs)]
    k_b = k.astype(BF16)
    v_b = v.astype(BF16)
    s_new = [_dot_nt(q_heads[h].astype(BF16), k_b) for h in range(ATT_HEADS)]
    b = _gla_exponents(log_a, gmat_ref[0:CHUNK, :])
    yield

    b3 = b.reshape(n_seqs, dec_seq, GK_W)
    b_last = jnp.broadcast_to(b3[:, dec_seq - 1:dec_seq, :], b3.shape).reshape(CHUNK, GK_W)
    q_dec = gq * jnp.exp(b)
    k_dec_t = jnp.transpose(gk * jnp.exp(b_last - b))
    decay_t = jnp.exp(jnp.transpose(b_last))
    scores = _gla_intra(gq, gk, log_a, gmat_ref[CHUNK:, :], levels, masks_ref)
    lane_k = lax.broadcasted_iota(jnp.int32, (dec_seq, GK_W), 1)
    lane_t = lax.broadcasted_iota(jnp.int32, (GLA_DK, LANES), 1)
    o_inter = []
    for s in range(n_seqs):
        qd = seq_rows(q_dec, s)
        lhs = jnp.concatenate(
            [jnp.where((lane_k >= GLA_DK * h) & (lane_k < GLA_DK * (h + 1)), qd, 0.0)
             for h in range(GLA_HEADS)], axis=0)
        o_inter.append(_dot(lhs, st_ref[s]))
    g_gla = g_gla_ref[...]
    gla_cols = []
    for h in range(GLA_HEADS):
        cs = slice(GLA_DV * h, GLA_DV * (h + 1))
        v_h = gv[:, cs].astype(BF16)
        o_h = _dot(scores[:, CHUNK * h:CHUNK * (h + 1)].astype(BF16), v_h) + scatter_heads(o_inter, h)
        gla_cols.append(_gla_out_gate(o_h, gr[:, cs], g_gla).astype(BF16))
    yield

    bias_win = bias_win_ref[...]
    bias_new = bias_new_ref[...]
    p_win = []
    p_new = []
    inv_den = []
    for h in range(ATT_HEADS):
        sw = scatter_heads(s_win, h) + bias_win
        sn = s_new[h] + bias_new
        sink = sinks_ref[h]
        m = jnp.maximum(jnp.maximum(jnp.max(sw, axis=-1, keepdims=True),
                                    jnp.max(sn, axis=-1, keepdims=True)), sink)
        pw = jnp.exp(sw - m)
        pn = jnp.exp(sn - m)
        den = (jnp.sum(pw, axis=-1, keepdims=True) + jnp.sum(pn, axis=-1, keepdims=True)
               + jnp.exp(sink - m))
        p_win.append(pw)
        p_new.append(pn.astype(BF16))
        inv_den.append(1.0 / den)
    o_new = [_dot(p_new[h], v_b) for h in range(ATT_HEADS)]
    o_win = [_dot_nt(gather_seq(p_win, s), cv_ref[s]) for s in range(n_seqs)]
    for h in range(GLA_HEADS):
        hs = slice(GLA_DK * h, GLA_DK * (h + 1))
        kd_h = k_dec_t[hs, :]
        lhs = jnp.concatenate(
            [jnp.where((lane_t >= dec_seq * s) & (lane_t < dec_seq * (s + 1)), kd_h, 0.0).astype(BF16)
             for s in range(n_seqs)], axis=0)
        upd = _dot(lhs, gv[:, GLA_DV * h:GLA_DV * (h + 1)].astype(BF16))
        for s in range(n_seqs):
            a_col = jnp.broadcast_to(decay_t[hs, dec_seq * s:dec_seq * s + 1], (GLA_DK, GLA_DV))
            sout_ref[s, hs, :] = a_col * st_ref[s, hs, :] + upd[GLA_DK * s:GLA_DK * (s + 1)]
    o_heads = [(scatter_heads(o_win, h) + o_new[h]) * inv_den[h] for h in range(ATT_HEADS)]
    mix_cols = []
    for j in range(ATT_W // LANES):
        g = (2 * j) // (ATT_HEADS // ATT_KV_HEADS)
        first = o_heads[2 * j] if g == 0 else pltpu.roll(o_heads[2 * j], HEAD_DIM, 1)
        second = o_heads[2 * j + 1] if g == 1 else pltpu.roll(o_heads[2 * j + 1], HEAD_DIM, 1)
        mix_cols.append(jnp.where(lo_lane, first, second).astype(BF16))
    results.update(x=x, mix=jnp.concatenate(mix_cols + gla_cols, axis=1))


def _gla_constants(seg, levels):
    t = np.arange(CHUNK)[:, None]
    u = np.arange(CHUNK)[None, :]
    same = (t // seg) == (u // seg)
    blocks = [same & (u <= t)]
    masks = [np.eye(CHUNK, dtype=bool)]
    for m in levels:
        ref = (t // (2 * m)) * (2 * m) + m - 1
        second = (t & m) != 0
        blocks.append(np.where(second, (u > ref) & (u <= t), (u > t) & (u <= ref)))
        masks.append(((t // (2 * m)) == (u // (2 * m))) & second & ((u & m) == 0))
    masks.append(blocks[0])
    gmat = jnp.asarray(np.concatenate(blocks, axis=0).astype(np.float32), dtype=BF16)
    return gmat, jnp.asarray(np.stack(masks).astype(np.float32))


def _attention_bias():
    t = (np.arange(2 * CHUNK) % CHUNK)[:, None]
    j = (np.arange(4 * CHUNK) % (2 * CHUNK))[None, :]
    band = (j >= t + 1) & (j <= t + WINDOW)
    first = band & (j >= CHUNK)
    neg = np.float32(-np.inf)
    return jnp.asarray(np.stack([np.where(band, np.float32(0), neg), np.where(first, np.float32(0), neg)]))


def _sample_attention_bias(dec_seq):
    r = np.arange(CHUNK)[:, None]
    c = np.arange(CHUNK)[None, :]
    t = r % dec_seq
    win = c > t
    new = ((r // dec_seq) == (c // dec_seq)) & ((c % dec_seq) <= t)
    neg = np.float32(-np.inf)
    zero = np.float32(0)
    return jnp.asarray(np.where(win, zero, neg)), jnp.asarray(np.where(new, zero, neg))


def _rope_tables(pos):
    half = HEAD_DIM // 2
    inv = ROPE_THETA ** (-jnp.arange(half, dtype=F32) / half)
    ang = inv[:, None] * pos.astype(F32)[None, :]
    return jnp.cos(ang), jnp.sin(ang)


def _const_spec(shape):
    zeros = (0,) * len(shape)
    return pl.BlockSpec(shape, lambda *_: zeros, pipeline_mode=pl.Buffered(1))


def kernel(x_prompt, x_sample, cache_k, cache_v, state_gla, w_in, w_gk2, b_gk, g_gla, sinks, w_out, g_mix_pre, g_mix_post, g_ffn_pre, g_ffn_post, w_up, w_down):
    depth = w_in.shape[0]
    assert depth == 1, "single trunk layer"
    batch, seq, d_model = x_prompt.shape
    dec_batch, dec_seq, _ = x_sample.shape
    d_ff = w_up.shape[2]
    assert seq % PROMPT_TILE == 0 and dec_batch % SAMPLE_SEQS == 0
    assert SAMPLE_SEQS * dec_seq == CHUNK and CHUNK % dec_seq == 0 and d_ff % FF_CHUNK == 0
    assert w_in.shape[2] == MAIN_W + GLA_RANK

    w_in_b = w_in[0].astype(BF16)
    w_gk2_b = w_gk2[0].astype(BF16)
    w_out_b = w_out[0].astype(BF16)
    w_up_b = w_up[0].astype(BF16)
    w_down_b = w_down[0].astype(BF16)
    g_pre = g_mix_pre[0][None, :]
    g_post = g_mix_post[0][None, :]
    g_fpre = g_ffn_pre[0][None, :]
    g_fpost = g_ffn_post[0][None, :]
    g_gla2 = g_gla[0][None, :]
    b_gk2 = b_gk[0][None, :]
    sinks1 = sinks[0]

    weights = (g_pre, g_post, g_fpre, g_fpost, g_gla2, b_gk2, w_in_b, w_gk2_b, w_out_b, w_up_b, w_down_b)
    weight_specs = [_const_spec(w.shape) for w in weights]
    smem_spec = pl.BlockSpec(memory_space=pltpu.SMEM)

    cos_p, sin_p = _rope_tables(jnp.arange(seq, dtype=jnp.int32))
    gmat_p, masks_p = _gla_constants(CHUNK, PROMPT_LEVELS)
    bias = _attention_bias()
    n_t = seq // PROMPT_TILE
    tile = PROMPT_TILE
    n_tiles = batch * n_t

    def cur_tile(j):
        return jnp.minimum(j, n_tiles - 1)

    def mix_tile(j):
        return jnp.clip(j - 1, 0, n_tiles - 1)

    def done_tile(j):
        return jnp.clip(j - 2, 0, n_tiles - 1)

    tab_spec = pl.BlockSpec((HEAD_DIM // 2, tile), lambda j: (0, cur_tile(j) % n_t))
    seq_out = lambda j: (cur_tile(j) // n_t, 0, 0)
    yp, kp, vp, sp = pl.pallas_call(
        functools.partial(_prompt_kernel, tiles_per_seq=n_t, n_tiles=n_tiles),
        grid=(n_tiles + 2,),
        in_specs=[smem_spec,
                  pl.BlockSpec((1, tile, d_model), lambda j: (cur_tile(j) // n_t, cur_tile(j) % n_t, 0)),
                  pl.BlockSpec((1, tile, d_model), lambda j: (done_tile(j) // n_t, done_tile(j) % n_t, 0)),
                  tab_spec, tab_spec,
                  _const_spec(bias.shape), _const_spec(gmat_p.shape), _const_spec(masks_p.shape)] + weight_specs,
        out_specs=[pl.BlockSpec((1, tile, d_model), lambda j: (done_tile(j) // n_t, done_tile(j) % n_t, 0)),
                   pl.BlockSpec((1, WINDOW, KV_W), seq_out),
                   pl.BlockSpec((1, WINDOW, KV_W), seq_out),
                   pl.BlockSpec((1, GK_W, GLA_DV), lambda j: (mix_tile(j) // n_t, 0, 0))],
        out_shape=[jax.ShapeDtypeStruct((batch, seq, d_model), F32),
                   jax.ShapeDtypeStruct((batch, WINDOW, KV_W), F32),
                   jax.ShapeDtypeStruct((batch, WINDOW, KV_W), F32),
                   jax.ShapeDtypeStruct((batch, GK_W, GLA_DV), F32)],
        scratch_shapes=[pltpu.VMEM((tile, ATT_W), BF16),
                        pltpu.VMEM((2 * ATT_KV_HEADS, CHUNK + tile, LANES), BF16),
                        pltpu.VMEM((2 * ATT_KV_HEADS, CHUNK + tile, LANES), BF16),
                        pltpu.VMEM((tile, GK_W), F32),
                        pltpu.VMEM((tile, GK_W), F32),
                        pltpu.VMEM((tile, GV_W), BF16),
                        pltpu.VMEM((tile, GV_W), F32),
                        pltpu.VMEM((tile, GK_W), F32),
                        pltpu.VMEM((GK_W, GLA_DV), F32),
                        pltpu.VMEM((2, tile, d_model), BF16),
                        pltpu.SMEM((1,), jnp.int32)],
        compiler_params=pltpu.CompilerParams(
            dimension_semantics=("arbitrary",), vmem_limit_bytes=VMEM_LIMIT_BYTES),
        name="prompt_layer",
    )(sinks1, x_prompt, x_prompt, cos_p, sin_p, bias, gmat_p, masks_p, *weights)

    levels_s = tuple(m for m in PROMPT_LEVELS if m < dec_seq)
    pos_s = PAST_LEN + jnp.arange(dec_seq, dtype=jnp.int32)
    cos_s, sin_s = (jnp.tile(a, (1, SAMPLE_SEQS)) for a in _rope_tables(pos_s))
    gmat_s, masks_s = _gla_constants(dec_seq, levels_s)
    bias_win, bias_new = _sample_attention_bias(dec_seq)
    rows = SAMPLE_SEQS * dec_seq
    n_g = dec_batch // SAMPLE_SEQS
    xs = x_sample.reshape(dec_batch * dec_seq, d_model)
    ck = jnp.transpose(cache_k[0].reshape(dec_batch, WINDOW, KV_W), (0, 2, 1))
    cv = jnp.transpose(cache_v[0].reshape(dec_batch, WINDOW, KV_W), (0, 2, 1))
    st = state_gla[0].reshape(dec_batch, GK_W, GLA_DV)
    seq_spec = pl.BlockSpec((SAMPLE_SEQS, WINDOW, KV_W), lambda i: (i, 0, 0))
    st_spec = pl.BlockSpec((SAMPLE_SEQS, GK_W, GLA_DV), lambda i: (i, 0, 0))
    row_spec = pl.BlockSpec((rows, d_model), lambda i: (i, 0))
    ys, ks, vs, ss = pl.pallas_call(
        functools.partial(_sample_kernel, dec_seq=dec_seq, levels=levels_s),
        grid=(n_g,),
        in_specs=[smem_spec, row_spec,
                  _const_spec(cos_s.shape), _const_spec(sin_s.shape),
                  _const_spec(gmat_s.shape), _const_spec(masks_s.shape),
                  _const_spec(bias_win.shape), _const_spec(bias_new.shape),
                  seq_spec, seq_spec, st_spec] + weight_specs,
        out_specs=[row_spec, seq_spec, seq_spec, st_spec],
        out_shape=[jax.ShapeDtypeStruct((dec_batch * dec_seq, d_model), F32),
                   jax.ShapeDtypeStruct((dec_batch, WINDOW, KV_W), F32),
                   jax.ShapeDtypeStruct((dec_batch, WINDOW, KV_W), F32),
                   jax.ShapeDtypeStruct((dec_batch, GK_W, GLA_DV), F32)],
        compiler_params=pltpu.CompilerParams(
            dimension_semantics=("arbitrary",), vmem_limit_bytes=VMEM_LIMIT_BYTES),
        name="sample_layer",
    )(sinks1, xs, cos_s, sin_s, gmat_s, masks_s, bias_win, bias_new, ck, cv, st, *weights)

    head_shape = (ATT_KV_HEADS, HEAD_DIM)
    state_shape = (GLA_HEADS, GLA_DK, GLA_DV)
    return (yp,
            ys.reshape(dec_batch, dec_seq, d_model),
            jnp.transpose(kp, (0, 2, 1)).reshape((1, batch, WINDOW) + head_shape),
            jnp.transpose(vp, (0, 2, 1)).reshape((1, batch, WINDOW) + head_shape),
            sp.reshape((1, batch) + state_shape),
            jnp.transpose(ks, (0, 2, 1)).reshape((1, dec_batch, WINDOW) + head_shape),
            jnp.transpose(vs, (0, 2, 1)).reshape((1, dec_batch, WINDOW) + head_shape),
            ss.reshape((1, dec_batch) + state_shape))
```

```python
import functools

import numpy as np
import jax
import jax.numpy as jnp
from jax import lax
from jax.experimental import pallas as pl
from jax.experimental.pallas import tpu as pltpu

F32 = jnp.float32
BF16 = jnp.bfloat16

HEAD_DIM = 64
ATT_HEADS = 8
ATT_KV_HEADS = 2
WINDOW = 128
PAST_LEN = 16384
ROPE_THETA = 10000.0
GLA_HEADS = 4
GLA_DK = 64
GLA_DV = 128
GLA_RANK = 16
GLA_GATE_TEMP = 16.0
EPS = 1e-6
ATT_W = ATT_HEADS * HEAD_DIM
KV_W = ATT_KV_HEADS * HEAD_DIM
GK_W = GLA_HEADS * GLA_DK
GV_W = GLA_HEADS * GLA_DV
COL_Q = 0
COL_K = COL_Q + ATT_W
COL_V = COL_K + KV_W
COL_GQ = COL_V + KV_W
COL_GK = COL_GQ + GK_W
COL_GV = COL_GK + GK_W
COL_GR = COL_GV + GV_W
MAIN_W = COL_GR + GV_W
QK_SCALE = HEAD_DIM ** -0.5
GLA_SCALE = GLA_DK ** -0.5

CHUNK = 128
LANES = 128
PROMPT_TILE = 256
SAMPLE_SEQS = 16
FF_CHUNK = 1024
PROMPT_LEVELS = (1, 2, 4, 8, 16, 32, 64)
BOUNDED_LOG_DECAY = 30.0
VMEM_LIMIT_BYTES = 56 * 1024 * 1024


def _dot(a, b):
    return jnp.dot(a, b, preferred_element_type=F32)


def _dot_nt(a, b):
    return lax.dot_general(a, b, (((1,), (1,)), ((), ())), preferred_element_type=F32)


def _rms(x, g):
    ms = jnp.mean(x * x, axis=-1, keepdims=True)
    return x * lax.rsqrt(ms + EPS) * g


def _rope(x, cos, sin_a, sin_b):
    return x * cos + pltpu.roll(x, 96, 1) * sin_a + pltpu.roll(x, 32, 1) * sin_b


def _rope_lanes(cos, sin):
    reps = LANES // cos.shape[0]
    cos_l = jnp.transpose(jnp.concatenate([cos] * reps, axis=0))
    sin_l = jnp.transpose(jnp.concatenate([sin] * reps, axis=0))
    lane = lax.broadcasted_iota(jnp.int32, cos_l.shape, 1)
    first_half = (lane & (HEAD_DIM // 2)) == 0
    return cos_l, jnp.where(first_half, -sin_l, 0.0), jnp.where(first_half, 0.0, sin_l)


def _split3(x):
    hi = x.astype(BF16)
    r1 = x - hi.astype(F32)
    mid = r1.astype(BF16)
    lo = (r1 - mid.astype(F32)).astype(BF16)
    return hi, mid, lo


def _project(x, g_pre, w_in_ref):
    h = _rms(x, g_pre).astype(BF16)
    return _dot(h, w_in_ref[:, 0:MAIN_W]), _dot(h, w_in_ref[:, MAIN_W:])


def _split_projection(proj, glr, cos_ref, sin_ref, w_gk2_ref, b_gk_ref):
    cos, sin_a, sin_b = _rope_lanes(cos_ref[...], sin_ref[...])
    z = _dot(glr.astype(BF16), w_gk2_ref[...]) + b_gk_ref[...]
    log_a = (jnp.minimum(z, 0.0) - jnp.log1p(jnp.exp(-jnp.abs(z)))) * (1.0 / GLA_GATE_TEMP)
    q = jnp.concatenate(
        [_rope(proj[:, COL_Q + LANES * j:COL_Q + LANES * (j + 1)], cos, sin_a, sin_b) * QK_SCALE
         for j in range(ATT_W // LANES)], axis=1)
    k = _rope(proj[:, COL_K:COL_V], cos, sin_a, sin_b)
    v = proj[:, COL_V:COL_GQ]
    gq = proj[:, COL_GQ:COL_GK] * GLA_SCALE
    gk = proj[:, COL_GK:COL_GV]
    gv = proj[:, COL_GV:COL_GR]
    gr = proj[:, COL_GR:MAIN_W]
    return q, k, v, gq, gk, gv, gr, log_a


def _output_stage(x_prev, mix_prev, g_post_ref, g_fpre_ref, g_fpost_ref, w_out_ref, w_up_ref, w_down_ref, write):
    mix = _dot(mix_prev[...], w_out_ref[...])
    yield
    x1 = x_prev[...] + _rms(mix, g_post_ref[...])
    h2 = _rms(x1, g_fpre_ref[...]).astype(BF16)
    f = None
    for i in range(w_up_ref.shape[1] // FF_CHUNK):
        u = jnp.maximum(_dot(h2, w_up_ref[:, FF_CHUNK * i:FF_CHUNK * (i + 1)]), 0.0)
        yield
        part = _dot((u * u).astype(BF16), w_down_ref[FF_CHUNK * i:FF_CHUNK * (i + 1), :])
        f = part if f is None else f + part
        yield
    write(x1 + _rms(f, g_fpost_ref[...]))


def _lo_hi_forms(a):
    lo = lax.broadcasted_iota(jnp.int32, a.shape, 1) < HEAD_DIM
    r = pltpu.roll(a, HEAD_DIM, 1)
    zero = jnp.zeros_like(a)
    return ((jnp.where(lo, a, zero), jnp.where(lo, zero, r)),
            (jnp.where(lo, r, zero), jnp.where(lo, zero, a)))


def _gla_exponents(log_a, gmat):
    hi, mid, lo = _split3(log_a)
    return _dot(gmat, hi) + _dot(gmat, mid) + _dot(gmat, lo)


def _head_lanes(h, rows):
    lane = lax.broadcasted_iota(jnp.int32, (rows, LANES), 1)
    e = h % 2
    return (lane >= GLA_DK * e) & (lane < GLA_DK * (e + 1))


def _gla_intra(gq, gk, log_a, gmat_lv, levels, masks_ref):
    expo = _gla_exponents(log_a, gmat_lv)
    row = lax.broadcasted_iota(jnp.int32, gq.shape, 0)
    terms = [(gq.astype(BF16), gk.astype(BF16), 0)]
    for li, m in enumerate(levels):
        ex = jnp.exp(expo[CHUNK * li:CHUNK * (li + 1)])
        z = (jnp.where((row & m) != 0, gq, gk) * ex).astype(BF16)
        terms.append((z, z, 1 + li))
    scores = []
    for h in range(GLA_HEADS):
        p = h // 2
        in_head = _head_lanes(h, gq.shape[0])
        acc = None
        for zq, zk, mi in terms:
            zq_p = zq[:, LANES * p:LANES * (p + 1)]
            zk_p = zk[:, LANES * p:LANES * (p + 1)]
            zk_h = jnp.where(in_head, zk_p, jnp.zeros_like(zk_p))
            t = _dot_nt(zq_p, zk_h) * masks_ref[mi]
            acc = t if acc is None else acc + t
        scores.append(acc)
    return jnp.concatenate(scores, axis=1)


def _gla_intra_bounded(q_dec, gk, b, causal):
    k_grow_t = jnp.transpose(gk * jnp.exp(-b)).astype(BF16)
    scores = []
    for h in range(GLA_HEADS):
        p = h // 2
        qd_p = q_dec[:, LANES * p:LANES * (p + 1)]
        qd_h = jnp.where(_head_lanes(h, q_dec.shape[0]), qd_p, jnp.zeros_like(qd_p))
        scores.append(_dot(qd_h, k_grow_t[LANES * p:LANES * (p + 1), :]) * causal)
    return jnp.concatenate(scores, axis=1)


def _gla_out_gate(o_h, gr_h, g_gla):
    on = _rms(o_h, g_gla)
    return on * (gr_h * (1.0 / (1.0 + jnp.exp(-gr_h))))


def _prompt_kernel(sinks_ref, x_ref, xo_ref, cos_ref, sin_ref, bias_ref, gmat_ref, masks_ref,
                   g_pre_ref, g_post_ref, g_fpre_ref, g_fpost_ref, g_gla_ref, b_gk_ref,
                   w_in_ref, w_gk2_ref, w_out_ref, w_up_ref, w_down_ref,
                   y_ref, kwin_ref, vwin_ref, sout_ref,
                   q_scr, kbd_scr, vbd_scr, gq_scr, gk_scr, gv_scr, gr_scr, la_scr, s_scr,
                   mix_scr, bounded_scr, *, tiles_per_seq, n_tiles):
    j = pl.program_id(0)
    t_step = lax.rem(jnp.clip(j - 1, 0, n_tiles - 1), tiles_per_seq)
    slot = lax.rem(j, 2)
    tile = x_ref.shape[1]
    n_chunks = tile // CHUNK

    @pl.when(j == 0)
    def _():
        for ref in (s_scr, q_scr, kbd_scr, vbd_scr, gq_scr, gk_scr, gv_scr, gr_scr, la_scr,
                    mix_scr):
            ref[...] = jnp.zeros_like(ref)
        bounded_scr[0] = 1

    branch = jnp.where(j == 0, 3, jnp.where(j == n_tiles + 1, 2, jnp.where(bounded_scr[0] == 1, 0, 1)))

    lo_lane = lax.broadcasted_iota(jnp.int32, (CHUNK, LANES), 1) < HEAD_DIM
    g_gla = g_gla_ref[...]

    def input_stage(results):
        x = x_ref[0]
        proj, glr = _project(x, g_pre_ref[...], w_in_ref)
        yield
        q, k, v, gq, gk, gv, gr, log_a = _split_projection(proj, glr, cos_ref, sin_ref, w_gk2_ref, b_gk_ref)
        chunk_decay = jnp.concatenate(
            [jnp.sum(log_a[CHUNK * c:CHUNK * (c + 1)], axis=0, keepdims=True) for c in range(n_chunks)], axis=0)
        results.update(
            q=q.astype(BF16), k=k, v=v, k_forms=_lo_hi_forms(k), v_forms=_lo_hi_forms(v),
            gq=gq, gk=gk, gv=gv.astype(BF16), gr=gr, log_a=log_a,
            bounded=jnp.min(chunk_decay) >= -BOUNDED_LOG_DECAY)
        yield

    def hand_over(results):
        kbd_scr[:, 0:CHUNK, :] = kbd_scr[:, tile:tile + CHUNK, :]
        vbd_scr[:, 0:CHUNK, :] = vbd_scr[:, tile:tile + CHUNK, :]
        for g in range(ATT_KV_HEADS):
            for e in range(2):
                kbd_scr[2 * g + e, CHUNK:CHUNK + tile, :] = results["k_forms"][g][e].astype(BF16)
                vbd_scr[2 * g + e, CHUNK:CHUNK + tile, :] = results["v_forms"][g][e].astype(BF16)
        q_scr[...] = results["q"]
        gq_scr[...] = results["gq"]
        gk_scr[...] = results["gk"]
        gv_scr[...] = results["gv"]
        gr_scr[...] = results["gr"]
        la_scr[...] = results["log_a"]
        bounded_scr[0] = jnp.where(results["bounded"], 1, 0)
        kwin_ref[0] = jnp.transpose(results["k"][tile - WINDOW:tile, :])
        vwin_ref[0] = jnp.transpose(results["v"][tile - WINDOW:tile, :])

    def attention_chunk(c):
        rows = slice(CHUNK * c, CHUNK * (c + 1))
        rows2 = slice(CHUNK * c, CHUNK * (c + 2))
        bias = bias_ref[jnp.where(t_step == 0, 1, 0)] if c == 0 else bias_ref[0]
        scores = []
        for g in range(ATT_KV_HEADS):
            lhs = jnp.concatenate([q_scr[rows, 2 * LANES * g:2 * LANES * g + LANES],
                                   q_scr[rows, 2 * LANES * g + LANES:2 * LANES * (g + 1)]], axis=0)
            kcat = jnp.concatenate([kbd_scr[2 * g, rows2, :], kbd_scr[2 * g + 1, rows2, :]], axis=0)
            scores.append(_dot_nt(lhs, kcat) + bias)
        yield
        for g in range(ATT_KV_HEADS):
            s = scores[g]
            p_blocks = []
            inv = []
            for blk in range(2):
                p_row = []
                inv_row = []
                for e in range(2):
                    sub = s[CHUNK * blk:CHUNK * (blk + 1), 2 * CHUNK * e:2 * CHUNK * (e + 1)]
                    sink = sinks_ref[4 * g + 2 * blk + e]
                    m = jnp.maximum(jnp.max(sub, axis=-1, keepdims=True), sink)
                    p = jnp.exp(sub - m)
                    den = jnp.sum(p, axis=-1, keepdims=True) + jnp.exp(sink - m)
                    p_row.append(p.astype(BF16))
                    inv_row.append(1.0 / den)
                p_blocks.append(jnp.concatenate(p_row, axis=1))
                inv.append(inv_row)
            p_all = jnp.concatenate(p_blocks, axis=0)
            vcat = jnp.concatenate([vbd_scr[2 * g, rows2, :], vbd_scr[2 * g + 1, rows2, :]], axis=0)
            o = _dot(p_all, vcat)
            for blk in range(2):
                scale = jnp.where(lo_lane, inv[blk][0], inv[blk][1])
                c0 = 2 * LANES * g + LANES * blk
                mix_scr[slot, rows, c0:c0 + LANES] = (o[CHUNK * blk:CHUNK * (blk + 1)] * scale).astype(BF16)
        yield

    def gla_chunk(c, state, bounded_decay):
        rows = slice(CHUNK * c, CHUNK * (c + 1))
        gq_c = gq_scr[rows, :]
        gk_c = gk_scr[rows, :]
        la_c = la_scr[rows, :]
        b = _gla_exponents(la_c, gmat_ref[0:CHUNK, :])
        b_last = jnp.broadcast_to(b[CHUNK - 1:CHUNK, :], b.shape)
        q_dec_f = gq_c * jnp.exp(b)
        q_dec = q_dec_f.astype(BF16)
        k_dec_t = jnp.transpose(gk_c * jnp.exp(b_last - b)).astype(BF16)
        decay_t = jnp.exp(jnp.transpose(b_last))
        yield
        if bounded_decay:
            scores = _gla_intra_bounded(q_dec_f, gk_c, b, masks_ref[len(PROMPT_LEVELS) + 1])
        else:
            scores = _gla_intra(gq_c, gk_c, la_c, gmat_ref[CHUNK:, :], PROMPT_LEVELS, masks_ref)
        kv = [_dot(k_dec_t[GLA_DK * h:GLA_DK * (h + 1), :], gv_scr[rows, GLA_DV * h:GLA_DV * (h + 1)])
              for h in range(GLA_HEADS)]
        yield
        for h in range(GLA_HEADS):
            p_ = h // 2
            v_h = gv_scr[rows, GLA_DV * h:GLA_DV * (h + 1)]
            qd_p = q_dec[:, LANES * p_:LANES * (p_ + 1)]
            qd_h = jnp.where(_head_lanes(h, CHUNK), qd_p, jnp.zeros_like(qd_p))
            s0_pair = jnp.concatenate([state[2 * p_], state[2 * p_ + 1]], axis=0).astype(BF16)
            o_h = _dot(scores[:, CHUNK * h:CHUNK * (h + 1)].astype(BF16), v_h) + _dot(qd_h, s0_pair)
            out_h = _gla_out_gate(o_h, gr_scr[rows, GLA_DV * h:GLA_DV * (h + 1)], g_gla)
            mix_scr[slot, rows, ATT_W + GLA_DV * h:ATT_W + GLA_DV * (h + 1)] = out_h.astype(BF16)
        state[:] = [decay_t[GLA_DK * h:GLA_DK * (h + 1), :] * state[h] + kv[h] for h in range(GLA_HEADS)]
        yield

    def write_y(y):
        y_ref[0] = y

    def output_stage():
        return _output_stage(xo_ref.at[0], mix_scr.at[1 - slot], g_post_ref, g_fpre_ref, g_fpost_ref,
                             w_out_ref, w_up_ref, w_down_ref, write_y)

    def run_step(bounded_decay):
        keep = jnp.where(t_step == 0, 0.0, 1.0)
        state = [s_scr[GLA_DK * h:GLA_DK * (h + 1), :] * keep for h in range(GLA_HEADS)]
        results = {}
        out = output_stage()
        inp = input_stage(results)
        att = [attention_chunk(c) for c in range(n_chunks)]
        gla = [gla_chunk(c, state, bounded_decay) for c in range(n_chunks)]
        next(out)
        for c in range(n_chunks):
            next(att[c])
            next(gla[c])
        next(inp)
        spread = [(gla[c], att[c]) for c in range(n_chunks)]
        spread += [(gla[c],) for c in range(n_chunks)]
        spread.append((inp,))
        for stage in spread:
            next(out)
            for gen in stage:
                next(gen)
        for _ in out:
            pass
        s_all = jnp.concatenate(state, axis=0)
        s_scr[...] = s_all
        sout_ref[0] = s_all
        hand_over(results)

    def finish_last_tile():
        for _ in output_stage():
            pass

    def start_first_tile():
        results = {}
        for _ in input_stage(results):
            pass
        hand_over(results)

    lax.switch(branch, [lambda: run_step(True), lambda: run_step(False), finish_last_tile, start_first_tile])


def _sample_kernel(sinks_ref, x_ref, cos_ref, sin_ref, gmat_ref, masks_ref, bias_win_ref, bias_new_ref,
                   ck_ref, cv_ref, st_ref,
                   g_pre_ref, g_post_ref, g_fpre_ref, g_fpost_ref, g_gla_ref, b_gk_ref,
                   w_in_ref, w_gk2_ref, w_out_ref, w_up_ref, w_down_ref,
                   y_ref, kwin_ref, vwin_ref, sout_ref,
                   *, dec_seq, levels):
    results = {}
    for _ in _sample_tile_stages(sinks_ref, x_ref, cos_ref, sin_ref, gmat_ref, masks_ref, bias_win_ref,
                                 bias_new_ref, ck_ref, cv_ref, st_ref, g_pre_ref, g_gla_ref, b_gk_ref,
                                 w_in_ref, w_gk2_ref, kwin_ref, vwin_ref, sout_ref, results,
                                 dec_seq=dec_seq, levels=levels):
        pass

    def write_y(y):
        y_ref[...] = y

    for _ in _output_stage(results["x"], results["mix"], g_post_ref, g_fpre_ref, g_fpost_ref,
                           w_out_ref, w_up_ref, w_down_ref, write_y):
        pass


def _sample_tile_stages(sinks_ref, x_ref, cos_ref, sin_ref, gmat_ref, masks_ref, bias_win_ref, bias_new_ref,
                        ck_ref, cv_ref, st_ref, g_pre_ref, g_gla_ref, b_gk_ref, w_in_ref, w_gk2_ref,
                        kwin_ref, vwin_ref, sout_ref, results, *, dec_seq, levels):
    n_seqs = ck_ref.shape[0]
    x = x_ref[...]
    proj, glr = _project(x, g_pre_ref[...], w_in_ref)
    yield
    q, k, v, gq, gk, gv, gr, log_a = _split_projection(proj, glr, cos_ref, sin_ref, w_gk2_ref, b_gk_ref)

    is_new = lax.broadcasted_iota(jnp.int32, (KV_W, WINDOW), 1) >= WINDOW - dec_seq
    for new, old_ref, out_ref in ((jnp.transpose(k), ck_ref, kwin_ref), (jnp.transpose(v), cv_ref, vwin_ref)):
        for s in range(n_seqs):
            shifted = pltpu.roll(old_ref[s], WINDOW - dec_seq, 1)
            tail = pltpu.roll(new, (WINDOW - dec_seq * (s + 1)) % CHUNK, 1)
            out_ref[s] = jnp.where(is_new, tail, shifted)

    def seq_rows(a, s):
        return a[dec_seq * s:dec_seq * (s + 1)]

    def gather_seq(per_head, s):
        return jnp.concatenate([seq_rows(a, s) for a in per_head], axis=0)

    def scatter_heads(per_seq, h):
        return jnp.concatenate([seq_rows(a, h) for a in per_seq], axis=0)

    lo_lane = lax.broadcasted_iota(jnp.int32, (CHUNK, LANES), 1) < HEAD_DIM
    q_heads = []
    for j in range(ATT_W // LANES):
        blk = q[:, LANES * j:LANES * (j + 1)]
        rolled = pltpu.roll(blk, HEAD_DIM, 1)
        g = (2 * j) // (ATT_HEADS // ATT_KV_HEADS)
        keep = lo_lane if g == 0 else jnp.logical_not(lo_lane)
        for e in range(2):
            q_heads.append(jnp.where(keep, blk if e == g else rolled, 0.0))
    s_win = [_dot(gather_seq(q_heads, s), ck_ref[s]) for s in range(n_seqs)]
    k_b = k.astype(BF16)
    v_b = v.astype(BF16)
    s_new = [_dot_nt(q_heads[h].astype(BF16), k_b) for h in range(ATT_HEADS)]
    b = _gla_exponents(log_a, gmat_ref[0:CHUNK, :])
    yield

    b3 = b.reshape(n_seqs, dec_seq, GK_W)
    b_last = jnp.broadcast_to(b3[:, dec_seq - 1:dec_seq, :], b3.shape).reshape(CHUNK, GK_W)
    q_dec = gq * jnp.exp(b)
    k_dec_t = jnp.transpose(gk * jnp.exp(b_last - b))
    decay_t = jnp.exp(jnp.transpose(b_last))
    scores = _gla_intra(gq, gk, log_a, gmat_ref[CHUNK:, :], levels, masks_ref)
    lane_k = lax.broadcasted_iota(jnp.int32, (dec_seq, GK_W), 1)
    lane_t = lax.broadcasted_iota(jnp.int32, (GLA_DK, LANES), 1)
    o_inter = []
    for s in range(n_seqs):
        qd = seq_rows(q_dec, s)
        lhs = jnp.concatenate(
            [jnp.where((lane_k >= GLA_DK * h) & (lane_k < GLA_DK * (h + 1)), qd, 0.0)
             for h in range(GLA_HEADS)], axis=0)
        o_inter.append(_dot(lhs, st_ref[s]))
    g_gla = g_gla_ref[...]
    gla_cols = []
    for h in range(GLA_HEADS):
        cs = slice(GLA_DV * h, GLA_DV * (h + 1))
        v_h = gv[:, cs].astype(BF16)
        o_h = _dot(scores[:, CHUNK * h:CHUNK * (h + 1)].astype(BF16), v_h) + scatter_heads(o_inter, h)
        gla_cols.append(_gla_out_gate(o_h, gr[:, cs], g_gla).astype(BF16))
    yield

    bias_win = bias_win_ref[...]
    bias_new = bias_new_ref[...]
    p_win = []
    p_new = []
    inv_den = []
    for h in range(ATT_HEADS):
        sw = scatter_heads(s_win, h) + bias_win
        sn = s_new[h] + bias_new
        sink = sinks_ref[h]
        m = jnp.maximum(jnp.maximum(jnp.max(sw, axis=-1, keepdims=True),
                                    jnp.max(sn, axis=-1, keepdims=True)), sink)
        pw = jnp.exp(sw - m)
        pn = jnp.exp(sn - m)
        den = (jnp.sum(pw, axis=-1, keepdims=True) + jnp.sum(pn, axis=-1, keepdims=True)
               + jnp.exp(sink - m))
        p_win.append(pw)
        p_new.append(pn.astype(BF16))
        inv_den.append(1.0 / den)
    o_new = [_dot(p_new[h], v_b) for h in range(ATT_HEADS)]
    o_win = [_dot_nt(gather_seq(p_win, s), cv_ref[s]) for s in range(n_seqs)]
    for h in range(GLA_HEADS):
        hs = slice(GLA_DK * h, GLA_DK * (h + 1))
        kd_h = k_dec_t[hs, :]
        lhs = jnp.concatenate(
            [jnp.where((lane_t >= dec_seq * s) & (lane_t < dec_seq * (s + 1)), kd_h, 0.0).astype(BF16)
             for s in range(n_seqs)], axis=0)
        upd = _dot(lhs, gv[:, GLA_DV * h:GLA_DV * (h + 1)].astype(BF16))
        for s in range(n_seqs):
            a_col = jnp.broadcast_to(decay_t[hs, dec_seq * s:dec_seq * s + 1], (GLA_DK, GLA_DV))
            sout_ref[s, hs, :] = a_col * st_ref[s, hs, :] + upd[GLA_DK * s:GLA_DK * (s + 1)]
    o_heads = [(scatter_heads(o_win, h) + o_new[h]) * inv_den[h] for h in range(ATT_HEADS)]
    mix_cols = []
    for j in range(ATT_W // LANES):
        g = (2 * j) // (ATT_HEADS // ATT_KV_HEADS)
        first = o_heads[2 * j] if g == 0 else pltpu.roll(o_heads[2 * j], HEAD_DIM, 1)
        second = o_heads[2 * j + 1] if g == 1 else pltpu.roll(o_heads[2 * j + 1], HEAD_DIM, 1)
        mix_cols.append(jnp.where(lo_lane, first, second).astype(BF16))
    results.update(x=x, mix=jnp.concatenate(mix_cols + gla_cols, axis=1))


def _gla_constants(seg, levels):
    t = np.arange(CHUNK)[:, None]
    u = np.arange(CHUNK)[None, :]
    same = (t // seg) == (u // seg)
    blocks = [same & (u <= t)]
    masks = [np.eye(CHUNK, dtype=bool)]
    for m in levels:
        ref = (t // (2 * m)) * (2 * m) + m - 1
        second = (t & m) != 0
        blocks.append(np.where(second, (u > ref) & (u <= t), (u > t) & (u <= ref)))
        masks.append(((t // (2 * m)) == (u // (2 * m))) & second & ((u & m) == 0))
    masks.append(blocks[0])
    gmat = jnp.asarray(np.concatenate(blocks, axis=0).astype(np.float32), dtype=BF16)
    return gmat, jnp.asarray(np.stack(masks).astype(np.float32))


def _attention_bias():
    t = (np.arange(2 * CHUNK) % CHUNK)[:, None]
    j = (np.arange(4 * CHUNK) % (2 * CHUNK))[None, :]
    band = (j >= t + 1) & (j <= t + WINDOW)
    first = band & (j >= CHUNK)
    neg = np.float32(-np.inf)
    return jnp.asarray(np.stack([np.where(band, np.float32(0), neg), np.where(first, np.float32(0), neg)]))


def _sample_attention_bias(dec_seq):
    r = np.arange(CHUNK)[:, None]
    c = np.arange(CHUNK)[None, :]
    t = r % dec_seq
    win = c > t
    new = ((r // dec_seq) == (c // dec_seq)) & ((c % dec_seq) <= t)
    neg = np.float32(-np.inf)
    zero = np.float32(0)
    return jnp.asarray(np.where(win, zero, neg)), jnp.asarray(np.where(new, zero, neg))


def _rope_tables(pos):
    half = HEAD_DIM // 2
    inv = ROPE_THETA ** (-jnp.arange(half, dtype=F32) / half)
    ang = inv[:, None] * pos.astype(F32)[None, :]
    return jnp.cos(ang), jnp.sin(ang)


def _const_spec(shape):
    zeros = (0,) * len(shape)
    return pl.BlockSpec(shape, lambda *_: zeros, pipeline_mode=pl.Buffered(1))


def kernel(x_prompt, x_sample, cache_k, cache_v, state_gla, w_in, w_gk2, b_gk, g_gla, sinks, w_out, g_mix_pre, g_mix_post, g_ffn_pre, g_ffn_post, w_up, w_down):
    depth = w_in.shape[0]
    assert depth == 1, "single trunk layer"
    batch, seq, d_model = x_prompt.shape
    dec_batch, dec_seq, _ = x_sample.shape
    d_ff = w_up.shape[2]
    assert seq % PROMPT_TILE == 0 and dec_batch % SAMPLE_SEQS == 0
    assert SAMPLE_SEQS * dec_seq == CHUNK and CHUNK % dec_seq == 0 and d_ff % FF_CHUNK == 0
    assert w_in.shape[2] == MAIN_W + GLA_RANK

    w_in_b = w_in[0].astype(BF16)
    w_gk2_b = w_gk2[0].astype(BF16)
    w_out_b = w_out[0].astype(BF16)
    w_up_b = w_up[0].astype(BF16)
    w_down_b = w_down[0].astype(BF16)
    g_pre = g_mix_pre[0][None, :]
    g_post = g_mix_post[0][None, :]
    g_fpre = g_ffn_pre[0][None, :]
    g_fpost = g_ffn_post[0][None, :]
    g_gla2 = g_gla[0][None, :]
    b_gk2 = b_gk[0][None, :]
    sinks1 = sinks[0]

    weights = (g_pre, g_post, g_fpre, g_fpost, g_gla2, b_gk2, w_in_b, w_gk2_b, w_out_b, w_up_b, w_down_b)
    weight_specs = [_const_spec(w.shape) for w in weights]
    smem_spec = pl.BlockSpec(memory_space=pltpu.SMEM)

    cos_p, sin_p = _rope_tables(jnp.arange(seq, dtype=jnp.int32))
    gmat_p, masks_p = _gla_constants(CHUNK, PROMPT_LEVELS)
    bias = _attention_bias()
    n_t = seq // PROMPT_TILE
    tile = PROMPT_TILE
    n_tiles = batch * n_t

    def cur_tile(j):
        return jnp.minimum(j, n_tiles - 1)

    def mix_tile(j):
        return jnp.clip(j - 1, 0, n_tiles - 1)

    def done_tile(j):
        return jnp.clip(j - 2, 0, n_tiles - 1)

    tab_spec = pl.BlockSpec((HEAD_DIM // 2, tile), lambda j: (0, cur_tile(j) % n_t))
    seq_out = lambda j: (cur_tile(j) // n_t, 0, 0)
    yp, kp, vp, sp = pl.pallas_call(
        functools.partial(_prompt_kernel, tiles_per_seq=n_t, n_tiles=n_tiles),
        grid=(n_tiles + 2,),
        in_specs=[smem_spec,
                  pl.BlockSpec((1, tile, d_model), lambda j: (cur_tile(j) // n_t, cur_tile(j) % n_t, 0)),
                  pl.BlockSpec((1, tile, d_model), lambda j: (done_tile(j) // n_t, done_tile(j) % n_t, 0)),
                  tab_spec, tab_spec,
                  _const_spec(bias.shape), _const_spec(gmat_p.shape), _const_spec(masks_p.shape)] + weight_specs,
        out_specs=[pl.BlockSpec((1, tile, d_model), lambda j: (done_tile(j) // n_t, done_tile(j) % n_t, 0)),
                   pl.BlockSpec((1, WINDOW, KV_W), seq_out),
                   pl.BlockSpec((1, WINDOW, KV_W), seq_out),
                   pl.BlockSpec((1, GK_W, GLA_DV), lambda j: (mix_tile(j) // n_t, 0, 0))],
        out_shape=[jax.ShapeDtypeStruct((batch, seq, d_model), F32),
                   jax.ShapeDtypeStruct((batch, WINDOW, KV_W), F32),
                   jax.ShapeDtypeStruct((batch, WINDOW, KV_W), F32),
                   jax.ShapeDtypeStruct((batch, GK_W, GLA_DV), F32)],
        scratch_shapes=[pltpu.VMEM((tile, ATT_W), BF16),
                        pltpu.VMEM((2 * ATT_KV_HEADS, CHUNK + tile, LANES), BF16),
                        pltpu.VMEM((2 * ATT_KV_HEADS, CHUNK + tile, LANES), BF16),
                        pltpu.VMEM((tile, GK_W), F32),
                        pltpu.VMEM((tile, GK_W), F32),
                        pltpu.VMEM((tile, GV_W), BF16),
                        pltpu.VMEM((tile, GV_W), F32),
                        pltpu.VMEM((tile, GK_W), F32),
                        pltpu.VMEM((GK_W, GLA_DV), F32),
                        pltpu.VMEM((2, tile, d_model), BF16),
                        pltpu.SMEM((1,), jnp.int32)],
        compiler_params=pltpu.CompilerParams(
            dimension_semantics=("arbitrary",), vmem_limit_bytes=VMEM_LIMIT_BYTES),
        name="prompt_layer",
    )(sinks1, x_prompt, x_prompt, cos_p, sin_p, bias, gmat_p, masks_p, *weights)

    levels_s = tuple(m for m in PROMPT_LEVELS if m < dec_seq)
    pos_s = PAST_LEN + jnp.arange(dec_seq, dtype=jnp.int32)
    cos_s, sin_s = (jnp.tile(a, (1, SAMPLE_SEQS)) for a in _rope_tables(pos_s))
    gmat_s, masks_s = _gla_constants(dec_seq, levels_s)
    bias_win, bias_new = _sample_attention_bias(dec_seq)
    rows = SAMPLE_SEQS * dec_seq
    n_g = dec_batch // SAMPLE_SEQS
    xs = x_sample.reshape(dec_batch * dec_seq, d_model)
    ck = jnp.transpose(cache_k[0].reshape(dec_batch, WINDOW, KV_W), (0, 2, 1))
    cv = jnp.transpose(cache_v[0].reshape(dec_batch, WINDOW, KV_W), (0, 2, 1))
    st = state_gla[0].reshape(dec_batch, GK_W, GLA_DV)
    seq_spec = pl.BlockSpec((SAMPLE_SEQS, WINDOW, KV_W), lambda i: (i, 0, 0))
    st_spec = pl.BlockSpec((SAMPLE_SEQS, GK_W, GLA_DV), lambda i: (i, 0, 0))
    row_spec = pl.BlockSpec((rows, d_model), lambda i: (i, 0))
    ys, ks, vs, ss = pl.pallas_call(
        functools.partial(_sample_kernel, dec_seq=dec_seq, levels=levels_s),
        grid=(n_g,),
        in_specs=[smem_spec, row_spec,
                  _const_spec(cos_s.shape), _const_spec(sin_s.shape),
                  _const_spec(gmat_s.shape), _const_spec(masks_s.shape),
                  _const_spec(bias_win.shape), _const_spec(bias_new.shape),
                  seq_spec, seq_spec, st_spec] + weight_specs,
        out_specs=[row_spec, seq_spec, seq_spec, st_spec],
        out_shape=[jax.ShapeDtypeStruct((dec_batch * dec_seq, d_model), F32),
                   jax.ShapeDtypeStruct((dec_batch, WINDOW, KV_W), F32),
                   jax.ShapeDtypeStruct((dec_batch, WINDOW, KV_W), F32),
                   jax.ShapeDtypeStruct((dec_batch, GK_W, GLA_DV), F32)],
        compiler_params=pltpu.CompilerParams(
            dimension_semantics=("arbitrary",), vmem_limit_bytes=VMEM_LIMIT_BYTES),
        name="sample_layer",
    )(sinks1, xs, cos_s, sin_s, gmat_s, masks_s, bias_win, bias_new, ck, cv, st, *weights)

    head_shape = (ATT_KV_HEADS, HEAD_DIM)
    state_shape = (GLA_HEADS, GLA_DK, GLA_DV)
    return (yp,
            ys.reshape(dec_batch, dec_seq, d_model),
            jnp.transpose(kp, (0, 2, 1)).reshape((1, batch, WINDOW) + head_shape),
            jnp.transpose(vp, (0, 2, 1)).reshape((1, batch, WINDOW) + head_shape),
            sp.reshape((1, batch) + state_shape),
            jnp.transpose(ks, (0, 2, 1)).reshape((1, dec_batch, WINDOW) + head_shape),
            jnp.transpose(vs, (0, 2, 1)).reshape((1, dec_batch, WINDOW) + head_shape),
            ss.reshape((1, dec_batch) + state_shape))
```

```python
import functools

import numpy as np
import jax
import jax.numpy as jnp
from jax import lax
from jax.experimental import pallas as pl
from jax.experimental.pallas import tpu as pltpu

F32 = jnp.float32
BF16 = jnp.bfloat16

HEAD_DIM = 64
ATT_HEADS = 8
ATT_KV_HEADS = 2
WINDOW = 128
PAST_LEN = 16384
ROPE_THETA = 10000.0
GLA_HEADS = 4
GLA_DK = 64
GLA_DV = 128
GLA_RANK = 16
GLA_GATE_TEMP = 16.0
EPS = 1e-6
ATT_W = ATT_HEADS * HEAD_DIM
KV_W = ATT_KV_HEADS * HEAD_DIM
GK_W = GLA_HEADS * GLA_DK
GV_W = GLA_HEADS * GLA_DV
COL_Q = 0
COL_K = COL_Q + ATT_W
COL_V = COL_K + KV_W
COL_GQ = COL_V + KV_W
COL_GK = COL_GQ + GK_W
COL_GV = COL_GK + GK_W
COL_GR = COL_GV + GV_W
MAIN_W = COL_GR + GV_W
QK_SCALE = HEAD_DIM ** -0.5
GLA_SCALE = GLA_DK ** -0.5

CHUNK = 128
LANES = 128
PROMPT_TILE = 256
SAMPLE_SEQS = 16
FF_CHUNK = 1024
PROMPT_LEVELS = (1, 2, 4, 8, 16, 32, 64)
BOUNDED_LOG_DECAY = 30.0
WEIGHT_STAGE_ELEMS = 1 << 18
VMEM_LIMIT_BYTES = 56 * 1024 * 1024


def _dot(a, b):
    return jnp.dot(a, b, preferred_element_type=F32)


def _dot_nt(a, b):
    return lax.dot_general(a, b, (((1,), (1,)), ((), ())), preferred_element_type=F32)


def _rms(x, g):
    ms = jnp.mean(x * x, axis=-1, keepdims=True)
    return x * lax.rsqrt(ms + EPS) * g


def _rope(x, cos, sin_a, sin_b):
    return x * cos + pltpu.roll(x, 96, 1) * sin_a + pltpu.roll(x, 32, 1) * sin_b


def _rope_lanes(cos, sin):
    reps = LANES // cos.shape[0]
    cos_l = jnp.transpose(jnp.concatenate([cos] * reps, axis=0))
    sin_l = jnp.transpose(jnp.concatenate([sin] * reps, axis=0))
    lane = lax.broadcasted_iota(jnp.int32, cos_l.shape, 1)
    first_half = (lane & (HEAD_DIM // 2)) == 0
    return cos_l, jnp.where(first_half, -sin_l, 0.0), jnp.where(first_half, 0.0, sin_l)


def _split3(x):
    hi = x.astype(BF16)
    r1 = x - hi.astype(F32)
    mid = r1.astype(BF16)
    lo = (r1 - mid.astype(F32)).astype(BF16)
    return hi, mid, lo


def _project(x, g_pre, w_in_ref):
    h = _rms(x, g_pre).astype(BF16)
    return _dot(h, w_in_ref[:, 0:MAIN_W]), _dot(h, w_in_ref[:, MAIN_W:])


def _split_projection(proj, glr, cos_ref, sin_ref, w_gk2_ref, b_gk_ref):
    cos, sin_a, sin_b = _rope_lanes(cos_ref[...], sin_ref[...])
    z = _dot(glr.astype(BF16), w_gk2_ref[...]) + b_gk_ref[...]
    log_a = (jnp.minimum(z, 0.0) - jnp.log1p(jnp.exp(-jnp.abs(z)))) * (1.0 / GLA_GATE_TEMP)
    q = jnp.concatenate(
        [_rope(proj[:, COL_Q + LANES * j:COL_Q + LANES * (j + 1)], cos, sin_a, sin_b) * QK_SCALE
         for j in range(ATT_W // LANES)], axis=1)
    k = _rope(proj[:, COL_K:COL_V], cos, sin_a, sin_b)
    v = proj[:, COL_V:COL_GQ]
    gq = proj[:, COL_GQ:COL_GK] * GLA_SCALE
    gk = proj[:, COL_GK:COL_GV]
    gv = proj[:, COL_GV:COL_GR]
    gr = proj[:, COL_GR:MAIN_W]
    return q, k, v, gq, gk, gv, gr, log_a


def _output_stage(x_prev, mix_prev, g_post_ref, g_fpre_ref, g_fpost_ref, w_out_ref, w_up_ref, w_down_ref, write):
    mix = _dot(mix_prev[...], w_out_ref[...])
    yield
    x1 = x_prev[...] + _rms(mix, g_post_ref[...])
    h2 = _rms(x1, g_fpre_ref[...]).astype(BF16)
    f = None
    for i in range(w_up_ref.shape[1] // FF_CHUNK):
        u = jnp.maximum(_dot(h2, w_up_ref[:, FF_CHUNK * i:FF_CHUNK * (i + 1)]), 0.0)
        yield
        part = _dot((u * u).astype(BF16), w_down_ref[FF_CHUNK * i:FF_CHUNK * (i + 1), :])
        f = part if f is None else f + part
        yield
    write(x1 + _rms(f, g_fpost_ref[...]))


def _lo_hi_forms(a):
    lo = lax.broadcasted_iota(jnp.int32, a.shape, 1) < HEAD_DIM
    r = pltpu.roll(a, HEAD_DIM, 1)
    zero = jnp.zeros_like(a)
    return ((jnp.where(lo, a, zero), jnp.where(lo, zero, r)),
            (jnp.where(lo, r, zero), jnp.where(lo, zero, a)))


def _gla_exponents(log_a, gmat):
    hi, mid, lo = _split3(log_a)
    return _dot(gmat, hi) + _dot(gmat, mid) + _dot(gmat, lo)


def _head_lanes(h, rows):
    lane = lax.broadcasted_iota(jnp.int32, (rows, LANES), 1)
    e = h % 2
    return (lane >= GLA_DK * e) & (lane < GLA_DK * (e + 1))


def _gla_intra(gq, gk, log_a, gmat_lv, levels, masks_ref):
    expo = _gla_exponents(log_a, gmat_lv)
    row = lax.broadcasted_iota(jnp.int32, gq.shape, 0)
    terms = [(gq.astype(BF16), gk.astype(BF16), 0)]
    for li, m in enumerate(levels):
        ex = jnp.exp(expo[CHUNK * li:CHUNK * (li + 1)])
        z = (jnp.where((row & m) != 0, gq, gk) * ex).astype(BF16)
        terms.append((z, z, 1 + li))
    scores = []
    for h in range(GLA_HEADS):
        p = h // 2
        in_head = _head_lanes(h, gq.shape[0])
        acc = None
        for zq, zk, mi in terms:
            zq_p = zq[:, LANES * p:LANES * (p + 1)]
            zk_p = zk[:, LANES * p:LANES * (p + 1)]
            zk_h = jnp.where(in_head, zk_p, jnp.zeros_like(zk_p))
            t = _dot_nt(zq_p, zk_h) * masks_ref[mi]
            acc = t if acc is None else acc + t
        scores.append(acc)
    return jnp.concatenate(scores, axis=1)


def _gla_intra_bounded(q_dec, gk, b, causal):
    k_grow_t = jnp.transpose(gk * jnp.exp(-b)).astype(BF16)
    scores = []
    for h in range(GLA_HEADS):
        p = h // 2
        qd_p = q_dec[:, LANES * p:LANES * (p + 1)]
        qd_h = jnp.where(_head_lanes(h, q_dec.shape[0]), qd_p, jnp.zeros_like(qd_p))
        scores.append(_dot(qd_h, k_grow_t[LANES * p:LANES * (p + 1), :]) * causal)
    return jnp.concatenate(scores, axis=1)


def _gla_out_gate(o_h, gr_h, g_gla):
    on = _rms(o_h, g_gla)
    return on * (gr_h * (1.0 / (1.0 + jnp.exp(-gr_h))))


def _weight_loads(hbm_refs, stage_refs, load_sem):
    jobs = []
    for w, (hbm, stage) in enumerate(zip(hbm_refs, stage_refs)):
        n = stage.shape[1]
        for k in range(hbm.shape[0] // n):
            copy = pltpu.make_async_copy(hbm.at[pl.ds(n * k, n)], stage.at[k % 2], load_sem.at[w, k % 2])
            jobs.append((copy, w, slice(n * k, n * (k + 1)), k % 2))
    return jobs


def _weight_stores(vmem_refs, out_refs, store_sem):
    return [pltpu.make_async_copy(v, o, store_sem.at[w]) for w, (v, o) in enumerate(zip(vmem_refs, out_refs))]


def _prompt_kernel(sinks_ref, x_ref, xo_ref, cos_ref, sin_ref, bias_ref, gmat_ref, masks_ref,
                   g_pre_ref, g_post_ref, g_fpre_ref, g_fpost_ref, g_gla_ref, b_gk_ref,
                   w_in_hbm, w_gk2_ref, w_out_hbm, w_up_hbm, w_down_hbm,
                   y_ref, kwin_ref, vwin_ref, sout_ref, w_in_bf, w_out_bf, w_up_bf, w_down_bf,
                   q_scr, kbd_scr, vbd_scr, gq_scr, gk_scr, gv_scr, gr_scr, la_scr, s_scr,
                   mix_scr, bounded_scr, w_in_ref, w_out_ref, w_up_ref, w_down_ref,
                   stage_in, stage_out, stage_up, stage_down, load_sem, store_sem, *, tiles_per_seq, n_tiles):
    j = pl.program_id(0)
    t_step = lax.rem(jnp.clip(j - 1, 0, n_tiles - 1), tiles_per_seq)
    slot = lax.rem(j, 2)
    tile = x_ref.shape[1]
    n_chunks = tile // CHUNK
    hbm_weights = (w_in_hbm, w_out_hbm, w_up_hbm, w_down_hbm)
    stages = (stage_in, stage_out, stage_up, stage_down)
    vmem_weights = (w_in_ref, w_out_ref, w_up_ref, w_down_ref)
    bf_weights = (w_in_bf, w_out_bf, w_up_bf, w_down_bf)

    @pl.when(j == 0)
    def _():
        for ref in (s_scr, q_scr, kbd_scr, vbd_scr, gq_scr, gk_scr, gv_scr, gr_scr, la_scr,
                    mix_scr):
            ref[...] = jnp.zeros_like(ref)
        bounded_scr[0] = 1
        loads = _weight_loads(hbm_weights, stages, load_sem)
        loads[0][0].start()
        for i, (copy, w, rows, stage_slot) in enumerate(loads):
            if i + 1 < len(loads):
                loads[i + 1][0].start()
            copy.wait()
            vmem_weights[w][rows, :] = stages[w][stage_slot].astype(BF16)
        for copy in _weight_stores(vmem_weights, bf_weights, store_sem):
            copy.start()

    @pl.when(j == n_tiles + 1)
    def _():
        for copy in _weight_stores(vmem_weights, bf_weights, store_sem):
            copy.wait()

    branch = jnp.where(j == 0, 3, jnp.where(j == n_tiles + 1, 2, jnp.where(bounded_scr[0] == 1, 0, 1)))

    lo_lane = lax.broadcasted_iota(jnp.int32, (CHUNK, LANES), 1) < HEAD_DIM
    g_gla = g_gla_ref[...]

    def input_stage(results):
        x = x_ref[0]
        proj, glr = _project(x, g_pre_ref[...], w_in_ref)
        yield
        q, k, v, gq, gk, gv, gr, log_a = _split_projection(proj, glr, cos_ref, sin_ref, w_gk2_ref, b_gk_ref)
        chunk_decay = jnp.concatenate(
            [jnp.sum(log_a[CHUNK * c:CHUNK * (c + 1)], axis=0, keepdims=True) for c in range(n_chunks)], axis=0)
        results.update(
            q=q.astype(BF16), k=k, v=v, k_forms=_lo_hi_forms(k), v_forms=_lo_hi_forms(v),
            gq=gq, gk=gk, gv=gv.astype(BF16), gr=gr, log_a=log_a,
            bounded=jnp.min(chunk_decay) >= -BOUNDED_LOG_DECAY)
        yield

    def hand_over(results):
        kbd_scr[:, 0:CHUNK, :] = kbd_scr[:, tile:tile + CHUNK, :]
        vbd_scr[:, 0:CHUNK, :] = vbd_scr[:, tile:tile + CHUNK, :]
        for g in range(ATT_KV_HEADS):
            for e in range(2):
                kbd_scr[2 * g + e, CHUNK:CHUNK + tile, :] = results["k_forms"][g][e].astype(BF16)
                vbd_scr[2 * g + e, CHUNK:CHUNK + tile, :] = results["v_forms"][g][e].astype(BF16)
        q_scr[...] = results["q"]
        gq_scr[...] = results["gq"]
        gk_scr[...] = results["gk"]
        gv_scr[...] = results["gv"]
        gr_scr[...] = results["gr"]
        la_scr[...] = results["log_a"]
        bounded_scr[0] = jnp.where(results["bounded"], 1, 0)
        kwin_ref[0] = jnp.transpose(results["k"][tile - WINDOW:tile, :])
        vwin_ref[0] = jnp.transpose(results["v"][tile - WINDOW:tile, :])

    def attention_chunk(c):
        rows = slice(CHUNK * c, CHUNK * (c + 1))
        rows2 = slice(CHUNK * c, CHUNK * (c + 2))
        bias = bias_ref[jnp.where(t_step == 0, 1, 0)] if c == 0 else bias_ref[0]
        scores = []
        for g in range(ATT_KV_HEADS):
            lhs = jnp.concatenate([q_scr[rows, 2 * LANES * g:2 * LANES * g + LANES],
                                   q_scr[rows, 2 * LANES * g + LANES:2 * LANES * (g + 1)]], axis=0)
            kcat = jnp.concatenate([kbd_scr[2 * g, rows2, :], kbd_scr[2 * g + 1, rows2, :]], axis=0)
            scores.append(_dot_nt(lhs, kcat) + bias)
        yield
        for g in range(ATT_KV_HEADS):
            s = scores[g]
            p_blocks = []
            inv = []
            for blk in range(2):
                p_row = []
                inv_row = []
                for e in range(2):
                    sub = s[CHUNK * blk:CHUNK * (blk + 1), 2 * CHUNK * e:2 * CHUNK * (e + 1)]
                    sink = sinks_ref[4 * g + 2 * blk + e]
                    m = jnp.maximum(jnp.max(sub, axis=-1, keepdims=True), sink)
                    p = jnp.exp(sub - m)
                    den = jnp.sum(p, axis=-1, keepdims=True) + jnp.exp(sink - m)
                    p_row.append(p.astype(BF16))
                    inv_row.append(1.0 / den)
                p_blocks.append(jnp.concatenate(p_row, axis=1))
                inv.append(inv_row)
            p_all = jnp.concatenate(p_blocks, axis=0)
            vcat = jnp.concatenate([vbd_scr[2 * g, rows2, :], vbd_scr[2 * g + 1, rows2, :]], axis=0)
            o = _dot(p_all, vcat)
            for blk in range(2):
                scale = jnp.where(lo_lane, inv[blk][0], inv[blk][1])
                c0 = 2 * LANES * g + LANES * blk
                mix_scr[slot, rows, c0:c0 + LANES] = (o[CHUNK * blk:CHUNK * (blk + 1)] * scale).astype(BF16)
        yield

    def gla_chunk(c, state, bounded_decay):
        rows = slice(CHUNK * c, CHUNK * (c + 1))
        gq_c = gq_scr[rows, :]
        gk_c = gk_scr[rows, :]
        la_c = la_scr[rows, :]
        b = _gla_exponents(la_c, gmat_ref[0:CHUNK, :])
        b_last = jnp.broadcast_to(b[CHUNK - 1:CHUNK, :], b.shape)
        q_dec_f = gq_c * jnp.exp(b)
        q_dec = q_dec_f.astype(BF16)
        k_dec_t = jnp.transpose(gk_c * jnp.exp(b_last - b)).astype(BF16)
        decay_t = jnp.exp(jnp.transpose(b_last))
        yield
        if bounded_decay:
            scores = _gla_intra_bounded(q_dec_f, gk_c, b, masks_ref[len(PROMPT_LEVELS) + 1])
        else:
            scores = _gla_intra(gq_c, gk_c, la_c, gmat_ref[CHUNK:, :], PROMPT_LEVELS, masks_ref)
        kv = [_dot(k_dec_t[GLA_DK * h:GLA_DK * (h + 1), :], gv_scr[rows, GLA_DV * h:GLA_DV * (h + 1)])
              for h in range(GLA_HEADS)]
        yield
        for h in range(GLA_HEADS):
            p_ = h // 2
            v_h = gv_scr[rows, GLA_DV * h:GLA_DV * (h + 1)]
            qd_p = q_dec[:, LANES * p_:LANES * (p_ + 1)]
            qd_h = jnp.where(_head_lanes(h, CHUNK), qd_p, jnp.zeros_like(qd_p))
            s0_pair = jnp.concatenate([state[2 * p_], state[2 * p_ + 1]], axis=0).astype(BF16)
            o_h = _dot(scores[:, CHUNK * h:CHUNK * (h + 1)].astype(BF16), v_h) + _dot(qd_h, s0_pair)
            out_h = _gla_out_gate(o_h, gr_scr[rows, GLA_DV * h:GLA_DV * (h + 1)], g_gla)
            mix_scr[slot, rows, ATT_W + GLA_DV * h:ATT_W + GLA_DV * (h + 1)] = out_h.astype(BF16)
        state[:] = [decay_t[GLA_DK * h:GLA_DK * (h + 1), :] * state[h] + kv[h] for h in range(GLA_HEADS)]
        yield

    def write_y(y):
        y_ref[0] = y

    def output_stage():
        return _output_stage(xo_ref.at[0], mix_scr.at[1 - slot], g_post_ref, g_fpre_ref, g_fpost_ref,
                             w_out_ref, w_up_ref, w_down_ref, write_y)

    def run_step(bounded_decay):
        keep = jnp.where(t_step == 0, 0.0, 1.0)
        state = [s_scr[GLA_DK * h:GLA_DK * (h + 1), :] * keep for h in range(GLA_HEADS)]
        results = {}
        out = output_stage()
        inp = input_stage(results)
        att = [attention_chunk(c) for c in range(n_chunks)]
        gla = [gla_chunk(c, state, bounded_decay) for c in range(n_chunks)]
        next(out)
        for c in range(n_chunks):
            next(att[c])
            next(gla[c])
        next(inp)
        spread = [(gla[c], att[c]) for c in range(n_chunks)]
        spread += [(gla[c],) for c in range(n_chunks)]
        spread.append((inp,))
        for stage in spread:
            next(out)
            for gen in stage:
                next(gen)
        for _ in out:
            pass
        s_all = jnp.concatenate(state, axis=0)
        s_scr[...] = s_all
        sout_ref[0] = s_all
        hand_over(results)

    def finish_last_tile():
        for _ in output_stage():
            pass

    def start_first_tile():
        results = {}
        for _ in input_stage(results):
            pass
        hand_over(results)

    lax.switch(branch, [lambda: run_step(True), lambda: run_step(False), finish_last_tile, start_first_tile])


def _sample_kernel(sinks_ref, x_ref, cos_ref, sin_ref, gmat_ref, masks_ref, bias_win_ref, bias_new_ref,
                   ck_ref, cv_ref, st_ref,
                   g_pre_ref, g_post_ref, g_fpre_ref, g_fpost_ref, g_gla_ref, b_gk_ref,
                   w_in_ref, w_gk2_ref, w_out_ref, w_up_ref, w_down_ref,
                   y_ref, kwin_ref, vwin_ref, sout_ref,
                   *, dec_seq, levels):
    results = {}
    for _ in _sample_tile_stages(sinks_ref, x_ref, cos_ref, sin_ref, gmat_ref, masks_ref, bias_win_ref,
                                 bias_new_ref, ck_ref, cv_ref, st_ref, g_pre_ref, g_gla_ref, b_gk_ref,
                                 w_in_ref, w_gk2_ref, kwin_ref, vwin_ref, sout_ref, results,
                                 dec_seq=dec_seq, levels=levels):
        pass

    def write_y(y):
        y_ref[...] = y

    for _ in _output_stage(results["x"], results["mix"], g_post_ref, g_fpre_ref, g_fpost_ref,
                           w_out_ref, w_up_ref, w_down_ref, write_y):
        pass


def _sample_tile_stages(sinks_ref, x_ref, cos_ref, sin_ref, gmat_ref, masks_ref, bias_win_ref, bias_new_ref,
                        ck_ref, cv_ref, st_ref, g_pre_ref, g_gla_ref, b_gk_ref, w_in_ref, w_gk2_ref,
                        kwin_ref, vwin_ref, sout_ref, results, *, dec_seq, levels):
    n_seqs = ck_ref.shape[0]
    x = x_ref[...]
    proj, glr = _project(x, g_pre_ref[...], w_in_ref)
    yield
    q, k, v, gq, gk, gv, gr, log_a = _split_projection(proj, glr, cos_ref, sin_ref, w_gk2_ref, b_gk_ref)

    is_new = lax.broadcasted_iota(jnp.int32, (KV_W, WINDOW), 1) >= WINDOW - dec_seq
    for new, old_ref, out_ref in ((jnp.transpose(k), ck_ref, kwin_ref), (jnp.transpose(v), cv_ref, vwin_ref)):
        for s in range(n_seqs):
            shifted = pltpu.roll(old_ref[s], WINDOW - dec_seq, 1)
            tail = pltpu.roll(new, (WINDOW - dec_seq * (s + 1)) % CHUNK, 1)
            out_ref[s] = jnp.where(is_new, tail, shifted)

    def seq_rows(a, s):
        return a[dec_seq * s:dec_seq * (s + 1)]

    def gather_seq(per_head, s):
        return jnp.concatenate([seq_rows(a, s) for a in per_head], axis=0)

    def scatter_heads(per_seq, h):
        return jnp.concatenate([seq_rows(a, h) for a in per_seq], axis=0)

    lo_lane = lax.broadcasted_iota(jnp.int32, (CHUNK, LANES), 1) < HEAD_DIM
    q_heads = []
    for j in range(ATT_W // LANES):
        blk = q[:, LANES * j:LANES * (j + 1)]
        rolled = pltpu.roll(blk, HEAD_DIM, 1)
        g = (2 * j) // (ATT_HEADS // ATT_KV_HEADS)
        keep = lo_lane if g == 0 else jnp.logical_not(lo_lane)
        for e in range(2):
            q_heads.append(jnp.where(keep, blk if e == g else rolled, 0.0))
    s_win = [_dot(gather_seq(q_heads, s), ck_ref[s]) for s in range(n_seqs)]
    k_b = k.astype(BF16)
    v_b = v.astype(BF16)
    s_new = [_dot_nt(q_heads[h].astype(BF16), k_b) for h in range(ATT_HEADS)]
    b = _gla_exponents(log_a, gmat_ref[0:CHUNK, :])
    yield

    b3 = b.reshape(n_seqs, dec_seq, GK_W)
    b_last = jnp.broadcast_to(b3[:, dec_seq - 1:dec_seq, :], b3.shape).reshape(CHUNK, GK_W)
    q_dec = gq * jnp.exp(b)
    k_dec_t = jnp.transpose(gk * jnp.exp(b_last - b))
    decay_t = jnp.exp(jnp.transpose(b_last))
    scores = _gla_intra(gq, gk, log_a, gmat_ref[CHUNK:, :], levels, masks_ref)
    lane_k = lax.broadcasted_iota(jnp.int32, (dec_seq, GK_W), 1)
    lane_t = lax.broadcasted_iota(jnp.int32, (GLA_DK, LANES), 1)
    o_inter = []
    for s in range(n_seqs):
        qd = seq_rows(q_dec, s)
        lhs = jnp.concatenate(
            [jnp.where((lane_k >= GLA_DK * h) & (lane_k < GLA_DK * (h + 1)), qd, 0.0)
             for h in range(GLA_HEADS)], axis=0)
        o_inter.append(_dot(lhs, st_ref[s]))
    g_gla = g_gla_ref[...]
    gla_cols = []
    for h in range(GLA_HEADS):
        cs = slice(GLA_DV * h, GLA_DV * (h + 1))
        v_h = gv[:, cs].astype(BF16)
        o_h = _dot(scores[:, CHUNK * h:CHUNK * (h + 1)].astype(BF16), v_h) + scatter_heads(o_inter, h)
        gla_cols.append(_gla_out_gate(o_h, gr[:, cs], g_gla).astype(BF16))
    yield

    bias_win = bias_win_ref[...]
    bias_new = bias_new_ref[...]
    p_win = []
    p_new = []
    inv_den = []
    for h in range(ATT_HEADS):
        sw = scatter_heads(s_win, h) + bias_win
        sn = s_new[h] + bias_new
        sink = sinks_ref[h]
        m = jnp.maximum(jnp.maximum(jnp.max(sw, axis=-1, keepdims=True),
                                    jnp.max(sn, axis=-1, keepdims=True)), sink)
        pw = jnp.exp(sw - m)
        pn = jnp.exp(sn - m)
        den = (jnp.sum(pw, axis=-1, keepdims=True) + jnp.sum(pn, axis=-1, keepdims=True)
               + jnp.exp(sink - m))
        p_win.append(pw)
        p_new.append(pn.astype(BF16))
        inv_den.append(1.0 / den)
    o_new = [_dot(p_new[h], v_b) for h in range(ATT_HEADS)]
    o_win = [_dot_nt(gather_seq(p_win, s), cv_ref[s]) for s in range(n_seqs)]
    for h in range(GLA_HEADS):
        hs = slice(GLA_DK * h, GLA_DK * (h + 1))
        kd_h = k_dec_t[hs, :]
        lhs = jnp.concatenate(
            [jnp.where((lane_t >= dec_seq * s) & (lane_t < dec_seq * (s + 1)), kd_h, 0.0).astype(BF16)
             for s in range(n_seqs)], axis=0)
        upd = _dot(lhs, gv[:, GLA_DV * h:GLA_DV * (h + 1)].astype(BF16))
        for s in range(n_seqs):
            a_col = jnp.broadcast_to(decay_t[hs, dec_seq * s:dec_seq * s + 1], (GLA_DK, GLA_DV))
            sout_ref[s, hs, :] = a_col * st_ref[s, hs, :] + upd[GLA_DK * s:GLA_DK * (s + 1)]
    o_heads = [(scatter_heads(o_win, h) + o_new[h]) * inv_den[h] for h in range(ATT_HEADS)]
    mix_cols = []
    for j in range(ATT_W // LANES):
        g = (2 * j) // (ATT_HEADS // ATT_KV_HEADS)
        first = o_heads[2 * j] if g == 0 else pltpu.roll(o_heads[2 * j], HEAD_DIM, 1)
        second = o_heads[2 * j + 1] if g == 1 else pltpu.roll(o_heads[2 * j + 1], HEAD_DIM, 1)
        mix_cols.append(jnp.where(lo_lane, first, second).astype(BF16))
    results.update(x=x, mix=jnp.concatenate(mix_cols + gla_cols, axis=1))


def _gla_constants(seg, levels):
    t = np.arange(CHUNK)[:, None]
    u = np.arange(CHUNK)[None, :]
    same = (t // seg) == (u // seg)
    blocks = [same & (u <= t)]
    masks = [np.eye(CHUNK, dtype=bool)]
    for m in levels:
        ref = (t // (2 * m)) * (2 * m) + m - 1
        second = (t & m) != 0
        blocks.append(np.where(second, (u > ref) & (u <= t), (u > t) & (u <= ref)))
        masks.append(((t // (2 * m)) == (u // (2 * m))) & second & ((u & m) == 0))
    masks.append(blocks[0])
    gmat = jnp.asarray(np.concatenate(blocks, axis=0).astype(np.float32), dtype=BF16)
    return gmat, jnp.asarray(np.stack(masks).astype(np.float32))


def _attention_bias():
    t = (np.arange(2 * CHUNK) % CHUNK)[:, None]
    j = (np.arange(4 * CHUNK) % (2 * CHUNK))[None, :]
    band = (j >= t + 1) & (j <= t + WINDOW)
    first = band & (j >= CHUNK)
    neg = np.float32(-np.inf)
    return jnp.asarray(np.stack([np.where(band, np.float32(0), neg), np.where(first, np.float32(0), neg)]))


def _sample_attention_bias(dec_seq):
    r = np.arange(CHUNK)[:, None]
    c = np.arange(CHUNK)[None, :]
    t = r % dec_seq
    win = c > t
    new = ((r // dec_seq) == (c // dec_seq)) & ((c % dec_seq) <= t)
    neg = np.float32(-np.inf)
    zero = np.float32(0)
    return jnp.asarray(np.where(win, zero, neg)), jnp.asarray(np.where(new, zero, neg))


def _rope_tables(pos):
    half = HEAD_DIM // 2
    inv = ROPE_THETA ** (-jnp.arange(half, dtype=F32) / half)
    ang = inv[:, None] * pos.astype(F32)[None, :]
    return jnp.cos(ang), jnp.sin(ang)


def _const_spec(shape):
    zeros = (0,) * len(shape)
    return pl.BlockSpec(shape, lambda *_: zeros, pipeline_mode=pl.Buffered(1))


def kernel(x_prompt, x_sample, cache_k, cache_v, state_gla, w_in, w_gk2, b_gk, g_gla, sinks, w_out, g_mix_pre, g_mix_post, g_ffn_pre, g_ffn_post, w_up, w_down):
    depth = w_in.shape[0]
    assert depth == 1, "single trunk layer"
    batch, seq, d_model = x_prompt.shape
    dec_batch, dec_seq, _ = x_sample.shape
    d_ff = w_up.shape[2]
    assert seq % PROMPT_TILE == 0 and dec_batch % SAMPLE_SEQS == 0
    assert SAMPLE_SEQS * dec_seq == CHUNK and CHUNK % dec_seq == 0 and d_ff % FF_CHUNK == 0
    assert w_in.shape[2] == MAIN_W + GLA_RANK

    w_gk2_b = w_gk2[0].astype(BF16)
    g_pre = g_mix_pre[0][None, :]
    g_post = g_mix_post[0][None, :]
    g_fpre = g_ffn_pre[0][None, :]
    g_fpost = g_ffn_post[0][None, :]
    g_gla2 = g_gla[0][None, :]
    b_gk2 = b_gk[0][None, :]
    sinks1 = sinks[0]

    gains = (g_pre, g_post, g_fpre, g_fpost, g_gla2, b_gk2)
    smem_spec = pl.BlockSpec(memory_space=pltpu.SMEM)
    any_spec = pl.BlockSpec(memory_space=pl.ANY)
    big = (w_in[0], w_out[0], w_up[0], w_down[0])
    stage_rows = [1 << ((WEIGHT_STAGE_ELEMS // w.shape[1]).bit_length() - 1) for w in big]
    assert all(w.shape[0] % (2 * r) == 0 for w, r in zip(big, stage_rows))
    prompt_weights = gains + (big[0], w_gk2_b, big[1], big[2], big[3])
    prompt_weight_specs = ([_const_spec(w.shape) for w in gains]
                           + [any_spec, _const_spec(w_gk2_b.shape), any_spec, any_spec, any_spec])

    cos_p, sin_p = _rope_tables(jnp.arange(seq, dtype=jnp.int32))
    gmat_p, masks_p = _gla_constants(CHUNK, PROMPT_LEVELS)
    bias = _attention_bias()
    n_t = seq // PROMPT_TILE
    tile = PROMPT_TILE
    n_tiles = batch * n_t

    def cur_tile(j):
        return jnp.minimum(j, n_tiles - 1)

    def mix_tile(j):
        return jnp.clip(j - 1, 0, n_tiles - 1)

    def done_tile(j):
        return jnp.clip(j - 2, 0, n_tiles - 1)

    tab_spec = pl.BlockSpec((HEAD_DIM // 2, tile), lambda j: (0, cur_tile(j) % n_t))
    seq_out = lambda j: (cur_tile(j) // n_t, 0, 0)
    yp, kp, vp, sp, w_in_b, w_out_b, w_up_b, w_down_b = pl.pallas_call(
        functools.partial(_prompt_kernel, tiles_per_seq=n_t, n_tiles=n_tiles),
        grid=(n_tiles + 2,),
        in_specs=[smem_spec,
                  pl.BlockSpec((1, tile, d_model), lambda j: (cur_tile(j) // n_t, cur_tile(j) % n_t, 0)),
                  pl.BlockSpec((1, tile, d_model), lambda j: (done_tile(j) // n_t, done_tile(j) % n_t, 0)),
                  tab_spec, tab_spec,
                  _const_spec(bias.shape), _const_spec(gmat_p.shape), _const_spec(masks_p.shape)]
        + prompt_weight_specs,
        out_specs=[pl.BlockSpec((1, tile, d_model), lambda j: (done_tile(j) // n_t, done_tile(j) % n_t, 0)),
                   pl.BlockSpec((1, WINDOW, KV_W), seq_out),
                   pl.BlockSpec((1, WINDOW, KV_W), seq_out),
                   pl.BlockSpec((1, GK_W, GLA_DV), lambda j: (mix_tile(j) // n_t, 0, 0))] + [any_spec] * len(big),
        out_shape=[jax.ShapeDtypeStruct((batch, seq, d_model), F32),
                   jax.ShapeDtypeStruct((batch, WINDOW, KV_W), F32),
                   jax.ShapeDtypeStruct((batch, WINDOW, KV_W), F32),
                   jax.ShapeDtypeStruct((batch, GK_W, GLA_DV), F32)]
        + [jax.ShapeDtypeStruct(w.shape, BF16) for w in big],
        scratch_shapes=[pltpu.VMEM((tile, ATT_W), BF16),
                        pltpu.VMEM((2 * ATT_KV_HEADS, CHUNK + tile, LANES), BF16),
                        pltpu.VMEM((2 * ATT_KV_HEADS, CHUNK + tile, LANES), BF16),
                        pltpu.VMEM((tile, GK_W), F32),
                        pltpu.VMEM((tile, GK_W), F32),
                        pltpu.VMEM((tile, GV_W), BF16),
                        pltpu.VMEM((tile, GV_W), F32),
                        pltpu.VMEM((tile, GK_W), F32),
                        pltpu.VMEM((GK_W, GLA_DV), F32),
                        pltpu.VMEM((2, tile, d_model), BF16),
                        pltpu.SMEM((1,), jnp.int32)]
        + [pltpu.VMEM(w.shape, BF16) for w in big]
        + [pltpu.VMEM((2, r, w.shape[1]), F32) for w, r in zip(big, stage_rows)]
        + [pltpu.SemaphoreType.DMA((len(big), 2)), pltpu.SemaphoreType.DMA((len(big),))],
        compiler_params=pltpu.CompilerParams(
            dimension_semantics=("arbitrary",), vmem_limit_bytes=VMEM_LIMIT_BYTES),
        name="prompt_layer",
    )(sinks1, x_prompt, x_prompt, cos_p, sin_p, bias, gmat_p, masks_p, *prompt_weights)
    weights = gains + (w_in_b, w_gk2_b, w_out_b, w_up_b, w_down_b)
    weight_specs = [_const_spec(w.shape) for w in weights]

    levels_s = tuple(m for m in PROMPT_LEVELS if m < dec_seq)
    pos_s = PAST_LEN + jnp.arange(dec_seq, dtype=jnp.int32)
    cos_s, sin_s = (jnp.tile(a, (1, SAMPLE_SEQS)) for a in _rope_tables(pos_s))
    gmat_s, masks_s = _gla_constants(dec_seq, levels_s)
    bias_win, bias_new = _sample_attention_bias(dec_seq)
    rows = SAMPLE_SEQS * dec_seq
    n_g = dec_batch // SAMPLE_SEQS
    xs = x_sample.reshape(dec_batch * dec_seq, d_model)
    ck = jnp.transpose(cache_k[0].reshape(dec_batch, WINDOW, KV_W), (0, 2, 1))
    cv = jnp.transpose(cache_v[0].reshape(dec_batch, WINDOW, KV_W), (0, 2, 1))
    st = state_gla[0].reshape(dec_batch, GK_W, GLA_DV)
    seq_spec = pl.BlockSpec((SAMPLE_SEQS, WINDOW, KV_W), lambda i: (i, 0, 0))
    st_spec = pl.BlockSpec((SAMPLE_SEQS, GK_W, GLA_DV), lambda i: (i, 0, 0))
    row_spec = pl.BlockSpec((rows, d_model), lambda i: (i, 0))
    ys, ks, vs, ss = pl.pallas_call(
        functools.partial(_sample_kernel, dec_seq=dec_seq, levels=levels_s),
        grid=(n_g,),
        in_specs=[smem_spec, row_spec,
                  _const_spec(cos_s.shape), _const_spec(sin_s.shape),
                  _const_spec(gmat_s.shape), _const_spec(masks_s.shape),
                  _const_spec(bias_win.shape), _const_spec(bias_new.shape),
                  seq_spec, seq_spec, st_spec] + weight_specs,
        out_specs=[row_spec, seq_spec, seq_spec, st_spec],
        out_shape=[jax.ShapeDtypeStruct((dec_batch * dec_seq, d_model), F32),
                   jax.ShapeDtypeStruct((dec_batch, WINDOW, KV_W), F32),
                   jax.ShapeDtypeStruct((dec_batch, WINDOW, KV_W), F32),
                   jax.ShapeDtypeStruct((dec_batch, GK_W, GLA_DV), F32)],
        compiler_params=pltpu.CompilerParams(
            dimension_semantics=("arbitrary",), vmem_limit_bytes=VMEM_LIMIT_BYTES),
        name="sample_layer",
    )(sinks1, xs, cos_s, sin_s, gmat_s, masks_s, bias_win, bias_new, ck, cv, st, *weights)

    head_shape = (ATT_KV_HEADS, HEAD_DIM)
    state_shape = (GLA_HEADS, GLA_DK, GLA_DV)
    return (yp,
            ys.reshape(dec_batch, dec_seq, d_model),
            jnp.transpose(kp, (0, 2, 1)).reshape((1, batch, WINDOW) + head_shape),
            jnp.transpose(vp, (0, 2, 1)).reshape((1, batch, WINDOW) + head_shape),
            sp.reshape((1, batch) + state_shape),
            jnp.transpose(ks, (0, 2, 1)).reshape((1, dec_batch, WINDOW) + head_shape),
            jnp.transpose(vs, (0, 2, 1)).reshape((1, dec_batch, WINDOW) + head_shape),
            ss.reshape((1, dec_batch) + state_shape))
```

```python
import functools

import numpy as np
import jax
import jax.numpy as jnp
from jax import lax
from jax.experimental import pallas as pl
from jax.experimental.pallas import tpu as pltpu

F32 = jnp.float32
BF16 = jnp.bfloat16

HEAD_DIM = 64
ATT_HEADS = 8
ATT_KV_HEADS = 2
WINDOW = 128
PAST_LEN = 16384
ROPE_THETA = 10000.0
GLA_HEADS = 4
GLA_DK = 64
GLA_DV = 128
GLA_RANK = 16
GLA_GATE_TEMP = 16.0
EPS = 1e-6
ATT_W = ATT_HEADS * HEAD_DIM
KV_W = ATT_KV_HEADS * HEAD_DIM
GK_W = GLA_HEADS * GLA_DK
GV_W = GLA_HEADS * GLA_DV
COL_Q = 0
COL_K = COL_Q + ATT_W
COL_V = COL_K + KV_W
COL_GQ = COL_V + KV_W
COL_GK = COL_GQ + GK_W
COL_GV = COL_GK + GK_W
COL_GR = COL_GV + GV_W
MAIN_W = COL_GR + GV_W
QK_SCALE = HEAD_DIM ** -0.5
GLA_SCALE = GLA_DK ** -0.5

CHUNK = 128
LANES = 128
PROMPT_TILE = 256
SAMPLE_SEQS = 16
FF_CHUNK = 1024
PROMPT_LEVELS = (1, 2, 4, 8, 16, 32, 64)
BOUNDED_LOG_DECAY = 30.0
WEIGHT_STAGE_ELEMS = 1 << 17
WEIGHT_STAGE_SLOTS = 4
VMEM_LIMIT_BYTES = 56 * 1024 * 1024


def _dot(a, b):
    return jnp.dot(a, b, preferred_element_type=F32)


def _dot_nt(a, b):
    return lax.dot_general(a, b, (((1,), (1,)), ((), ())), preferred_element_type=F32)


def _rms(x, g):
    ms = jnp.mean(x * x, axis=-1, keepdims=True)
    return x * lax.rsqrt(ms + EPS) * g


def _rope(x, cos, sin_a, sin_b):
    return x * cos + pltpu.roll(x, 96, 1) * sin_a + pltpu.roll(x, 32, 1) * sin_b


def _rope_lanes(cos, sin):
    reps = LANES // cos.shape[0]
    cos_l = jnp.transpose(jnp.concatenate([cos] * reps, axis=0))
    sin_l = jnp.transpose(jnp.concatenate([sin] * reps, axis=0))
    lane = lax.broadcasted_iota(jnp.int32, cos_l.shape, 1)
    first_half = (lane & (HEAD_DIM // 2)) == 0
    return cos_l, jnp.where(first_half, -sin_l, 0.0), jnp.where(first_half, 0.0, sin_l)


def _split3(x):
    hi = x.astype(BF16)
    r1 = x - hi.astype(F32)
    mid = r1.astype(BF16)
    lo = (r1 - mid.astype(F32)).astype(BF16)
    return hi, mid, lo


def _project(x, g_pre, w_in_ref):
    h = _rms(x, g_pre).astype(BF16)
    return _dot(h, w_in_ref[:, 0:MAIN_W]), _dot(h, w_in_ref[:, MAIN_W:])


def _split_projection(proj, glr, cos_ref, sin_ref, w_gk2_ref, b_gk_ref):
    cos, sin_a, sin_b = _rope_lanes(cos_ref[...], sin_ref[...])
    z = _dot(glr.astype(BF16), w_gk2_ref[...]) + b_gk_ref[...]
    log_a = (jnp.minimum(z, 0.0) - jnp.log1p(jnp.exp(-jnp.abs(z)))) * (1.0 / GLA_GATE_TEMP)
    q = jnp.concatenate(
        [_rope(proj[:, COL_Q + LANES * j:COL_Q + LANES * (j + 1)], cos, sin_a, sin_b) * QK_SCALE
         for j in range(ATT_W // LANES)], axis=1)
    k = _rope(proj[:, COL_K:COL_V], cos, sin_a, sin_b)
    v = proj[:, COL_V:COL_GQ]
    gq = proj[:, COL_GQ:COL_GK] * GLA_SCALE
    gk = proj[:, COL_GK:COL_GV]
    gv = proj[:, COL_GV:COL_GR]
    gr = proj[:, COL_GR:MAIN_W]
    return q, k, v, gq, gk, gv, gr, log_a


def _output_stage(x_prev, mix_prev, g_post_ref, g_fpre_ref, g_fpost_ref, w_out_ref, w_up_ref, w_down_ref, write):
    mix = _dot(mix_prev[...], w_out_ref[...])
    yield
    x1 = x_prev[...] + _rms(mix, g_post_ref[...])
    h2 = _rms(x1, g_fpre_ref[...]).astype(BF16)
    f = None
    for i in range(w_up_ref.shape[1] // FF_CHUNK):
        u = jnp.maximum(_dot(h2, w_up_ref[:, FF_CHUNK * i:FF_CHUNK * (i + 1)]), 0.0)
        yield
        part = _dot((u * u).astype(BF16), w_down_ref[FF_CHUNK * i:FF_CHUNK * (i + 1), :])
        f = part if f is None else f + part
        yield
    write(x1 + _rms(f, g_fpost_ref[...]))


def _lo_hi_forms(a):
    lo = lax.broadcasted_iota(jnp.int32, a.shape, 1) < HEAD_DIM
    r = pltpu.roll(a, HEAD_DIM, 1)
    zero = jnp.zeros_like(a)
    return ((jnp.where(lo, a, zero), jnp.where(lo, zero, r)),
            (jnp.where(lo, r, zero), jnp.where(lo, zero, a)))


def _gla_exponents(log_a, gmat):
    hi, mid, lo = _split3(log_a)
    return _dot(gmat, hi) + _dot(gmat, mid) + _dot(gmat, lo)


def _head_lanes(h, rows):
    lane = lax.broadcasted_iota(jnp.int32, (rows, LANES), 1)
    e = h % 2
    return (lane >= GLA_DK * e) & (lane < GLA_DK * (e + 1))


def _gla_intra(gq, gk, log_a, gmat_lv, levels, masks_ref):
    expo = _gla_exponents(log_a, gmat_lv)
    row = lax.broadcasted_iota(jnp.int32, gq.shape, 0)
    terms = [(gq.astype(BF16), gk.astype(BF16), 0)]
    for li, m in enumerate(levels):
        ex = jnp.exp(expo[CHUNK * li:CHUNK * (li + 1)])
        z = (jnp.where((row & m) != 0, gq, gk) * ex).astype(BF16)
        terms.append((z, z, 1 + li))
    scores = []
    for h in range(GLA_HEADS):
        p = h // 2
        in_head = _head_lanes(h, gq.shape[0])
        acc = None
        for zq, zk, mi in terms:
            zq_p = zq[:, LANES * p:LANES * (p + 1)]
            zk_p = zk[:, LANES * p:LANES * (p + 1)]
            zk_h = jnp.where(in_head, zk_p, jnp.zeros_like(zk_p))
            t = _dot_nt(zq_p, zk_h) * masks_ref[mi]
            acc = t if acc is None else acc + t
        scores.append(acc)
    return jnp.concatenate(scores, axis=1)


def _gla_intra_bounded(q_dec, gk, b, causal):
    k_grow_t = jnp.transpose(gk * jnp.exp(-b)).astype(BF16)
    scores = []
    for h in range(GLA_HEADS):
        p = h // 2
        qd_p = q_dec[:, LANES * p:LANES * (p + 1)]
        qd_h = jnp.where(_head_lanes(h, q_dec.shape[0]), qd_p, jnp.zeros_like(qd_p))
        scores.append(_dot(qd_h, k_grow_t[LANES * p:LANES * (p + 1), :]) * causal)
    return jnp.concatenate(scores, axis=1)


def _gla_out_gate(o_h, gr_h, g_gla):
    on = _rms(o_h, g_gla)
    return on * (gr_h * (1.0 / (1.0 + jnp.exp(-gr_h))))


def _weight_loads(hbm_refs, stage_refs, load_sem):
    jobs = []
    for w, (hbm, stage) in enumerate(zip(hbm_refs, stage_refs)):
        slots, n = stage.shape[0], stage.shape[1]
        for k in range(hbm.shape[1] // n):
            copy = pltpu.make_async_copy(hbm.at[0, pl.ds(n * k, n)], stage.at[k % slots], load_sem.at[w, k % slots])
            jobs.append((copy, w, slice(n * k, n * (k + 1)), k % slots))
    return jobs


def _weight_stores(vmem_refs, out_refs, store_sem):
    return [pltpu.make_async_copy(v, o, store_sem.at[w]) for w, (v, o) in enumerate(zip(vmem_refs, out_refs))]


def _prompt_kernel(sinks_ref, x_ref, xo_ref, cos_ref, sin_ref, bias_ref, gmat_ref, masks_ref,
                   g_pre_ref, g_post_ref, g_fpre_ref, g_fpost_ref, g_gla_ref, b_gk_ref,
                   w_in_hbm, w_gk2_ref, w_out_hbm, w_up_hbm, w_down_hbm,
                   y_ref, kwin_ref, vwin_ref, sout_ref, w_in_bf, w_out_bf, w_up_bf, w_down_bf,
                   q_scr, kbd_scr, vbd_scr, gq_scr, gk_scr, gv_scr, gr_scr, la_scr, s_scr,
                   mix_scr, bounded_scr, w_in_ref, w_out_ref, w_up_ref, w_down_ref,
                   stage_in, stage_out, stage_up, stage_down, load_sem, store_sem, *, tiles_per_seq, n_tiles):
    j = pl.program_id(0)
    t_step = lax.rem(jnp.clip(j - 1, 0, n_tiles - 1), tiles_per_seq)
    slot = lax.rem(j, 2)
    tile = x_ref.shape[1]
    n_chunks = tile // CHUNK
    hbm_weights = (w_in_hbm, w_out_hbm, w_up_hbm, w_down_hbm)
    stages = (stage_in, stage_out, stage_up, stage_down)
    vmem_weights = (w_in_ref, w_out_ref, w_up_ref, w_down_ref)
    bf_weights = (w_in_bf, w_out_bf, w_up_bf, w_down_bf)

    @pl.when(j == 0)
    def _():
        for ref in (s_scr, q_scr, kbd_scr, vbd_scr, gq_scr, gk_scr, gv_scr, gr_scr, la_scr,
                    mix_scr):
            ref[...] = jnp.zeros_like(ref)
        bounded_scr[0] = 1
        loads = _weight_loads(hbm_weights, stages, load_sem)
        ahead = stage_in.shape[0] - 1
        for copy, *_ in loads[:ahead]:
            copy.start()
        for i, (copy, w, rows, stage_slot) in enumerate(loads):
            if i + ahead < len(loads):
                loads[i + ahead][0].start()
            copy.wait()
            vmem_weights[w][rows, :] = stages[w][stage_slot].astype(BF16)
        for copy in _weight_stores(vmem_weights, bf_weights, store_sem):
            copy.start()

    @pl.when(j == n_tiles + 1)
    def _():
        for copy in _weight_stores(vmem_weights, bf_weights, store_sem):
            copy.wait()

    branch = jnp.where(j == 0, 3, jnp.where(j == n_tiles + 1, 2, jnp.where(bounded_scr[0] == 1, 0, 1)))

    lo_lane = lax.broadcasted_iota(jnp.int32, (CHUNK, LANES), 1) < HEAD_DIM
    g_gla = g_gla_ref[...]

    def input_stage(results):
        x = x_ref[0]
        proj, glr = _project(x, g_pre_ref[...], w_in_ref)
        yield
        q, k, v, gq, gk, gv, gr, log_a = _split_projection(proj, glr, cos_ref, sin_ref, w_gk2_ref, b_gk_ref)
        chunk_decay = jnp.concatenate(
            [jnp.sum(log_a[CHUNK * c:CHUNK * (c + 1)], axis=0, keepdims=True) for c in range(n_chunks)], axis=0)
        results.update(
            q=q.astype(BF16), k=k, v=v, k_forms=_lo_hi_forms(k), v_forms=_lo_hi_forms(v),
            gq=gq, gk=gk, gv=gv.astype(BF16), gr=gr, log_a=log_a,
            bounded=jnp.min(chunk_decay) >= -BOUNDED_LOG_DECAY)
        yield

    def hand_over(results):
        kbd_scr[:, 0:CHUNK, :] = kbd_scr[:, tile:tile + CHUNK, :]
        vbd_scr[:, 0:CHUNK, :] = vbd_scr[:, tile:tile + CHUNK, :]
        for g in range(ATT_KV_HEADS):
            for e in range(2):
                kbd_scr[2 * g + e, CHUNK:CHUNK + tile, :] = results["k_forms"][g][e].astype(BF16)
                vbd_scr[2 * g + e, CHUNK:CHUNK + tile, :] = results["v_forms"][g][e].astype(BF16)
        q_scr[...] = results["q"]
        gq_scr[...] = results["gq"]
        gk_scr[...] = results["gk"]
        gv_scr[...] = results["gv"]
        gr_scr[...] = results["gr"]
        la_scr[...] = results["log_a"]
        bounded_scr[0] = jnp.where(results["bounded"], 1, 0)
        kwin_ref[0] = jnp.transpose(results["k"][tile - WINDOW:tile, :])
        vwin_ref[0] = jnp.transpose(results["v"][tile - WINDOW:tile, :])

    def attention_chunk(c):
        rows = slice(CHUNK * c, CHUNK * (c + 1))
        rows2 = slice(CHUNK * c, CHUNK * (c + 2))
        bias = bias_ref[jnp.where(t_step == 0, 1, 0)] if c == 0 else bias_ref[0]
        scores = []
        for g in range(ATT_KV_HEADS):
            lhs = jnp.concatenate([q_scr[rows, 2 * LANES * g:2 * LANES * g + LANES],
                                   q_scr[rows, 2 * LANES * g + LANES:2 * LANES * (g + 1)]], axis=0)
            kcat = jnp.concatenate([kbd_scr[2 * g, rows2, :], kbd_scr[2 * g + 1, rows2, :]], axis=0)
            scores.append(_dot_nt(lhs, kcat) + bias)
        yield
        for g in range(ATT_KV_HEADS):
            s = scores[g]
            p_blocks = []
            inv = []
            for blk in range(2):
                p_row = []
                inv_row = []
                for e in range(2):
                    sub = s[CHUNK * blk:CHUNK * (blk + 1), 2 * CHUNK * e:2 * CHUNK * (e + 1)]
                    sink = sinks_ref[4 * g + 2 * blk + e]
                    m = jnp.maximum(jnp.max(sub, axis=-1, keepdims=True), sink)
                    p = jnp.exp(sub - m)
                    den = jnp.sum(p, axis=-1, keepdims=True) + jnp.exp(sink - m)
                    p_row.append(p.astype(BF16))
                    inv_row.append(1.0 / den)
                p_blocks.append(jnp.concatenate(p_row, axis=1))
                inv.append(inv_row)
            p_all = jnp.concatenate(p_blocks, axis=0)
            vcat = jnp.concatenate([vbd_scr[2 * g, rows2, :], vbd_scr[2 * g + 1, rows2, :]], axis=0)
            o = _dot(p_all, vcat)
            for blk in range(2):
                scale = jnp.where(lo_lane, inv[blk][0], inv[blk][1])
                c0 = 2 * LANES * g + LANES * blk
                mix_scr[slot, rows, c0:c0 + LANES] = (o[CHUNK * blk:CHUNK * (blk + 1)] * scale).astype(BF16)
        yield

    def gla_chunk(c, state, bounded_decay):
        rows = slice(CHUNK * c, CHUNK * (c + 1))
        gq_c = gq_scr[rows, :]
        gk_c = gk_scr[rows, :]
        la_c = la_scr[rows, :]
        b = _gla_exponents(la_c, gmat_ref[0:CHUNK, :])
        b_last = jnp.broadcast_to(b[CHUNK - 1:CHUNK, :], b.shape)
        q_dec_f = gq_c * jnp.exp(b)
        q_dec = q_dec_f.astype(BF16)
        k_dec_t = jnp.transpose(gk_c * jnp.exp(b_last - b)).astype(BF16)
        decay_t = jnp.exp(jnp.transpose(b_last))
        yield
        if bounded_decay:
            scores = _gla_intra_bounded(q_dec_f, gk_c, b, masks_ref[len(PROMPT_LEVELS) + 1])
        else:
            scores = _gla_intra(gq_c, gk_c, la_c, gmat_ref[CHUNK:, :], PROMPT_LEVELS, masks_ref)
        kv = [_dot(k_dec_t[GLA_DK * h:GLA_DK * (h + 1), :], gv_scr[rows, GLA_DV * h:GLA_DV * (h + 1)])
              for h in range(GLA_HEADS)]
        yield
        for h in range(GLA_HEADS):
            p_ = h // 2
            v_h = gv_scr[rows, GLA_DV * h:GLA_DV * (h + 1)]
            qd_p = q_dec[:, LANES * p_:LANES * (p_ + 1)]
            qd_h = jnp.where(_head_lanes(h, CHUNK), qd_p, jnp.zeros_like(qd_p))
            s0_pair = jnp.concatenate([state[2 * p_], state[2 * p_ + 1]], axis=0).astype(BF16)
            o_h = _dot(scores[:, CHUNK * h:CHUNK * (h + 1)].astype(BF16), v_h) + _dot(qd_h, s0_pair)
            out_h = _gla_out_gate(o_h, gr_scr[rows, GLA_DV * h:GLA_DV * (h + 1)], g_gla)
            mix_scr[slot, rows, ATT_W + GLA_DV * h:ATT_W + GLA_DV * (h + 1)] = out_h.astype(BF16)
        state[:] = [decay_t[GLA_DK * h:GLA_DK * (h + 1), :] * state[h] + kv[h] for h in range(GLA_HEADS)]
        yield

    def write_y(y):
        y_ref[0] = y

    def output_stage():
        return _output_stage(xo_ref.at[0], mix_scr.at[1 - slot], g_post_ref, g_fpre_ref, g_fpost_ref,
                             w_out_ref, w_up_ref, w_down_ref, write_y)

    def run_step(bounded_decay):
        keep = jnp.where(t_step == 0, 0.0, 1.0)
        state = [s_scr[GLA_DK * h:GLA_DK * (h + 1), :] * keep for h in range(GLA_HEADS)]
        results = {}
        out = output_stage()
        inp = input_stage(results)
        att = [attention_chunk(c) for c in range(n_chunks)]
        gla = [gla_chunk(c, state, bounded_decay) for c in range(n_chunks)]
        next(out)
        for c in range(n_chunks):
            next(att[c])
            next(gla[c])
        next(inp)
        spread = [(gla[c], att[c]) for c in range(n_chunks)]
        spread += [(gla[c],) for c in range(n_chunks)]
        spread.append((inp,))
        for stage in spread:
            next(out)
            for gen in stage:
                next(gen)
        for _ in out:
            pass
        s_all = jnp.concatenate(state, axis=0)
        s_scr[...] = s_all
        sout_ref[0] = s_all
        hand_over(results)

    def finish_last_tile():
        for _ in output_stage():
            pass

    def start_first_tile():
        results = {}
        for _ in input_stage(results):
            pass
        hand_over(results)

    lax.switch(branch, [lambda: run_step(True), lambda: run_step(False), finish_last_tile, start_first_tile])


def _sample_kernel(sinks_ref, x_ref, cos_ref, sin_ref, gmat_ref, masks_ref, bias_win_ref, bias_new_ref,
                   ck_ref, cv_ref, st_ref,
                   g_pre_ref, g_post_ref, g_fpre_ref, g_fpost_ref, g_gla_ref, b_gk_ref,
                   w_in_ref, w_gk2_ref, w_out_ref, w_up_ref, w_down_ref,
                   y_ref, kwin_ref, vwin_ref, sout_ref,
                   *, dec_seq, levels):
    results = {}
    for _ in _sample_tile_stages(sinks_ref, x_ref, cos_ref, sin_ref, gmat_ref, masks_ref, bias_win_ref,
                                 bias_new_ref, ck_ref, cv_ref, st_ref, g_pre_ref, g_gla_ref, b_gk_ref,
                                 w_in_ref, w_gk2_ref, kwin_ref, vwin_ref, sout_ref, results,
                                 dec_seq=dec_seq, levels=levels):
        pass

    def write_y(y):
        y_ref[...] = y

    for _ in _output_stage(results["x"], results["mix"], g_post_ref, g_fpre_ref, g_fpost_ref,
                           w_out_ref, w_up_ref, w_down_ref, write_y):
        pass


def _sample_tile_stages(sinks_ref, x_ref, cos_ref, sin_ref, gmat_ref, masks_ref, bias_win_ref, bias_new_ref,
                        ck_ref, cv_ref, st_ref, g_pre_ref, g_gla_ref, b_gk_ref, w_in_ref, w_gk2_ref,
                        kwin_ref, vwin_ref, sout_ref, results, *, dec_seq, levels):
    n_seqs = ck_ref.shape[0]
    x = x_ref[...]
    proj, glr = _project(x, g_pre_ref[...], w_in_ref)
    yield
    q, k, v, gq, gk, gv, gr, log_a = _split_projection(proj, glr, cos_ref, sin_ref, w_gk2_ref, b_gk_ref)

    is_new = lax.broadcasted_iota(jnp.int32, (KV_W, WINDOW), 1) >= WINDOW - dec_seq
    for new, old_ref, out_ref in ((jnp.transpose(k), ck_ref, kwin_ref), (jnp.transpose(v), cv_ref, vwin_ref)):
        for s in range(n_seqs):
            shifted = pltpu.roll(old_ref[s], WINDOW - dec_seq, 1)
            tail = pltpu.roll(new, (WINDOW - dec_seq * (s + 1)) % CHUNK, 1)
            out_ref[s] = jnp.where(is_new, tail, shifted)

    def seq_rows(a, s):
        return a[dec_seq * s:dec_seq * (s + 1)]

    def gather_seq(per_head, s):
        return jnp.concatenate([seq_rows(a, s) for a in per_head], axis=0)

    def scatter_heads(per_seq, h):
        return jnp.concatenate([seq_rows(a, h) for a in per_seq], axis=0)

    lo_lane = lax.broadcasted_iota(jnp.int32, (CHUNK, LANES), 1) < HEAD_DIM
    q_heads = []
    for j in range(ATT_W // LANES):
        blk = q[:, LANES * j:LANES * (j + 1)]
        rolled = pltpu.roll(blk, HEAD_DIM, 1)
        g = (2 * j) // (ATT_HEADS // ATT_KV_HEADS)
        keep = lo_lane if g == 0 else jnp.logical_not(lo_lane)
        for e in range(2):
            q_heads.append(jnp.where(keep, blk if e == g else rolled, 0.0))
    s_win = [_dot(gather_seq(q_heads, s), ck_ref[s]) for s in range(n_seqs)]
    k_b = k.astype(BF16)
    v_b = v.astype(BF16)
    s_new = [_dot_nt(q_heads[h].astype(BF16), k_b) for h in range(ATT_HEADS)]
    b = _gla_exponents(log_a, gmat_ref[0:CHUNK, :])
    yield

    b3 = b.reshape(n_seqs, dec_seq, GK_W)
    b_last = jnp.broadcast_to(b3[:, dec_seq - 1:dec_seq, :], b3.shape).reshape(CHUNK, GK_W)
    q_dec = gq * jnp.exp(b)
    k_dec_t = jnp.transpose(gk * jnp.exp(b_last - b))
    decay_t = jnp.exp(jnp.transpose(b_last))
    scores = _gla_intra(gq, gk, log_a, gmat_ref[CHUNK:, :], levels, masks_ref)
    lane_k = lax.broadcasted_iota(jnp.int32, (dec_seq, GK_W), 1)
    lane_t = lax.broadcasted_iota(jnp.int32, (GLA_DK, LANES), 1)
    o_inter = []
    for s in range(n_seqs):
        qd = seq_rows(q_dec, s)
        lhs = jnp.concatenate(
            [jnp.where((lane_k >= GLA_DK * h) & (lane_k < GLA_DK * (h + 1)), qd, 0.0)
             for h in range(GLA_HEADS)], axis=0)
        o_inter.append(_dot(lhs, st_ref[s]))
    g_gla = g_gla_ref[...]
    gla_cols = []
    for h in range(GLA_HEADS):
        cs = slice(GLA_DV * h, GLA_DV * (h + 1))
        v_h = gv[:, cs].astype(BF16)
        o_h = _dot(scores[:, CHUNK * h:CHUNK * (h + 1)].astype(BF16), v_h) + scatter_heads(o_inter, h)
        gla_cols.append(_gla_out_gate(o_h, gr[:, cs], g_gla).astype(BF16))
    yield

    bias_win = bias_win_ref[...]
    bias_new = bias_new_ref[...]
    p_win = []
    p_new = []
    inv_den = []
    for h in range(ATT_HEADS):
        sw = scatter_heads(s_win, h) + bias_win
        sn = s_new[h] + bias_new
        sink = sinks_ref[h]
        m = jnp.maximum(jnp.maximum(jnp.max(sw, axis=-1, keepdims=True),
                                    jnp.max(sn, axis=-1, keepdims=True)), sink)
        pw = jnp.exp(sw - m)
        pn = jnp.exp(sn - m)
        den = (jnp.sum(pw, axis=-1, keepdims=True) + jnp.sum(pn, axis=-1, keepdims=True)
               + jnp.exp(sink - m))
        p_win.append(pw)
        p_new.append(pn.astype(BF16))
        inv_den.append(1.0 / den)
    o_new = [_dot(p_new[h], v_b) for h in range(ATT_HEADS)]
    o_win = [_dot_nt(gather_seq(p_win, s), cv_ref[s]) for s in range(n_seqs)]
    for h in range(GLA_HEADS):
        hs = slice(GLA_DK * h, GLA_DK * (h + 1))
        kd_h = k_dec_t[hs, :]
        lhs = jnp.concatenate(
            [jnp.where((lane_t >= dec_seq * s) & (lane_t < dec_seq * (s + 1)), kd_h, 0.0).astype(BF16)
             for s in range(n_seqs)], axis=0)
        upd = _dot(lhs, gv[:, GLA_DV * h:GLA_DV * (h + 1)].astype(BF16))
        for s in range(n_seqs):
            a_col = jnp.broadcast_to(decay_t[hs, dec_seq * s:dec_seq * s + 1], (GLA_DK, GLA_DV))
            sout_ref[s, hs, :] = a_col * st_ref[s, hs, :] + upd[GLA_DK * s:GLA_DK * (s + 1)]
    o_heads = [(scatter_heads(o_win, h) + o_new[h]) * inv_den[h] for h in range(ATT_HEADS)]
    mix_cols = []
    for j in range(ATT_W // LANES):
        g = (2 * j) // (ATT_HEADS // ATT_KV_HEADS)
        first = o_heads[2 * j] if g == 0 else pltpu.roll(o_heads[2 * j], HEAD_DIM, 1)
        second = o_heads[2 * j + 1] if g == 1 else pltpu.roll(o_heads[2 * j + 1], HEAD_DIM, 1)
        mix_cols.append(jnp.where(lo_lane, first, second).astype(BF16))
    results.update(x=x, mix=jnp.concatenate(mix_cols + gla_cols, axis=1))


def _gla_constants(seg, levels):
    t = np.arange(CHUNK)[:, None]
    u = np.arange(CHUNK)[None, :]
    same = (t // seg) == (u // seg)
    blocks = [same & (u <= t)]
    masks = [np.eye(CHUNK, dtype=bool)]
    for m in levels:
        ref = (t // (2 * m)) * (2 * m) + m - 1
        second = (t & m) != 0
        blocks.append(np.where(second, (u > ref) & (u <= t), (u > t) & (u <= ref)))
        masks.append(((t // (2 * m)) == (u // (2 * m))) & second & ((u & m) == 0))
    masks.append(blocks[0])
    gmat = jnp.asarray(np.concatenate(blocks, axis=0).astype(np.float32), dtype=BF16)
    return gmat, jnp.asarray(np.stack(masks).astype(np.float32))


def _attention_bias():
    t = (np.arange(2 * CHUNK) % CHUNK)[:, None]
    j = (np.arange(4 * CHUNK) % (2 * CHUNK))[None, :]
    band = (j >= t + 1) & (j <= t + WINDOW)
    first = band & (j >= CHUNK)
    neg = np.float32(-np.inf)
    return jnp.asarray(np.stack([np.where(band, np.float32(0), neg), np.where(first, np.float32(0), neg)]))


def _sample_attention_bias(dec_seq):
    r = np.arange(CHUNK)[:, None]
    c = np.arange(CHUNK)[None, :]
    t = r % dec_seq
    win = c > t
    new = ((r // dec_seq) == (c // dec_seq)) & ((c % dec_seq) <= t)
    neg = np.float32(-np.inf)
    zero = np.float32(0)
    return jnp.asarray(np.where(win, zero, neg)), jnp.asarray(np.where(new, zero, neg))


def _rope_tables(pos):
    half = HEAD_DIM // 2
    inv = ROPE_THETA ** (-jnp.arange(half, dtype=F32) / half)
    ang = inv[:, None] * pos.astype(F32)[None, :]
    return jnp.cos(ang), jnp.sin(ang)


def _const_spec(shape):
    zeros = (0,) * len(shape)
    return pl.BlockSpec(shape, lambda *_: zeros, pipeline_mode=pl.Buffered(1))


def kernel(x_prompt, x_sample, cache_k, cache_v, state_gla, w_in, w_gk2, b_gk, g_gla, sinks, w_out, g_mix_pre, g_mix_post, g_ffn_pre, g_ffn_post, w_up, w_down):
    depth = w_in.shape[0]
    assert depth == 1, "single trunk layer"
    batch, seq, d_model = x_prompt.shape
    dec_batch, dec_seq, _ = x_sample.shape
    d_ff = w_up.shape[2]
    assert seq % PROMPT_TILE == 0 and dec_batch % SAMPLE_SEQS == 0
    assert SAMPLE_SEQS * dec_seq == CHUNK and CHUNK % dec_seq == 0 and d_ff % FF_CHUNK == 0
    assert w_in.shape[2] == MAIN_W + GLA_RANK

    w_gk2_b = w_gk2[0].astype(BF16)
    g_pre = g_mix_pre[0][None, :]
    g_post = g_mix_post[0][None, :]
    g_fpre = g_ffn_pre[0][None, :]
    g_fpost = g_ffn_post[0][None, :]
    g_gla2 = g_gla[0][None, :]
    b_gk2 = b_gk[0][None, :]
    sinks1 = sinks[0]

    gains = (g_pre, g_post, g_fpre, g_fpost, g_gla2, b_gk2)
    smem_spec = pl.BlockSpec(memory_space=pltpu.SMEM)
    any_spec = pl.BlockSpec(memory_space=pl.ANY)
    big = (w_in, w_out, w_up, w_down)
    stage_rows = [1 << ((WEIGHT_STAGE_ELEMS // w.shape[2]).bit_length() - 1) for w in big]
    assert all(w.shape[1] % (WEIGHT_STAGE_SLOTS * r) == 0 for w, r in zip(big, stage_rows))
    prompt_weights = gains + (big[0], w_gk2_b, big[1], big[2], big[3])
    prompt_weight_specs = ([_const_spec(w.shape) for w in gains]
                           + [any_spec, _const_spec(w_gk2_b.shape), any_spec, any_spec, any_spec])

    cos_p, sin_p = _rope_tables(jnp.arange(seq, dtype=jnp.int32))
    gmat_p, masks_p = _gla_constants(CHUNK, PROMPT_LEVELS)
    bias = _attention_bias()
    n_t = seq // PROMPT_TILE
    tile = PROMPT_TILE
    n_tiles = batch * n_t

    def cur_tile(j):
        return jnp.minimum(j, n_tiles - 1)

    def mix_tile(j):
        return jnp.clip(j - 1, 0, n_tiles - 1)

    def done_tile(j):
        return jnp.clip(j - 2, 0, n_tiles - 1)

    tab_spec = pl.BlockSpec((HEAD_DIM // 2, tile), lambda j: (0, cur_tile(j) % n_t))
    seq_out = lambda j: (cur_tile(j) // n_t, 0, 0)
    yp, kp, vp, sp, w_in_b, w_out_b, w_up_b, w_down_b = pl.pallas_call(
        functools.partial(_prompt_kernel, tiles_per_seq=n_t, n_tiles=n_tiles),
        grid=(n_tiles + 2,),
        in_specs=[smem_spec,
                  pl.BlockSpec((1, tile, d_model), lambda j: (cur_tile(j) // n_t, cur_tile(j) % n_t, 0)),
                  pl.BlockSpec((1, tile, d_model), lambda j: (done_tile(j) // n_t, done_tile(j) % n_t, 0)),
                  tab_spec, tab_spec,
                  _const_spec(bias.shape), _const_spec(gmat_p.shape), _const_spec(masks_p.shape)]
        + prompt_weight_specs,
        out_specs=[pl.BlockSpec((1, tile, d_model), lambda j: (done_tile(j) // n_t, done_tile(j) % n_t, 0)),
                   pl.BlockSpec((1, WINDOW, KV_W), seq_out),
                   pl.BlockSpec((1, WINDOW, KV_W), seq_out),
                   pl.BlockSpec((1, GK_W, GLA_DV), lambda j: (mix_tile(j) // n_t, 0, 0))] + [any_spec] * len(big),
        out_shape=[jax.ShapeDtypeStruct((batch, seq, d_model), F32),
                   jax.ShapeDtypeStruct((batch, WINDOW, KV_W), F32),
                   jax.ShapeDtypeStruct((batch, WINDOW, KV_W), F32),
                   jax.ShapeDtypeStruct((batch, GK_W, GLA_DV), F32)]
        + [jax.ShapeDtypeStruct(w.shape[1:], BF16) for w in big],
        scratch_shapes=[pltpu.VMEM((tile, ATT_W), BF16),
                        pltpu.VMEM((2 * ATT_KV_HEADS, CHUNK + tile, LANES), BF16),
                        pltpu.VMEM((2 * ATT_KV_HEADS, CHUNK + tile, LANES), BF16),
                        pltpu.VMEM((tile, GK_W), F32),
                        pltpu.VMEM((tile, GK_W), F32),
                        pltpu.VMEM((tile, GV_W), BF16),
                        pltpu.VMEM((tile, GV_W), F32),
                        pltpu.VMEM((tile, GK_W), F32),
                        pltpu.VMEM((GK_W, GLA_DV), F32),
                        pltpu.VMEM((2, tile, d_model), BF16),
                        pltpu.SMEM((1,), jnp.int32)]
        + [pltpu.VMEM(w.shape[1:], BF16) for w in big]
        + [pltpu.VMEM((WEIGHT_STAGE_SLOTS, r, w.shape[2]), F32) for w, r in zip(big, stage_rows)]
        + [pltpu.SemaphoreType.DMA((len(big), WEIGHT_STAGE_SLOTS)), pltpu.SemaphoreType.DMA((len(big),))],
        compiler_params=pltpu.CompilerParams(
            dimension_semantics=("arbitrary",), vmem_limit_bytes=VMEM_LIMIT_BYTES),
        name="prompt_layer",
    )(sinks1, x_prompt, x_prompt, cos_p, sin_p, bias, gmat_p, masks_p, *prompt_weights)
    weights = gains + (w_in_b, w_gk2_b, w_out_b, w_up_b, w_down_b)
    weight_specs = [_const_spec(w.shape) for w in weights]

    levels_s = tuple(m for m in PROMPT_LEVELS if m < dec_seq)
    pos_s = PAST_LEN + jnp.arange(dec_seq, dtype=jnp.int32)
    cos_s, sin_s = (jnp.tile(a, (1, SAMPLE_SEQS)) for a in _rope_tables(pos_s))
    gmat_s, masks_s = _gla_constants(dec_seq, levels_s)
    bias_win, bias_new = _sample_attention_bias(dec_seq)
    rows = SAMPLE_SEQS * dec_seq
    n_g = dec_batch // SAMPLE_SEQS
    xs = x_sample.reshape(dec_batch * dec_seq, d_model)
    ck = jnp.transpose(cache_k[0].reshape(dec_batch, WINDOW, KV_W), (0, 2, 1))
    cv = jnp.transpose(cache_v[0].reshape(dec_batch, WINDOW, KV_W), (0, 2, 1))
    st = state_gla[0].reshape(dec_batch, GK_W, GLA_DV)
    seq_spec = pl.BlockSpec((SAMPLE_SEQS, WINDOW, KV_W), lambda i: (i, 0, 0))
    st_spec = pl.BlockSpec((SAMPLE_SEQS, GK_W, GLA_DV), lambda i: (i, 0, 0))
    row_spec = pl.BlockSpec((rows, d_model), lambda i: (i, 0))
    ys, ks, vs, ss = pl.pallas_call(
        functools.partial(_sample_kernel, dec_seq=dec_seq, levels=levels_s),
        grid=(n_g,),
        in_specs=[smem_spec, row_spec,
                  _const_spec(cos_s.shape), _const_spec(sin_s.shape),
                  _const_spec(gmat_s.shape), _const_spec(masks_s.shape),
                  _const_spec(bias_win.shape), _const_spec(bias_new.shape),
                  seq_spec, seq_spec, st_spec] + weight_specs,
        out_specs=[row_spec, seq_spec, seq_spec, st_spec],
        out_shape=[jax.ShapeDtypeStruct((dec_batch * dec_seq, d_model), F32),
                   jax.ShapeDtypeStruct((dec_batch, WINDOW, KV_W), F32),
                   jax.ShapeDtypeStruct((dec_batch, WINDOW, KV_W), F32),
                   jax.ShapeDtypeStruct((dec_batch, GK_W, GLA_DV), F32)],
        compiler_params=pltpu.CompilerParams(
            dimension_semantics=("arbitrary",), vmem_limit_bytes=VMEM_LIMIT_BYTES),
        name="sample_layer",
    )(sinks1, xs, cos_s, sin_s, gmat_s, masks_s, bias_win, bias_new, ck, cv, st, *weights)

    head_shape = (ATT_KV_HEADS, HEAD_DIM)
    state_shape = (GLA_HEADS, GLA_DK, GLA_DV)
    return (yp,
            ys.reshape(dec_batch, dec_seq, d_model),
            jnp.transpose(kp, (0, 2, 1)).reshape((1, batch, WINDOW) + head_shape),
            jnp.transpose(vp, (0, 2, 1)).reshape((1, batch, WINDOW) + head_shape),
            sp.reshape((1, batch) + state_shape),
            jnp.transpose(ks, (0, 2, 1)).reshape((1, dec_batch, WINDOW) + head_shape),
            jnp.transpose(vs, (0, 2, 1)).reshape((1, dec_batch, WINDOW) + head_shape),
            ss.reshape((1, dec_batch) + state_shape))
```
